```python
import math
import jax, jax.numpy as jnp
from jax import lax
import numpy as np

D_MODEL = 1024
BATCH = 8
SEQ = 2048
DEPTH = 2

CTX_LEN = 256
GRID_W = 64
F32 = jnp.float32
NORM_EPS = 1e-6
N_MOD = 6
N_EVEN = (DEPTH + 1) // 2
N_ODD = DEPTH // 2

A_HEAD_DIM = 64
A_HEADS = D_MODEL // (2 * A_HEAD_DIM)
A_WIDTH = A_HEADS * A_HEAD_DIM
A_DECAY_LORA = 64
A_ICLR_LORA = 64
A_GATE_LORA = 128
A_LN_EPS = 64e-5
A_SIZES = (A_WIDTH, A_WIDTH, A_WIDTH, 2 * A_DECAY_LORA, 2 * A_ICLR_LORA, A_GATE_LORA)
A_COLS = sum(A_SIZES)

B_HEAD_DIM = 128
B_HEADS = D_MODEL // (2 * B_HEAD_DIM)
B_WIDTH = B_HEADS * B_HEAD_DIM
B_CONV = 5
B_CHUNK = 64
B_SIZES = (B_WIDTH, B_WIDTH, B_WIDTH, 2 * B_HEADS, 2 * B_HEADS, B_WIDTH)
B_COLS = sum(B_SIZES)
EVEN_IN = A_COLS + B_COLS

C_HEAD_DIM = 64
C_Q_HEADS = D_MODEL // C_HEAD_DIM
C_KV_HEADS = C_Q_HEADS // 4
C_GROUP = C_Q_HEADS // C_KV_HEADS
C_WINDOW = 128
C_BLOCK = 128
ROPE_BASE = 10000.0
ODD_IN = (C_Q_HEADS + 2 * C_KV_HEADS) * C_HEAD_DIM

N_EXPERTS = 16
MOE_HIDDEN = 1024
EC_CAPACITY = 2

kernel_name = "hybrid_rwkv7_gdn_swa_ec_moe_dit"


def _split(t, sizes):
    return jnp.split(t, np.cumsum(sizes)[:-1].tolist(), axis=-1)


def _heads(t, n_heads):
    return t.reshape(t.shape[:-1] + (n_heads, -1))


def _rms_norm(t, gain, eps=NORM_EPS):
    tf = t.astype(F32)
    tf = tf * lax.rsqrt(jnp.mean(tf * tf, axis=-1, keepdims=True) + eps)
    return (tf * gain.astype(F32)).astype(t.dtype)


def _l2norm(t, eps=1e-6):
    tf = t.astype(F32)
    return (tf * lax.rsqrt(jnp.sum(tf * tf, axis=-1, keepdims=True) + eps)).astype(t.dtype)


def _modulate(t, shift, scale):
    return t * (1 + scale) + shift


def _centred_shift_delta(p):
    pad = jnp.pad(p, ((0, 0), (1, 1), (0, 0)))
    return 0.5 * (pad[:, :-2] + pad[:, 2:]) - p


def _centred_dwconv(p, w):
    n_tap = w.shape[0]
    half = n_tap // 2
    n_tok = p.shape[1]
    pad = jnp.pad(p, ((0, 0), (half, half), (0, 0)))
    out = pad[:, 0:n_tok] * w[0]
    for j in range(1, n_tap):
        out = out + pad[:, j:j + n_tok] * w[j]
    return out


def _two_stream_scan(scan_fn, state0, ctx_in, lat_in, reverse):
    if reverse:
        ctx_in = [jnp.flip(t, axis=1) for t in ctx_in]
        lat_in = [jnp.flip(t, axis=1) for t in lat_in]
    state_ctx, y_ctx = scan_fn(state0, *ctx_in)
    _, y_lat = scan_fn(state_ctx, *lat_in)
    if reverse:
        y_ctx, y_lat = jnp.flip(y_ctx, axis=1), jnp.flip(y_lat, axis=1)
    return y_ctx, y_lat


def _rwkv_inputs(pa, mu, w0, w2, a0, a2, g2, k_k, k_a):
    bsz, n_tok, _ = pa.shape
    pa = pa + mu * _centred_shift_delta(pa)
    r, k, v, wd, ad, gd = _split(pa, A_SIZES)
    wd = wd.reshape(bsz, n_tok, 2, A_DECAY_LORA)
    ad = ad.reshape(bsz, n_tok, 2, A_ICLR_LORA)
    w_log = -jax.nn.softplus(-(w0 + jnp.einsum('btdr,drc->btdc', jnp.tanh(wd), w2))) - 0.5
    decay = jnp.exp(-jnp.exp(w_log.astype(F32)))
    a = jax.nn.sigmoid(a0 + jnp.einsum('btdr,drc->btdc', ad, a2))
    g = jax.nn.sigmoid(gd) @ g2
    kk = _l2norm(_heads(k * k_k, A_HEADS)).reshape(k.shape)
    k_dir = k[:, :, None] * (1 + (a - 1) * k_a)
    b_dir = kk[:, :, None] * a
    return r, v, g, kk, decay, k_dir, b_dir


def _rwkv_dir(f, d):
    r, v, g, kk, decay, k_dir, b_dir = f
    return [_heads(t, A_HEADS) for t in (r, decay[:, :, d], k_dir[:, :, d], v, kk, b_dir[:, :, d])]


def _rwkv_scan(state0, r, decay, k, v, kk, b):
    xs = [jnp.moveaxis(t.astype(F32), 1, 0) for t in (r, decay, k, v, kk, b)]

    def step(state, inp):
        r_t, w_t, k_t, v_t, kk_t, b_t = inp
        sa = jnp.einsum('bhij,bhj->bhi', state, kk_t)
        state = (state * w_t[:, :, None, :] - sa[..., None] * b_t[:, :, None, :]
                 + v_t[..., None] * k_t[:, :, None, :])
        return state, jnp.einsum('bhij,bhj->bhi', state, r_t)

    state, y = lax.scan(step, state0, xs)
    return state, jnp.moveaxis(y, 0, 1)


def _rwkv_output(f, y, ln_w, ln_b, r_k):
    r, v, g, kk, decay, k_dir, b_dir = f
    rh, vh = _heads(r, A_HEADS), _heads(v, A_HEADS)
    kh = _heads(k_dir, A_HEADS)
    mean = jnp.mean(y, axis=-1, keepdims=True)
    var = jnp.mean(jnp.square(y - mean), axis=-1, keepdims=True)
    yn = ((y - mean) * lax.rsqrt(var + A_LN_EPS)).reshape(y.shape[:2] + (A_WIDTH,))
    yn = (yn * ln_w.astype(F32) + ln_b.astype(F32)).astype(r.dtype)
    bonus = jnp.sum(rh[:, :, None] * kh * r_k, axis=(2, 4))[..., None] * vh
    return (yn + bonus.reshape(yn.shape)) * g


def _gdn_inputs(pb, conv_w, a_log, dt_bias):
    bsz, n_tok, _ = pb.shape
    q, k, v, al, be, z = _split(pb, B_SIZES)
    q, k, v = jnp.split(jax.nn.silu(_centred_dwconv(jnp.concatenate([q, k, v], axis=-1), conv_w)), 3, axis=-1)
    q = _l2norm(_heads(q, B_HEADS)) * B_HEAD_DIM ** -0.5
    k = _l2norm(_heads(k, B_HEADS))
    v = _heads(v, B_HEADS)
    al = al.reshape(bsz, n_tok, 2, B_HEADS)
    be = be.reshape(bsz, n_tok, 2, B_HEADS)
    g = -jnp.exp(a_log.astype(F32)) * jax.nn.softplus((al + dt_bias).astype(F32))
    beta = jax.nn.sigmoid(be.astype(F32))
    return q, k, v, g, beta, z


def _gdn_dir(f, d):
    q, k, v, g, beta, z = f
    return [q, k, v, g[:, :, d], beta[:, :, d]]


def _gdn_chunked(state0, q, k, v, g, beta):
    bsz, n_tok, n_heads, _ = q.shape
    dv = v.shape[-1]
    n_chunk = n_tok // B_CHUNK

    def chunked(t):
        t = t.astype(F32).reshape((bsz, n_chunk, B_CHUNK) + t.shape[2:])
        return jnp.moveaxis(jnp.moveaxis(t, 1, 0), 2, 3)

    qc, kc, vc, gc, bc = [chunked(t) for t in (q, k, v, g, beta)]
    gcum = jnp.cumsum(gc, axis=-1)
    pos = jnp.arange(B_CHUNK)
    incl = pos[:, None] >= pos[None, :]
    strict = pos[:, None] > pos[None, :]
    decay = jnp.exp(jnp.where(incl, gcum[..., :, None] - gcum[..., None, :], -jnp.inf))
    k_beta = kc * bc[..., None]
    l_mat = jnp.where(strict, jnp.einsum('nbhid,nbhjd->nbhij', k_beta, kc) * decay, 0.0)
    rhs = jnp.concatenate([vc * bc[..., None], k_beta * jnp.exp(gcum)[..., None]], axis=-1)
    sol = lax.linalg.triangular_solve(l_mat, rhs, left_side=True, lower=True, unit_diagonal=True)
    u, w = sol[..., :dv], sol[..., dv:]
    a_intra = jnp.where(incl, jnp.einsum('nbhid,nbhjd->nbhij', qc, kc) * decay, 0.0)

    def step(state, inp):
        q_i, k_i, u_i, w_i, g_i, a_i = inp
        v_new = u_i - jnp.einsum('bhlk,bhkv->bhlv', w_i, state)
        o_i = (jnp.einsum('bhlk,bhkv->bhlv', q_i * jnp.exp(g_i)[..., None], state)
               + jnp.einsum('bhlm,bhmv->bhlv', a_i, v_new))
        g_last = g_i[..., -1:]
        state = (state * jnp.exp(g_last)[..., None]
                 + jnp.einsum('bhlk,bhlv->bhkv', k_i * jnp.exp(g_last - g_i)[..., None], v_new))
        return state, o_i

    state, o = lax.scan(step, state0, (qc, kc, u, w, gcum, a_intra))
    o = jnp.moveaxis(jnp.moveaxis(o, 3, 2), 0, 1).reshape(bsz, n_tok, n_heads, dv)
    return state, o


def _gdn_output(f, o, norm_w):
    z = f[-1]
    on = o * lax.rsqrt(jnp.mean(o * o, axis=-1, keepdims=True) + NORM_EPS) * norm_w.astype(F32)
    return (on.astype(z.dtype) * jax.nn.silu(_heads(z, B_HEADS))).reshape(z.shape)


def _even_mixer(h_ctx, h_lat, w_in, w_out, a_mu, a_w0, a_w2, a_a0, a_a2, a_g2, a_k_k, a_k_a, a_r_k,
                a_ln_w, a_ln_b, b_conv, b_a_log, b_dt_bias, b_norm):
    bsz = h_lat.shape[0]
    pa_c, pb_c = _split(h_ctx @ w_in, (A_COLS, B_COLS))
    pa_l, pb_l = _split(h_lat @ w_in, (A_COLS, B_COLS))
    ra_c = _rwkv_inputs(pa_c, a_mu, a_w0, a_w2, a_a0, a_a2, a_g2, a_k_k, a_k_a)
    ra_l = _rwkv_inputs(pa_l, a_mu, a_w0, a_w2, a_a0, a_a2, a_g2, a_k_k, a_k_a)
    gb_c = _gdn_inputs(pb_c, b_conv, b_a_log, b_dt_bias)
    gb_l = _gdn_inputs(pb_l, b_conv, b_a_log, b_dt_bias)
    s0_a = jnp.zeros((bsz, A_HEADS, A_HEAD_DIM, A_HEAD_DIM), F32)
    s0_b = jnp.zeros((bsz, B_HEADS, B_HEAD_DIM, B_HEAD_DIM), F32)
    ya_cf, ya_lf = _two_stream_scan(_rwkv_scan, s0_a, _rwkv_dir(ra_c, 0), _rwkv_dir(ra_l, 0), False)
    ya_cb, ya_lb = _two_stream_scan(_rwkv_scan, s0_a, _rwkv_dir(ra_c, 1), _rwkv_dir(ra_l, 1), True)
    yb_cf, yb_lf = _two_stream_scan(_gdn_chunked, s0_b, _gdn_dir(gb_c, 0), _gdn_dir(gb_l, 0), False)
    yb_cb, yb_lb = _two_stream_scan(_gdn_chunked, s0_b, _gdn_dir(gb_c, 1), _gdn_dir(gb_l, 1), True)
    out_ctx = jnp.concatenate([_rwkv_output(ra_c, ya_cf + ya_cb, a_ln_w, a_ln_b, a_r_k),
                               _gdn_output(gb_c, yb_cf + yb_cb, b_norm)], axis=-1) @ w_out
    out_lat = jnp.concatenate([_rwkv_output(ra_l, ya_lf + ya_lb, a_ln_w, a_ln_b, a_r_k),
                               _gdn_output(gb_l, yb_lf + yb_lb, b_norm)], axis=-1) @ w_out
    return out_ctx, out_lat


def _axial_rope(t, row, col):
    quarter = C_HEAD_DIM // 4
    half = C_HEAD_DIM // 2
    inv = ROPE_BASE ** (-jnp.arange(quarter, dtype=F32) / quarter)

    def rot(part, pos):
        ang = pos[:, None] * inv[None, :]
        cos = jnp.cos(ang)[None, :, None, :].astype(t.dtype)
        sin = jnp.sin(ang)[None, :, None, :].astype(t.dtype)
        p1, p2 = part[..., :quarter], part[..., quarter:]
        return jnp.concatenate([p1 * cos - p2 * sin, p2 * cos + p1 * sin], axis=-1)

    return jnp.concatenate([rot(t[..., :half], row), rot(t[..., half:], col)], axis=-1)


def _odd_mixer(h_ctx, h_lat, w_in, w_out, sink):
    bsz, n_lat, _ = h_lat.shape
    q_cols = C_Q_HEADS * C_HEAD_DIM
    kv_cols = C_KV_HEADS * C_HEAD_DIM
    k_ctx, v_ctx = [_heads(t, C_KV_HEADS) for t in jnp.split(h_ctx @ w_in[:, q_cols:], 2, axis=-1)]
    q, k, v = _split(h_lat @ w_in, (q_cols, kv_cols, kv_cols))
    rows = n_lat // GRID_W
    row = jnp.repeat(jnp.arange(rows, dtype=F32), GRID_W)
    col = jnp.tile(jnp.arange(GRID_W, dtype=F32), rows)
    q = _axial_rope(_heads(q, C_Q_HEADS), row, col) * C_HEAD_DIM ** -0.5
    q = q.reshape(bsz, n_lat, C_KV_HEADS, C_GROUP, C_HEAD_DIM)
    k = _axial_rope(_heads(k, C_KV_HEADS), row, col)
    v = _heads(v, C_KV_HEADS)
    pad = ((0, 0), (C_BLOCK, C_BLOCK), (0, 0), (0, 0))
    k_pad, v_pad = jnp.pad(k, pad), jnp.pad(v, pad)
    span = 3 * C_BLOCK
    sink_logit = jnp.broadcast_to(sink.astype(F32).reshape(1, C_KV_HEADS, C_GROUP, 1, 1),
                                  (bsz, C_KV_HEADS, C_GROUP, C_BLOCK, 1))

    def block(n):
        start = n * C_BLOCK
        qb = lax.dynamic_slice_in_dim(q, start, C_BLOCK, axis=1)
        kb = lax.dynamic_slice_in_dim(k_pad, start, span, axis=1)
        vb = lax.dynamic_slice_in_dim(v_pad, start, span, axis=1)
        q_pos = start + jnp.arange(C_BLOCK)
        k_pos = start - C_BLOCK + jnp.arange(span)
        valid = ((jnp.abs(q_pos[:, None] - k_pos[None, :]) <= C_WINDOW)
                 & (k_pos >= 0)[None, :] & (k_pos < n_lat)[None, :])
        s_loc = jnp.where(valid, jnp.einsum('bqhgd,bshd->bhgqs', qb, kb).astype(F32), -jnp.inf)
        s_ctx = jnp.einsum('bqhgd,bshd->bhgqs', qb, k_ctx).astype(F32)
        p = jax.nn.softmax(jnp.concatenate([s_loc, s_ctx, sink_logit], axis=-1), axis=-1).astype(vb.dtype)
        return (jnp.einsum('bhgqs,bshd->bqhgd', p[..., :span], vb)
                + jnp.einsum('bhgqs,bshd->bqhgd', p[..., span:-1], v_ctx))

    out = lax.map(block, jnp.arange(n_lat // C_BLOCK))
    out = jnp.moveaxis(out, 0, 1).reshape(bsz, n_lat, q_cols)
    return out @ w_out


def _ec_moe(h, router, w1, w3, w2):
    bsz, n_tok, d = h.shape
    cap = EC_CAPACITY * n_tok // N_EXPERTS
    aff = jax.nn.softmax(jnp.einsum('btd,de->bte', h, router).astype(F32), axis=-1)
    gate, idx = lax.top_k(jnp.swapaxes(aff, 1, 2), cap)
    xe = jax.vmap(lambda hb, ib: hb[ib])(h, idx)
    hid = jax.nn.silu(jnp.einsum('becd,edf->becf', xe, w1)) * jnp.einsum('becd,edf->becf', xe, w3)
    ye = jnp.einsum('becf,efd->becd', hid, w2) * gate[..., None].astype(h.dtype)
    return jax.vmap(lambda ib, yb: jnp.zeros((n_tok, d), yb.dtype).at[ib.reshape(-1)].add(yb.reshape(-1, d)))(idx, ye)


def setup_inputs(seed: int = 0) -> dict:
    key = jax.random.key(seed)
    ks = iter(jax.random.split(key, 48))

    def nrm(shape, scale):
        return scale * jax.random.normal(next(ks), shape, F32)

    def unif(shape, lo, hi):
        return jax.random.uniform(next(ks), shape, F32, lo, hi)

    dt = jnp.exp(unif((N_EVEN, 2, B_HEADS), math.log(1e-3), math.log(1e-1)))
    return {
        "x": nrm((BATCH, SEQ, D_MODEL), 1.0),
        "c": nrm((BATCH, D_MODEL), 1.0),
        "ctx": nrm((BATCH, CTX_LEN, D_MODEL), 1.0),
        "c_ctx": nrm((D_MODEL,), 1.0),
        "ada_w": nrm((DEPTH, D_MODEL, N_MOD * D_MODEL), 0.5 * D_MODEL ** -0.5),
        "ada_b": nrm((DEPTH, N_MOD * D_MODEL), 0.01),
        "norm_mix": 1.0 + nrm((DEPTH, D_MODEL), 0.1),
        "norm_ffn": 1.0 + nrm((DEPTH, D_MODEL), 0.1),
        "e_w_in": nrm((N_EVEN, D_MODEL, EVEN_IN), D_MODEL ** -0.5),
        "e_w_out": nrm((N_EVEN, A_WIDTH + B_WIDTH, D_MODEL), (A_WIDTH + B_WIDTH) ** -0.5),
        "a_mu": unif((N_EVEN, A_COLS), 0.0, 1.0),
        "a_w0": unif((N_EVEN, 2, A_WIDTH), -6.0, -1.0),
        "a_w2": nrm((N_EVEN, 2, A_DECAY_LORA, A_WIDTH), 0.1 * A_DECAY_LORA ** -0.5),
        "a_a0": nrm((N_EVEN, 2, A_WIDTH), 0.1),
        "a_a2": nrm((N_EVEN, 2, A_ICLR_LORA, A_WIDTH), 0.1 * A_ICLR_LORA ** -0.5),
        "a_g2": nrm((N_EVEN, A_GATE_LORA, A_WIDTH), A_GATE_LORA ** -0.5),
        "a_k_k": 0.85 + nrm((N_EVEN, A_WIDTH), 0.05),
        "a_k_a": 1.0 + nrm((N_EVEN, A_WIDTH), 0.05),
        "a_r_k": nrm((N_EVEN, A_HEADS, A_HEAD_DIM), 0.1),
        "a_ln_w": 1.0 + nrm((N_EVEN, A_WIDTH), 0.1),
        "a_ln_b": nrm((N_EVEN, A_WIDTH), 0.01),
        "b_conv": nrm((N_EVEN, B_CONV, 3 * B_WIDTH), B_CONV ** -0.5),
        "b_a_log": jnp.log(unif((N_EVEN, 2, B_HEADS), 1.0, 16.0)),
        "b_dt_bias": dt + jnp.log(-jnp.expm1(-dt)),
        "b_norm": 1.0 + nrm((N_EVEN, B_HEAD_DIM), 0.1),
        "o_w_in": nrm((N_ODD, D_MODEL, ODD_IN), D_MODEL ** -0.5),
        "o_w_out": nrm((N_ODD, C_Q_HEADS * C_HEAD_DIM, D_MODEL), (C_Q_HEADS * C_HEAD_DIM) ** -0.5),
        "o_sink": nrm((N_ODD, C_Q_HEADS), 0.5),
        "moe_router": nrm((DEPTH, D_MODEL, N_EXPERTS), D_MODEL ** -0.5),
        "moe_w1": nrm((DEPTH, N_EXPERTS, D_MODEL, MOE_HIDDEN), D_MODEL ** -0.5),
        "moe_w3": nrm((DEPTH, N_EXPERTS, D_MODEL, MOE_HIDDEN), D_MODEL ** -0.5),
        "moe_w2": nrm((DEPTH, N_EXPERTS, MOE_HIDDEN, D_MODEL), MOE_HIDDEN ** -0.5),
        "final_norm": 1.0 + nrm((D_MODEL,), 0.1),
    }


def reference(x, c, ctx, c_ctx, ada_w, ada_b, norm_mix, norm_ffn, e_w_in, e_w_out, a_mu, a_w0, a_w2, a_a0,
              a_a2, a_g2, a_k_k, a_k_a, a_r_k, a_ln_w, a_ln_b, b_conv, b_a_log, b_dt_bias, b_norm, o_w_in,
              o_w_out, o_sink, moe_router, moe_w1, moe_w3, moe_w2, final_norm):
    x_lat, x_ctx = x, ctx
    silu_c, silu_cc = jax.nn.silu(c), jax.nn.silu(c_ctx)
    for i in range(DEPTH):
        j = i // 2
        mod_l = jnp.split((silu_c @ ada_w[i] + ada_b[i])[:, None, :], N_MOD, axis=-1)
        mod_c = jnp.split(silu_cc @ ada_w[i] + ada_b[i], N_MOD, axis=-1)
        h_lat = _modulate(_rms_norm(x_lat, norm_mix[i]), mod_l[0], mod_l[1])
        h_ctx = _modulate(_rms_norm(x_ctx, norm_mix[i]), mod_c[0], mod_c[1])
        if i % 2 == 0:
            o_ctx, o_lat = _even_mixer(h_ctx, h_lat, e_w_in[j], e_w_out[j], a_mu[j], a_w0[j], a_w2[j], a_a0[j],
                                       a_a2[j], a_g2[j], a_k_k[j], a_k_a[j], a_r_k[j], a_ln_w[j], a_ln_b[j],
                                       b_conv[j], b_a_log[j], b_dt_bias[j], b_norm[j])
            x_ctx = x_ctx + mod_c[2] * o_ctx
        else:
            o_lat = _odd_mixer(h_ctx, h_lat, o_w_in[j], o_w_out[j], o_sink[j])
        x_lat = x_lat + mod_l[2] * o_lat
        x_lat = x_lat + mod_l[5] * _ec_moe(_modulate(_rms_norm(x_lat, norm_ffn[i]), mod_l[3], mod_l[4]),
                                           moe_router[i], moe_w1[i], moe_w3[i], moe_w2[i])
        if i < DEPTH - 1:
            x_ctx = x_ctx + mod_c[5] * _ec_moe(_modulate(_rms_norm(x_ctx, norm_ffn[i]), mod_c[3], mod_c[4]),
                                               moe_router[i], moe_w1[i], moe_w3[i], moe_w2[i])
    return _rms_norm(x_lat, final_norm)
```

```python
import functools
import math

import jax
import jax.numpy as jnp
import numpy as np
from jax import lax
from jax.experimental import pallas as pl
from jax.experimental.pallas import tpu as pltpu

F32 = jnp.float32
BF16 = jnp.bfloat16
HIGHEST = lax.Precision.HIGHEST

D = 1024
N_MOD = 6
NORM_EPS = 1e-6
A_HD, A_H, A_W = 64, 8, 512
A_LN_EPS = 64e-5
B_HD, B_H, B_W = 128, 4, 512
B_CONV = 5
CHUNK = 64
C_HD, C_QH, C_KVH, C_GROUP = 64, 16, 4, 4
C_BLOCK = 128
ROPE_BASE = 10000.0
GRID_W = 64
N_EXP = 16
EC_CAP = 2
ROUTE_BISECT = 40
ROUTE_SNAP = 3
MOD_ROWS = 8
LANE = 128
SUB = 8
VMEM_LIMIT = 56 * 1024 * 1024

NT = (((1,), (1,)), ((), ()))
NN = (((1,), (0,)), ((), ()))
TN = (((0,), (0,)), ((), ()))


def _cparams(sem):
    return pltpu.CompilerParams(dimension_semantics=sem, vmem_limit_bytes=VMEM_LIMIT)


def _sigmoid(x):
    return 1.0 / (1.0 + jnp.exp(-x))


def _softplus(x):
    return jnp.maximum(x, 0.0) + jnp.log(1.0 + jnp.exp(-jnp.abs(x)))


def _split2(x):
    hi = x.astype(BF16)
    lo = (x - hi.astype(F32)).astype(BF16)
    return hi, lo


def _dot(a, b, dims=NN, passes=1):
    dg = functools.partial(lax.dot_general, dimension_numbers=dims, preferred_element_type=F32)
    if passes == 1:
        return dg(a.astype(BF16), b.astype(BF16))
    ah, al = _split2(a)
    bh, bl = _split2(b)
    return dg(ah, bh) + (dg(ah, bl) + dg(al, bh))


def _dot_mask(x, mask_bf16, dims, mask_first):
    dg = functools.partial(lax.dot_general, dimension_numbers=dims, preferred_element_type=F32)
    h1 = x.astype(BF16)
    r1 = x - h1.astype(F32)
    h2 = r1.astype(BF16)
    h3 = (r1 - h2.astype(F32)).astype(BF16)
    if mask_first:
        return dg(mask_bf16, h1) + (dg(mask_bf16, h2) + dg(mask_bf16, h3))
    return dg(h1, mask_bf16) + (dg(h2, mask_bf16) + dg(h3, mask_bf16))


def _group_sum(x, ones_bd):
    hi, lo = _split2(x)
    dg = functools.partial(lax.dot_general, dimension_numbers=NN, preferred_element_type=F32)
    return dg(hi, ones_bd) + dg(lo, ones_bd)


def _adaln_kernel(c_ref, w_ref, b_ref, o_ref):
    c = c_ref[...]
    s = c * _sigmoid(c)
    o_ref[0] = jnp.dot(s, w_ref[0], precision=HIGHEST, preferred_element_type=F32) + b_ref[0]


def _adaln(cc, ada_w, ada_b):
    depth, _, n = ada_w.shape
    rows = cc.shape[0]
    tn = 768
    return pl.pallas_call(
        _adaln_kernel,
        grid=(depth, n // tn),
        in_specs=[pl.BlockSpec((rows, D), lambda i, j: (0, 0)),
                  pl.BlockSpec((1, D, tn), lambda i, j: (i, 0, j)),
                  pl.BlockSpec((1, 1, tn), lambda i, j: (i, 0, j))],
        out_specs=pl.BlockSpec((1, rows, tn), lambda i, j: (i, 0, j)),
        out_shape=jax.ShapeDtypeStruct((depth, rows, n), F32),
        compiler_params=_cparams(("parallel", "parallel")),
        name="adaln",
    )(cc, ada_w, ada_b.reshape(depth, 1, n))


def _norm_kernel(*refs, k, kres, has_res, has_mod, has_router):
    it = iter(refs)
    x_ref = next(it)
    d_ref = next(it) if has_res else None
    mres_ref = next(it) if has_res else None
    gain_ref = next(it)
    mod_ref = next(it) if has_mod else None
    rt_ref = next(it) if has_router else None
    xo_ref = next(it) if has_res else None
    h_ref = next(it)
    lg_ref = next(it) if has_router else None

    x = x_ref[0]
    if has_res:
        x = x + mres_ref[0, kres:kres + 1, :] * d_ref[0]
        xo_ref[0] = x
    xn = x * lax.rsqrt(jnp.mean(x * x, axis=-1, keepdims=True) + NORM_EPS) * gain_ref[...]
    if has_mod:
        xn = xn * (1.0 + mod_ref[0, k + 1:k + 2, :]) + mod_ref[0, k:k + 1, :]
    h_ref[0] = xn.astype(h_ref.dtype)
    if has_router:
        lg_ref[0] = lax.dot_general(rt_ref[...], xn, NT, precision=HIGHEST, preferred_element_type=F32)


def _norm(x, gain, mod=None, *, k=0, res=None, router_t=None, tt, out_dtype=BF16):
    g, t, _ = x.shape
    has_res, has_mod, has_router = res is not None, mod is not None, router_t is not None
    kres = res[2] if has_res else 0
    tile = pl.BlockSpec((1, tt, D), lambda gi, i: (gi, i, 0))
    modspec = pl.BlockSpec((1, MOD_ROWS, D), lambda gi, i: (gi, 0, 0))
    ins, specs = [x], [tile]
    if has_res:
        ins += [res[0], res[1]]
        specs += [tile, modspec]
    ins.append(gain.reshape(1, D))
    specs.append(pl.BlockSpec((1, D), lambda gi, i: (0, 0)))
    if has_mod:
        ins.append(mod)
        specs.append(modspec)
    if has_router:
        ins.append(router_t)
        specs.append(pl.BlockSpec((N_EXP, D), lambda gi, i: (0, 0)))
    outs, ospecs = [], []
    if has_res:
        outs.append(jax.ShapeDtypeStruct((g, t, D), F32))
        ospecs.append(tile)
    outs.append(jax.ShapeDtypeStruct((g, t, D), out_dtype))
    ospecs.append(tile)
    if has_router:
        outs.append(jax.ShapeDtypeStruct((g, N_EXP, t), F32))
        ospecs.append(pl.BlockSpec((1, N_EXP, tt), lambda gi, i: (gi, 0, i)))
    res_out = list(pl.pallas_call(
        functools.partial(_norm_kernel, k=k, kres=kres, has_res=has_res, has_mod=has_mod, has_router=has_router),
        grid=(g, t // tt), in_specs=specs, out_specs=ospecs, out_shape=outs,
        compiler_params=_cparams(("parallel", "parallel")), name="norm_mod",
    )(*ins))
    x_new = res_out.pop(0) if has_res else None
    h = res_out.pop(0)
    lg = res_out.pop(0) if has_router else None
    return x_new, h, lg


def _mm_kernel(x_ref, w_ref, o_ref):
    o_ref[0] = jnp.dot(x_ref[0], w_ref[...], preferred_element_type=F32).astype(o_ref.dtype)


def _mm(x, w, out_dtype, tm, tn):
    g, t, kdim = x.shape
    n = w.shape[1]
    return pl.pallas_call(
        _mm_kernel,
        grid=(g, t // tm, n // tn),
        in_specs=[pl.BlockSpec((1, tm, kdim), lambda gi, i, j: (gi, i, 0)),
                  pl.BlockSpec((kdim, tn), lambda gi, i, j: (0, j))],
        out_specs=pl.BlockSpec((1, tm, tn), lambda gi, i, j: (gi, i, j)),
        out_shape=jax.ShapeDtypeStruct((g, t, n), out_dtype),
        compiler_params=_cparams(("parallel", "parallel", "parallel")), name="proj",
    )(x, w)


def _mm_res_kernel(y_ref, w_ref, x_ref, mod_ref, o_ref, *, k):
    acc = jnp.dot(y_ref[0], w_ref[...], preferred_element_type=F32)
    o_ref[0] = x_ref[0] + mod_ref[0, k:k + 1, :] * acc


def _mm_res(y, w, x, mod, k, tm, groups=None):
    g_all, t, kdim = y.shape
    g = g_all if groups is None else groups
    return pl.pallas_call(
        functools.partial(_mm_res_kernel, k=k),
        grid=(g, t // tm),
        in_specs=[pl.BlockSpec((1, tm, kdim), lambda gi, i: (gi, i, 0)),
                  pl.BlockSpec((kdim, D), lambda gi, i: (0, 0)),
                  pl.BlockSpec((1, tm, D), lambda gi, i: (gi, i, 0)),
                  pl.BlockSpec((1, MOD_ROWS, D), lambda gi, i: (gi, 0, 0))],
        out_specs=pl.BlockSpec((1, tm, D), lambda gi, i: (gi, i, 0)),
        out_shape=jax.ShapeDtypeStruct((g, t, D), F32),
        compiler_params=_cparams(("parallel", "parallel")), name="out_proj_res",
    )(y, w, x, mod)


def _fill_halo(buf, x, pv_ref, nx_ref, has_prev, has_next, tt):
    buf[0:SUB, :] = jnp.where(has_prev, pv_ref[0], 0.0)
    buf[SUB:SUB + tt, :] = x
    buf[SUB + tt:2 * SUB + tt, :] = jnp.where(has_next, nx_ref[0], 0.0)


def _rwkv_prep_kernel(pa_ref, pv_ref, nx_ref, mu_ref, w0_ref, w2_ref, a0_ref, a2_ref, g2_ref, kk_ref, ka_ref,
                      rk_ref, bd_ref, rvk_ref, lw_ref, kd_ref, bdir_ref, bg_ref, buf, *, tt, lat_groups, ntile):
    gi, i = pl.program_id(0), pl.program_id(1)
    is_lat = gi < lat_groups
    x = pa_ref[0]
    _fill_halo(buf, x, pv_ref, nx_ref, jnp.logical_and(is_lat, i > 0), jnp.logical_and(is_lat, i < ntile - 1), tt)
    xm = buf[SUB - 1:SUB - 1 + tt, :]
    xp = buf[SUB + 1:SUB + 1 + tt, :]
    x = x + mu_ref[...] * (0.5 * (xm + xp) - x)
    w = A_W
    r, kx, v = x[:, 0:w], x[:, w:2 * w], x[:, 2 * w:3 * w]
    wd, ad, gd = x[:, 3 * w:3 * w + LANE], x[:, 3 * w + LANE:3 * w + 2 * LANE], x[:, 3 * w + 2 * LANE:3 * w + 3 * LANE]
    w_log = -_softplus(-(w0_ref[...] + _dot(jnp.tanh(wd), w2_ref[...], NN, 3))) - 0.5
    lw = -jnp.exp(w_log)
    a = _sigmoid(a0_ref[...] + _dot(ad, a2_ref[...], NN, 3))
    gate = _dot(_sigmoid(gd), g2_ref[...], NN, 3)
    ones_bd = bd_ref[...]
    kq = kx * kk_ref[...]
    kk = kq * lax.rsqrt(_group_sum(kq * kq, ones_bd) + 1e-6)
    k2 = jnp.concatenate([kx, kx], axis=1)
    kdir = k2 * (1.0 + (a - 1.0) * ka_ref[...])
    bdir = jnp.concatenate([kk, kk], axis=1) * a
    rr = r * rk_ref[...]
    bonus = _group_sum(rr * (kdir[:, 0:w] + kdir[:, w:2 * w]), ones_bd) * v
    rvk_ref[0] = jnp.concatenate([r, v, kk], axis=1)
    lw_ref[0] = lw
    kd_ref[0] = kdir
    bdir_ref[0] = bdir
    bg_ref[0] = jnp.concatenate([bonus, gate], axis=1)


def _halo_specs(width, tt, t):
    nb = tt // SUB
    last = t // SUB - 1
    prev = pl.BlockSpec((1, SUB, width), lambda gi, i: (gi, jnp.maximum(i * nb - 1, 0), 0))
    nxt = pl.BlockSpec((1, SUB, width), lambda gi, i: (gi, jnp.minimum((i + 1) * nb, last), 0))
    return prev, nxt


def _rwkv_prep(pa, p, tt, lat_groups):
    g, t, wa = pa.shape
    full = lambda arr: pl.BlockSpec(arr.shape, lambda gi, i: (0,) * arr.ndim)
    tile = lambda width: pl.BlockSpec((1, tt, width), lambda gi, i: (gi, i, 0))
    prev, nxt = _halo_specs(wa, tt, t)
    consts = [p["mu"], p["w0"], p["w2"], p["a0"], p["a2"], p["g2"], p["k_k"], p["k_a"], p["r_k"], p["ones_bd"]]
    widths = (3 * A_W, 2 * A_W, 2 * A_W, 2 * A_W, 2 * A_W)
    return pl.pallas_call(
        functools.partial(_rwkv_prep_kernel, tt=tt, lat_groups=lat_groups, ntile=t // tt),
        grid=(g, t // tt),
        in_specs=[tile(wa), prev, nxt] + [full(c) for c in consts],
        out_specs=[tile(wd) for wd in widths],
        out_shape=[jax.ShapeDtypeStruct((g, t, wd), F32) for wd in widths],
        scratch_shapes=[pltpu.VMEM((tt + 2 * SUB, wa), F32)],
        compiler_params=_cparams(("parallel", "parallel")), name="rwkv_prep",
    )(pa, pa, pa, *consts)


def _order_masks(n_rows, blk, fwd):
    ii = lax.broadcasted_iota(jnp.int32, (n_rows, n_rows), 0)
    jj = lax.broadcasted_iota(jnp.int32, (n_rows, n_rows), 1)
    same = (ii // blk) == (jj // blk)
    before = (jj < ii) if fwd else (jj > ii)
    strict = jnp.logical_and(same, before)
    incl = jnp.logical_and(same, jnp.logical_or(before, ii == jj))
    return same, strict, incl


def _rwkv_scan_kernel(rvkf_ref, lwf_ref, kdf_ref, bdf_ref, rvkb_ref, lwb_ref, kdb_ref, bdb_ref, yf_ref, yb_ref,
                      s_ref, *, passes):
    @pl.when(pl.program_id(1) == 0)
    def _():
        s_ref[...] = jnp.zeros_like(s_ref)

    n_pair = A_W // LANE
    s_all = s_ref[...]
    chains = (_rwkv_operands(rvkf_ref[0], lwf_ref[0], kdf_ref[0], bdf_ref[0], True)
              + _rwkv_operands(rvkb_ref[0], lwb_ref[0], kdb_ref[0], bdb_ref[0], False))
    states = [s_all[d, p] for d in range(2) for p in range(n_pair)]
    ys, s_new = _rwkv_solve(chains, states, passes)
    yf_ref[0] = jnp.concatenate(ys[0:n_pair], axis=1)
    yb_ref[0] = jnp.concatenate(ys[n_pair:2 * n_pair], axis=1)
    for i, s_i in enumerate(s_new):
        s_ref[i // n_pair, i % n_pair] = s_i


def _rwkv_operands(rvk, lw, kd, bd, fwd):
    L = CHUNK
    L2 = 2 * L
    _, _, incl_l = _order_masks(L, L, fwd)
    _, strict, incl = _order_masks(L2, L, fwd)
    cum = jnp.dot(jnp.where(incl_l, 1.0, 0.0), lw, precision=HIGHEST, preferred_element_type=F32)
    tot = jnp.sum(lw, axis=0, keepdims=True)
    g_inv = jnp.exp(-cum)
    g_rest = jnp.exp(tot - cum)
    g_tot = jnp.exp(tot)
    w = A_W
    r_t = rvk[:, 0:w] * jnp.exp(cum)
    v = rvk[:, w:2 * w]
    a_t = rvk[:, 2 * w:3 * w] * jnp.exp(cum - lw)
    b_t, k_t = bd * g_inv, kd * g_inv
    b_g, k_g = bd * g_rest, kd * g_rest
    low = lax.broadcasted_iota(jnp.int32, (1, LANE), 1) < A_HD

    def expand(x):
        return jnp.concatenate([jnp.where(low, x, 0.0), jnp.where(low, 0.0, x)], axis=0)

    chains = []
    for p in range(A_W // LANE):
        sl = slice(p * LANE, (p + 1) * LANE)
        chains.append(dict(
            ev=expand(v[:, sl]),
            lhs=jnp.concatenate([expand(a_t[:, sl]), expand(r_t[:, sl])], axis=0),
            rhs=jnp.concatenate([expand(b_t[:, sl]), expand(k_t[:, sl])], axis=0),
            bk=jnp.concatenate([expand(b_g[:, sl]), expand(k_g[:, sl])], axis=0),
            g_tot=g_tot[:, sl], strict=strict, incl=incl))
    return chains


def _rwkv_solve(chains, states, passes):
    L = CHUNK
    L2 = 2 * L
    each = lambda fn, *cols: [fn(*args) for args in zip(*cols)]
    quad = each(lambda c: _dot(c["lhs"], c["rhs"], NT, passes), chains)
    es = each(lambda c, s: _dot(c["lhs"], s, NT, passes), chains, states)
    n_ab = each(lambda c, q: jnp.where(c["strict"], q[0:L2, 0:L2], 0.0), chains, quad)
    x = each(lambda c, q, e: e[0:L2] + _dot(jnp.where(c["strict"], q[0:L2, L2:2 * L2], 0.0), c["ev"], NN, passes),
             chains, quad, es)
    x = each(lambda n, xi: xi - _dot(n, xi, NN, passes), n_ab, x)
    npow = n_ab
    for _ in range(int(math.log2(L)) - 1):
        npow = each(lambda n: _dot(n, n, NN, passes), npow)
        x = each(lambda n, xi: xi + _dot(n, xi, NN, passes), npow, x)
    pv = each(lambda c, xi: jnp.concatenate([-xi, c["ev"]], axis=0), chains, x)
    y2 = each(lambda c, q, e, pvi: e[L2:2 * L2] + _dot(
        jnp.concatenate([jnp.where(c["incl"], q[L2:2 * L2, 0:L2], 0.0),
                         jnp.where(c["incl"], q[L2:2 * L2, L2:2 * L2], 0.0)], axis=1), pvi, NN, passes),
        chains, quad, es, pv)
    ys = [y[0:L] + y[L:L2] for y in y2]
    s_new = each(lambda c, s, pvi: s * c["g_tot"] + _dot(pvi, c["bk"], TN, passes), chains, states, pv)
    return ys, s_new


def _chunk_index(b, c, fwd, n_ctx_chunk, n_lat_chunk, lat_groups):
    in_ctx = c < n_ctx_chunk
    cc = c if fwd else n_ctx_chunk - 1 - c
    lc = c - n_ctx_chunk if fwd else n_lat_chunk - 1 - (c - n_ctx_chunk)
    grp = jnp.where(in_ctx, lat_groups, b)
    chunk = jnp.where(in_ctx, b * n_ctx_chunk + cc, lc)
    return grp, chunk


def _scan_specs(widths, bsz, ctx_len, t):
    ncc, nlc = ctx_len // CHUNK, t // CHUNK

    def spec(width, lane_blk, fwd):
        def index(b, c):
            grp, ch = _chunk_index(b, c, fwd, ncc, nlc, bsz)
            return grp, ch, lane_blk
        return pl.BlockSpec((1, CHUNK, width), index)

    return [[spec(w, (0 if fwd else 1) if blk is None else blk, fwd) for w, blk in widths] for fwd in (True, False)]


def _rwkv_scan(rvk, lw, kd, bd, bsz, ctx_len, passes):
    g, t, _ = rvk.shape
    ins_f, ins_b = _scan_specs([(3 * A_W, 0), (A_W, None), (A_W, None), (A_W, None)], bsz, ctx_len, t)
    (out_f,), (out_b,) = _scan_specs([(A_W, 0)], bsz, ctx_len, t)
    return pl.pallas_call(
        functools.partial(_rwkv_scan_kernel, passes=passes),
        grid=(bsz, (ctx_len + t) // CHUNK),
        in_specs=ins_f + ins_b,
        out_specs=[out_f, out_b],
        out_shape=[jax.ShapeDtypeStruct((g, t, A_W), F32)] * 2,
        scratch_shapes=[pltpu.VMEM((2, A_W // LANE, LANE, LANE), F32)],
        compiler_params=_cparams(("parallel", "arbitrary")), name="rwkv_scan",
    )(rvk, lw, kd, bd, rvk, lw, kd, bd)


def _gdn_prep_kernel(pb_ref, pv_ref, nx_ref, ab_ref, cw_ref, alog_ref, dtb_ref, qkv_ref, gb_ref, buf,
                     *, tt, lat_groups, ntile):
    gi, i = pl.program_id(0), pl.program_id(1)
    is_lat = gi < lat_groups
    _fill_halo(buf, pb_ref[0], pv_ref, nx_ref, jnp.logical_and(is_lat, i > 0),
               jnp.logical_and(is_lat, i < ntile - 1), tt)
    half = B_CONV // 2
    acc = buf[SUB - half:SUB - half + tt, :] * cw_ref[0:1, :]
    for j in range(1, B_CONV):
        acc = acc + buf[SUB - half + j:SUB - half + j + tt, :] * cw_ref[j:j + 1, :]
    act = acc * _sigmoid(acc)
    outs = []
    for h in range(2 * B_H):
        xh = act[:, h * B_HD:(h + 1) * B_HD]
        xh = xh * lax.rsqrt(jnp.sum(xh * xh, axis=-1, keepdims=True) + 1e-6)
        if h < B_H:
            xh = xh * (B_HD ** -0.5)
        outs.append(xh)
    outs.append(act[:, 2 * B_W:3 * B_W])
    qkv_ref[0] = jnp.concatenate(outs, axis=1)
    ab = ab_ref[0]
    lane = lax.broadcasted_iota(jnp.int32, ab.shape, 1)
    gval = -jnp.exp(alog_ref[...]) * _softplus(ab + dtb_ref[...])
    gb_ref[0] = jnp.where(lane < 2 * B_H, gval, _sigmoid(ab))


def _gdn_prep(pb, zab, conv_w, alog_row, dtb_row, tt, lat_groups):
    g, t, wb = pb.shape
    tile = lambda width: pl.BlockSpec((1, tt, width), lambda gi, i: (gi, i, 0))
    full = lambda arr: pl.BlockSpec(arr.shape, lambda gi, i: (0,) * arr.ndim)
    prev, nxt = _halo_specs(wb, tt, t)
    ab_spec = pl.BlockSpec((1, tt, LANE), lambda gi, i: (gi, i, B_W // LANE))
    return pl.pallas_call(
        functools.partial(_gdn_prep_kernel, tt=tt, lat_groups=lat_groups, ntile=t // tt),
        grid=(g, t // tt),
        in_specs=[tile(wb), prev, nxt, ab_spec, full(conv_w), full(alog_row), full(dtb_row)],
        out_specs=[tile(wb), tile(LANE)],
        out_shape=[jax.ShapeDtypeStruct((g, t, wb), F32), jax.ShapeDtypeStruct((g, t, LANE), F32)],
        scratch_shapes=[pltpu.VMEM((tt + 2 * SUB, wb), F32)],
        compiler_params=_cparams(("parallel", "parallel")), name="gdn_prep",
    )(pb, pb, pb, zab, conv_w, alog_row, dtb_row)


def _gdn_scan_kernel(qkvf_ref, gbf_ref, qkvb_ref, gbb_ref, of_ref, ob_ref, s_ref, *, passes):
    @pl.when(pl.program_id(1) == 0)
    def _():
        s_ref[...] = jnp.zeros_like(s_ref)

    n_pair = B_H // 2
    s_all = s_ref[...]
    chains = _gdn_operands(qkvf_ref[0], gbf_ref[0], 0) + _gdn_operands(qkvb_ref[0], gbb_ref[0], 1)
    states = [(s_all[d, 2 * p], s_all[d, 2 * p + 1]) for d in range(2) for p in range(n_pair)]
    outs, s_new = _gdn_solve(chains, states, passes)
    of_ref[0] = jnp.concatenate([o for pair in outs[0:n_pair] for o in pair], axis=1)
    ob_ref[0] = jnp.concatenate([o for pair in outs[n_pair:2 * n_pair] for o in pair], axis=1)
    for i, pair in enumerate(s_new):
        for j, s_h in enumerate(pair):
            s_ref[i // n_pair, 2 * (i % n_pair) + j] = s_h


def _gdn_operands(qkv, gb, d):
    fwd = d == 0
    L = CHUNK
    L2 = 2 * L
    same, strict, incl = _order_masks(L2, L, fwd)
    _, _, incl_rev = _order_masks(L2, L, not fwd)
    incl_bf = jnp.where(incl, 1.0, 0.0).astype(BF16)
    incl_t_bf = jnp.where(incl_rev, 1.0, 0.0).astype(BF16)
    same_bf = jnp.where(same, 1.0, 0.0).astype(BF16)
    lane = lax.broadcasted_iota(jnp.int32, gb.shape, 1)

    def column(idx):
        return jnp.sum(jnp.where(lane == idx, gb, 0.0), axis=-1, keepdims=True)

    chains = []
    for pr in range(B_H // 2):
        heads = (2 * pr, 2 * pr + 1)
        stack = lambda off: jnp.concatenate([qkv[:, off + h * B_HD:off + (h + 1) * B_HD] for h in heads], axis=0)
        q2, k2, v2 = stack(0), stack(B_W), stack(2 * B_W)
        gcol = jnp.concatenate([column(d * B_H + h) for h in heads], axis=0)
        bcol = jnp.concatenate([column(2 * B_H + d * B_H + h) for h in heads], axis=0)
        chains.append(dict(q2=q2, k2=k2, v2=v2, bcol=bcol, gfull=jnp.broadcast_to(gcol, (L2, LANE)),
                           strict=strict, incl=incl, incl_bf=incl_bf, incl_t_bf=incl_t_bf, same_bf=same_bf))
    return chains


def _gdn_solve(chains, states, passes):
    L = CHUNK
    L2 = 2 * L
    each = lambda fn, *cols: [fn(*args) for args in zip(*cols)]
    gc_row = each(lambda c: _dot_mask(c["gfull"], c["incl_bf"], NN, True), chains)
    gc_col = each(lambda c: _dot_mask(c["gfull"], c["incl_t_bf"], TN, False), chains)
    g_tot = each(lambda c: _dot_mask(c["gfull"], c["same_bf"], NN, True), chains)
    decay = each(lambda c, r, cl: jnp.exp(jnp.where(c["incl"], r - cl, -1e30)), chains, gc_row, gc_col)
    egc = each(jnp.exp, gc_row)
    k2b = each(lambda c: c["k2"] * c["bcol"], chains)
    kq = each(lambda c, kb: _dot(jnp.concatenate([kb, c["q2"]], axis=0), c["k2"], NT, passes), chains, k2b)
    l_mat = each(lambda c, m, dc: jnp.where(c["strict"], m[0:L2] * dc, 0.0), chains, kq, decay)
    a_int = each(lambda c, m, dc: jnp.where(c["incl"], m[L2:2 * L2] * dc, 0.0), chains, kq, decay)
    x = each(lambda c, kb, eg: jnp.concatenate([c["v2"] * c["bcol"], kb * eg], axis=1), chains, k2b, egc)
    x = each(lambda n, xi: xi - _dot(n, xi, NN, passes), l_mat, x)
    npow = l_mat
    for _ in range(int(math.log2(L)) - 1):
        npow = each(lambda n: _dot(n, n, NN, passes), npow)
        x = each(lambda n, xi: xi + _dot(n, xi, NN, passes), npow, x)
    qe = each(lambda c, eg: c["q2"] * eg, chains, egc)
    rows = (slice(0, L), slice(L, L2))
    vnew = each(lambda xi, st: [xi[r, 0:B_HD] - _dot(xi[r, B_HD:2 * B_HD], s, NN, passes)
                                for r, s in zip(rows, st)], x, states)
    o1 = each(lambda qi, st: [_dot(qi[r], s, NN, passes) for r, s in zip(rows, st)], qe, states)
    o2 = each(lambda a, vn, o: jnp.concatenate(o, axis=0) + _dot(a, jnp.concatenate(vn, axis=0), NN, passes),
              a_int, vnew, o1)
    outs = [[o[r] for r in rows] for o in o2]
    k_rest = each(lambda c, gt, r: c["k2"] * jnp.exp(gt - r), chains, g_tot, gc_row)
    s_new = each(lambda st, gt, kr, vn: [s * jnp.exp(gt[r][0:1, :]) + _dot(kr[r], v, TN, passes)
                                         for r, s, v in zip(rows, st, vn)], states, g_tot, k_rest, vnew)
    return outs, s_new


def _gdn_scan(qkv, gb, bsz, ctx_len, passes):
    g, t, _ = qkv.shape
    ins_f, ins_b = _scan_specs([(3 * B_W, 0), (LANE, 0)], bsz, ctx_len, t)
    (out_f,), (out_b,) = _scan_specs([(B_W, 0)], bsz, ctx_len, t)
    return pl.pallas_call(
        functools.partial(_gdn_scan_kernel, passes=passes),
        grid=(bsz, (ctx_len + t) // CHUNK),
        in_specs=ins_f + ins_b,
        out_specs=[out_f, out_b],
        out_shape=[jax.ShapeDtypeStruct((g, t, B_W), F32)] * 2,
        scratch_shapes=[pltpu.VMEM((2, B_H, B_HD, B_HD), F32)],
        compiler_params=_cparams(("parallel", "arbitrary")), name="gdn_scan",
    )(qkv, gb, qkv, gb)


def _mix_out_kernel(yaf_ref, yab_ref, bg_ref, lnw_ref, lnb_ref, bd_ref, obf_ref, obb_ref, z_ref, nw_ref, o_ref):
    y = yaf_ref[0] + yab_ref[0]
    ones_bd = bd_ref[...]
    mean = _group_sum(y, ones_bd) * (1.0 / A_HD)
    yc = y - mean
    var = _group_sum(yc * yc, ones_bd) * (1.0 / A_HD)
    yn = yc * lax.rsqrt(var + A_LN_EPS) * lnw_ref[...] + lnb_ref[...]
    bg = bg_ref[0]
    out_a = (yn + bg[:, 0:A_W]) * bg[:, A_W:2 * A_W]
    o = obf_ref[0] + obb_ref[0]
    z = z_ref[0]
    outs = [out_a.astype(o_ref.dtype)]
    for h in range(B_H):
        oh = o[:, h * B_HD:(h + 1) * B_HD]
        on = oh * lax.rsqrt(jnp.mean(oh * oh, axis=-1, keepdims=True) + NORM_EPS) * nw_ref[...]
        zh = z[:, h * B_HD:(h + 1) * B_HD]
        outs.append((on * (zh * _sigmoid(zh))).astype(o_ref.dtype))
    o_ref[0] = jnp.concatenate(outs, axis=1)


def _mix_out(ya, bg, ln_w, ln_b, ones_bd, ob, zab, norm_w, tt):
    g, t, _ = bg.shape
    tile = lambda width: pl.BlockSpec((1, tt, width), lambda gi, i: (gi, i, 0))
    full = lambda arr: pl.BlockSpec(arr.shape, lambda gi, i: (0,) * arr.ndim)
    return pl.pallas_call(
        _mix_out_kernel,
        grid=(g, t // tt),
        in_specs=[tile(A_W), tile(A_W), tile(2 * A_W), full(ln_w), full(ln_b), full(ones_bd), tile(B_W),
                  tile(B_W), tile(B_W), full(norm_w)],
        out_specs=tile(A_W + B_W),
        out_shape=jax.ShapeDtypeStruct((g, t, A_W + B_W), BF16),
        compiler_params=_cparams(("parallel", "parallel")), name="mix_out",
    )(ya[0], ya[1], bg, ln_w, ln_b, ones_bd, ob[0], ob[1], zab, norm_w)


def _qkv_rope_kernel(x_ref, w_ref, cos_ref, sin_ref, o_ref, *, lat_groups, n_q_tiles, n_rope_tiles, tn):
    gi, j = pl.program_id(0), pl.program_id(2)
    acc = jnp.dot(x_ref[0], w_ref[...], preferred_element_type=F32)
    rope_on = jnp.logical_and(gi < lat_groups, j < n_rope_tiles)
    scale = jnp.where(j < n_q_tiles, C_HD ** -0.5, 1.0)
    cos = jnp.where(rope_on, cos_ref[...], 1.0) * scale
    sin = jnp.where(rope_on, sin_ref[...], 0.0) * scale
    lane = lax.broadcasted_iota(jnp.int32, (1, LANE), 1)
    first = (lane % (C_HD // 2)) < (C_HD // 4)
    quarter = C_HD // 4
    outs = []
    for blk in range(tn // LANE):
        xb = acc[:, blk * LANE:(blk + 1) * LANE]
        partner = jnp.where(first, pltpu.roll(xb, LANE - quarter, axis=1), pltpu.roll(xb, quarter, axis=1))
        outs.append((xb * cos + partner * sin).astype(o_ref.dtype))
    o_ref[0] = jnp.concatenate(outs, axis=1)


def _qkv_rope(h, w, cos, sin, lat_groups, tm, tn, n_q_tiles, n_rope_tiles):
    g, t, kdim = h.shape
    n = w.shape[1]
    return pl.pallas_call(
        functools.partial(_qkv_rope_kernel, lat_groups=lat_groups, n_q_tiles=n_q_tiles,
                          n_rope_tiles=n_rope_tiles, tn=tn),
        grid=(g, t // tm, n // tn),
        in_specs=[pl.BlockSpec((1, tm, kdim), lambda gi, i, j: (gi, i, 0)),
                  pl.BlockSpec((kdim, tn), lambda gi, i, j: (0, j)),
                  pl.BlockSpec((tm, LANE), lambda gi, i, j: (i, 0)),
                  pl.BlockSpec((tm, LANE), lambda gi, i, j: (i, 0))],
        out_specs=pl.BlockSpec((1, tm, tn), lambda gi, i, j: (gi, i, j)),
        out_shape=jax.ShapeDtypeStruct((g, t, n), BF16),
        compiler_params=_cparams(("parallel", "parallel", "parallel")), name="qkv_rope",
    )(h, w, cos, sin)


def _attn_kernel(sink_ref, q_ref, kp_ref, km_ref, kn_ref, kc_ref, o_ref, *, n_blocks, kv_w):
    n = pl.program_id(1)
    blk = C_BLOCK
    ii = lax.broadcasted_iota(jnp.int32, (blk, blk), 0)
    jj = lax.broadcasted_iota(jnp.int32, (blk, blk), 1)
    ok_prev = jnp.logical_and(jj >= ii, n > 0)
    ok_next = jnp.logical_and(jj <= ii, n < n_blocks - 1)
    n_ctx = kc_ref.shape[1]
    valid1 = jnp.concatenate([ok_prev, jnp.full((blk, blk), True), ok_next, jnp.full((blk, n_ctx), True)], axis=1)
    valid = jnp.concatenate([valid1] * C_GROUP, axis=0)
    low = lax.broadcasted_iota(jnp.int32, (1, LANE), 1) < C_HD
    row_head = lax.broadcasted_iota(jnp.int32, (C_GROUP * blk, 1), 0) // blk
    q = q_ref[0]
    kv = jnp.concatenate([kp_ref[0], km_ref[0], kn_ref[0], kc_ref[0]], axis=0)
    zero = jnp.zeros((), q.dtype)
    outs = []
    for h in range(C_KVH):
        kh = kv[:, h * LANE:(h + 1) * LANE]
        vh = kv[:, kv_w + h * LANE:kv_w + (h + 1) * LANE]
        qa = q[:, (2 * h) * LANE:(2 * h + 1) * LANE]
        qb = q[:, (2 * h + 1) * LANE:(2 * h + 2) * LANE]
        qs = jnp.concatenate([jnp.where(low, qa, zero), jnp.where(low, zero, qa),
                              jnp.where(low, qb, zero), jnp.where(low, zero, qb)], axis=0)
        s = lax.dot_general(qs, kh, NT, preferred_element_type=F32)
        s = jnp.where(valid, s, -1e30)
        sk = jnp.full((C_GROUP * blk, 1), 0.0, F32)
        for gq in range(C_GROUP):
            sk = jnp.where(row_head == gq, sink_ref[h * C_GROUP + gq], sk)
        m = jnp.maximum(jnp.max(s, axis=-1, keepdims=True), sk)
        p = jnp.exp(s - m)
        denom = jnp.sum(p, axis=-1, keepdims=True) + jnp.exp(sk - m)
        o = lax.dot_general(p.astype(vh.dtype), vh, NN, preferred_element_type=F32) / denom
        outs.append(jnp.where(low, o[0:blk], o[blk:2 * blk]).astype(o_ref.dtype))
        outs.append(jnp.where(low, o[2 * blk:3 * blk], o[3 * blk:4 * blk]).astype(o_ref.dtype))
    o_ref[0] = jnp.concatenate(outs, axis=1)


def _attention(qkv, sink, bsz, ctx_len):
    g, t, _ = qkv.shape
    q_w = C_QH * C_HD
    kv_w = C_KVH * LANE
    nq = q_w // (2 * kv_w)
    assert q_w % (2 * kv_w) == 0
    n_blocks = t // C_BLOCK
    cpb = ctx_len // C_BLOCK
    assert ctx_len % C_BLOCK == 0
    kvspec = lambda fn: pl.BlockSpec((1, C_BLOCK, 2 * kv_w), fn)
    return pl.pallas_call(
        functools.partial(_attn_kernel, n_blocks=n_blocks, kv_w=kv_w),
        grid=(bsz, n_blocks),
        in_specs=[pl.BlockSpec(memory_space=pltpu.SMEM),
                  pl.BlockSpec((1, C_BLOCK, q_w), lambda b, n: (b, n, 0)),
                  kvspec(lambda b, n: (b, jnp.maximum(n - 1, 0), nq)),
                  kvspec(lambda b, n: (b, n, nq)),
                  kvspec(lambda b, n: (b, jnp.minimum(n + 1, n_blocks - 1), nq)),
                  pl.BlockSpec((1, ctx_len, 2 * kv_w), lambda b, n: (bsz, b, nq))],
        out_specs=pl.BlockSpec((1, C_BLOCK, q_w), lambda b, n: (b, n, 0)),
        out_shape=jax.ShapeDtypeStruct((bsz, t, q_w), BF16),
        compiler_params=_cparams(("parallel", "parallel")), name="window_attn",
    )(sink, qkv, qkv, qkv, qkv, qkv)


def _lane_cumsum(x):
    n = x.shape[-1]
    lane = lax.broadcasted_iota(jnp.int32, x.shape, x.ndim - 1)
    k = 1
    while k < n:
        x = x + jnp.where(lane >= k, pltpu.roll(x, k, axis=x.ndim - 1), 0)
        k *= 2
    return x


def _route_kernel(lg_ref, rank_ref, aff_ref, *, cap, slot_stride):
    lg = lg_ref[0]
    m = jnp.max(lg, axis=0, keepdims=True)
    e = jnp.exp(lg - m)
    z = jnp.sum(e, axis=0, keepdims=True)
    aff = e / z
    key = (lg - m) - jnp.log(z)
    count_ge = lambda v: jnp.sum(jnp.where(key >= v, 1, 0), axis=-1, keepdims=True)

    def body(_, carry):
        lo, hi = carry
        mid = 0.5 * (lo + hi)
        ok = count_ge(mid) >= cap
        return jnp.where(ok, mid, lo), jnp.where(ok, hi, mid)

    lo0 = jnp.min(key, axis=-1, keepdims=True)
    lo, hi = lax.fori_loop(0, ROUTE_BISECT, body, (lo0, jnp.ones_like(lo0)))
    thr, found = lo, jnp.zeros(lo.shape, jnp.int32)
    for _ in range(ROUTE_SNAP):
        v = jnp.max(jnp.where(key < hi, key, -3e38), axis=-1, keepdims=True)
        ok = jnp.where(count_ge(v) >= cap, 1, 0)
        thr = jnp.where(jnp.logical_and(found == 0, ok == 1), v, thr)
        hi = jnp.where(jnp.logical_or(found == 1, ok == 1), hi, v)
        found = jnp.maximum(found, ok)
    gt = key > thr
    eq = key == thr
    need = cap - jnp.sum(jnp.where(gt, 1, 0), axis=-1, keepdims=True)
    take_eq = jnp.logical_and(eq, _lane_cumsum(jnp.where(eq, 1, 0)) <= need)
    sel = jnp.logical_or(gt, take_eq)
    slot = _lane_cumsum(jnp.where(sel, 1, 0)) - 1 + pl.program_id(0) * slot_stride
    rank_ref[0] = jnp.where(sel, slot, -1)
    aff_ref[0] = aff


def _route(lg, cap, slot_stride):
    g, e, t = lg.shape
    spec = pl.BlockSpec((1, e, t), lambda gi: (gi, 0, 0))
    return pl.pallas_call(
        functools.partial(_route_kernel, cap=cap, slot_stride=slot_stride),
        grid=(g,), in_specs=[spec], out_specs=[spec, spec],
        out_shape=[jax.ShapeDtypeStruct((g, e, t), jnp.int32), jax.ShapeDtypeStruct((g, e, t), F32)],
        compiler_params=_cparams(("parallel",)), name="route",
    )(lg)


def _expert_kernel(h_ref, rank_ref, aff_ref, w1_ref, w3_ref, w2_ref, o_ref, *, cap):
    e = pl.program_id(1)

    @pl.when(e == 0)
    def _():
        o_ref[...] = jnp.zeros_like(o_ref)

    rank = rank_ref[0, 0]
    t = rank.shape[-1]
    hit = lax.broadcasted_iota(jnp.int32, (cap, t), 0) == rank
    pick = jnp.where(hit, 1.0, 0.0).astype(BF16)
    gate = jnp.sum(jnp.where(hit, aff_ref[0, 0], 0.0), axis=-1, keepdims=True)
    xe = jnp.dot(pick, h_ref[0], preferred_element_type=F32).astype(BF16)
    h1 = jnp.dot(xe, w1_ref[0], preferred_element_type=F32)
    h3 = jnp.dot(xe, w3_ref[0], preferred_element_type=F32)
    hid = (h1 * _sigmoid(h1) * h3).astype(BF16)
    ye = (jnp.dot(hid, w2_ref[0], preferred_element_type=F32) * gate).astype(BF16)
    o_ref[0] += lax.dot_general(pick, ye, TN, preferred_element_type=F32)


def _experts(h, rank, aff, w1, w3, w2, cap):
    g, t, _ = h.shape
    n_exp, _, f = w1.shape
    sel = pl.BlockSpec((1, 1, 1, t), lambda gi, e: (gi, e, 0, 0))
    return pl.pallas_call(
        functools.partial(_expert_kernel, cap=cap),
        grid=(g, n_exp),
        in_specs=[pl.BlockSpec((1, t, D), lambda gi, e: (gi, 0, 0)), sel, sel,
                  pl.BlockSpec((1, D, f), lambda gi, e: (e, 0, 0)),
                  pl.BlockSpec((1, D, f), lambda gi, e: (e, 0, 0)),
                  pl.BlockSpec((1, f, D), lambda gi, e: (e, 0, 0))],
        out_specs=pl.BlockSpec((1, t, D), lambda gi, e: (gi, 0, 0)),
        out_shape=jax.ShapeDtypeStruct((g, t, D), F32),
        compiler_params=_cparams(("parallel", "arbitrary")), name="experts",
    )(h, rank.reshape(g, n_exp, 1, t), aff.reshape(g, n_exp, 1, t), w1, w3, w2)


def _moe(h, lg, w1, w3, w2, bsz, ctx_len, lat_groups_only):
    g, t, _ = h.shape
    cap_lat = EC_CAP * t // N_EXP
    rank, aff = _route(lg[:bsz], cap_lat, 0)
    if not lat_groups_only:
        cap_ctx = EC_CAP * ctx_len // N_EXP
        lg_c = lg[bsz].reshape(N_EXP, bsz, ctx_len).transpose(1, 0, 2)
        rank_c, aff_c = _route(lg_c, cap_ctx, cap_ctx)
        back = lambda a: a.transpose(1, 0, 2).reshape(1, N_EXP, t)
        rank = jnp.concatenate([rank, back(rank_c)], axis=0)
        aff = jnp.concatenate([aff, back(aff_c)], axis=0)
        assert cap_ctx * bsz == cap_lat
    else:
        h = h[:bsz] if h.shape[0] != bsz else h
    return _experts(h, rank, aff, w1, w3, w2, cap_lat)


def _block_diag2(w):
    z = jnp.zeros_like(w[0])
    return jnp.concatenate([jnp.concatenate([w[0], z], axis=1), jnp.concatenate([z, w[1]], axis=1)], axis=0)


def _rope_tables(t):
    quarter = C_HD // 4
    inv = ROPE_BASE ** (-jnp.arange(quarter, dtype=F32) / quarter)
    pos = jnp.arange(t)
    row = (pos // GRID_W).astype(F32)
    col = (pos % GRID_W).astype(F32)
    lane = np.arange(LANE)
    use_col = ((lane % C_HD) >= C_HD // 2)
    ang = jnp.where(use_col[None, :], col[:, None], row[:, None]) * inv[lane % quarter][None, :]
    sign = np.where((lane % (C_HD // 2)) < quarter, -1.0, 1.0).astype(np.float32)
    return jnp.cos(ang), jnp.sin(ang) * sign[None, :]


def kernel(x, c, ctx, c_ctx, ada_w, ada_b, norm_mix, norm_ffn, e_w_in, e_w_out, a_mu, a_w0, a_w2, a_a0, a_a2, a_g2,
           a_k_k, a_k_a, a_r_k, a_ln_w, a_ln_b, b_conv, b_a_log, b_dt_bias, b_norm, o_w_in, o_w_out, o_sink,
           moe_router, moe_w1, moe_w3, moe_w2, final_norm):
    bsz, t, _ = x.shape
    ctx_len = ctx.shape[1]
    depth = ada_w.shape[0]
    assert bsz * ctx_len == t and ctx_len % C_BLOCK == 0 and t % C_BLOCK == 0
    tt = ctx_len
    passes = 3

    rows = 2 * SUB * ((bsz + 1 + 2 * SUB - 1) // (2 * SUB))
    cc = jnp.zeros((rows, D), F32).at[:bsz].set(c).at[bsz].set(c_ctx)
    mods = _adaln(cc, ada_w, ada_b)[:, :bsz + 1].reshape(depth, bsz + 1, N_MOD, D)
    mods = jnp.pad(mods, ((0, 0), (0, 0), (0, MOD_ROWS - N_MOD), (0, 0)))

    xs = jnp.concatenate([x, ctx.reshape(1, t, D)], axis=0)
    ones_bd = jnp.asarray(np.kron(np.eye(A_H), np.ones((A_HD, A_HD))), BF16)
    a_cols = e_w_in.shape[2] - (4 * B_W + 4 * B_H)
    cos_t, sin_t = _rope_tables(t)

    res = None
    for i in range(depth):
        j = i // 2
        mod = mods[i]
        x_new, h, _ = _norm(xs, norm_mix[i], mod, k=0, res=res, tt=tt)
        xs = xs if res is None else x_new
        if i % 2 == 0:
            w_in = e_w_in[j]
            w_a = w_in[:, :a_cols].astype(BF16)
            w_b = w_in[:, a_cols:a_cols + 3 * B_W].astype(BF16)
            w_ab = w_in[:, a_cols + 3 * B_W:a_cols + 3 * B_W + 4 * B_H]
            w_z = w_in[:, a_cols + 3 * B_W + 4 * B_H:]
            w_zab = jnp.concatenate([w_z, w_ab, jnp.zeros((D, LANE - 4 * B_H), F32)], axis=1).astype(BF16)
            pa = _mm(h, w_a, F32, tt, a_cols // 3)
            pb = _mm(h, w_b, F32, tt, B_W)
            zab = _mm(h, w_zab, F32, tt, B_W + LANE)
            prm = {
                "mu": a_mu[j].reshape(1, -1), "w0": a_w0[j].reshape(1, -1), "w2": _block_diag2(a_w2[j]),
                "a0": a_a0[j].reshape(1, -1), "a2": _block_diag2(a_a2[j]), "g2": a_g2[j],
                "k_k": a_k_k[j].reshape(1, -1), "k_a": jnp.tile(a_k_a[j].reshape(1, -1), (1, 2)),
                "r_k": a_r_k[j].reshape(1, -1), "ones_bd": ones_bd,
            }
            rvk, lw, kd, bd, bg = _rwkv_prep(pa, prm, tt, bsz)
            ya = _rwkv_scan(rvk, lw, kd, bd, bsz, ctx_len, passes)
            pad_row = lambda v: jnp.pad(v.reshape(1, -1), ((0, 0), (0, LANE - v.size)))
            qkv_c, gb = _gdn_prep(pb, zab, b_conv[j], pad_row(b_a_log[j]), pad_row(b_dt_bias[j]), tt, bsz)
            ob = _gdn_scan(qkv_c, gb, bsz, ctx_len, passes)
            mix = _mix_out(ya, bg, a_ln_w[j].reshape(1, -1), a_ln_b[j].reshape(1, -1), ones_bd, ob,
                           zab, b_norm[j].reshape(1, -1), tt)
            xs = _mm_res(mix, e_w_out[j].astype(BF16), xs, mod, 2, tt)
        else:
            w_in = o_w_in[j]
            q_w = C_QH * C_HD
            kv_cols = C_KVH * C_HD
            dup = lambda w: jnp.repeat(w.reshape(D, C_KVH, 1, C_HD), 2, axis=2).reshape(D, 2 * kv_cols)
            w_all = jnp.concatenate([w_in[:, :q_w], dup(w_in[:, q_w:q_w + kv_cols]), dup(w_in[:, q_w + kv_cols:])],
                                    axis=1).astype(BF16)
            tn = 2 * kv_cols
            qkv = _qkv_rope(h, w_all, cos_t, sin_t, bsz, tt, tn, q_w // tn, q_w // tn + 1)
            att = _attention(qkv, o_sink[j], bsz, ctx_len)
            x_lat = _mm_res(att, o_w_out[j].astype(BF16), xs, mod, 2, tt, groups=bsz)
            xs = x_lat if xs.shape[0] == bsz else jnp.concatenate([x_lat, xs[bsz:]], axis=0)
        last = i == depth - 1
        if last:
            xs = xs[:bsz]
            mod = mod[:bsz]
        _, hf, lg = _norm(xs, norm_ffn[i], mod, k=3, router_t=moe_router[i].T, tt=tt)
        delta = _moe(hf, lg, moe_w1[i].astype(BF16), moe_w3[i].astype(BF16), moe_w2[i].astype(BF16),
                     bsz, ctx_len, last)
        res = (delta, mod, 5)
    _, out, _ = _norm(xs, final_norm, None, res=res, tt=tt, out_dtype=F32)
    return out
```

```python
import functools
import math

import jax
import jax.numpy as jnp
import numpy as np
from jax import lax
from jax.experimental import pallas as pl
from jax.experimental.pallas import tpu as pltpu

F32 = jnp.float32
BF16 = jnp.bfloat16
HIGHEST = lax.Precision.HIGHEST

D = 1024
N_MOD = 6
NORM_EPS = 1e-6
A_HD, A_H, A_W = 64, 8, 512
A_LN_EPS = 64e-5
B_HD, B_H, B_W = 128, 4, 512
B_CONV = 5
CHUNK = 64
C_HD, C_QH, C_KVH, C_GROUP = 64, 16, 4, 4
C_BLOCK = 128
ROPE_BASE = 10000.0
GRID_W = 64
N_EXP = 16
EC_CAP = 2
ROUTE_BISECT = 40
ROUTE_SNAP = 3
RWKV_PASSES = {"quad": 1, "apply": 1, "solve": "b", "square": 1, "state": "a"}
GDN_PASSES = {"quad": 1, "apply": 1, "solve": "b", "square": 1, "state": 1}
MOD_ROWS = 8
LANE = 128
SUB = 8
VMEM_LIMIT = 56 * 1024 * 1024

NT = (((1,), (1,)), ((), ()))
NN = (((1,), (0,)), ((), ()))
TN = (((0,), (0,)), ((), ()))


def _cparams(sem):
    return pltpu.CompilerParams(dimension_semantics=sem, vmem_limit_bytes=VMEM_LIMIT)


def _sigmoid(x):
    return 1.0 / (1.0 + jnp.exp(-x))


def _softplus(x):
    return jnp.maximum(x, 0.0) + jnp.log(1.0 + jnp.exp(-jnp.abs(x)))


def _split2(x):
    hi = x.astype(BF16)
    lo = (x - hi.astype(F32)).astype(BF16)
    return hi, lo


def _dot(a, b, dims=NN, passes=1):
    dg = functools.partial(lax.dot_general, dimension_numbers=dims, preferred_element_type=F32)
    if passes == 1:
        return dg(a.astype(BF16), b.astype(BF16))
    if passes == "a":
        ah, al = _split2(a)
        bh = b.astype(BF16)
        return dg(ah, bh) + dg(al, bh)
    if passes == "b":
        bh, bl = _split2(b)
        ah = a.astype(BF16)
        return dg(ah, bh) + dg(ah, bl)
    ah, al = _split2(a)
    bh, bl = _split2(b)
    return dg(ah, bh) + (dg(ah, bl) + dg(al, bh))


def _dot_mask(x, mask_bf16, dims, mask_first):
    dg = functools.partial(lax.dot_general, dimension_numbers=dims, preferred_element_type=F32)
    h1 = x.astype(BF16)
    r1 = x - h1.astype(F32)
    h2 = r1.astype(BF16)
    h3 = (r1 - h2.astype(F32)).astype(BF16)
    if mask_first:
        return dg(mask_bf16, h1) + (dg(mask_bf16, h2) + dg(mask_bf16, h3))
    return dg(h1, mask_bf16) + (dg(h2, mask_bf16) + dg(h3, mask_bf16))


def _group_sum(x, ones_bd):
    hi, lo = _split2(x)
    dg = functools.partial(lax.dot_general, dimension_numbers=NN, preferred_element_type=F32)
    return dg(hi, ones_bd) + dg(lo, ones_bd)


def _adaln_kernel(c_ref, w_ref, b_ref, o_ref):
    c = c_ref[...]
    s = c * _sigmoid(c)
    o_ref[0] = jnp.dot(s, w_ref[0], precision=HIGHEST, preferred_element_type=F32) + b_ref[0]


def _adaln(cc, ada_w, ada_b):
    depth, _, n = ada_w.shape
    rows = cc.shape[0]
    tn = 768
    return pl.pallas_call(
        _adaln_kernel,
        grid=(depth, n // tn),
        in_specs=[pl.BlockSpec((rows, D), lambda i, j: (0, 0)),
                  pl.BlockSpec((1, D, tn), lambda i, j: (i, 0, j)),
                  pl.BlockSpec((1, 1, tn), lambda i, j: (i, 0, j))],
        out_specs=pl.BlockSpec((1, rows, tn), lambda i, j: (i, 0, j)),
        out_shape=jax.ShapeDtypeStruct((depth, rows, n), F32),
        compiler_params=_cparams(("parallel", "parallel")),
        name="adaln",
    )(cc, ada_w, ada_b.reshape(depth, 1, n))


def _norm_kernel(*refs, k, kres, has_res, has_mod, has_router):
    it = iter(refs)
    x_ref = next(it)
    d_ref = next(it) if has_res else None
    mres_ref = next(it) if has_res else None
    gain_ref = next(it)
    mod_ref = next(it) if has_mod else None
    rt_ref = next(it) if has_router else None
    xo_ref = next(it) if has_res else None
    h_ref = next(it)
    lg_ref = next(it) if has_router else None

    x = x_ref[0]
    if has_res:
        x = x + mres_ref[0, kres:kres + 1, :] * d_ref[0]
        xo_ref[0] = x
    xn = x * lax.rsqrt(jnp.mean(x * x, axis=-1, keepdims=True) + NORM_EPS) * gain_ref[...]
    if has_mod:
        xn = xn * (1.0 + mod_ref[0, k + 1:k + 2, :]) + mod_ref[0, k:k + 1, :]
    h_ref[0] = xn.astype(h_ref.dtype)
    if has_router:
        lg_ref[0] = lax.dot_general(rt_ref[...], xn, NT, precision=HIGHEST, preferred_element_type=F32)


def _norm(x, gain, mod=None, *, k=0, res=None, router_t=None, tt, out_dtype=BF16):
    g, t, _ = x.shape
    has_res, has_mod, has_router = res is not None, mod is not None, router_t is not None
    kres = res[2] if has_res else 0
    tile = pl.BlockSpec((1, tt, D), lambda gi, i: (gi, i, 0))
    modspec = pl.BlockSpec((1, MOD_ROWS, D), lambda gi, i: (gi, 0, 0))
    ins, specs = [x], [tile]
    if has_res:
        ins += [res[0], res[1]]
        specs += [tile, modspec]
    ins.append(gain.reshape(1, D))
    specs.append(pl.BlockSpec((1, D), lambda gi, i: (0, 0)))
    if has_mod:
        ins.append(mod)
        specs.append(modspec)
    if has_router:
        ins.append(router_t)
        specs.append(pl.BlockSpec((N_EXP, D), lambda gi, i: (0, 0)))
    outs, ospecs = [], []
    if has_res:
        outs.append(jax.ShapeDtypeStruct((g, t, D), F32))
        ospecs.append(tile)
    outs.append(jax.ShapeDtypeStruct((g, t, D), out_dtype))
    ospecs.append(tile)
    if has_router:
        outs.append(jax.ShapeDtypeStruct((g, N_EXP, t), F32))
        ospecs.append(pl.BlockSpec((1, N_EXP, tt), lambda gi, i: (gi, 0, i)))
    res_out = list(pl.pallas_call(
        functools.partial(_norm_kernel, k=k, kres=kres, has_res=has_res, has_mod=has_mod, has_router=has_router),
        grid=(g, t // tt), in_specs=specs, out_specs=ospecs, out_shape=outs,
        compiler_params=_cparams(("parallel", "parallel")), name="norm_mod",
    )(*ins))
    x_new = res_out.pop(0) if has_res else None
    h = res_out.pop(0)
    lg = res_out.pop(0) if has_router else None
    return x_new, h, lg


def _mm_split_kernel(x_ref, w_ref, *o_refs, widths):
    x = x_ref[0]
    start = 0
    for o_ref, width in zip(o_refs, widths):
        o_ref[0] = jnp.dot(x, w_ref[:, start:start + width], preferred_element_type=F32).astype(o_ref.dtype)
        start += width


def _mm_split(x, w, widths, out_dtype, tm):
    g, t, kdim = x.shape
    assert sum(widths) == w.shape[1] and all(wd % LANE == 0 for wd in widths)
    return pl.pallas_call(
        functools.partial(_mm_split_kernel, widths=widths),
        grid=(g, t // tm),
        in_specs=[pl.BlockSpec((1, tm, kdim), lambda gi, i: (gi, i, 0)),
                  pl.BlockSpec(w.shape, lambda gi, i: (0, 0))],
        out_specs=[pl.BlockSpec((1, tm, wd), lambda gi, i: (gi, i, 0)) for wd in widths],
        out_shape=[jax.ShapeDtypeStruct((g, t, wd), out_dtype) for wd in widths],
        compiler_params=_cparams(("parallel", "parallel")), name="proj",
    )(x, w)


def _mm_res_kernel(y_ref, w_ref, x_ref, mod_ref, o_ref, *, k):
    acc = jnp.dot(y_ref[0], w_ref[...], preferred_element_type=F32)
    o_ref[0] = x_ref[0] + mod_ref[0, k:k + 1, :] * acc


def _mm_res(y, w, x, mod, k, tm, groups=None):
    g_all, t, kdim = y.shape
    g = g_all if groups is None else groups
    return pl.pallas_call(
        functools.partial(_mm_res_kernel, k=k),
        grid=(g, t // tm),
        in_specs=[pl.BlockSpec((1, tm, kdim), lambda gi, i: (gi, i, 0)),
                  pl.BlockSpec((kdim, D), lambda gi, i: (0, 0)),
                  pl.BlockSpec((1, tm, D), lambda gi, i: (gi, i, 0)),
                  pl.BlockSpec((1, MOD_ROWS, D), lambda gi, i: (gi, 0, 0))],
        out_specs=pl.BlockSpec((1, tm, D), lambda gi, i: (gi, i, 0)),
        out_shape=jax.ShapeDtypeStruct((g, t, D), F32),
        compiler_params=_cparams(("parallel", "parallel")), name="out_proj_res",
    )(y, w, x, mod)


def _fill_halo(buf, x, pv_ref, nx_ref, has_prev, has_next, tt):
    buf[0:SUB, :] = jnp.where(has_prev, pv_ref[0], 0.0)
    buf[SUB:SUB + tt, :] = x
    buf[SUB + tt:2 * SUB + tt, :] = jnp.where(has_next, nx_ref[0], 0.0)


def _rwkv_prep_kernel(pa_ref, pv_ref, nx_ref, mu_ref, w0_ref, w2_ref, a0_ref, a2_ref, g2_ref, kk_ref, ka_ref,
                      rk_ref, bd_ref, rvk_ref, lw_ref, kd_ref, bdir_ref, bg_ref, buf, *, tt, lat_groups, ntile):
    gi, i = pl.program_id(0), pl.program_id(1)
    is_lat = gi < lat_groups
    x = pa_ref[0]
    _fill_halo(buf, x, pv_ref, nx_ref, jnp.logical_and(is_lat, i > 0), jnp.logical_and(is_lat, i < ntile - 1), tt)
    xm = buf[SUB - 1:SUB - 1 + tt, :]
    xp = buf[SUB + 1:SUB + 1 + tt, :]
    x = x + mu_ref[...] * (0.5 * (xm + xp) - x)
    w = A_W
    r, kx, v = x[:, 0:w], x[:, w:2 * w], x[:, 2 * w:3 * w]
    wd, ad, gd = x[:, 3 * w:3 * w + LANE], x[:, 3 * w + LANE:3 * w + 2 * LANE], x[:, 3 * w + 2 * LANE:3 * w + 3 * LANE]
    w_log = -_softplus(-(w0_ref[...] + _dot(jnp.tanh(wd), w2_ref[...], NN, 3))) - 0.5
    lw = -jnp.exp(w_log)
    a = _sigmoid(a0_ref[...] + _dot(ad, a2_ref[...], NN, 3))
    gate = _dot(_sigmoid(gd), g2_ref[...], NN, 3)
    ones_bd = bd_ref[...]
    kq = kx * kk_ref[...]
    kk = kq * lax.rsqrt(_group_sum(kq * kq, ones_bd) + 1e-6)
    k2 = jnp.concatenate([kx, kx], axis=1)
    kdir = k2 * (1.0 + (a - 1.0) * ka_ref[...])
    bdir = jnp.concatenate([kk, kk], axis=1) * a
    rr = r * rk_ref[...]
    bonus = _group_sum(rr * (kdir[:, 0:w] + kdir[:, w:2 * w]), ones_bd) * v
    rvk_ref[0] = jnp.concatenate([r, v, kk], axis=1)
    lw_ref[0] = lw
    kd_ref[0] = kdir
    bdir_ref[0] = bdir
    bg_ref[0] = jnp.concatenate([bonus, gate], axis=1)


def _halo_specs(width, tt, t):
    nb = tt // SUB
    last = t // SUB - 1
    prev = pl.BlockSpec((1, SUB, width), lambda gi, i: (gi, jnp.maximum(i * nb - 1, 0), 0))
    nxt = pl.BlockSpec((1, SUB, width), lambda gi, i: (gi, jnp.minimum((i + 1) * nb, last), 0))
    return prev, nxt


def _rwkv_prep(pa, p, tt, lat_groups):
    g, t, wa = pa.shape
    full = lambda arr: pl.BlockSpec(arr.shape, lambda gi, i: (0,) * arr.ndim)
    tile = lambda width: pl.BlockSpec((1, tt, width), lambda gi, i: (gi, i, 0))
    prev, nxt = _halo_specs(wa, tt, t)
    consts = [p["mu"], p["w0"], p["w2"], p["a0"], p["a2"], p["g2"], p["k_k"], p["k_a"], p["r_k"], p["ones_bd"]]
    widths = (3 * A_W, 2 * A_W, 2 * A_W, 2 * A_W, 2 * A_W)
    return pl.pallas_call(
        functools.partial(_rwkv_prep_kernel, tt=tt, lat_groups=lat_groups, ntile=t // tt),
        grid=(g, t // tt),
        in_specs=[tile(wa), prev, nxt] + [full(c) for c in consts],
        out_specs=[tile(wd) for wd in widths],
        out_shape=[jax.ShapeDtypeStruct((g, t, wd), F32) for wd in widths],
        scratch_shapes=[pltpu.VMEM((tt + 2 * SUB, wa), F32)],
        compiler_params=_cparams(("parallel", "parallel")), name="rwkv_prep",
    )(pa, pa, pa, *consts)


def _order_masks(n_rows, blk, fwd):
    ii = lax.broadcasted_iota(jnp.int32, (n_rows, n_rows), 0)
    jj = lax.broadcasted_iota(jnp.int32, (n_rows, n_rows), 1)
    same = (ii // blk) == (jj // blk)
    before = (jj < ii) if fwd else (jj > ii)
    strict = jnp.logical_and(same, before)
    incl = jnp.logical_and(same, jnp.logical_or(before, ii == jj))
    return same, strict, incl


def _rwkv_scan_kernel(rvkf_ref, lwf_ref, kdf_ref, bdf_ref, rvkb_ref, lwb_ref, kdb_ref, bdb_ref, yf_ref, yb_ref,
                      s_ref, *, passes):
    @pl.when(pl.program_id(1) == 0)
    def _():
        s_ref[...] = jnp.zeros_like(s_ref)

    n_pair = A_W // LANE
    s_all = s_ref[...]
    chains = (_rwkv_operands(rvkf_ref[0], lwf_ref[0], kdf_ref[0], bdf_ref[0], True)
              + _rwkv_operands(rvkb_ref[0], lwb_ref[0], kdb_ref[0], bdb_ref[0], False))
    states = [s_all[d, p] for d in range(2) for p in range(n_pair)]
    ys, s_new = _rwkv_solve(chains, states, passes)
    yf_ref[0] = jnp.concatenate(ys[0:n_pair], axis=1)
    yb_ref[0] = jnp.concatenate(ys[n_pair:2 * n_pair], axis=1)
    for i, s_i in enumerate(s_new):
        s_ref[i // n_pair, i % n_pair] = s_i


def _rwkv_operands(rvk, lw, kd, bd, fwd):
    L = CHUNK
    L2 = 2 * L
    _, _, incl_l = _order_masks(L, L, fwd)
    _, strict, incl = _order_masks(L2, L, fwd)
    cum = jnp.dot(jnp.where(incl_l, 1.0, 0.0), lw, precision=HIGHEST, preferred_element_type=F32)
    tot = jnp.sum(lw, axis=0, keepdims=True)
    g_inv = jnp.exp(-cum)
    g_rest = jnp.exp(tot - cum)
    g_tot = jnp.exp(tot)
    w = A_W
    r_t = rvk[:, 0:w] * jnp.exp(cum)
    v = rvk[:, w:2 * w]
    a_t = rvk[:, 2 * w:3 * w] * jnp.exp(cum - lw)
    b_t, k_t = bd * g_inv, kd * g_inv
    b_g, k_g = bd * g_rest, kd * g_rest
    low = lax.broadcasted_iota(jnp.int32, (1, LANE), 1) < A_HD

    def expand(x):
        return jnp.concatenate([jnp.where(low, x, 0.0), jnp.where(low, 0.0, x)], axis=0)

    chains = []
    for p in range(A_W // LANE):
        sl = slice(p * LANE, (p + 1) * LANE)
        chains.append(dict(
            ev=expand(v[:, sl]),
            lhs=jnp.concatenate([expand(a_t[:, sl]), expand(r_t[:, sl])], axis=0),
            rhs=jnp.concatenate([expand(b_t[:, sl]), expand(k_t[:, sl])], axis=0),
            bk=jnp.concatenate([expand(b_g[:, sl]), expand(k_g[:, sl])], axis=0),
            g_tot=g_tot[:, sl], strict=strict, incl=incl))
    return chains


def _rwkv_solve(chains, states, passes):
    L = CHUNK
    L2 = 2 * L
    each = lambda fn, *cols: [fn(*args) for args in zip(*cols)]
    quad = each(lambda c: _dot(c["lhs"], c["rhs"], NT, passes["quad"]), chains)
    es = each(lambda c, s: _dot(c["lhs"], s, NT, passes["state"]), chains, states)
    n_ab = each(lambda c, q: jnp.where(c["strict"], q[0:L2, 0:L2], 0.0), chains, quad)
    x = each(lambda c, q, e: e[0:L2] + _dot(jnp.where(c["strict"], q[0:L2, L2:2 * L2], 0.0), c["ev"], NN,
                                            passes["apply"]), chains, quad, es)
    x = each(lambda n, xi: xi - _dot(n, xi, NN, passes["solve"]), n_ab, x)
    npow = n_ab
    for _ in range(int(math.log2(L)) - 1):
        npow = each(lambda n: _dot(n, n, NN, passes["square"]), npow)
        x = each(lambda n, xi: xi + _dot(n, xi, NN, passes["solve"]), npow, x)
    pv = each(lambda c, xi: jnp.concatenate([-xi, c["ev"]], axis=0), chains, x)
    y2 = each(lambda c, q, e, pvi: e[L2:2 * L2] + _dot(
        jnp.concatenate([jnp.where(c["incl"], q[L2:2 * L2, 0:L2], 0.0),
                         jnp.where(c["incl"], q[L2:2 * L2, L2:2 * L2], 0.0)], axis=1), pvi, NN, passes["apply"]),
        chains, quad, es, pv)
    ys = [y[0:L] + y[L:L2] for y in y2]
    s_new = each(lambda c, s, pvi: s * c["g_tot"] + _dot(pvi, c["bk"], TN, passes["state"]), chains, states, pv)
    return ys, s_new


def _chunk_index(b, c, fwd, n_ctx_chunk, n_lat_chunk, lat_groups):
    in_ctx = c < n_ctx_chunk
    cc = c if fwd else n_ctx_chunk - 1 - c
    lc = c - n_ctx_chunk if fwd else n_lat_chunk - 1 - (c - n_ctx_chunk)
    grp = jnp.where(in_ctx, lat_groups, b)
    chunk = jnp.where(in_ctx, b * n_ctx_chunk + cc, lc)
    return grp, chunk


def _scan_specs(widths, bsz, ctx_len, t):
    ncc, nlc = ctx_len // CHUNK, t // CHUNK

    def spec(width, lane_blk, fwd):
        def index(b, c):
            grp, ch = _chunk_index(b, c, fwd, ncc, nlc, bsz)
            return grp, ch, lane_blk
        return pl.BlockSpec((1, CHUNK, width), index)

    return [[spec(w, (0 if fwd else 1) if blk is None else blk, fwd) for w, blk in widths] for fwd in (True, False)]


def _rwkv_scan(rvk, lw, kd, bd, bsz, ctx_len, passes):
    g, t, _ = rvk.shape
    ins_f, ins_b = _scan_specs([(3 * A_W, 0), (A_W, None), (A_W, None), (A_W, None)], bsz, ctx_len, t)
    (out_f,), (out_b,) = _scan_specs([(A_W, 0)], bsz, ctx_len, t)
    return pl.pallas_call(
        functools.partial(_rwkv_scan_kernel, passes=passes),
        grid=(bsz, (ctx_len + t) // CHUNK),
        in_specs=ins_f + ins_b,
        out_specs=[out_f, out_b],
        out_shape=[jax.ShapeDtypeStruct((g, t, A_W), F32)] * 2,
        scratch_shapes=[pltpu.VMEM((2, A_W // LANE, LANE, LANE), F32)],
        compiler_params=_cparams(("parallel", "arbitrary")), name="rwkv_scan",
    )(rvk, lw, kd, bd, rvk, lw, kd, bd)


def _gdn_prep_kernel(pb_ref, pv_ref, nx_ref, ab_ref, cw_ref, alog_ref, dtb_ref, qkv_ref, gb_ref, buf,
                     *, tt, lat_groups, ntile):
    gi, i = pl.program_id(0), pl.program_id(1)
    is_lat = gi < lat_groups
    _fill_halo(buf, pb_ref[0], pv_ref, nx_ref, jnp.logical_and(is_lat, i > 0),
               jnp.logical_and(is_lat, i < ntile - 1), tt)
    half = B_CONV // 2
    acc = buf[SUB - half:SUB - half + tt, :] * cw_ref[0:1, :]
    for j in range(1, B_CONV):
        acc = acc + buf[SUB - half + j:SUB - half + j + tt, :] * cw_ref[j:j + 1, :]
    act = acc * _sigmoid(acc)
    outs = []
    for h in range(2 * B_H):
        xh = act[:, h * B_HD:(h + 1) * B_HD]
        xh = xh * lax.rsqrt(jnp.sum(xh * xh, axis=-1, keepdims=True) + 1e-6)
        if h < B_H:
            xh = xh * (B_HD ** -0.5)
        outs.append(xh)
    outs.append(act[:, 2 * B_W:3 * B_W])
    qkv_ref[0] = jnp.concatenate(outs, axis=1)
    ab = ab_ref[0]
    lane = lax.broadcasted_iota(jnp.int32, ab.shape, 1)
    gval = -jnp.exp(alog_ref[...]) * _softplus(ab + dtb_ref[...])
    gb_ref[0] = jnp.where(lane < 2 * B_H, gval, _sigmoid(ab))


def _gdn_prep(pb, zab, conv_w, alog_row, dtb_row, tt, lat_groups):
    g, t, wb = pb.shape
    tile = lambda width: pl.BlockSpec((1, tt, width), lambda gi, i: (gi, i, 0))
    full = lambda arr: pl.BlockSpec(arr.shape, lambda gi, i: (0,) * arr.ndim)
    prev, nxt = _halo_specs(wb, tt, t)
    ab_spec = pl.BlockSpec((1, tt, LANE), lambda gi, i: (gi, i, B_W // LANE))
    return pl.pallas_call(
        functools.partial(_gdn_prep_kernel, tt=tt, lat_groups=lat_groups, ntile=t // tt),
        grid=(g, t // tt),
        in_specs=[tile(wb), prev, nxt, ab_spec, full(conv_w), full(alog_row), full(dtb_row)],
        out_specs=[tile(wb), tile(LANE)],
        out_shape=[jax.ShapeDtypeStruct((g, t, wb), F32), jax.ShapeDtypeStruct((g, t, LANE), F32)],
        scratch_shapes=[pltpu.VMEM((tt + 2 * SUB, wb), F32)],
        compiler_params=_cparams(("parallel", "parallel")), name="gdn_prep",
    )(pb, pb, pb, zab, conv_w, alog_row, dtb_row)


def _gdn_scan_kernel(qkvf_ref, gbf_ref, qkvb_ref, gbb_ref, of_ref, ob_ref, s_ref, *, passes):
    @pl.when(pl.program_id(1) == 0)
    def _():
        s_ref[...] = jnp.zeros_like(s_ref)

    n_pair = B_H // 2
    s_all = s_ref[...]
    chains = _gdn_operands(qkvf_ref[0], gbf_ref[0], 0) + _gdn_operands(qkvb_ref[0], gbb_ref[0], 1)
    states = [(s_all[d, 2 * p], s_all[d, 2 * p + 1]) for d in range(2) for p in range(n_pair)]
    outs, s_new = _gdn_solve(chains, states, passes)
    of_ref[0] = jnp.concatenate([o for pair in outs[0:n_pair] for o in pair], axis=1)
    ob_ref[0] = jnp.concatenate([o for pair in outs[n_pair:2 * n_pair] for o in pair], axis=1)
    for i, pair in enumerate(s_new):
        for j, s_h in enumerate(pair):
            s_ref[i // n_pair, 2 * (i % n_pair) + j] = s_h


def _gdn_operands(qkv, gb, d):
    fwd = d == 0
    L = CHUNK
    L2 = 2 * L
    same, strict, incl = _order_masks(L2, L, fwd)
    _, _, incl_rev = _order_masks(L2, L, not fwd)
    incl_bf = jnp.where(incl, 1.0, 0.0).astype(BF16)
    incl_t_bf = jnp.where(incl_rev, 1.0, 0.0).astype(BF16)
    same_bf = jnp.where(same, 1.0, 0.0).astype(BF16)
    lane = lax.broadcasted_iota(jnp.int32, gb.shape, 1)

    def column(idx):
        return jnp.sum(jnp.where(lane == idx, gb, 0.0), axis=-1, keepdims=True)

    chains = []
    for pr in range(B_H // 2):
        heads = (2 * pr, 2 * pr + 1)
        stack = lambda off: jnp.concatenate([qkv[:, off + h * B_HD:off + (h + 1) * B_HD] for h in heads], axis=0)
        q2, k2, v2 = stack(0), stack(B_W), stack(2 * B_W)
        gcol = jnp.concatenate([column(d * B_H + h) for h in heads], axis=0)
        bcol = jnp.concatenate([column(2 * B_H + d * B_H + h) for h in heads], axis=0)
        chains.append(dict(q2=q2, k2=k2, v2=v2, bcol=bcol, gfull=jnp.broadcast_to(gcol, (L2, LANE)),
                           strict=strict, incl=incl, incl_bf=incl_bf, incl_t_bf=incl_t_bf, same_bf=same_bf))
    return chains


def _gdn_solve(chains, states, passes):
    L = CHUNK
    L2 = 2 * L
    each = lambda fn, *cols: [fn(*args) for args in zip(*cols)]
    gc_row = each(lambda c: _dot_mask(c["gfull"], c["incl_bf"], NN, True), chains)
    gc_col = each(lambda c: _dot_mask(c["gfull"], c["incl_t_bf"], TN, False), chains)
    g_tot = each(lambda c: _dot_mask(c["gfull"], c["same_bf"], NN, True), chains)
    decay = each(lambda c, r, cl: jnp.exp(jnp.where(c["incl"], r - cl, -1e30)), chains, gc_row, gc_col)
    egc = each(jnp.exp, gc_row)
    k2b = each(lambda c: c["k2"] * c["bcol"], chains)
    kq = each(lambda c, kb: _dot(jnp.concatenate([kb, c["q2"]], axis=0), c["k2"], NT, passes["quad"]), chains, k2b)
    l_mat = each(lambda c, m, dc: jnp.where(c["strict"], m[0:L2] * dc, 0.0), chains, kq, decay)
    a_int = each(lambda c, m, dc: jnp.where(c["incl"], m[L2:2 * L2] * dc, 0.0), chains, kq, decay)
    x = each(lambda c, kb, eg: jnp.concatenate([c["v2"] * c["bcol"], kb * eg], axis=1), chains, k2b, egc)
    x = each(lambda n, xi: xi - _dot(n, xi, NN, passes["solve"]), l_mat, x)
    npow = l_mat
    for _ in range(int(math.log2(L)) - 1):
        npow = each(lambda n: _dot(n, n, NN, passes["square"]), npow)
        x = each(lambda n, xi: xi + _dot(n, xi, NN, passes["solve"]), npow, x)
    qe = each(lambda c, eg: c["q2"] * eg, chains, egc)
    rows = (slice(0, L), slice(L, L2))
    vnew = each(lambda xi, st: [xi[r, 0:B_HD] - _dot(xi[r, B_HD:2 * B_HD], s, NN, passes["state"])
                                for r, s in zip(rows, st)], x, states)
    o1 = each(lambda qi, st: [_dot(qi[r], s, NN, passes["state"]) for r, s in zip(rows, st)], qe, states)
    o2 = each(lambda a, vn, o: jnp.concatenate(o, axis=0) + _dot(a, jnp.concatenate(vn, axis=0), NN,
                                                                 passes["apply"]), a_int, vnew, o1)
    outs = [[o[r] for r in rows] for o in o2]
    k_rest = each(lambda c, gt, r: c["k2"] * jnp.exp(gt - r), chains, g_tot, gc_row)
    s_new = each(lambda st, gt, kr, vn: [s * jnp.exp(gt[r][0:1, :]) + _dot(kr[r], v, TN, passes["state"])
                                         for r, s, v in zip(rows, st, vn)], states, g_tot, k_rest, vnew)
    return outs, s_new


def _gdn_scan(qkv, gb, bsz, ctx_len, passes):
    g, t, _ = qkv.shape
    ins_f, ins_b = _scan_specs([(3 * B_W, 0), (LANE, 0)], bsz, ctx_len, t)
    (out_f,), (out_b,) = _scan_specs([(B_W, 0)], bsz, ctx_len, t)
    return pl.pallas_call(
        functools.partial(_gdn_scan_kernel, passes=passes),
        grid=(bsz, (ctx_len + t) // CHUNK),
        in_specs=ins_f + ins_b,
        out_specs=[out_f, out_b],
        out_shape=[jax.ShapeDtypeStruct((g, t, B_W), F32)] * 2,
        scratch_shapes=[pltpu.VMEM((2, B_H, B_HD, B_HD), F32)],
        compiler_params=_cparams(("parallel", "arbitrary")), name="gdn_scan",
    )(qkv, gb, qkv, gb)


def _mix_out_kernel(yaf_ref, yab_ref, bg_ref, lnw_ref, lnb_ref, bd_ref, obf_ref, obb_ref, z_ref, nw_ref, o_ref):
    y = yaf_ref[0] + yab_ref[0]
    ones_bd = bd_ref[...]
    mean = _group_sum(y, ones_bd) * (1.0 / A_HD)
    yc = y - mean
    var = _group_sum(yc * yc, ones_bd) * (1.0 / A_HD)
    yn = yc * lax.rsqrt(var + A_LN_EPS) * lnw_ref[...] + lnb_ref[...]
    bg = bg_ref[0]
    out_a = (yn + bg[:, 0:A_W]) * bg[:, A_W:2 * A_W]
    o = obf_ref[0] + obb_ref[0]
    z = z_ref[0]
    outs = [out_a.astype(o_ref.dtype)]
    for h in range(B_H):
        oh = o[:, h * B_HD:(h + 1) * B_HD]
        on = oh * lax.rsqrt(jnp.mean(oh * oh, axis=-1, keepdims=True) + NORM_EPS) * nw_ref[...]
        zh = z[:, h * B_HD:(h + 1) * B_HD]
        outs.append((on * (zh * _sigmoid(zh))).astype(o_ref.dtype))
    o_ref[0] = jnp.concatenate(outs, axis=1)


def _mix_out(ya, bg, ln_w, ln_b, ones_bd, ob, zab, norm_w, tt):
    g, t, _ = bg.shape
    tile = lambda width: pl.BlockSpec((1, tt, width), lambda gi, i: (gi, i, 0))
    full = lambda arr: pl.BlockSpec(arr.shape, lambda gi, i: (0,) * arr.ndim)
    return pl.pallas_call(
        _mix_out_kernel,
        grid=(g, t // tt),
        in_specs=[tile(A_W), tile(A_W), tile(2 * A_W), full(ln_w), full(ln_b), full(ones_bd), tile(B_W),
                  tile(B_W), tile(B_W), full(norm_w)],
        out_specs=tile(A_W + B_W),
        out_shape=jax.ShapeDtypeStruct((g, t, A_W + B_W), BF16),
        compiler_params=_cparams(("parallel", "parallel")), name="mix_out",
    )(ya[0], ya[1], bg, ln_w, ln_b, ones_bd, ob[0], ob[1], zab, norm_w)


def _qkv_rope_kernel(x_ref, w_ref, cos_ref, sin_ref, o_ref, *, lat_groups, q_cols, rope_cols, seg):
    is_lat = pl.program_id(0) < lat_groups
    x = x_ref[0]
    lane = lax.broadcasted_iota(jnp.int32, (1, LANE), 1)
    first = (lane % (C_HD // 2)) < (C_HD // 4)
    quarter = C_HD // 4
    cos_l = jnp.where(is_lat, cos_ref[...], 1.0)
    sin_l = jnp.where(is_lat, sin_ref[...], 0.0)
    for start in range(0, w_ref.shape[1], seg):
        acc = jnp.dot(x, w_ref[:, start:start + seg], preferred_element_type=F32)
        if start >= rope_cols:
            o_ref[0, :, start:start + seg] = acc.astype(o_ref.dtype)
            continue
        scale = C_HD ** -0.5 if start < q_cols else 1.0
        cos, sin = cos_l * scale, sin_l * scale
        for blk in range(seg // LANE):
            xb = acc[:, blk * LANE:(blk + 1) * LANE]
            partner = jnp.where(first, pltpu.roll(xb, LANE - quarter, axis=1), pltpu.roll(xb, quarter, axis=1))
            col = start + blk * LANE
            o_ref[0, :, col:col + LANE] = (xb * cos + partner * sin).astype(o_ref.dtype)


def _qkv_rope(h, w, cos, sin, lat_groups, tm, seg, q_cols, rope_cols):
    g, t, kdim = h.shape
    n = w.shape[1]
    assert q_cols % seg == 0 and rope_cols % seg == 0 and n % seg == 0
    return pl.pallas_call(
        functools.partial(_qkv_rope_kernel, lat_groups=lat_groups, q_cols=q_cols, rope_cols=rope_cols, seg=seg),
        grid=(g, t // tm),
        in_specs=[pl.BlockSpec((1, tm, kdim), lambda gi, i: (gi, i, 0)),
                  pl.BlockSpec((kdim, n), lambda gi, i: (0, 0)),
                  pl.BlockSpec((tm, LANE), lambda gi, i: (i, 0)),
                  pl.BlockSpec((tm, LANE), lambda gi, i: (i, 0))],
        out_specs=pl.BlockSpec((1, tm, n), lambda gi, i: (gi, i, 0)),
        out_shape=jax.ShapeDtypeStruct((g, t, n), BF16),
        compiler_params=_cparams(("parallel", "parallel")), name="qkv_rope",
    )(h, w, cos, sin)


def _attn_kernel(sink_ref, q_ref, kp_ref, km_ref, kn_ref, kc_ref, o_ref, *, n_blocks, kv_w):
    n = pl.program_id(1)
    blk = C_BLOCK
    ii = lax.broadcasted_iota(jnp.int32, (blk, blk), 0)
    jj = lax.broadcasted_iota(jnp.int32, (blk, blk), 1)
    ok_prev = jnp.logical_and(jj >= ii, n > 0)
    ok_next = jnp.logical_and(jj <= ii, n < n_blocks - 1)
    n_ctx = kc_ref.shape[1]
    valid1 = jnp.concatenate([ok_prev, jnp.full((blk, blk), True), ok_next, jnp.full((blk, n_ctx), True)], axis=1)
    valid = jnp.concatenate([valid1] * C_GROUP, axis=0)
    low = lax.broadcasted_iota(jnp.int32, (1, LANE), 1) < C_HD
    row_head = lax.broadcasted_iota(jnp.int32, (C_GROUP * blk, 1), 0) // blk
    q = q_ref[0]
    kv = jnp.concatenate([kp_ref[0], km_ref[0], kn_ref[0], kc_ref[0]], axis=0)
    zero = jnp.zeros((), q.dtype)
    outs = []
    for h in range(C_KVH):
        kh = kv[:, h * LANE:(h + 1) * LANE]
        vh = kv[:, kv_w + h * LANE:kv_w + (h + 1) * LANE]
        qa = q[:, (2 * h) * LANE:(2 * h + 1) * LANE]
        qb = q[:, (2 * h + 1) * LANE:(2 * h + 2) * LANE]
        qs = jnp.concatenate([jnp.where(low, qa, zero), jnp.where(low, zero, qa),
                              jnp.where(low, qb, zero), jnp.where(low, zero, qb)], axis=0)
        s = lax.dot_general(qs, kh, NT, preferred_element_type=F32)
        s = jnp.where(valid, s, -1e30)
        sk = jnp.full((C_GROUP * blk, 1), 0.0, F32)
        for gq in range(C_GROUP):
            sk = jnp.where(row_head == gq, sink_ref[h * C_GROUP + gq], sk)
        m = jnp.maximum(jnp.max(s, axis=-1, keepdims=True), sk)
        p = jnp.exp(s - m)
        denom = jnp.sum(p, axis=-1, keepdims=True) + jnp.exp(sk - m)
        o = lax.dot_general(p.astype(vh.dtype), vh, NN, preferred_element_type=F32) / denom
        outs.append(jnp.where(low, o[0:blk], o[blk:2 * blk]).astype(o_ref.dtype))
        outs.append(jnp.where(low, o[2 * blk:3 * blk], o[3 * blk:4 * blk]).astype(o_ref.dtype))
    o_ref[0] = jnp.concatenate(outs, axis=1)


def _attention(qkv, sink, bsz, ctx_len):
    g, t, _ = qkv.shape
    q_w = C_QH * C_HD
    kv_w = C_KVH * LANE
    nq = q_w // (2 * kv_w)
    assert q_w % (2 * kv_w) == 0
    n_blocks = t // C_BLOCK
    cpb = ctx_len // C_BLOCK
    assert ctx_len % C_BLOCK == 0
    kvspec = lambda fn: pl.BlockSpec((1, C_BLOCK, 2 * kv_w), fn)
    return pl.pallas_call(
        functools.partial(_attn_kernel, n_blocks=n_blocks, kv_w=kv_w),
        grid=(bsz, n_blocks),
        in_specs=[pl.BlockSpec(memory_space=pltpu.SMEM),
                  pl.BlockSpec((1, C_BLOCK, q_w), lambda b, n: (b, n, 0)),
                  kvspec(lambda b, n: (b, jnp.maximum(n - 1, 0), nq)),
                  kvspec(lambda b, n: (b, n, nq)),
                  kvspec(lambda b, n: (b, jnp.minimum(n + 1, n_blocks - 1), nq)),
                  pl.BlockSpec((1, ctx_len, 2 * kv_w), lambda b, n: (bsz, b, nq))],
        out_specs=pl.BlockSpec((1, C_BLOCK, q_w), lambda b, n: (b, n, 0)),
        out_shape=jax.ShapeDtypeStruct((bsz, t, q_w), BF16),
        compiler_params=_cparams(("parallel", "parallel")), name="window_attn",
    )(sink, qkv, qkv, qkv, qkv, qkv)


def _lane_cumsum(x):
    n = x.shape[-1]
    lane = lax.broadcasted_iota(jnp.int32, x.shape, x.ndim - 1)
    k = 1
    while k < n:
        x = x + jnp.where(lane >= k, pltpu.roll(x, k, axis=x.ndim - 1), 0)
        k *= 2
    return x


def _route_kernel(lg_ref, rank_ref, aff_ref, *, cap, slot_stride):
    lg = lg_ref[0]
    m = jnp.max(lg, axis=0, keepdims=True)
    e = jnp.exp(lg - m)
    z = jnp.sum(e, axis=0, keepdims=True)
    aff = e / z
    key = (lg - m) - jnp.log(z)
    count_ge = lambda v: jnp.sum(jnp.where(key >= v, 1, 0), axis=-1, keepdims=True)

    def body(_, carry):
        lo, hi = carry
        mid = 0.5 * (lo + hi)
        ok = count_ge(mid) >= cap
        return jnp.where(ok, mid, lo), jnp.where(ok, hi, mid)

    lo0 = jnp.min(key, axis=-1, keepdims=True)
    lo, hi = lax.fori_loop(0, ROUTE_BISECT, body, (lo0, jnp.ones_like(lo0)))
    thr, found = lo, jnp.zeros(lo.shape, jnp.int32)
    for _ in range(ROUTE_SNAP):
        v = jnp.max(jnp.where(key < hi, key, -3e38), axis=-1, keepdims=True)
        ok = jnp.where(count_ge(v) >= cap, 1, 0)
        thr = jnp.where(jnp.logical_and(found == 0, ok == 1), v, thr)
        hi = jnp.where(jnp.logical_or(found == 1, ok == 1), hi, v)
        found = jnp.maximum(found, ok)
    gt = key > thr
    eq = key == thr
    need = cap - jnp.sum(jnp.where(gt, 1, 0), axis=-1, keepdims=True)
    take_eq = jnp.logical_and(eq, _lane_cumsum(jnp.where(eq, 1, 0)) <= need)
    sel = jnp.logical_or(gt, take_eq)
    slot = _lane_cumsum(jnp.where(sel, 1, 0)) - 1 + pl.program_id(0) * slot_stride
    rank_ref[0] = jnp.where(sel, slot, -1)
    aff_ref[0] = aff


def _route(lg, cap, slot_stride):
    g, e, t = lg.shape
    spec = pl.BlockSpec((1, e, t), lambda gi: (gi, 0, 0))
    return pl.pallas_call(
        functools.partial(_route_kernel, cap=cap, slot_stride=slot_stride),
        grid=(g,), in_specs=[spec], out_specs=[spec, spec],
        out_shape=[jax.ShapeDtypeStruct((g, e, t), jnp.int32), jax.ShapeDtypeStruct((g, e, t), F32)],
        compiler_params=_cparams(("parallel",)), name="route",
    )(lg)


def _expert_kernel(h_ref, rank_ref, aff_ref, w1_ref, w3_ref, w2_ref, o_ref, *, cap):
    e = pl.program_id(1)

    @pl.when(e == 0)
    def _():
        o_ref[...] = jnp.zeros_like(o_ref)

    rank = rank_ref[0, 0]
    t = rank.shape[-1]
    hit = lax.broadcasted_iota(jnp.int32, (cap, t), 0) == rank
    pick = jnp.where(hit, 1.0, 0.0).astype(BF16)
    gate = jnp.sum(jnp.where(hit, aff_ref[0, 0], 0.0), axis=-1, keepdims=True)
    xe = jnp.dot(pick, h_ref[0], preferred_element_type=F32).astype(BF16)
    h1 = jnp.dot(xe, w1_ref[0], preferred_element_type=F32)
    h3 = jnp.dot(xe, w3_ref[0], preferred_element_type=F32)
    hid = (h1 * _sigmoid(h1) * h3).astype(BF16)
    ye = (jnp.dot(hid, w2_ref[0], preferred_element_type=F32) * gate).astype(BF16)
    o_ref[0] += lax.dot_general(pick, ye, TN, preferred_element_type=F32)


def _experts(h, rank, aff, w1, w3, w2, cap):
    g, t, _ = h.shape
    n_exp, _, f = w1.shape
    sel = pl.BlockSpec((1, 1, 1, t), lambda gi, e: (gi, e, 0, 0))
    return pl.pallas_call(
        functools.partial(_expert_kernel, cap=cap),
        grid=(g, n_exp),
        in_specs=[pl.BlockSpec((1, t, D), lambda gi, e: (gi, 0, 0)), sel, sel,
                  pl.BlockSpec((1, D, f), lambda gi, e: (e, 0, 0)),
                  pl.BlockSpec((1, D, f), lambda gi, e: (e, 0, 0)),
                  pl.BlockSpec((1, f, D), lambda gi, e: (e, 0, 0))],
        out_specs=pl.BlockSpec((1, t, D), lambda gi, e: (gi, 0, 0)),
        out_shape=jax.ShapeDtypeStruct((g, t, D), F32),
        compiler_params=_cparams(("parallel", "arbitrary")), name="experts",
    )(h, rank.reshape(g, n_exp, 1, t), aff.reshape(g, n_exp, 1, t), w1, w3, w2)


def _moe(h, lg, w1, w3, w2, bsz, ctx_len, lat_groups_only):
    g, t, _ = h.shape
    cap_lat = EC_CAP * t // N_EXP
    rank, aff = _route(lg[:bsz], cap_lat, 0)
    if not lat_groups_only:
        cap_ctx = EC_CAP * ctx_len // N_EXP
        lg_c = lg[bsz].reshape(N_EXP, bsz, ctx_len).transpose(1, 0, 2)
        rank_c, aff_c = _route(lg_c, cap_ctx, cap_ctx)
        back = lambda a: a.transpose(1, 0, 2).reshape(1, N_EXP, t)
        rank = jnp.concatenate([rank, back(rank_c)], axis=0)
        aff = jnp.concatenate([aff, back(aff_c)], axis=0)
        assert cap_ctx * bsz == cap_lat
    else:
        h = h[:bsz] if h.shape[0] != bsz else h
    return _experts(h, rank, aff, w1, w3, w2, cap_lat)


def _block_diag2(w):
    z = jnp.zeros_like(w[0])
    return jnp.concatenate([jnp.concatenate([w[0], z], axis=1), jnp.concatenate([z, w[1]], axis=1)], axis=0)


def _rope_tables(t):
    quarter = C_HD // 4
    inv = ROPE_BASE ** (-jnp.arange(quarter, dtype=F32) / quarter)
    pos = jnp.arange(t)
    row = (pos // GRID_W).astype(F32)
    col = (pos % GRID_W).astype(F32)
    lane = np.arange(LANE)
    use_col = ((lane % C_HD) >= C_HD // 2)
    ang = jnp.where(use_col[None, :], col[:, None], row[:, None]) * inv[lane % quarter][None, :]
    sign = np.where((lane % (C_HD // 2)) < quarter, -1.0, 1.0).astype(np.float32)
    return jnp.cos(ang), jnp.sin(ang) * sign[None, :]


def kernel(x, c, ctx, c_ctx, ada_w, ada_b, norm_mix, norm_ffn, e_w_in, e_w_out, a_mu, a_w0, a_w2, a_a0, a_a2, a_g2,
           a_k_k, a_k_a, a_r_k, a_ln_w, a_ln_b, b_conv, b_a_log, b_dt_bias, b_norm, o_w_in, o_w_out, o_sink,
           moe_router, moe_w1, moe_w3, moe_w2, final_norm):
    bsz, t, _ = x.shape
    ctx_len = ctx.shape[1]
    depth = ada_w.shape[0]
    assert bsz * ctx_len == t and ctx_len % C_BLOCK == 0 and t % C_BLOCK == 0
    tt = ctx_len

    rows = 2 * SUB * ((bsz + 1 + 2 * SUB - 1) // (2 * SUB))
    cc = jnp.zeros((rows, D), F32).at[:bsz].set(c).at[bsz].set(c_ctx)
    mods = _adaln(cc, ada_w, ada_b)[:, :bsz + 1].reshape(depth, bsz + 1, N_MOD, D)
    mods = jnp.pad(mods, ((0, 0), (0, 0), (0, MOD_ROWS - N_MOD), (0, 0)))

    xs = jnp.concatenate([x, ctx.reshape(1, t, D)], axis=0)
    ones_bd = jnp.asarray(np.kron(np.eye(A_H), np.ones((A_HD, A_HD))), BF16)
    a_cols = e_w_in.shape[2] - (4 * B_W + 4 * B_H)
    cos_t, sin_t = _rope_tables(t)

    res = None
    for i in range(depth):
        j = i // 2
        mod = mods[i]
        x_new, h, _ = _norm(xs, norm_mix[i], mod, k=0, res=res, tt=tt)
        xs = xs if res is None else x_new
        if i % 2 == 0:
            w_in = e_w_in[j]
            split_ab = a_cols + 3 * B_W
            w_all = jnp.concatenate([w_in[:, :split_ab], w_in[:, split_ab + 4 * B_H:],
                                     w_in[:, split_ab:split_ab + 4 * B_H],
                                     jnp.zeros((D, LANE - 4 * B_H), F32)], axis=1).astype(BF16)
            pa, pb, zab = _mm_split(h, w_all, (a_cols, 3 * B_W, B_W + LANE), F32, 2 * tt)
            prm = {
                "mu": a_mu[j].reshape(1, -1), "w0": a_w0[j].reshape(1, -1), "w2": _block_diag2(a_w2[j]),
                "a0": a_a0[j].reshape(1, -1), "a2": _block_diag2(a_a2[j]), "g2": a_g2[j],
                "k_k": a_k_k[j].reshape(1, -1), "k_a": jnp.tile(a_k_a[j].reshape(1, -1), (1, 2)),
                "r_k": a_r_k[j].reshape(1, -1), "ones_bd": ones_bd,
            }
            rvk, lw, kd, bd, bg = _rwkv_prep(pa, prm, tt, bsz)
            ya = _rwkv_scan(rvk, lw, kd, bd, bsz, ctx_len, RWKV_PASSES)
            pad_row = lambda v: jnp.pad(v.reshape(1, -1), ((0, 0), (0, LANE - v.size)))
            qkv_c, gb = _gdn_prep(pb, zab, b_conv[j], pad_row(b_a_log[j]), pad_row(b_dt_bias[j]), tt, bsz)
            ob = _gdn_scan(qkv_c, gb, bsz, ctx_len, GDN_PASSES)
            mix = _mix_out(ya, bg, a_ln_w[j].reshape(1, -1), a_ln_b[j].reshape(1, -1), ones_bd, ob,
                           zab, b_norm[j].reshape(1, -1), tt)
            xs = _mm_res(mix, e_w_out[j].astype(BF16), xs, mod, 2, tt)
        else:
            w_in = o_w_in[j]
            q_w = C_QH * C_HD
            kv_cols = C_KVH * C_HD
            dup = lambda w: jnp.repeat(w.reshape(D, C_KVH, 1, C_HD), 2, axis=2).reshape(D, 2 * kv_cols)
            w_all = jnp.concatenate([w_in[:, :q_w], dup(w_in[:, q_w:q_w + kv_cols]), dup(w_in[:, q_w + kv_cols:])],
                                    axis=1).astype(BF16)
            tn = 2 * kv_cols
            qkv = _qkv_rope(h, w_all, cos_t, sin_t, bsz, 2 * tt, tn, q_w, q_w + tn)
            att = _attention(qkv, o_sink[j], bsz, ctx_len)
            x_lat = _mm_res(att, o_w_out[j].astype(BF16), xs, mod, 2, tt, groups=bsz)
            xs = x_lat if xs.shape[0] == bsz else jnp.concatenate([x_lat, xs[bsz:]], axis=0)
        last = i == depth - 1
        if last:
            xs = xs[:bsz]
            mod = mod[:bsz]
        _, hf, lg = _norm(xs, norm_ffn[i], mod, k=3, router_t=moe_router[i].T, tt=tt)
        delta = _moe(hf, lg, moe_w1[i].astype(BF16), moe_w3[i].astype(BF16), moe_w2[i].astype(BF16),
                     bsz, ctx_len, last)
        res = (delta, mod, 5)
    _, out, _ = _norm(xs, final_norm, None, res=res, tt=tt, out_dtype=F32)
    return out
```

```python
import functools
import math

import jax
import jax.numpy as jnp
import numpy as np
from jax import lax
from jax.experimental import pallas as pl
from jax.experimental.pallas import tpu as pltpu

F32 = jnp.float32
BF16 = jnp.bfloat16
HIGHEST = lax.Precision.HIGHEST

D = 1024
N_MOD = 6
NORM_EPS = 1e-6
A_HD, A_H, A_W = 64, 8, 512
A_LN_EPS = 64e-5
B_HD, B_H, B_W = 128, 4, 512
B_CONV = 5
CHUNK = 64
C_HD, C_QH, C_KVH, C_GROUP = 64, 16, 4, 4
C_BLOCK = 128
ROPE_BASE = 10000.0
GRID_W = 64
N_EXP = 16
EC_CAP = 2
ROUTE_BISECT = 40
ROUTE_SNAP = 3
RWKV_PASSES = {"quad": 1, "apply": 1, "solve": "b", "square": 1, "state_in": "a", "state_out": "b"}
GDN_PASSES = {"quad": 1, "apply": 1, "solve": "b", "square": 1, "state": 1}
MOD_ROWS = 8
LANE = 128
SUB = 8
VMEM_LIMIT = 56 * 1024 * 1024

NT = (((1,), (1,)), ((), ()))
NN = (((1,), (0,)), ((), ()))
TN = (((0,), (0,)), ((), ()))


def _cparams(sem):
    return pltpu.CompilerParams(dimension_semantics=sem, vmem_limit_bytes=VMEM_LIMIT)


def _sigmoid(x):
    return 1.0 / (1.0 + jnp.exp(-x))


def _softplus(x):
    return jnp.maximum(x, 0.0) + jnp.log(1.0 + jnp.exp(-jnp.abs(x)))


def _split2(x):
    hi = x.astype(BF16)
    lo = (x - hi.astype(F32)).astype(BF16)
    return hi, lo


def _dot(a, b, dims=NN, passes=1):
    dg = functools.partial(lax.dot_general, dimension_numbers=dims, preferred_element_type=F32)
    if passes == 1:
        return dg(a.astype(BF16), b.astype(BF16))
    (ka,), (kb,) = dims[0]
    if passes == "a":
        ah, al = _split2(a)
        bh = b.astype(BF16)
        return dg(jnp.concatenate([ah, al], axis=ka), jnp.concatenate([bh, bh], axis=kb))
    if passes == "b":
        bh, bl = _split2(b)
        n = b.shape[1 - kb]
        res = dg(a.astype(BF16), jnp.concatenate([bh, bl], axis=1 - kb))
        return res[:, 0:n] + res[:, n:2 * n]
    ah, al = _split2(a)
    bh, bl = _split2(b)
    return dg(ah, bh) + (dg(ah, bl) + dg(al, bh))


def _group_sum(x, ones_bd):
    hi, lo = _split2(x)
    dg = functools.partial(lax.dot_general, dimension_numbers=NN, preferred_element_type=F32)
    return dg(hi, ones_bd) + dg(lo, ones_bd)


def _cast_kernel(w_ref, o_ref):
    o_ref[...] = w_ref[...].astype(o_ref.dtype)


def _to_bf16(w, layer):
    _, n, r, c = w.shape
    return pl.pallas_call(_cast_kernel, grid=(n,),
                          in_specs=[pl.BlockSpec((1, 1, r, c), lambda e: (layer, e, 0, 0))],
                          out_specs=pl.BlockSpec((1, 1, r, c), lambda e: (0, e, 0, 0)),
                          out_shape=jax.ShapeDtypeStruct((1, n, r, c), BF16),
                          compiler_params=_cparams(("parallel",)), name="cast_bf16")(w)[0]


def _adaln_kernel(c_ref, w_ref, b_ref, o_ref):
    c = c_ref[...]
    s = c * _sigmoid(c)
    o_ref[0] = jnp.dot(s, w_ref[0], precision=HIGHEST, preferred_element_type=F32) + b_ref[0]


def _adaln(cc, ada_w, ada_b):
    depth, _, n = ada_w.shape
    rows = cc.shape[0]
    tn = 768
    return pl.pallas_call(
        _adaln_kernel,
        grid=(depth, n // tn),
        in_specs=[pl.BlockSpec((rows, D), lambda i, j: (0, 0)),
                  pl.BlockSpec((1, D, tn), lambda i, j: (i, 0, j)),
                  pl.BlockSpec((1, 1, tn), lambda i, j: (i, 0, j))],
        out_specs=pl.BlockSpec((1, rows, tn), lambda i, j: (i, 0, j)),
        out_shape=jax.ShapeDtypeStruct((depth, rows, n), F32),
        compiler_params=_cparams(("parallel", "parallel")),
        name="adaln",
    )(cc, ada_w, ada_b.reshape(depth, 1, n))


def _norm_kernel(*refs, k, kres, has_res, has_mod, has_router):
    it = iter(refs)
    x_ref = next(it)
    d_ref = next(it) if has_res else None
    mres_ref = next(it) if has_res else None
    gain_ref = next(it)
    mod_ref = next(it) if has_mod else None
    rt_ref = next(it) if has_router else None
    xo_ref = next(it) if has_res else None
    h_ref = next(it)
    lg_ref = next(it) if has_router else None

    x = x_ref[0]
    if has_res:
        x = x + mres_ref[0, kres:kres + 1, :] * d_ref[0]
        xo_ref[0] = x
    xn = x * lax.rsqrt(jnp.mean(x * x, axis=-1, keepdims=True) + NORM_EPS) * gain_ref[...]
    if has_mod:
        xn = xn * (1.0 + mod_ref[0, k + 1:k + 2, :]) + mod_ref[0, k:k + 1, :]
    h_ref[0] = xn.astype(h_ref.dtype)
    if has_router:
        lg_ref[0] = lax.dot_general(rt_ref[...], xn, NT, precision=HIGHEST, preferred_element_type=F32)


def _norm(x, gain, mod=None, *, k=0, res=None, router_t=None, tt, out_dtype=BF16):
    g, t, _ = x.shape
    has_res, has_mod, has_router = res is not None, mod is not None, router_t is not None
    kres = res[2] if has_res else 0
    tile = pl.BlockSpec((1, tt, D), lambda gi, i: (gi, i, 0))
    modspec = pl.BlockSpec((1, MOD_ROWS, D), lambda gi, i: (gi, 0, 0))
    ins, specs = [x], [tile]
    if has_res:
        ins += [res[0], res[1]]
        specs += [tile, modspec]
    ins.append(gain.reshape(1, D))
    specs.append(pl.BlockSpec((1, D), lambda gi, i: (0, 0)))
    if has_mod:
        ins.append(mod)
        specs.append(modspec)
    if has_router:
        ins.append(router_t)
        specs.append(pl.BlockSpec((N_EXP, D), lambda gi, i: (0, 0)))
    outs, ospecs = [], []
    if has_res:
        outs.append(jax.ShapeDtypeStruct((g, t, D), F32))
        ospecs.append(tile)
    outs.append(jax.ShapeDtypeStruct((g, t, D), out_dtype))
    ospecs.append(tile)
    if has_router:
        outs.append(jax.ShapeDtypeStruct((g, N_EXP, t), F32))
        ospecs.append(pl.BlockSpec((1, N_EXP, tt), lambda gi, i: (gi, 0, i)))
    res_out = list(pl.pallas_call(
        functools.partial(_norm_kernel, k=k, kres=kres, has_res=has_res, has_mod=has_mod, has_router=has_router),
        grid=(g, t // tt), in_specs=specs, out_specs=ospecs, out_shape=outs,
        compiler_params=_cparams(("parallel", "parallel")), name="norm_mod",
    )(*ins))
    x_new = res_out.pop(0) if has_res else None
    h = res_out.pop(0)
    lg = res_out.pop(0) if has_router else None
    return x_new, h, lg


def _mm_split_kernel(x_ref, w_ref, *o_refs, widths):
    x = x_ref[0]
    start = 0
    for o_ref, width in zip(o_refs, widths):
        o_ref[0] = jnp.dot(x, w_ref[:, start:start + width], preferred_element_type=F32).astype(o_ref.dtype)
        start += width


def _mm_split(x, w, widths, out_dtype, tm):
    g, t, kdim = x.shape
    assert sum(widths) == w.shape[1] and all(wd % LANE == 0 for wd in widths)
    return pl.pallas_call(
        functools.partial(_mm_split_kernel, widths=widths),
        grid=(g, t // tm),
        in_specs=[pl.BlockSpec((1, tm, kdim), lambda gi, i: (gi, i, 0)),
                  pl.BlockSpec(w.shape, lambda gi, i: (0, 0))],
        out_specs=[pl.BlockSpec((1, tm, wd), lambda gi, i: (gi, i, 0)) for wd in widths],
        out_shape=[jax.ShapeDtypeStruct((g, t, wd), out_dtype) for wd in widths],
        compiler_params=_cparams(("parallel", "parallel")), name="proj",
    )(x, w)


def _mm_res_kernel(y_ref, w_ref, x_ref, mod_ref, o_ref, *, k):
    acc = jnp.dot(y_ref[0], w_ref[...], preferred_element_type=F32)
    o_ref[0] = x_ref[0] + mod_ref[0, k:k + 1, :] * acc


def _mm_res(y, w, x, mod, k, tm, groups=None):
    g_all, t, kdim = y.shape
    g = g_all if groups is None else groups
    return pl.pallas_call(
        functools.partial(_mm_res_kernel, k=k),
        grid=(g, t // tm),
        in_specs=[pl.BlockSpec((1, tm, kdim), lambda gi, i: (gi, i, 0)),
                  pl.BlockSpec((kdim, D), lambda gi, i: (0, 0)),
                  pl.BlockSpec((1, tm, D), lambda gi, i: (gi, i, 0)),
                  pl.BlockSpec((1, MOD_ROWS, D), lambda gi, i: (gi, 0, 0))],
        out_specs=pl.BlockSpec((1, tm, D), lambda gi, i: (gi, i, 0)),
        out_shape=jax.ShapeDtypeStruct((g, t, D), F32),
        compiler_params=_cparams(("parallel", "parallel")), name="out_proj_res",
    )(y, w, x, mod)


def _fill_halo(buf, x, pv_ref, nx_ref, has_prev, has_next, tt):
    buf[0:SUB, :] = jnp.where(has_prev, pv_ref[0], 0.0)
    buf[SUB:SUB + tt, :] = x
    buf[SUB + tt:2 * SUB + tt, :] = jnp.where(has_next, nx_ref[0], 0.0)


def _rwkv_prep_kernel(pa_ref, pv_ref, nx_ref, mu_ref, w0_ref, w2_ref, a0_ref, a2_ref, g2_ref, kk_ref, ka_ref,
                      rk_ref, bd_ref, rvk_ref, lw_ref, kd_ref, bdir_ref, bg_ref, buf, *, tt, lat_groups, ntile):
    gi, i = pl.program_id(0), pl.program_id(1)
    is_lat = gi < lat_groups
    x = pa_ref[0]
    _fill_halo(buf, x, pv_ref, nx_ref, jnp.logical_and(is_lat, i > 0), jnp.logical_and(is_lat, i < ntile - 1), tt)
    xm = buf[SUB - 1:SUB - 1 + tt, :]
    xp = buf[SUB + 1:SUB + 1 + tt, :]
    x = x + mu_ref[...] * (0.5 * (xm + xp) - x)
    w = A_W
    r, kx, v = x[:, 0:w], x[:, w:2 * w], x[:, 2 * w:3 * w]
    wd, ad, gd = x[:, 3 * w:3 * w + LANE], x[:, 3 * w + LANE:3 * w + 2 * LANE], x[:, 3 * w + 2 * LANE:3 * w + 3 * LANE]
    w_log = -_softplus(-(w0_ref[...] + _dot(jnp.tanh(wd), w2_ref[...], NN, 3))) - 0.5
    lw = -jnp.exp(w_log)
    a = _sigmoid(a0_ref[...] + _dot(ad, a2_ref[...], NN, 3))
    gate = _dot(_sigmoid(gd), g2_ref[...], NN, 3)
    ones_bd = bd_ref[...]
    kq = kx * kk_ref[...]
    kk = kq * lax.rsqrt(_group_sum(kq * kq, ones_bd) + 1e-6)
    k2 = jnp.concatenate([kx, kx], axis=1)
    kdir = k2 * (1.0 + (a - 1.0) * ka_ref[...])
    bdir = jnp.concatenate([kk, kk], axis=1) * a
    rr = r * rk_ref[...]
    bonus = _group_sum(rr * (kdir[:, 0:w] + kdir[:, w:2 * w]), ones_bd) * v
    rvk_ref[0] = jnp.concatenate([r, v, kk], axis=1)
    lw_ref[0] = lw
    kd_ref[0] = kdir
    bdir_ref[0] = bdir
    bg_ref[0] = jnp.concatenate([bonus, gate], axis=1)


def _halo_specs(width, tt, t):
    nb = tt // SUB
    last = t // SUB - 1
    prev = pl.BlockSpec((1, SUB, width), lambda gi, i: (gi, jnp.maximum(i * nb - 1, 0), 0))
    nxt = pl.BlockSpec((1, SUB, width), lambda gi, i: (gi, jnp.minimum((i + 1) * nb, last), 0))
    return prev, nxt


def _rwkv_prep(pa, p, tt, lat_groups):
    g, t, wa = pa.shape
    full = lambda arr: pl.BlockSpec(arr.shape, lambda gi, i: (0,) * arr.ndim)
    tile = lambda width: pl.BlockSpec((1, tt, width), lambda gi, i: (gi, i, 0))
    prev, nxt = _halo_specs(wa, tt, t)
    consts = [p["mu"], p["w0"], p["w2"], p["a0"], p["a2"], p["g2"], p["k_k"], p["k_a"], p["r_k"], p["ones_bd"]]
    widths = (3 * A_W, 2 * A_W, 2 * A_W, 2 * A_W, 2 * A_W)
    return pl.pallas_call(
        functools.partial(_rwkv_prep_kernel, tt=tt, lat_groups=lat_groups, ntile=t // tt),
        grid=(g, t // tt),
        in_specs=[tile(wa), prev, nxt] + [full(c) for c in consts],
        out_specs=[tile(wd) for wd in widths],
        out_shape=[jax.ShapeDtypeStruct((g, t, wd), F32) for wd in widths],
        scratch_shapes=[pltpu.VMEM((tt + 2 * SUB, wa), F32)],
        compiler_params=_cparams(("parallel", "parallel")), name="rwkv_prep",
    )(pa, pa, pa, *consts)


def _order_masks(n_rows, blk, fwd):
    ii = lax.broadcasted_iota(jnp.int32, (n_rows, n_rows), 0)
    jj = lax.broadcasted_iota(jnp.int32, (n_rows, n_rows), 1)
    same = (ii // blk) == (jj // blk)
    before = (jj < ii) if fwd else (jj > ii)
    strict = jnp.logical_and(same, before)
    incl = jnp.logical_and(same, jnp.logical_or(before, ii == jj))
    return same, strict, incl


def _even_scan_kernel(rvkf_ref, lwf_ref, kdf_ref, bdf_ref, qkvf_ref, gbf_ref,
                      rvkb_ref, lwb_ref, kdb_ref, bdb_ref, qkvb_ref, gbb_ref,
                      yf_ref, of_ref, yb_ref, ob_ref, sa_ref, sb_ref):
    @pl.when(pl.program_id(1) == 0)
    def _():
        sa_ref[...] = jnp.zeros_like(sa_ref)
        sb_ref[...] = jnp.zeros_like(sb_ref)

    na, nb = A_W // LANE, B_H // 2
    sa, sb = sa_ref[...], sb_ref[...]
    chains_a = (_rwkv_operands(rvkf_ref[0], lwf_ref[0], kdf_ref[0], bdf_ref[0], True)
                + _rwkv_operands(rvkb_ref[0], lwb_ref[0], kdb_ref[0], bdb_ref[0], False))
    chains_b = _gdn_operands(qkvf_ref[0], gbf_ref[0], 0) + _gdn_operands(qkvb_ref[0], gbb_ref[0], 1)
    states_a = [sa[d, p] for d in range(2) for p in range(na)]
    states_b = [(sb[d, 2 * p], sb[d, 2 * p + 1]) for d in range(2) for p in range(nb)]
    res_a, res_b = {}, {}
    stages = [_rwkv_solve(chains_a, states_a, RWKV_PASSES, res_a), _gdn_solve(chains_b, states_b, GDN_PASSES, res_b)]
    while stages:
        stages = [s for s in stages if next(s, "done") != "done"]
    yf_ref[0] = jnp.concatenate(res_a["y"][0:na], axis=1)
    yb_ref[0] = jnp.concatenate(res_a["y"][na:2 * na], axis=1)
    of_ref[0] = jnp.concatenate([o for pair in res_b["o"][0:nb] for o in pair], axis=1)
    ob_ref[0] = jnp.concatenate([o for pair in res_b["o"][nb:2 * nb] for o in pair], axis=1)
    for i, s_i in enumerate(res_a["s"]):
        sa_ref[i // na, i % na] = s_i
    for i, pair in enumerate(res_b["s"]):
        for j, s_h in enumerate(pair):
            sb_ref[i // nb, 2 * (i % nb) + j] = s_h


def _rwkv_operands(rvk, lw, kd, bd, fwd):
    L = CHUNK
    L2 = 2 * L
    _, _, incl_l = _order_masks(L, L, fwd)
    _, strict, incl = _order_masks(L2, L, fwd)
    cum = jnp.dot(jnp.where(incl_l, 1.0, 0.0), lw, precision=HIGHEST, preferred_element_type=F32)
    tot = jnp.sum(lw, axis=0, keepdims=True)
    g_inv = jnp.exp(-cum)
    g_rest = jnp.exp(tot - cum)
    g_tot = jnp.exp(tot)
    w = A_W
    r_t = rvk[:, 0:w] * jnp.exp(cum)
    v = rvk[:, w:2 * w]
    a_t = rvk[:, 2 * w:3 * w] * jnp.exp(cum - lw)
    b_t, k_t = bd * g_inv, kd * g_inv
    b_g, k_g = bd * g_rest, kd * g_rest
    low = lax.broadcasted_iota(jnp.int32, (1, LANE), 1) < A_HD

    def expand(x):
        return jnp.concatenate([jnp.where(low, x, 0.0), jnp.where(low, 0.0, x)], axis=0)

    chains = []
    for p in range(A_W // LANE):
        sl = slice(p * LANE, (p + 1) * LANE)
        chains.append(dict(
            ev=expand(v[:, sl]),
            lhs=jnp.concatenate([expand(a_t[:, sl]), expand(r_t[:, sl])], axis=0),
            rhs=jnp.concatenate([expand(b_t[:, sl]), expand(k_t[:, sl])], axis=0),
            bk=jnp.concatenate([expand(b_g[:, sl]), expand(k_g[:, sl])], axis=0),
            g_col=jnp.transpose(jnp.broadcast_to(g_tot[:, sl], (LANE, LANE))),
            strict=strict, incl=incl))
    return chains


def _square_pairs(mats, passes):
    if passes != 1 or len(mats) % 2:
        return [_dot(m, m, NN, passes) for m in mats]
    out = []
    for m0, m1 in zip(mats[0::2], mats[1::2]):
        n = m0.shape[0]
        m0, m1 = m0.astype(BF16), m1.astype(BF16)
        zero = jnp.zeros_like(m0)
        diag = jnp.concatenate([jnp.concatenate([m0, zero], axis=1), jnp.concatenate([zero, m1], axis=1)], axis=0)
        sq = lax.dot_general(jnp.concatenate([m0, m1], axis=1), diag, NN, preferred_element_type=F32)
        out += [sq[:, 0:n], sq[:, n:2 * n]]
    return out


def _rwkv_solve(chains, states, passes, res):
    L = CHUNK
    L2 = 2 * L
    each = lambda fn, *cols: [fn(*args) for args in zip(*cols)]
    quad = each(lambda c: _dot(c["lhs"], c["rhs"], NT, passes["quad"]), chains)
    yield
    es = each(lambda c, s: _dot(c["lhs"], s, NN, passes["state_in"]), chains, states)
    yield
    n_ab = each(lambda c, q: jnp.where(c["strict"], q[0:L2, 0:L2], 0.0), chains, quad)
    x = each(lambda c, q, e: e[0:L2] + _dot(jnp.where(c["strict"], q[0:L2, L2:2 * L2], 0.0), c["ev"], NN,
                                            passes["apply"]), chains, quad, es)
    yield
    x = each(lambda n, xi: xi - _dot(n, xi, NN, passes["solve"]), n_ab, x)
    yield
    npow = n_ab
    for _ in range(int(math.log2(L)) - 1):
        npow = _square_pairs(npow, passes["square"])
        yield
        x = each(lambda n, xi: xi + _dot(n, xi, NN, passes["solve"]), npow, x)
        yield
    pv = each(lambda c, xi: jnp.concatenate([-xi, c["ev"]], axis=0), chains, x)
    y2 = each(lambda c, q, e, pvi: e[L2:2 * L2] + _dot(
        jnp.concatenate([jnp.where(c["incl"], q[L2:2 * L2, 0:L2], 0.0),
                         jnp.where(c["incl"], q[L2:2 * L2, L2:2 * L2], 0.0)], axis=1), pvi, NN, passes["apply"]),
        chains, quad, es, pv)
    res["y"] = [y[0:L] + y[L:L2] for y in y2]
    yield
    res["s"] = each(lambda c, s, pvi: s * c["g_col"] + _dot(c["bk"], pvi, TN, passes["state_out"]),
                    chains, states, pv)


def _chunk_index(b, c, fwd, n_ctx_chunk, n_lat_chunk, lat_groups):
    in_ctx = c < n_ctx_chunk
    cc = c if fwd else n_ctx_chunk - 1 - c
    lc = c - n_ctx_chunk if fwd else n_lat_chunk - 1 - (c - n_ctx_chunk)
    grp = jnp.where(in_ctx, lat_groups, b)
    chunk = jnp.where(in_ctx, b * n_ctx_chunk + cc, lc)
    return grp, chunk


def _scan_specs(widths, bsz, ctx_len, t):
    ncc, nlc = ctx_len // CHUNK, t // CHUNK

    def spec(width, lane_blk, fwd):
        def index(b, c):
            grp, ch = _chunk_index(b, c, fwd, ncc, nlc, bsz)
            return grp, ch, lane_blk
        return pl.BlockSpec((1, CHUNK, width), index)

    return [[spec(w, (0 if fwd else 1) if blk is None else blk, fwd) for w, blk in widths] for fwd in (True, False)]


def _even_scan(rvk, lw, kd, bd, qkv, gb, bsz, ctx_len):
    g, t, _ = rvk.shape
    ins_f, ins_b = _scan_specs([(3 * A_W, 0), (A_W, None), (A_W, None), (A_W, None), (3 * B_W, 0), (LANE, 0)],
                               bsz, ctx_len, t)
    outs_f, outs_b = _scan_specs([(A_W, 0), (B_W, 0)], bsz, ctx_len, t)
    shapes = [jax.ShapeDtypeStruct((g, t, A_W), F32), jax.ShapeDtypeStruct((g, t, B_W), F32)]
    return pl.pallas_call(
        _even_scan_kernel,
        grid=(bsz, (ctx_len + t) // CHUNK),
        in_specs=ins_f + ins_b,
        out_specs=outs_f + outs_b,
        out_shape=shapes + shapes,
        scratch_shapes=[pltpu.VMEM((2, A_W // LANE, LANE, LANE), F32), pltpu.VMEM((2, B_H, B_HD, B_HD), F32)],
        compiler_params=_cparams(("parallel", "arbitrary")), name="even_scan",
    )(rvk, lw, kd, bd, qkv, gb, rvk, lw, kd, bd, qkv, gb)


def _gdn_prep_kernel(pb_ref, pv_ref, nx_ref, ab_ref, cw_ref, alog_ref, dtb_ref, qkv_ref, gb_ref, buf,
                     *, tt, lat_groups, ntile):
    gi, i = pl.program_id(0), pl.program_id(1)
    is_lat = gi < lat_groups
    _fill_halo(buf, pb_ref[0], pv_ref, nx_ref, jnp.logical_and(is_lat, i > 0),
               jnp.logical_and(is_lat, i < ntile - 1), tt)
    half = B_CONV // 2
    acc = buf[SUB - half:SUB - half + tt, :] * cw_ref[0:1, :]
    for j in range(1, B_CONV):
        acc = acc + buf[SUB - half + j:SUB - half + j + tt, :] * cw_ref[j:j + 1, :]
    act = acc * _sigmoid(acc)
    outs = []
    for h in range(2 * B_H):
        xh = act[:, h * B_HD:(h + 1) * B_HD]
        xh = xh * lax.rsqrt(jnp.sum(xh * xh, axis=-1, keepdims=True) + 1e-6)
        if h < B_H:
            xh = xh * (B_HD ** -0.5)
        outs.append(xh)
    outs.append(act[:, 2 * B_W:3 * B_W])
    qkv_ref[0] = jnp.concatenate(outs, axis=1)
    ab = ab_ref[0]
    lane = lax.broadcasted_iota(jnp.int32, ab.shape, 1)
    gval = -jnp.exp(alog_ref[...]) * _softplus(ab + dtb_ref[...])
    gb_ref[0] = jnp.where(lane < 2 * B_H, gval, _sigmoid(ab))


def _gdn_prep(pb, zab, conv_w, alog_row, dtb_row, tt, lat_groups):
    g, t, wb = pb.shape
    tile = lambda width: pl.BlockSpec((1, tt, width), lambda gi, i: (gi, i, 0))
    full = lambda arr: pl.BlockSpec(arr.shape, lambda gi, i: (0,) * arr.ndim)
    prev, nxt = _halo_specs(wb, tt, t)
    ab_spec = pl.BlockSpec((1, tt, LANE), lambda gi, i: (gi, i, B_W // LANE))
    return pl.pallas_call(
        functools.partial(_gdn_prep_kernel, tt=tt, lat_groups=lat_groups, ntile=t // tt),
        grid=(g, t // tt),
        in_specs=[tile(wb), prev, nxt, ab_spec, full(conv_w), full(alog_row), full(dtb_row)],
        out_specs=[tile(wb), tile(LANE)],
        out_shape=[jax.ShapeDtypeStruct((g, t, wb), F32), jax.ShapeDtypeStruct((g, t, LANE), F32)],
        scratch_shapes=[pltpu.VMEM((tt + 2 * SUB, wb), F32)],
        compiler_params=_cparams(("parallel", "parallel")), name="gdn_prep",
    )(pb, pb, pb, zab, conv_w, alog_row, dtb_row)


def _gdn_operands(qkv, gb, d):
    fwd = d == 0
    L = CHUNK
    L2 = 2 * L
    same, strict, incl = _order_masks(L2, L, fwd)
    _, _, incl_rev = _order_masks(L2, L, not fwd)
    incl_bf = jnp.where(incl, 1.0, 0.0).astype(BF16)
    incl_t_bf = jnp.where(incl_rev, 1.0, 0.0).astype(BF16)
    same_bf = jnp.where(same, 1.0, 0.0).astype(BF16)
    lane = lax.broadcasted_iota(jnp.int32, gb.shape, 1)

    def column(idx):
        return jnp.sum(jnp.where(lane == idx, gb, 0.0), axis=-1, keepdims=True)

    chains = []
    for pr in range(B_H // 2):
        heads = (2 * pr, 2 * pr + 1)
        stack = lambda off: jnp.concatenate([qkv[:, off + h * B_HD:off + (h + 1) * B_HD] for h in heads], axis=0)
        q2, k2, v2 = stack(0), stack(B_W), stack(2 * B_W)
        gcol = jnp.concatenate([column(d * B_H + h) for h in heads], axis=0)
        bcol = jnp.concatenate([column(2 * B_H + d * B_H + h) for h in heads], axis=0)
        chains.append(dict(q2=q2, k2=k2, v2=v2, bcol=bcol, g3=_three_pieces(gcol), strict=strict, incl=incl,
                           sum_bf=jnp.concatenate([incl_bf, same_bf], axis=0), incl_t_bf=incl_t_bf))
    return chains


def _three_pieces(col):
    lane = lax.broadcasted_iota(jnp.int32, (col.shape[0], LANE), 1)
    p1 = col.astype(BF16).astype(F32)
    p2 = (col - p1).astype(BF16).astype(F32)
    p3 = col - p1 - p2
    return jnp.where(lane == 0, p1, jnp.where(lane == 1, p2, jnp.where(lane == 2, p3, 0.0))).astype(BF16)


def _gdn_solve(chains, states, passes, res):
    L = CHUNK
    L2 = 2 * L
    each = lambda fn, *cols: [fn(*args) for args in zip(*cols)]
    dg = functools.partial(lax.dot_general, preferred_element_type=F32)
    sums = each(lambda c: dg(c["sum_bf"], c["g3"], NN), chains)
    sums_t = each(lambda c: dg(c["g3"], c["incl_t_bf"], TN), chains)
    k2b = each(lambda c: c["k2"] * c["bcol"], chains)
    kq = each(lambda c, kb: _dot(jnp.concatenate([kb, c["q2"]], axis=0), c["k2"], NT, passes["quad"]), chains, k2b)
    yield
    gc_row = each(lambda m: jnp.sum(m[0:L2], axis=-1, keepdims=True), sums)
    g_tot = each(lambda m: jnp.sum(m[L2:2 * L2], axis=-1, keepdims=True), sums)
    gc_col = each(lambda m: m[0:1] + m[1:2] + m[2:3], sums_t)
    decay = each(lambda c, r, cl: jnp.exp(jnp.where(c["incl"], r - cl, -1e30)), chains, gc_row, gc_col)
    egc = each(jnp.exp, gc_row)
    l_mat = each(lambda c, m, dc: jnp.where(c["strict"], m[0:L2] * dc, 0.0), chains, kq, decay)
    a_int = each(lambda c, m, dc: jnp.where(c["incl"], m[L2:2 * L2] * dc, 0.0), chains, kq, decay)
    x = each(lambda c, kb, eg: jnp.concatenate([c["v2"] * c["bcol"], kb * eg], axis=1), chains, k2b, egc)
    yield
    x = each(lambda n, xi: xi - _dot(n, xi, NN, passes["solve"]), l_mat, x)
    yield
    npow = l_mat
    for _ in range(int(math.log2(L)) - 1):
        npow = _square_pairs(npow, passes["square"])
        yield
        x = each(lambda n, xi: xi + _dot(n, xi, NN, passes["solve"]), npow, x)
        yield
    qe = each(lambda c, eg: c["q2"] * eg, chains, egc)
    rows = (slice(0, L), slice(L, L2))
    vnew = each(lambda xi, st: [xi[r, 0:B_HD] - _dot(xi[r, B_HD:2 * B_HD], s, NN, passes["state"])
                                for r, s in zip(rows, st)], x, states)
    o1 = each(lambda qi, st: [_dot(qi[r], s, NN, passes["state"]) for r, s in zip(rows, st)], qe, states)
    yield
    o2 = each(lambda a, vn, o: jnp.concatenate(o, axis=0) + _dot(a, jnp.concatenate(vn, axis=0), NN,
                                                                 passes["apply"]), a_int, vnew, o1)
    res["o"] = [[o[r] for r in rows] for o in o2]
    k_rest = each(lambda c, gt, r: c["k2"] * jnp.exp(gt - r), chains, g_tot, gc_row)
    res["s"] = each(lambda st, gt, kr, vn: [s * jnp.exp(gt[r][0:1, :]) + _dot(kr[r], v, TN, passes["state"])
                                            for r, s, v in zip(rows, st, vn)], states, g_tot, k_rest, vnew)


def _mix_out_kernel(yaf_ref, yab_ref, bg_ref, lnw_ref, lnb_ref, bd_ref, obf_ref, obb_ref, z_ref, nw_ref, o_ref):
    y = yaf_ref[0] + yab_ref[0]
    ones_bd = bd_ref[...]
    mean = _group_sum(y, ones_bd) * (1.0 / A_HD)
    yc = y - mean
    var = _group_sum(yc * yc, ones_bd) * (1.0 / A_HD)
    yn = yc * lax.rsqrt(var + A_LN_EPS) * lnw_ref[...] + lnb_ref[...]
    bg = bg_ref[0]
    out_a = (yn + bg[:, 0:A_W]) * bg[:, A_W:2 * A_W]
    o = obf_ref[0] + obb_ref[0]
    z = z_ref[0]
    outs = [out_a.astype(o_ref.dtype)]
    for h in range(B_H):
        oh = o[:, h * B_HD:(h + 1) * B_HD]
        on = oh * lax.rsqrt(jnp.mean(oh * oh, axis=-1, keepdims=True) + NORM_EPS) * nw_ref[...]
        zh = z[:, h * B_HD:(h + 1) * B_HD]
        outs.append((on * (zh * _sigmoid(zh))).astype(o_ref.dtype))
    o_ref[0] = jnp.concatenate(outs, axis=1)


def _mix_out(ya, bg, ln_w, ln_b, ones_bd, ob, zab, norm_w, tt):
    g, t, _ = bg.shape
    tile = lambda width: pl.BlockSpec((1, tt, width), lambda gi, i: (gi, i, 0))
    full = lambda arr: pl.BlockSpec(arr.shape, lambda gi, i: (0,) * arr.ndim)
    return pl.pallas_call(
        _mix_out_kernel,
        grid=(g, t // tt),
        in_specs=[tile(A_W), tile(A_W), tile(2 * A_W), full(ln_w), full(ln_b), full(ones_bd), tile(B_W),
                  tile(B_W), tile(B_W), full(norm_w)],
        out_specs=tile(A_W + B_W),
        out_shape=jax.ShapeDtypeStruct((g, t, A_W + B_W), BF16),
        compiler_params=_cparams(("parallel", "parallel")), name="mix_out",
    )(ya[0], ya[1], bg, ln_w, ln_b, ones_bd, ob[0], ob[1], zab, norm_w)


def _qkv_rope_kernel(x_ref, w_ref, cos_ref, sin_ref, o_ref, *, lat_groups, q_cols, rope_cols, seg):
    is_lat = pl.program_id(0) < lat_groups
    x = x_ref[0]
    lane = lax.broadcasted_iota(jnp.int32, (1, LANE), 1)
    first = (lane % (C_HD // 2)) < (C_HD // 4)
    quarter = C_HD // 4
    cos_l = jnp.where(is_lat, cos_ref[...], 1.0)
    sin_l = jnp.where(is_lat, sin_ref[...], 0.0)
    for start in range(0, w_ref.shape[1], seg):
        acc = jnp.dot(x, w_ref[:, start:start + seg], preferred_element_type=F32)
        if start >= rope_cols:
            o_ref[0, :, start:start + seg] = acc.astype(o_ref.dtype)
            continue
        scale = C_HD ** -0.5 if start < q_cols else 1.0
        cos, sin = cos_l * scale, sin_l * scale
        for blk in range(seg // LANE):
            xb = acc[:, blk * LANE:(blk + 1) * LANE]
            partner = jnp.where(first, pltpu.roll(xb, LANE - quarter, axis=1), pltpu.roll(xb, quarter, axis=1))
            col = start + blk * LANE
            o_ref[0, :, col:col + LANE] = (xb * cos + partner * sin).astype(o_ref.dtype)


def _qkv_rope(h, w, cos, sin, lat_groups, tm, seg, q_cols, rope_cols):
    g, t, kdim = h.shape
    n = w.shape[1]
    assert q_cols % seg == 0 and rope_cols % seg == 0 and n % seg == 0
    return pl.pallas_call(
        functools.partial(_qkv_rope_kernel, lat_groups=lat_groups, q_cols=q_cols, rope_cols=rope_cols, seg=seg),
        grid=(g, t // tm),
        in_specs=[pl.BlockSpec((1, tm, kdim), lambda gi, i: (gi, i, 0)),
                  pl.BlockSpec((kdim, n), lambda gi, i: (0, 0)),
                  pl.BlockSpec((tm, LANE), lambda gi, i: (i, 0)),
                  pl.BlockSpec((tm, LANE), lambda gi, i: (i, 0))],
        out_specs=pl.BlockSpec((1, tm, n), lambda gi, i: (gi, i, 0)),
        out_shape=jax.ShapeDtypeStruct((g, t, n), BF16),
        compiler_params=_cparams(("parallel", "parallel")), name="qkv_rope",
    )(h, w, cos, sin)


def _attn_kernel(sink_ref, q_ref, kp_ref, km_ref, kn_ref, kc_ref, o_ref, *, n_blocks, kv_w):
    n = pl.program_id(1)
    blk = C_BLOCK
    ii = lax.broadcasted_iota(jnp.int32, (blk, blk), 0)
    jj = lax.broadcasted_iota(jnp.int32, (blk, blk), 1)
    ok_prev = jnp.logical_and(jj >= ii, n > 0)
    ok_next = jnp.logical_and(jj <= ii, n < n_blocks - 1)
    n_ctx = kc_ref.shape[1]
    valid1 = jnp.concatenate([ok_prev, jnp.full((blk, blk), True), ok_next, jnp.full((blk, n_ctx), True)], axis=1)
    valid = jnp.concatenate([valid1] * C_GROUP, axis=0)
    low = lax.broadcasted_iota(jnp.int32, (1, LANE), 1) < C_HD
    row_head = lax.broadcasted_iota(jnp.int32, (C_GROUP * blk, 1), 0) // blk
    q = q_ref[0]
    kv = jnp.concatenate([kp_ref[0], km_ref[0], kn_ref[0], kc_ref[0]], axis=0)
    zero = jnp.zeros((), q.dtype)
    outs = []
    for h in range(C_KVH):
        kh = kv[:, h * LANE:(h + 1) * LANE]
        vh = kv[:, kv_w + h * LANE:kv_w + (h + 1) * LANE]
        qa = q[:, (2 * h) * LANE:(2 * h + 1) * LANE]
        qb = q[:, (2 * h + 1) * LANE:(2 * h + 2) * LANE]
        qs = jnp.concatenate([jnp.where(low, qa, zero), jnp.where(low, zero, qa),
                              jnp.where(low, qb, zero), jnp.where(low, zero, qb)], axis=0)
        s = lax.dot_general(qs, kh, NT, preferred_element_type=F32)
        s = jnp.where(valid, s, -1e30)
        sk = jnp.full((C_GROUP * blk, 1), 0.0, F32)
        for gq in range(C_GROUP):
            sk = jnp.where(row_head == gq, sink_ref[h * C_GROUP + gq], sk)
        m = jnp.maximum(jnp.max(s, axis=-1, keepdims=True), sk)
        p = jnp.exp(s - m)
        denom = jnp.sum(p, axis=-1, keepdims=True) + jnp.exp(sk - m)
        o = lax.dot_general(p.astype(vh.dtype), vh, NN, preferred_element_type=F32) / denom
        outs.append(jnp.where(low, o[0:blk], o[blk:2 * blk]).astype(o_ref.dtype))
        outs.append(jnp.where(low, o[2 * blk:3 * blk], o[3 * blk:4 * blk]).astype(o_ref.dtype))
    o_ref[0] = jnp.concatenate(outs, axis=1)


def _attention(qkv, sink, bsz, ctx_len):
    g, t, _ = qkv.shape
    q_w = C_QH * C_HD
    kv_w = C_KVH * LANE
    nq = q_w // (2 * kv_w)
    assert q_w % (2 * kv_w) == 0
    n_blocks = t // C_BLOCK
    cpb = ctx_len // C_BLOCK
    assert ctx_len % C_BLOCK == 0
    kvspec = lambda fn: pl.BlockSpec((1, C_BLOCK, 2 * kv_w), fn)
    return pl.pallas_call(
        functools.partial(_attn_kernel, n_blocks=n_blocks, kv_w=kv_w),
        grid=(bsz, n_blocks),
        in_specs=[pl.BlockSpec(memory_space=pltpu.SMEM),
                  pl.BlockSpec((1, C_BLOCK, q_w), lambda b, n: (b, n, 0)),
                  kvspec(lambda b, n: (b, jnp.maximum(n - 1, 0), nq)),
                  kvspec(lambda b, n: (b, n, nq)),
                  kvspec(lambda b, n: (b, jnp.minimum(n + 1, n_blocks - 1), nq)),
                  pl.BlockSpec((1, ctx_len, 2 * kv_w), lambda b, n: (bsz, b, nq))],
        out_specs=pl.BlockSpec((1, C_BLOCK, q_w), lambda b, n: (b, n, 0)),
        out_shape=jax.ShapeDtypeStruct((bsz, t, q_w), BF16),
        compiler_params=_cparams(("parallel", "parallel")), name="window_attn",
    )(sink, qkv, qkv, qkv, qkv, qkv)


def _lane_cumsum(x):
    n = x.shape[-1]
    lane = lax.broadcasted_iota(jnp.int32, x.shape, x.ndim - 1)
    k = 1
    while k < n:
        x = x + jnp.where(lane >= k, pltpu.roll(x, k, axis=x.ndim - 1), 0)
        k *= 2
    return x


def _route_kernel(lg_ref, rank_ref, aff_ref, *, cap, slot_stride):
    lg = lg_ref[0]
    m = jnp.max(lg, axis=0, keepdims=True)
    e = jnp.exp(lg - m)
    z = jnp.sum(e, axis=0, keepdims=True)
    aff = e / z
    key = (lg - m) - jnp.log(z)
    count_ge = lambda v: jnp.sum(jnp.where(key >= v, 1, 0), axis=-1, keepdims=True)

    def body(_, carry):
        lo, hi = carry
        mid = 0.5 * (lo + hi)
        ok = count_ge(mid) >= cap
        return jnp.where(ok, mid, lo), jnp.where(ok, hi, mid)

    lo0 = jnp.min(key, axis=-1, keepdims=True)
    lo, hi = lax.fori_loop(0, ROUTE_BISECT, body, (lo0, jnp.ones_like(lo0)))
    thr, found = lo, jnp.zeros(lo.shape, jnp.int32)
    for _ in range(ROUTE_SNAP):
        v = jnp.max(jnp.where(key < hi, key, -3e38), axis=-1, keepdims=True)
        ok = jnp.where(count_ge(v) >= cap, 1, 0)
        thr = jnp.where(jnp.logical_and(found == 0, ok == 1), v, thr)
        hi = jnp.where(jnp.logical_or(found == 1, ok == 1), hi, v)
        found = jnp.maximum(found, ok)
    gt = key > thr
    eq = key == thr
    need = cap - jnp.sum(jnp.where(gt, 1, 0), axis=-1, keepdims=True)
    take_eq = jnp.logical_and(eq, _lane_cumsum(jnp.where(eq, 1, 0)) <= need)
    sel = jnp.logical_or(gt, take_eq)
    slot = _lane_cumsum(jnp.where(sel, 1, 0)) - 1 + pl.program_id(0) * slot_stride
    rank_ref[0] = jnp.where(sel, slot, -1)
    aff_ref[0] = aff


def _route(lg, cap, slot_stride):
    g, e, t = lg.shape
    spec = pl.BlockSpec((1, e, t), lambda gi: (gi, 0, 0))
    return pl.pallas_call(
        functools.partial(_route_kernel, cap=cap, slot_stride=slot_stride),
        grid=(g,), in_specs=[spec], out_specs=[spec, spec],
        out_shape=[jax.ShapeDtypeStruct((g, e, t), jnp.int32), jax.ShapeDtypeStruct((g, e, t), F32)],
        compiler_params=_cparams(("parallel",)), name="route",
    )(lg)


def _expert_kernel(h_ref, rank_ref, aff_ref, w1_ref, w3_ref, w2_ref, o_ref, *, cap):
    e = pl.program_id(1)

    @pl.when(e == 0)
    def _():
        o_ref[...] = jnp.zeros_like(o_ref)

    rank = rank_ref[0, 0]
    t = rank.shape[-1]
    hit = lax.broadcasted_iota(jnp.int32, (cap, t), 0) == rank
    pick = jnp.where(hit, 1.0, 0.0).astype(BF16)
    gate = jnp.sum(jnp.where(hit, aff_ref[0, 0], 0.0), axis=-1, keepdims=True)
    xe = jnp.dot(pick, h_ref[0], preferred_element_type=F32).astype(BF16)
    h1 = jnp.dot(xe, w1_ref[0], preferred_element_type=F32)
    h3 = jnp.dot(xe, w3_ref[0], preferred_element_type=F32)
    hid = (h1 * _sigmoid(h1) * h3).astype(BF16)
    ye = (jnp.dot(hid, w2_ref[0], preferred_element_type=F32) * gate).astype(BF16)
    o_ref[0] += lax.dot_general(pick, ye, TN, preferred_element_type=F32)


def _experts(h, rank, aff, w1, w3, w2, cap):
    g, t, _ = h.shape
    n_exp, _, f = w1.shape
    sel = pl.BlockSpec((1, 1, 1, t), lambda gi, e: (gi, e, 0, 0))
    return pl.pallas_call(
        functools.partial(_expert_kernel, cap=cap),
        grid=(g, n_exp),
        in_specs=[pl.BlockSpec((1, t, D), lambda gi, e: (gi, 0, 0)), sel, sel,
                  pl.BlockSpec((1, D, f), lambda gi, e: (e, 0, 0)),
                  pl.BlockSpec((1, D, f), lambda gi, e: (e, 0, 0)),
                  pl.BlockSpec((1, f, D), lambda gi, e: (e, 0, 0))],
        out_specs=pl.BlockSpec((1, t, D), lambda gi, e: (gi, 0, 0)),
        out_shape=jax.ShapeDtypeStruct((g, t, D), F32),
        compiler_params=_cparams(("parallel", "arbitrary")), name="experts",
    )(h, rank.reshape(g, n_exp, 1, t), aff.reshape(g, n_exp, 1, t), w1, w3, w2)


def _moe(h, lg, w1, w3, w2, bsz, ctx_len, lat_groups_only):
    g, t, _ = h.shape
    cap_lat = EC_CAP * t // N_EXP
    rank, aff = _route(lg[:bsz], cap_lat, 0)
    if not lat_groups_only:
        cap_ctx = EC_CAP * ctx_len // N_EXP
        lg_c = lg[bsz].reshape(N_EXP, bsz, ctx_len).transpose(1, 0, 2)
        rank_c, aff_c = _route(lg_c, cap_ctx, cap_ctx)
        back = lambda a: a.transpose(1, 0, 2).reshape(1, N_EXP, t)
        rank = jnp.concatenate([rank, back(rank_c)], axis=0)
        aff = jnp.concatenate([aff, back(aff_c)], axis=0)
        assert cap_ctx * bsz == cap_lat
    else:
        h = h[:bsz] if h.shape[0] != bsz else h
    return _experts(h, rank, aff, w1, w3, w2, cap_lat)


def _block_diag2(w):
    z = jnp.zeros_like(w[0])
    return jnp.concatenate([jnp.concatenate([w[0], z], axis=1), jnp.concatenate([z, w[1]], axis=1)], axis=0)


def _rope_tables(t):
    quarter = C_HD // 4
    inv = ROPE_BASE ** (-jnp.arange(quarter, dtype=F32) / quarter)
    pos = jnp.arange(t)
    row = (pos // GRID_W).astype(F32)
    col = (pos % GRID_W).astype(F32)
    lane = np.arange(LANE)
    use_col = ((lane % C_HD) >= C_HD // 2)
    ang = jnp.where(use_col[None, :], col[:, None], row[:, None]) * inv[lane % quarter][None, :]
    sign = np.where((lane % (C_HD // 2)) < quarter, -1.0, 1.0).astype(np.float32)
    return jnp.cos(ang), jnp.sin(ang) * sign[None, :]


def kernel(x, c, ctx, c_ctx, ada_w, ada_b, norm_mix, norm_ffn, e_w_in, e_w_out, a_mu, a_w0, a_w2, a_a0, a_a2, a_g2,
           a_k_k, a_k_a, a_r_k, a_ln_w, a_ln_b, b_conv, b_a_log, b_dt_bias, b_norm, o_w_in, o_w_out, o_sink,
           moe_router, moe_w1, moe_w3, moe_w2, final_norm):
    bsz, t, _ = x.shape
    ctx_len = ctx.shape[1]
    depth = ada_w.shape[0]
    assert bsz * ctx_len == t and ctx_len % C_BLOCK == 0 and t % C_BLOCK == 0
    tt = ctx_len

    rows = 2 * SUB * ((bsz + 1 + 2 * SUB - 1) // (2 * SUB))
    cc = jnp.zeros((rows, D), F32).at[:bsz].set(c).at[bsz].set(c_ctx)
    mods = _adaln(cc, ada_w, ada_b)[:, :bsz + 1].reshape(depth, bsz + 1, N_MOD, D)
    mods = jnp.pad(mods, ((0, 0), (0, 0), (0, MOD_ROWS - N_MOD), (0, 0)))

    xs = jnp.concatenate([x, ctx.reshape(1, t, D)], axis=0)
    ones_bd = jnp.asarray(np.kron(np.eye(A_H), np.ones((A_HD, A_HD))), BF16)
    a_cols = e_w_in.shape[2] - (4 * B_W + 4 * B_H)
    cos_t, sin_t = _rope_tables(t)

    res = None
    for i in range(depth):
        j = i // 2
        mod = mods[i]
        x_new, h, _ = _norm(xs, norm_mix[i], mod, k=0, res=res, tt=tt)
        xs = xs if res is None else x_new
        if i % 2 == 0:
            w_in = e_w_in[j]
            split_ab = a_cols + 3 * B_W
            w_all = jnp.concatenate([w_in[:, :split_ab], w_in[:, split_ab + 4 * B_H:],
                                     w_in[:, split_ab:split_ab + 4 * B_H],
                                     jnp.zeros((D, LANE - 4 * B_H), F32)], axis=1).astype(BF16)
            pa, pb, zab = _mm_split(h, w_all, (a_cols, 3 * B_W, B_W + LANE), F32, 2 * tt)
            prm = {
                "mu": a_mu[j].reshape(1, -1), "w0": a_w0[j].reshape(1, -1), "w2": _block_diag2(a_w2[j]),
                "a0": a_a0[j].reshape(1, -1), "a2": _block_diag2(a_a2[j]), "g2": a_g2[j],
                "k_k": a_k_k[j].reshape(1, -1), "k_a": jnp.tile(a_k_a[j].reshape(1, -1), (1, 2)),
                "r_k": a_r_k[j].reshape(1, -1), "ones_bd": ones_bd,
            }
            rvk, lw, kd, bd, bg = _rwkv_prep(pa, prm, tt, bsz)
            pad_row = lambda v: jnp.pad(v.reshape(1, -1), ((0, 0), (0, LANE - v.size)))
            qkv_c, gb = _gdn_prep(pb, zab, b_conv[j], pad_row(b_a_log[j]), pad_row(b_dt_bias[j]), tt, bsz)
            ya_f, ob_f, ya_b, ob_b = _even_scan(rvk, lw, kd, bd, qkv_c, gb, bsz, ctx_len)
            mix = _mix_out((ya_f, ya_b), bg, a_ln_w[j].reshape(1, -1), a_ln_b[j].reshape(1, -1), ones_bd,
                           (ob_f, ob_b), zab, b_norm[j].reshape(1, -1), tt)
            xs = _mm_res(mix, e_w_out[j].astype(BF16), xs, mod, 2, tt)
        else:
            w_in = o_w_in[j]
            q_w = C_QH * C_HD
            kv_cols = C_KVH * C_HD
            dup = lambda w: jnp.repeat(w.reshape(D, C_KVH, 1, C_HD), 2, axis=2).reshape(D, 2 * kv_cols)
            w_all = jnp.concatenate([w_in[:, :q_w], dup(w_in[:, q_w:q_w + kv_cols]), dup(w_in[:, q_w + kv_cols:])],
                                    axis=1).astype(BF16)
            tn = 2 * kv_cols
            qkv = _qkv_rope(h, w_all, cos_t, sin_t, bsz, 2 * tt, tn, q_w, q_w + tn)
            att = _attention(qkv, o_sink[j], bsz, ctx_len)
            x_lat = _mm_res(att, o_w_out[j].astype(BF16), xs, mod, 2, tt, groups=bsz)
            xs = x_lat if xs.shape[0] == bsz else jnp.concatenate([x_lat, xs[bsz:]], axis=0)
        last = i == depth - 1
        if last:
            xs = xs[:bsz]
            mod = mod[:bsz]
        _, hf, lg = _norm(xs, norm_ffn[i], mod, k=3, router_t=moe_router[i].T, tt=tt)
        delta = _moe(hf, lg, _to_bf16(moe_w1, i), _to_bf16(moe_w3, i), _to_bf16(moe_w2, i), bsz, ctx_len, last)
        res = (delta, mod, 5)
    _, out, _ = _norm(xs, final_norm, None, res=res, tt=tt, out_dtype=F32)
    return out
```

```python
import functools
import math

import jax
import jax.numpy as jnp
import numpy as np
from jax import lax
from jax.experimental import pallas as pl
from jax.experimental.pallas import tpu as pltpu

F32 = jnp.float32
BF16 = jnp.bfloat16
HIGHEST = lax.Precision.HIGHEST

D = 1024
N_MOD = 6
NORM_EPS = 1e-6
A_HD, A_H, A_W = 64, 8, 512
A_LN_EPS = 64e-5
B_HD, B_H, B_W = 128, 4, 512
B_CONV = 5
CHUNK = 64
C_HD, C_QH, C_KVH, C_GROUP = 64, 16, 4, 4
C_BLOCK = 128
ROPE_BASE = 10000.0
GRID_W = 64
N_EXP = 16
EC_CAP = 2
ROUTE_BISECT = 40
ROUTE_SNAP = 3
RWKV_PASSES = {"quad": 1, "apply": 1, "solve": "b", "square": 1, "state_in": "a", "state_out": "b"}
GDN_PASSES = {"quad": 1, "apply": 1, "solve": "b", "square": 1, "state": 1}
MOD_ROWS = 8
LANE = 128
SUB = 8
VMEM_LIMIT = 56 * 1024 * 1024

NT = (((1,), (1,)), ((), ()))
NN = (((1,), (0,)), ((), ()))
TN = (((0,), (0,)), ((), ()))


def _cparams(sem):
    return pltpu.CompilerParams(dimension_semantics=sem, vmem_limit_bytes=VMEM_LIMIT)


def _sigmoid(x):
    return 1.0 / (1.0 + jnp.exp(-x))


def _softplus(x):
    return jnp.maximum(x, 0.0) + jnp.log(1.0 + jnp.exp(-jnp.abs(x)))


def _split2(x):
    hi = x.astype(BF16)
    lo = (x - hi.astype(F32)).astype(BF16)
    return hi, lo


def _dot(a, b, dims=NN, passes=1):
    dg = functools.partial(lax.dot_general, dimension_numbers=dims, preferred_element_type=F32)
    if passes == 1:
        return dg(a.astype(BF16), b.astype(BF16))
    (ka,), (kb,) = dims[0]
    if passes == "a":
        ah, al = _split2(a)
        bh = b.astype(BF16)
        return dg(jnp.concatenate([ah, al], axis=ka), jnp.concatenate([bh, bh], axis=kb))
    if passes == "b":
        bh, bl = _split2(b)
        n = b.shape[1 - kb]
        res = dg(a.astype(BF16), jnp.concatenate([bh, bl], axis=1 - kb))
        return res[:, 0:n] + res[:, n:2 * n]
    ah, al = _split2(a)
    bh, bl = _split2(b)
    return dg(ah, bh) + (dg(ah, bl) + dg(al, bh))


def _group_sum(x, ones_bd):
    hi, lo = _split2(x)
    dg = functools.partial(lax.dot_general, dimension_numbers=NN, preferred_element_type=F32)
    return dg(hi, ones_bd) + dg(lo, ones_bd)


def _cast_kernel(w_ref, o_ref):
    o_ref[...] = w_ref[...].astype(o_ref.dtype)


def _to_bf16(w, layer):
    _, n, r, c = w.shape
    return pl.pallas_call(_cast_kernel, grid=(n,),
                          in_specs=[pl.BlockSpec((1, 1, r, c), lambda e: (layer, e, 0, 0))],
                          out_specs=pl.BlockSpec((1, 1, r, c), lambda e: (0, e, 0, 0)),
                          out_shape=jax.ShapeDtypeStruct((1, n, r, c), BF16),
                          compiler_params=_cparams(("parallel",)), name="cast_bf16")(w)[0]


def _adaln_kernel(c_ref, w_ref, b_ref, o_ref):
    c = c_ref[...]
    s = c * _sigmoid(c)
    o_ref[0] = jnp.dot(s, w_ref[0], precision=HIGHEST, preferred_element_type=F32) + b_ref[0]


def _adaln(cc, ada_w, ada_b):
    depth, _, n = ada_w.shape
    rows = cc.shape[0]
    tn = 768
    return pl.pallas_call(
        _adaln_kernel,
        grid=(depth, n // tn),
        in_specs=[pl.BlockSpec((rows, D), lambda i, j: (0, 0)),
                  pl.BlockSpec((1, D, tn), lambda i, j: (i, 0, j)),
                  pl.BlockSpec((1, 1, tn), lambda i, j: (i, 0, j))],
        out_specs=pl.BlockSpec((1, rows, tn), lambda i, j: (i, 0, j)),
        out_shape=jax.ShapeDtypeStruct((depth, rows, n), F32),
        compiler_params=_cparams(("parallel", "parallel")),
        name="adaln",
    )(cc, ada_w, ada_b.reshape(depth, 1, n))


def _norm_kernel(*refs, k, kres, has_res, has_mod, has_router):
    it = iter(refs)
    x_ref = next(it)
    d_ref = next(it) if has_res else None
    mres_ref = next(it) if has_res else None
    gain_ref = next(it)
    mod_ref = next(it) if has_mod else None
    rt_ref = next(it) if has_router else None
    xo_ref = next(it) if has_res else None
    h_ref = next(it)
    lg_ref = next(it) if has_router else None

    x = x_ref[0]
    if has_res:
        x = x + mres_ref[0, kres:kres + 1, :] * d_ref[0]
        xo_ref[0] = x
    xn = x * lax.rsqrt(jnp.mean(x * x, axis=-1, keepdims=True) + NORM_EPS) * gain_ref[...]
    if has_mod:
        xn = xn * (1.0 + mod_ref[0, k + 1:k + 2, :]) + mod_ref[0, k:k + 1, :]
    h_ref[0] = xn.astype(h_ref.dtype)
    if has_router:
        lg_ref[0] = lax.dot_general(rt_ref[...], xn, NT, precision=HIGHEST, preferred_element_type=F32)


def _norm(x, gain, mod=None, *, k=0, res=None, router_t=None, tt, out_dtype=BF16):
    g, t, _ = x.shape
    has_res, has_mod, has_router = res is not None, mod is not None, router_t is not None
    kres = res[2] if has_res else 0
    tile = pl.BlockSpec((1, tt, D), lambda gi, i: (gi, i, 0))
    modspec = pl.BlockSpec((1, MOD_ROWS, D), lambda gi, i: (gi, 0, 0))
    ins, specs = [x], [tile]
    if has_res:
        ins += [res[0], res[1]]
        specs += [tile, modspec]
    ins.append(gain.reshape(1, D))
    specs.append(pl.BlockSpec((1, D), lambda gi, i: (0, 0)))
    if has_mod:
        ins.append(mod)
        specs.append(modspec)
    if has_router:
        ins.append(router_t)
        specs.append(pl.BlockSpec((N_EXP, D), lambda gi, i: (0, 0)))
    outs, ospecs = [], []
    if has_res:
        outs.append(jax.ShapeDtypeStruct((g, t, D), F32))
        ospecs.append(tile)
    outs.append(jax.ShapeDtypeStruct((g, t, D), out_dtype))
    ospecs.append(tile)
    if has_router:
        outs.append(jax.ShapeDtypeStruct((g, N_EXP, t), F32))
        ospecs.append(pl.BlockSpec((1, N_EXP, tt), lambda gi, i: (gi, 0, i)))
    res_out = list(pl.pallas_call(
        functools.partial(_norm_kernel, k=k, kres=kres, has_res=has_res, has_mod=has_mod, has_router=has_router),
        grid=(g, t // tt), in_specs=specs, out_specs=ospecs, out_shape=outs,
        compiler_params=_cparams(("parallel", "parallel")), name="norm_mod",
    )(*ins))
    x_new = res_out.pop(0) if has_res else None
    h = res_out.pop(0)
    lg = res_out.pop(0) if has_router else None
    return x_new, h, lg


def _mm_split_kernel(x_ref, w_ref, *o_refs, widths):
    x = x_ref[0]
    start = 0
    for o_ref, width in zip(o_refs, widths):
        o_ref[0] = jnp.dot(x, w_ref[:, start:start + width], preferred_element_type=F32).astype(o_ref.dtype)
        start += width


def _mm_split(x, w, widths, out_dtype, tm):
    g, t, kdim = x.shape
    assert sum(widths) == w.shape[1] and all(wd % LANE == 0 for wd in widths)
    return pl.pallas_call(
        functools.partial(_mm_split_kernel, widths=widths),
        grid=(g, t // tm),
        in_specs=[pl.BlockSpec((1, tm, kdim), lambda gi, i: (gi, i, 0)),
                  pl.BlockSpec(w.shape, lambda gi, i: (0, 0))],
        out_specs=[pl.BlockSpec((1, tm, wd), lambda gi, i: (gi, i, 0)) for wd in widths],
        out_shape=[jax.ShapeDtypeStruct((g, t, wd), out_dtype) for wd in widths],
        compiler_params=_cparams(("parallel", "parallel")), name="proj",
    )(x, w)


def _mm_res_kernel(y_ref, w_ref, x_ref, mod_ref, o_ref, *, k):
    acc = jnp.dot(y_ref[0], w_ref[...], preferred_element_type=F32)
    o_ref[0] = x_ref[0] + mod_ref[0, k:k + 1, :] * acc


def _mm_res(y, w, x, mod, k, tm, groups=None):
    g_all, t, kdim = y.shape
    g = g_all if groups is None else groups
    return pl.pallas_call(
        functools.partial(_mm_res_kernel, k=k),
        grid=(g, t // tm),
        in_specs=[pl.BlockSpec((1, tm, kdim), lambda gi, i: (gi, i, 0)),
                  pl.BlockSpec((kdim, D), lambda gi, i: (0, 0)),
                  pl.BlockSpec((1, tm, D), lambda gi, i: (gi, i, 0)),
                  pl.BlockSpec((1, MOD_ROWS, D), lambda gi, i: (gi, 0, 0))],
        out_specs=pl.BlockSpec((1, tm, D), lambda gi, i: (gi, i, 0)),
        out_shape=jax.ShapeDtypeStruct((g, t, D), F32),
        compiler_params=_cparams(("parallel", "parallel")), name="out_proj_res",
    )(y, w, x, mod)


def _fill_halo(buf, x, pv_ref, nx_ref, has_prev, has_next, tt):
    buf[0:SUB, :] = jnp.where(has_prev, pv_ref[0], 0.0)
    buf[SUB:SUB + tt, :] = x
    buf[SUB + tt:2 * SUB + tt, :] = jnp.where(has_next, nx_ref[0], 0.0)


def _rwkv_prep_kernel(pa_ref, pv_ref, nx_ref, mu_ref, w0_ref, w2_ref, a0_ref, a2_ref, g2_ref, kk_ref, ka_ref,
                      rk_ref, bd_ref, rvk_ref, cum_ref, kd_ref, bdir_ref, bg_ref, buf, *, tt, lat_groups, ntile):
    gi, i = pl.program_id(0), pl.program_id(1)
    is_lat = gi < lat_groups
    x = pa_ref[0]
    _fill_halo(buf, x, pv_ref, nx_ref, jnp.logical_and(is_lat, i > 0), jnp.logical_and(is_lat, i < ntile - 1), tt)
    xm = buf[SUB - 1:SUB - 1 + tt, :]
    xp = buf[SUB + 1:SUB + 1 + tt, :]
    x = x + mu_ref[...] * (0.5 * (xm + xp) - x)
    w = A_W
    r, kx, v = x[:, 0:w], x[:, w:2 * w], x[:, 2 * w:3 * w]
    wd, ad, gd = x[:, 3 * w:3 * w + LANE], x[:, 3 * w + LANE:3 * w + 2 * LANE], x[:, 3 * w + 2 * LANE:3 * w + 3 * LANE]
    w_log = -_softplus(-(w0_ref[...] + _dot(jnp.tanh(wd), w2_ref[...], NN, 3))) - 0.5
    lw = -jnp.exp(w_log)
    a = _sigmoid(a0_ref[...] + _dot(ad, a2_ref[...], NN, 3))
    gate = _dot(_sigmoid(gd), g2_ref[...], NN, 3)
    ones_bd = bd_ref[...]
    kq = kx * kk_ref[...]
    kk = kq * lax.rsqrt(_group_sum(kq * kq, ones_bd) + 1e-6)
    k2 = jnp.concatenate([kx, kx], axis=1)
    kdir = k2 * (1.0 + (a - 1.0) * ka_ref[...])
    bdir = jnp.concatenate([kk, kk], axis=1) * a
    rr = r * rk_ref[...]
    bonus = _group_sum(rr * (kdir[:, 0:w] + kdir[:, w:2 * w]), ones_bd) * v
    rvk_ref[0] = jnp.concatenate([r, v, kk], axis=1)
    cum_ref[0] = jnp.concatenate([_chunk_cumsum(lw[:, 0:w], True), _chunk_cumsum(lw[:, w:2 * w], False)], axis=1)
    kd_ref[0] = kdir.astype(kd_ref.dtype)
    bdir_ref[0] = bdir.astype(bdir_ref.dtype)
    bg_ref[0] = jnp.concatenate([bonus, gate], axis=1).astype(bg_ref.dtype)


def _chunk_cumsum(x, fwd):
    _, _, incl = _order_masks(x.shape[0], CHUNK, fwd)
    mask = jnp.where(incl, 1.0, 0.0).astype(BF16)
    p1 = x.astype(BF16)
    r1 = x - p1.astype(F32)
    p2 = r1.astype(BF16)
    p3 = (r1 - p2.astype(F32)).astype(BF16)
    dg = functools.partial(lax.dot_general, dimension_numbers=NN, preferred_element_type=F32)
    return dg(mask, p1) + (dg(mask, p2) + dg(mask, p3))


def _halo_specs(width, tt, t):
    nb = tt // SUB
    last = t // SUB - 1
    prev = pl.BlockSpec((1, SUB, width), lambda gi, i: (gi, jnp.maximum(i * nb - 1, 0), 0))
    nxt = pl.BlockSpec((1, SUB, width), lambda gi, i: (gi, jnp.minimum((i + 1) * nb, last), 0))
    return prev, nxt


def _rwkv_prep(pa, p, tt, lat_groups):
    g, t, wa = pa.shape
    full = lambda arr: pl.BlockSpec(arr.shape, lambda gi, i: (0,) * arr.ndim)
    tile = lambda width: pl.BlockSpec((1, tt, width), lambda gi, i: (gi, i, 0))
    prev, nxt = _halo_specs(wa, tt, t)
    consts = [p["mu"], p["w0"], p["w2"], p["a0"], p["a2"], p["g2"], p["k_k"], p["k_a"], p["r_k"], p["ones_bd"]]
    widths = (3 * A_W, 2 * A_W, 2 * A_W, 2 * A_W, 2 * A_W)
    return pl.pallas_call(
        functools.partial(_rwkv_prep_kernel, tt=tt, lat_groups=lat_groups, ntile=t // tt),
        grid=(g, t // tt),
        in_specs=[tile(wa), prev, nxt] + [full(c) for c in consts],
        out_specs=[tile(wd) for wd in widths],
        out_shape=[jax.ShapeDtypeStruct((g, t, wd), dt) for wd, dt in zip(widths, (F32, F32, BF16, BF16, BF16))],
        scratch_shapes=[pltpu.VMEM((tt + 2 * SUB, wa), F32)],
        compiler_params=_cparams(("parallel", "parallel")), name="rwkv_prep",
    )(pa, pa, pa, *consts)


def _order_masks(n_rows, blk, fwd):
    ii = lax.broadcasted_iota(jnp.int32, (n_rows, n_rows), 0)
    jj = lax.broadcasted_iota(jnp.int32, (n_rows, n_rows), 1)
    same = (ii // blk) == (jj // blk)
    before = (jj < ii) if fwd else (jj > ii)
    strict = jnp.logical_and(same, before)
    incl = jnp.logical_and(same, jnp.logical_or(before, ii == jj))
    return same, strict, incl


def _even_scan_kernel(rvkf_ref, cumf_ref, kdf_ref, bdf_ref, qkvf_ref, gbf_ref,
                      rvkb_ref, cumb_ref, kdb_ref, bdb_ref, qkvb_ref, gbb_ref,
                      yf_ref, of_ref, yb_ref, ob_ref, sa_ref, sb_ref):
    @pl.when(pl.program_id(1) == 0)
    def _():
        sa_ref[...] = jnp.zeros_like(sa_ref)
        sb_ref[...] = jnp.zeros_like(sb_ref)

    na, nb = A_W // LANE, B_H // 2
    sa, sb = sa_ref[...], sb_ref[...]
    chains_a = (_rwkv_operands(rvkf_ref[0], cumf_ref[0], kdf_ref[0], bdf_ref[0], True)
                + _rwkv_operands(rvkb_ref[0], cumb_ref[0], kdb_ref[0], bdb_ref[0], False))
    chains_b = _gdn_operands(qkvf_ref[0], gbf_ref[0], 0) + _gdn_operands(qkvb_ref[0], gbb_ref[0], 1)
    states_a = [sa[d, p] for d in range(2) for p in range(na)]
    states_b = [(sb[d, 2 * p], sb[d, 2 * p + 1]) for d in range(2) for p in range(nb)]
    res_a, res_b = {}, {}
    stages = [_rwkv_solve(chains_a, states_a, RWKV_PASSES, res_a), _gdn_solve(chains_b, states_b, GDN_PASSES, res_b)]
    while stages:
        stages = [s for s in stages if next(s, "done") != "done"]
    yf_ref[0] = jnp.concatenate(res_a["y"][0:na], axis=1)
    yb_ref[0] = jnp.concatenate(res_a["y"][na:2 * na], axis=1)
    of_ref[0] = jnp.concatenate([o for pair in res_b["o"][0:nb] for o in pair], axis=1)
    ob_ref[0] = jnp.concatenate([o for pair in res_b["o"][nb:2 * nb] for o in pair], axis=1)
    for i, s_i in enumerate(res_a["s"]):
        sa_ref[i // na, i % na] = s_i
    for i, pair in enumerate(res_b["s"]):
        for j, s_h in enumerate(pair):
            sb_ref[i // nb, 2 * (i % nb) + j] = s_h


def _rwkv_operands(rvk, cum, kd, bd, fwd):
    L = CHUNK
    L2 = 2 * L
    _, strict, incl = _order_masks(L2, L, fwd)
    row = lax.broadcasted_iota(jnp.int32, (L, 1), 0)
    if fwd:
        tot = cum[L - 1:L]
        cum_ex = jnp.where(row == 0, 0.0, pltpu.roll(cum, 1, axis=0))
    else:
        tot = cum[0:1]
        cum_ex = jnp.where(row == L - 1, 0.0, pltpu.roll(cum, L - 1, axis=0))
    g_inv = jnp.exp(-cum)
    g_rest = jnp.exp(tot - cum)
    g_tot = jnp.exp(tot)
    w = A_W
    r_t = rvk[:, 0:w] * jnp.exp(cum)
    v = rvk[:, w:2 * w]
    a_t = rvk[:, 2 * w:3 * w] * jnp.exp(cum_ex)
    b_t, k_t = bd * g_inv, kd * g_inv
    b_g, k_g = bd * g_rest, kd * g_rest
    low = lax.broadcasted_iota(jnp.int32, (1, LANE), 1) < A_HD

    def expand(x):
        return jnp.concatenate([jnp.where(low, x, 0.0), jnp.where(low, 0.0, x)], axis=0)

    chains = []
    for p in range(A_W // LANE):
        sl = slice(p * LANE, (p + 1) * LANE)
        chains.append(dict(
            ev=expand(v[:, sl]),
            lhs=jnp.concatenate([expand(a_t[:, sl]), expand(r_t[:, sl])], axis=0),
            rhs=jnp.concatenate([expand(b_t[:, sl]), expand(k_t[:, sl])], axis=0),
            bk=jnp.concatenate([expand(b_g[:, sl]), expand(k_g[:, sl])], axis=0),
            g_col=jnp.transpose(jnp.broadcast_to(g_tot[:, sl], (LANE, LANE))),
            strict=strict, incl=incl))
    return chains


def _square_pairs(mats, passes):
    if passes != 1 or len(mats) % 2:
        return [_dot(m, m, NN, passes) for m in mats]
    out = []
    for m0, m1 in zip(mats[0::2], mats[1::2]):
        n = m0.shape[0]
        m0, m1 = m0.astype(BF16), m1.astype(BF16)
        zero = jnp.zeros_like(m0)
        diag = jnp.concatenate([jnp.concatenate([m0, zero], axis=1), jnp.concatenate([zero, m1], axis=1)], axis=0)
        sq = lax.dot_general(jnp.concatenate([m0, m1], axis=1), diag, NN, preferred_element_type=F32)
        out += [sq[:, 0:n], sq[:, n:2 * n]]
    return out


def _rwkv_solve(chains, states, passes, res):
    L = CHUNK
    L2 = 2 * L
    each = lambda fn, *cols: [fn(*args) for args in zip(*cols)]
    quad = each(lambda c: _dot(c["lhs"], c["rhs"], NT, passes["quad"]), chains)
    yield
    es = each(lambda c, s: _dot(c["lhs"], s, NN, passes["state_in"]), chains, states)
    yield
    n_ab = each(lambda c, q: jnp.where(c["strict"], q[0:L2, 0:L2], 0.0), chains, quad)
    x = each(lambda c, q, e: e[0:L2] + _dot(jnp.where(c["strict"], q[0:L2, L2:2 * L2], 0.0), c["ev"], NN,
                                            passes["apply"]), chains, quad, es)
    yield
    x = each(lambda n, xi: xi - _dot(n, xi, NN, passes["solve"]), n_ab, x)
    yield
    npow = n_ab
    for _ in range(int(math.log2(L)) - 1):
        npow = _square_pairs(npow, passes["square"])
        yield
        x = each(lambda n, xi: xi + _dot(n, xi, NN, passes["solve"]), npow, x)
        yield
    pv = each(lambda c, xi: jnp.concatenate([-xi, c["ev"]], axis=0), chains, x)
    y2 = each(lambda c, q, e, pvi: e[L2:2 * L2] + _dot(
        jnp.concatenate([jnp.where(c["incl"], q[L2:2 * L2, 0:L2], 0.0),
                         jnp.where(c["incl"], q[L2:2 * L2, L2:2 * L2], 0.0)], axis=1), pvi, NN, passes["apply"]),
        chains, quad, es, pv)
    res["y"] = [y[0:L] + y[L:L2] for y in y2]
    yield
    res["s"] = each(lambda c, s, pvi: s * c["g_col"] + _dot(c["bk"], pvi, TN, passes["state_out"]),
                    chains, states, pv)


def _chunk_index(b, c, fwd, n_ctx_chunk, n_lat_chunk, lat_groups):
    in_ctx = c < n_ctx_chunk
    cc = c if fwd else n_ctx_chunk - 1 - c
    lc = c - n_ctx_chunk if fwd else n_lat_chunk - 1 - (c - n_ctx_chunk)
    grp = jnp.where(in_ctx, lat_groups, b)
    chunk = jnp.where(in_ctx, b * n_ctx_chunk + cc, lc)
    return grp, chunk


def _scan_specs(widths, bsz, ctx_len, t):
    ncc, nlc = ctx_len // CHUNK, t // CHUNK

    def spec(width, lane_blk, fwd):
        def index(b, c):
            grp, ch = _chunk_index(b, c, fwd, ncc, nlc, bsz)
            return grp, ch, lane_blk
        return pl.BlockSpec((1, CHUNK, width), index)

    return [[spec(w, (0 if fwd else 1) if blk is None else blk, fwd) for w, blk in widths] for fwd in (True, False)]


def _even_scan(rvk, cum, kd, bd, qkv, gb, bsz, ctx_len):
    g, t, _ = rvk.shape
    ins_f, ins_b = _scan_specs([(3 * A_W, 0), (A_W, None), (A_W, None), (A_W, None), (3 * B_W, 0), (LANE, 0)],
                               bsz, ctx_len, t)
    outs_f, outs_b = _scan_specs([(A_W, 0), (B_W, 0)], bsz, ctx_len, t)
    shapes = [jax.ShapeDtypeStruct((g, t, A_W), F32), jax.ShapeDtypeStruct((g, t, B_W), F32)]
    return pl.pallas_call(
        _even_scan_kernel,
        grid=(bsz, (ctx_len + t) // CHUNK),
        in_specs=ins_f + ins_b,
        out_specs=outs_f + outs_b,
        out_shape=shapes + shapes,
        scratch_shapes=[pltpu.VMEM((2, A_W // LANE, LANE, LANE), F32), pltpu.VMEM((2, B_H, B_HD, B_HD), F32)],
        compiler_params=_cparams(("parallel", "arbitrary")), name="even_scan",
    )(rvk, cum, kd, bd, qkv, gb, rvk, cum, kd, bd, qkv, gb)


def _gdn_prep_kernel(pb_ref, pv_ref, nx_ref, ab_ref, cw_ref, alog_ref, dtb_ref, qkv_ref, gb_ref, buf,
                     *, tt, lat_groups, ntile):
    gi, i = pl.program_id(0), pl.program_id(1)
    is_lat = gi < lat_groups
    _fill_halo(buf, pb_ref[0], pv_ref, nx_ref, jnp.logical_and(is_lat, i > 0),
               jnp.logical_and(is_lat, i < ntile - 1), tt)
    half = B_CONV // 2
    acc = buf[SUB - half:SUB - half + tt, :] * cw_ref[0:1, :]
    for j in range(1, B_CONV):
        acc = acc + buf[SUB - half + j:SUB - half + j + tt, :] * cw_ref[j:j + 1, :]
    act = acc * _sigmoid(acc)
    outs = []
    for h in range(2 * B_H):
        xh = act[:, h * B_HD:(h + 1) * B_HD]
        xh = xh * lax.rsqrt(jnp.sum(xh * xh, axis=-1, keepdims=True) + 1e-6)
        if h < B_H:
            xh = xh * (B_HD ** -0.5)
        outs.append(xh)
    outs.append(act[:, 2 * B_W:3 * B_W])
    qkv_ref[0] = jnp.concatenate(outs, axis=1)
    ab = ab_ref[0]
    lane = lax.broadcasted_iota(jnp.int32, ab.shape, 1)
    gval = -jnp.exp(alog_ref[...]) * _softplus(ab + dtb_ref[...])
    gb_ref[0] = jnp.where(lane < 2 * B_H, gval, _sigmoid(ab))


def _gdn_prep(pb, zab, conv_w, alog_row, dtb_row, tt, lat_groups):
    g, t, wb = pb.shape
    tile = lambda width: pl.BlockSpec((1, tt, width), lambda gi, i: (gi, i, 0))
    full = lambda arr: pl.BlockSpec(arr.shape, lambda gi, i: (0,) * arr.ndim)
    prev, nxt = _halo_specs(wb, tt, t)
    ab_spec = pl.BlockSpec((1, tt, LANE), lambda gi, i: (gi, i, B_W // LANE))
    return pl.pallas_call(
        functools.partial(_gdn_prep_kernel, tt=tt, lat_groups=lat_groups, ntile=t // tt),
        grid=(g, t // tt),
        in_specs=[tile(wb), prev, nxt, ab_spec, full(conv_w), full(alog_row), full(dtb_row)],
        out_specs=[tile(wb), tile(LANE)],
        out_shape=[jax.ShapeDtypeStruct((g, t, wb), F32), jax.ShapeDtypeStruct((g, t, LANE), F32)],
        scratch_shapes=[pltpu.VMEM((tt + 2 * SUB, wb), F32)],
        compiler_params=_cparams(("parallel", "parallel")), name="gdn_prep",
    )(pb, pb, pb, zab, conv_w, alog_row, dtb_row)


def _gdn_operands(qkv, gb, d):
    fwd = d == 0
    L = CHUNK
    L2 = 2 * L
    same, strict, incl = _order_masks(L2, L, fwd)
    _, _, incl_rev = _order_masks(L2, L, not fwd)
    incl_bf = jnp.where(incl, 1.0, 0.0).astype(BF16)
    incl_t_bf = jnp.where(incl_rev, 1.0, 0.0).astype(BF16)
    same_bf = jnp.where(same, 1.0, 0.0).astype(BF16)
    lane = lax.broadcasted_iota(jnp.int32, gb.shape, 1)

    def column(idx):
        return jnp.sum(jnp.where(lane == idx, gb, 0.0), axis=-1, keepdims=True)

    chains = []
    for pr in range(B_H // 2):
        heads = (2 * pr, 2 * pr + 1)
        stack = lambda off: jnp.concatenate([qkv[:, off + h * B_HD:off + (h + 1) * B_HD] for h in heads], axis=0)
        q2, k2, v2 = stack(0), stack(B_W), stack(2 * B_W)
        gcol = jnp.concatenate([column(d * B_H + h) for h in heads], axis=0)
        bcol = jnp.concatenate([column(2 * B_H + d * B_H + h) for h in heads], axis=0)
        chains.append(dict(q2=q2, k2=k2, v2=v2, bcol=bcol, g3=_three_pieces(gcol), strict=strict, incl=incl,
                           sum_bf=jnp.concatenate([incl_bf, same_bf], axis=0), incl_t_bf=incl_t_bf))
    return chains


def _three_pieces(col):
    lane = lax.broadcasted_iota(jnp.int32, (col.shape[0], LANE), 1)
    p1 = col.astype(BF16).astype(F32)
    p2 = (col - p1).astype(BF16).astype(F32)
    p3 = col - p1 - p2
    return jnp.where(lane == 0, p1, jnp.where(lane == 1, p2, jnp.where(lane == 2, p3, 0.0))).astype(BF16)


def _gdn_solve(chains, states, passes, res):
    L = CHUNK
    L2 = 2 * L
    each = lambda fn, *cols: [fn(*args) for args in zip(*cols)]
    dg = functools.partial(lax.dot_general, preferred_element_type=F32)
    sums = each(lambda c: dg(c["sum_bf"], c["g3"], NN), chains)
    sums_t = each(lambda c: dg(c["g3"], c["incl_t_bf"], TN), chains)
    k2b = each(lambda c: c["k2"] * c["bcol"], chains)
    kq = each(lambda c, kb: _dot(jnp.concatenate([kb, c["q2"]], axis=0), c["k2"], NT, passes["quad"]), chains, k2b)
    yield
    gc_row = each(lambda m: jnp.sum(m[0:L2], axis=-1, keepdims=True), sums)
    g_tot = each(lambda m: jnp.sum(m[L2:2 * L2], axis=-1, keepdims=True), sums)
    gc_col = each(lambda m: m[0:1] + m[1:2] + m[2:3], sums_t)
    decay = each(lambda c, r, cl: jnp.exp(jnp.where(c["incl"], r - cl, -1e30)), chains, gc_row, gc_col)
    egc = each(jnp.exp, gc_row)
    l_mat = each(lambda c, m, dc: jnp.where(c["strict"], m[0:L2] * dc, 0.0), chains, kq, decay)
    a_int = each(lambda c, m, dc: jnp.where(c["incl"], m[L2:2 * L2] * dc, 0.0), chains, kq, decay)
    x = each(lambda c, kb, eg: jnp.concatenate([c["v2"] * c["bcol"], kb * eg], axis=1), chains, k2b, egc)
    yield
    x = each(lambda n, xi: xi - _dot(n, xi, NN, passes["solve"]), l_mat, x)
    yield
    npow = l_mat
    for _ in range(int(math.log2(L)) - 1):
        npow = _square_pairs(npow, passes["square"])
        yield
        x = each(lambda n, xi: xi + _dot(n, xi, NN, passes["solve"]), npow, x)
        yield
    qe = each(lambda c, eg: c["q2"] * eg, chains, egc)
    rows = (slice(0, L), slice(L, L2))
    vnew = each(lambda xi, st: [xi[r, 0:B_HD] - _dot(xi[r, B_HD:2 * B_HD], s, NN, passes["state"])
                                for r, s in zip(rows, st)], x, states)
    o1 = each(lambda qi, st: [_dot(qi[r], s, NN, passes["state"]) for r, s in zip(rows, st)], qe, states)
    yield
    o2 = each(lambda a, vn, o: jnp.concatenate(o, axis=0) + _dot(a, jnp.concatenate(vn, axis=0), NN,
                                                                 passes["apply"]), a_int, vnew, o1)
    res["o"] = [[o[r] for r in rows] for o in o2]
    k_rest = each(lambda c, gt, r: c["k2"] * jnp.exp(gt - r), chains, g_tot, gc_row)
    res["s"] = each(lambda st, gt, kr, vn: [s * jnp.exp(gt[r][0:1, :]) + _dot(kr[r], v, TN, passes["state"])
                                            for r, s, v in zip(rows, st, vn)], states, g_tot, k_rest, vnew)


def _mix_out_kernel(yaf_ref, yab_ref, bg_ref, lnw_ref, lnb_ref, bd_ref, obf_ref, obb_ref, z_ref, nw_ref,
                    w_ref, x_ref, mod_ref, o_ref, *, k):
    y = yaf_ref[0] + yab_ref[0]
    ones_bd = bd_ref[...]
    mean = _group_sum(y, ones_bd) * (1.0 / A_HD)
    yc = y - mean
    var = _group_sum(yc * yc, ones_bd) * (1.0 / A_HD)
    yn = yc * lax.rsqrt(var + A_LN_EPS) * lnw_ref[...] + lnb_ref[...]
    bg = bg_ref[0].astype(F32)
    out_a = (yn + bg[:, 0:A_W]) * bg[:, A_W:2 * A_W]
    o = obf_ref[0] + obb_ref[0]
    z = z_ref[0]
    outs = [out_a.astype(BF16)]
    for h in range(B_H):
        oh = o[:, h * B_HD:(h + 1) * B_HD]
        on = oh * lax.rsqrt(jnp.mean(oh * oh, axis=-1, keepdims=True) + NORM_EPS) * nw_ref[...]
        zh = z[:, h * B_HD:(h + 1) * B_HD]
        outs.append((on * (zh * _sigmoid(zh))).astype(BF16))
    mixed = jnp.concatenate(outs, axis=1)
    o_ref[0] = x_ref[0] + mod_ref[0, k:k + 1, :] * jnp.dot(mixed, w_ref[...], preferred_element_type=F32)


def _mix_out(ya, bg, ln_w, ln_b, ones_bd, ob, zab, norm_w, w_out, x, mod, k, tt):
    g, t, _ = bg.shape
    tile = lambda width: pl.BlockSpec((1, tt, width), lambda gi, i: (gi, i, 0))
    full = lambda arr: pl.BlockSpec(arr.shape, lambda gi, i: (0,) * arr.ndim)
    return pl.pallas_call(
        functools.partial(_mix_out_kernel, k=k),
        grid=(g, t // tt),
        in_specs=[tile(A_W), tile(A_W), tile(2 * A_W), full(ln_w), full(ln_b), full(ones_bd), tile(B_W),
                  tile(B_W), tile(B_W), full(norm_w), full(w_out), tile(D),
                  pl.BlockSpec((1, MOD_ROWS, D), lambda gi, i: (gi, 0, 0))],
        out_specs=tile(D),
        out_shape=jax.ShapeDtypeStruct((g, t, D), F32),
        compiler_params=_cparams(("parallel", "parallel")), name="mix_out_proj",
    )(ya[0], ya[1], bg, ln_w, ln_b, ones_bd, ob[0], ob[1], zab, norm_w, w_out, x, mod)


def _qkv_rope_kernel(x_ref, w_ref, cos_ref, sin_ref, o_ref, *, lat_groups, q_cols, rope_cols, seg):
    is_lat = pl.program_id(0) < lat_groups
    x = x_ref[0]
    lane = lax.broadcasted_iota(jnp.int32, (1, LANE), 1)
    first = (lane % (C_HD // 2)) < (C_HD // 4)
    quarter = C_HD // 4
    cos_l = jnp.where(is_lat, cos_ref[...], 1.0)
    sin_l = jnp.where(is_lat, sin_ref[...], 0.0)
    for start in range(0, w_ref.shape[1], seg):
        acc = jnp.dot(x, w_ref[:, start:start + seg], preferred_element_type=F32)
        if start >= rope_cols:
            o_ref[0, :, start:start + seg] = acc.astype(o_ref.dtype)
            continue
        scale = C_HD ** -0.5 if start < q_cols else 1.0
        cos, sin = cos_l * scale, sin_l * scale
        for blk in range(seg // LANE):
            xb = acc[:, blk * LANE:(blk + 1) * LANE]
            partner = jnp.where(first, pltpu.roll(xb, LANE - quarter, axis=1), pltpu.roll(xb, quarter, axis=1))
            col = start + blk * LANE
            o_ref[0, :, col:col + LANE] = (xb * cos + partner * sin).astype(o_ref.dtype)


def _qkv_rope(h, w, cos, sin, lat_groups, tm, seg, q_cols, rope_cols):
    g, t, kdim = h.shape
    n = w.shape[1]
    assert q_cols % seg == 0 and rope_cols % seg == 0 and n % seg == 0
    return pl.pallas_call(
        functools.partial(_qkv_rope_kernel, lat_groups=lat_groups, q_cols=q_cols, rope_cols=rope_cols, seg=seg),
        grid=(g, t // tm),
        in_specs=[pl.BlockSpec((1, tm, kdim), lambda gi, i: (gi, i, 0)),
                  pl.BlockSpec((kdim, n), lambda gi, i: (0, 0)),
                  pl.BlockSpec((tm, LANE), lambda gi, i: (i, 0)),
                  pl.BlockSpec((tm, LANE), lambda gi, i: (i, 0))],
        out_specs=pl.BlockSpec((1, tm, n), lambda gi, i: (gi, i, 0)),
        out_shape=jax.ShapeDtypeStruct((g, t, n), BF16),
        compiler_params=_cparams(("parallel", "parallel")), name="qkv_rope",
    )(h, w, cos, sin)


def _attn_kernel(sink_ref, q_ref, kp_ref, km_ref, kn_ref, kc_ref, o_ref, *, n_blocks, kv_w):
    n = pl.program_id(1)
    blk = C_BLOCK
    ii = lax.broadcasted_iota(jnp.int32, (blk, blk), 0)
    jj = lax.broadcasted_iota(jnp.int32, (blk, blk), 1)
    ok_prev = jnp.logical_and(jj >= ii, n > 0)
    ok_next = jnp.logical_and(jj <= ii, n < n_blocks - 1)
    n_ctx = kc_ref.shape[1]
    valid1 = jnp.concatenate([ok_prev, jnp.full((blk, blk), True), ok_next, jnp.full((blk, n_ctx), True)], axis=1)
    valid = jnp.concatenate([valid1] * C_GROUP, axis=0)
    low = lax.broadcasted_iota(jnp.int32, (1, LANE), 1) < C_HD
    row_head = lax.broadcasted_iota(jnp.int32, (C_GROUP * blk, 1), 0) // blk
    q = q_ref[0]
    kv = jnp.concatenate([kp_ref[0], km_ref[0], kn_ref[0], kc_ref[0]], axis=0)
    zero = jnp.zeros((), q.dtype)
    outs = []
    for h in range(C_KVH):
        kh = kv[:, h * LANE:(h + 1) * LANE]
        vh = kv[:, kv_w + h * LANE:kv_w + (h + 1) * LANE]
        qa = q[:, (2 * h) * LANE:(2 * h + 1) * LANE]
        qb = q[:, (2 * h + 1) * LANE:(2 * h + 2) * LANE]
        qs = jnp.concatenate([jnp.where(low, qa, zero), jnp.where(low, zero, qa),
                              jnp.where(low, qb, zero), jnp.where(low, zero, qb)], axis=0)
        s = lax.dot_general(qs, kh, NT, preferred_element_type=F32)
        s = jnp.where(valid, s, -1e30)
        sk = jnp.full((C_GROUP * blk, 1), 0.0, F32)
        for gq in range(C_GROUP):
            sk = jnp.where(row_head == gq, sink_ref[h * C_GROUP + gq], sk)
        m = jnp.maximum(jnp.max(s, axis=-1, keepdims=True), sk)
        p = jnp.exp(s - m)
        denom = jnp.sum(p, axis=-1, keepdims=True) + jnp.exp(sk - m)
        o = lax.dot_general(p.astype(vh.dtype), vh, NN, preferred_element_type=F32) / denom
        outs.append(jnp.where(low, o[0:blk], o[blk:2 * blk]).astype(o_ref.dtype))
        outs.append(jnp.where(low, o[2 * blk:3 * blk], o[3 * blk:4 * blk]).astype(o_ref.dtype))
    o_ref[0] = jnp.concatenate(outs, axis=1)


def _attention(qkv, sink, bsz, ctx_len):
    g, t, _ = qkv.shape
    q_w = C_QH * C_HD
    kv_w = C_KVH * LANE
    nq = q_w // (2 * kv_w)
    assert q_w % (2 * kv_w) == 0
    n_blocks = t // C_BLOCK
    cpb = ctx_len // C_BLOCK
    assert ctx_len % C_BLOCK == 0
    kvspec = lambda fn: pl.BlockSpec((1, C_BLOCK, 2 * kv_w), fn)
    return pl.pallas_call(
        functools.partial(_attn_kernel, n_blocks=n_blocks, kv_w=kv_w),
        grid=(bsz, n_blocks),
        in_specs=[pl.BlockSpec(memory_space=pltpu.SMEM),
                  pl.BlockSpec((1, C_BLOCK, q_w), lambda b, n: (b, n, 0)),
                  kvspec(lambda b, n: (b, jnp.maximum(n - 1, 0), nq)),
                  kvspec(lambda b, n: (b, n, nq)),
                  kvspec(lambda b, n: (b, jnp.minimum(n + 1, n_blocks - 1), nq)),
                  pl.BlockSpec((1, ctx_len, 2 * kv_w), lambda b, n: (bsz, b, nq))],
        out_specs=pl.BlockSpec((1, C_BLOCK, q_w), lambda b, n: (b, n, 0)),
        out_shape=jax.ShapeDtypeStruct((bsz, t, q_w), BF16),
        compiler_params=_cparams(("parallel", "parallel")), name="window_attn",
    )(sink, qkv, qkv, qkv, qkv, qkv)


def _lane_cumsum(x):
    n = x.shape[-1]
    lane = lax.broadcasted_iota(jnp.int32, x.shape, x.ndim - 1)
    k = 1
    while k < n:
        x = x + jnp.where(lane >= k, pltpu.roll(x, k, axis=x.ndim - 1), 0)
        k *= 2
    return x


def _route_kernel(lg_ref, rank_ref, aff_ref, *, cap, slot_stride):
    lg = lg_ref[0]
    m = jnp.max(lg, axis=0, keepdims=True)
    e = jnp.exp(lg - m)
    z = jnp.sum(e, axis=0, keepdims=True)
    aff = e / z
    key = (lg - m) - jnp.log(z)
    count_ge = lambda v: jnp.sum(jnp.where(key >= v, 1, 0), axis=-1, keepdims=True)

    def body(_, carry):
        lo, hi = carry
        mid = 0.5 * (lo + hi)
        ok = count_ge(mid) >= cap
        return jnp.where(ok, mid, lo), jnp.where(ok, hi, mid)

    lo0 = jnp.min(key, axis=-1, keepdims=True)
    lo, hi = lax.fori_loop(0, ROUTE_BISECT, body, (lo0, jnp.ones_like(lo0)))
    thr, found = lo, jnp.zeros(lo.shape, jnp.int32)
    for _ in range(ROUTE_SNAP):
        v = jnp.max(jnp.where(key < hi, key, -3e38), axis=-1, keepdims=True)
        ok = jnp.where(count_ge(v) >= cap, 1, 0)
        thr = jnp.where(jnp.logical_and(found == 0, ok == 1), v, thr)
        hi = jnp.where(jnp.logical_or(found == 1, ok == 1), hi, v)
        found = jnp.maximum(found, ok)
    gt = key > thr
    eq = key == thr
    need = cap - jnp.sum(jnp.where(gt, 1, 0), axis=-1, keepdims=True)
    take_eq = jnp.logical_and(eq, _lane_cumsum(jnp.where(eq, 1, 0)) <= need)
    sel = jnp.logical_or(gt, take_eq)
    slot = _lane_cumsum(jnp.where(sel, 1, 0)) - 1 + pl.program_id(0) * slot_stride
    rank_ref[0] = jnp.where(sel, slot, -1)
    aff_ref[0] = aff


def _route(lg, cap, slot_stride):
    g, e, t = lg.shape
    spec = pl.BlockSpec((1, e, t), lambda gi: (gi, 0, 0))
    return pl.pallas_call(
        functools.partial(_route_kernel, cap=cap, slot_stride=slot_stride),
        grid=(g,), in_specs=[spec], out_specs=[spec, spec],
        out_shape=[jax.ShapeDtypeStruct((g, e, t), jnp.int32), jax.ShapeDtypeStruct((g, e, t), F32)],
        compiler_params=_cparams(("parallel",)), name="route",
    )(lg)


def _expert_kernel(h_ref, rank_ref, aff_ref, w1_ref, w3_ref, w2_ref, o_ref, *, cap):
    e = pl.program_id(1)

    @pl.when(e == 0)
    def _():
        o_ref[...] = jnp.zeros_like(o_ref)

    rank = rank_ref[0, 0]
    t = rank.shape[-1]
    hit = lax.broadcasted_iota(jnp.int32, (cap, t), 0) == rank
    pick = jnp.where(hit, 1.0, 0.0).astype(BF16)
    gate = jnp.sum(jnp.where(hit, aff_ref[0, 0], 0.0), axis=-1, keepdims=True)
    xe = jnp.dot(pick, h_ref[0], preferred_element_type=F32).astype(BF16)
    h1 = jnp.dot(xe, w1_ref[0], preferred_element_type=F32)
    h3 = jnp.dot(xe, w3_ref[0], preferred_element_type=F32)
    hid = (h1 * _sigmoid(h1) * h3).astype(BF16)
    ye = (jnp.dot(hid, w2_ref[0], preferred_element_type=F32) * gate).astype(BF16)
    o_ref[0] += lax.dot_general(pick, ye, TN, preferred_element_type=F32)


def _experts(h, rank, aff, w1, w3, w2, cap):
    g, t, _ = h.shape
    n_exp, _, f = w1.shape
    sel = pl.BlockSpec((1, 1, 1, t), lambda gi, e: (gi, e, 0, 0))
    return pl.pallas_call(
        functools.partial(_expert_kernel, cap=cap),
        grid=(g, n_exp),
        in_specs=[pl.BlockSpec((1, t, D), lambda gi, e: (gi, 0, 0)), sel, sel,
                  pl.BlockSpec((1, D, f), lambda gi, e: (e, 0, 0)),
                  pl.BlockSpec((1, D, f), lambda gi, e: (e, 0, 0)),
                  pl.BlockSpec((1, f, D), lambda gi, e: (e, 0, 0))],
        out_specs=pl.BlockSpec((1, t, D), lambda gi, e: (gi, 0, 0)),
        out_shape=jax.ShapeDtypeStruct((g, t, D), F32),
        compiler_params=_cparams(("parallel", "arbitrary")), name="experts",
    )(h, rank.reshape(g, n_exp, 1, t), aff.reshape(g, n_exp, 1, t), w1, w3, w2)


def _moe(h, lg, w1, w3, w2, bsz, ctx_len, lat_groups_only):
    g, t, _ = h.shape
    cap_lat = EC_CAP * t // N_EXP
    rank, aff = _route(lg[:bsz], cap_lat, 0)
    if not lat_groups_only:
        cap_ctx = EC_CAP * ctx_len // N_EXP
        lg_c = lg[bsz].reshape(N_EXP, bsz, ctx_len).transpose(1, 0, 2)
        rank_c, aff_c = _route(lg_c, cap_ctx, cap_ctx)
        back = lambda a: a.transpose(1, 0, 2).reshape(1, N_EXP, t)
        rank = jnp.concatenate([rank, back(rank_c)], axis=0)
        aff = jnp.concatenate([aff, back(aff_c)], axis=0)
        assert cap_ctx * bsz == cap_lat
    else:
        h = h[:bsz] if h.shape[0] != bsz else h
    return _experts(h, rank, aff, w1, w3, w2, cap_lat)


def _block_diag2(w):
    z = jnp.zeros_like(w[0])
    return jnp.concatenate([jnp.concatenate([w[0], z], axis=1), jnp.concatenate([z, w[1]], axis=1)], axis=0)


def _rope_tables(t):
    quarter = C_HD // 4
    inv = ROPE_BASE ** (-jnp.arange(quarter, dtype=F32) / quarter)
    pos = jnp.arange(t)
    row = (pos // GRID_W).astype(F32)
    col = (pos % GRID_W).astype(F32)
    lane = np.arange(LANE)
    use_col = ((lane % C_HD) >= C_HD // 2)
    ang = jnp.where(use_col[None, :], col[:, None], row[:, None]) * inv[lane % quarter][None, :]
    sign = np.where((lane % (C_HD // 2)) < quarter, -1.0, 1.0).astype(np.float32)
    return jnp.cos(ang), jnp.sin(ang) * sign[None, :]


def kernel(x, c, ctx, c_ctx, ada_w, ada_b, norm_mix, norm_ffn, e_w_in, e_w_out, a_mu, a_w0, a_w2, a_a0, a_a2, a_g2,
           a_k_k, a_k_a, a_r_k, a_ln_w, a_ln_b, b_conv, b_a_log, b_dt_bias, b_norm, o_w_in, o_w_out, o_sink,
           moe_router, moe_w1, moe_w3, moe_w2, final_norm):
    bsz, t, _ = x.shape
    ctx_len = ctx.shape[1]
    depth = ada_w.shape[0]
    assert bsz * ctx_len == t and ctx_len % C_BLOCK == 0 and t % C_BLOCK == 0
    tt = ctx_len

    rows = 2 * SUB * ((bsz + 1 + 2 * SUB - 1) // (2 * SUB))
    cc = jnp.zeros((rows, D), F32).at[:bsz].set(c).at[bsz].set(c_ctx)
    mods = _adaln(cc, ada_w, ada_b)[:, :bsz + 1].reshape(depth, bsz + 1, N_MOD, D)
    mods = jnp.pad(mods, ((0, 0), (0, 0), (0, MOD_ROWS - N_MOD), (0, 0)))

    xs = jnp.concatenate([x, ctx.reshape(1, t, D)], axis=0)
    ones_bd = jnp.asarray(np.kron(np.eye(A_H), np.ones((A_HD, A_HD))), BF16)
    a_cols = e_w_in.shape[2] - (4 * B_W + 4 * B_H)
    cos_t, sin_t = _rope_tables(t)

    res = None
    for i in range(depth):
        j = i // 2
        mod = mods[i]
        x_new, h, _ = _norm(xs, norm_mix[i], mod, k=0, res=res, tt=tt)
        xs = xs if res is None else x_new
        if i % 2 == 0:
            w_in = e_w_in[j]
            split_ab = a_cols + 3 * B_W
            w_all = jnp.concatenate([w_in[:, :split_ab], w_in[:, split_ab + 4 * B_H:],
                                     w_in[:, split_ab:split_ab + 4 * B_H],
                                     jnp.zeros((D, LANE - 4 * B_H), F32)], axis=1).astype(BF16)
            pa, pb, zab = _mm_split(h, w_all, (a_cols, 3 * B_W, B_W + LANE), F32, 2 * tt)
            prm = {
                "mu": a_mu[j].reshape(1, -1), "w0": a_w0[j].reshape(1, -1), "w2": _block_diag2(a_w2[j]),
                "a0": a_a0[j].reshape(1, -1), "a2": _block_diag2(a_a2[j]), "g2": a_g2[j],
                "k_k": a_k_k[j].reshape(1, -1), "k_a": jnp.tile(a_k_a[j].reshape(1, -1), (1, 2)),
                "r_k": a_r_k[j].reshape(1, -1), "ones_bd": ones_bd,
            }
            rvk, cum, kd, bd, bg = _rwkv_prep(pa, prm, tt, bsz)
            pad_row = lambda v: jnp.pad(v.reshape(1, -1), ((0, 0), (0, LANE - v.size)))
            qkv_c, gb = _gdn_prep(pb, zab, b_conv[j], pad_row(b_a_log[j]), pad_row(b_dt_bias[j]), tt, bsz)
            ya_f, ob_f, ya_b, ob_b = _even_scan(rvk, cum, kd, bd, qkv_c, gb, bsz, ctx_len)
            xs = _mix_out((ya_f, ya_b), bg, a_ln_w[j].reshape(1, -1), a_ln_b[j].reshape(1, -1), ones_bd,
                          (ob_f, ob_b), zab, b_norm[j].reshape(1, -1), e_w_out[j].astype(BF16), xs, mod, 2, tt)
        else:
            w_in = o_w_in[j]
            q_w = C_QH * C_HD
            kv_cols = C_KVH * C_HD
            dup = lambda w: jnp.repeat(w.reshape(D, C_KVH, 1, C_HD), 2, axis=2).reshape(D, 2 * kv_cols)
            w_all = jnp.concatenate([w_in[:, :q_w], dup(w_in[:, q_w:q_w + kv_cols]), dup(w_in[:, q_w + kv_cols:])],
                                    axis=1).astype(BF16)
            tn = 2 * kv_cols
            qkv = _qkv_rope(h, w_all, cos_t, sin_t, bsz, 2 * tt, tn, q_w, q_w + tn)
            att = _attention(qkv, o_sink[j], bsz, ctx_len)
            x_lat = _mm_res(att, o_w_out[j].astype(BF16), xs, mod, 2, tt, groups=bsz)
            xs = x_lat if xs.shape[0] == bsz else jnp.concatenate([x_lat, xs[bsz:]], axis=0)
        last = i == depth - 1
        if last:
            xs = xs[:bsz]
            mod = mod[:bsz]
        _, hf, lg = _norm(xs, norm_ffn[i], mod, k=3, router_t=moe_router[i].T, tt=tt)
        delta = _moe(hf, lg, _to_bf16(moe_w1, i), _to_bf16(moe_w3, i), _to_bf16(moe_w2, i), bsz, ctx_len, last)
        res = (delta, mod, 5)
    _, out, _ = _norm(xs, final_norm, None, res=res, tt=tt, out_dtype=F32)
    return out
```

```python
import functools
import math

import jax
import jax.numpy as jnp
import numpy as np
from jax import lax
from jax.experimental import pallas as pl
from jax.experimental.pallas import tpu as pltpu

F32 = jnp.float32
BF16 = jnp.bfloat16
HIGHEST = lax.Precision.HIGHEST

D = 1024
N_MOD = 6
NORM_EPS = 1e-6
A_HD, A_H, A_W = 64, 8, 512
A_LN_EPS = 64e-5
B_HD, B_H, B_W = 128, 4, 512
B_CONV = 5
CHUNK = 64
C_HD, C_QH, C_KVH, C_GROUP = 64, 16, 4, 4
C_BLOCK = 128
ROPE_BASE = 10000.0
GRID_W = 64
N_EXP = 16
EC_CAP = 2
ROUTE_BISECT = 40
ROUTE_SNAP = 3
RWKV_PASSES = {"quad": 1, "apply": 1, "solve": "b", "square": 1, "state_in": "a", "state_out": "b"}
GDN_PASSES = {"quad": 1, "apply": 1, "solve": "b", "square": 1, "state": 1}
GATHER_TILE = 256
GATHER_WIN = 64
GATHER_ALIGN = 16
MOD_ROWS = 8
LANE = 128
SUB = 8
VMEM_LIMIT = 56 * 1024 * 1024

NT = (((1,), (1,)), ((), ()))
NN = (((1,), (0,)), ((), ()))
TN = (((0,), (0,)), ((), ()))


def _cparams(sem):
    return pltpu.CompilerParams(dimension_semantics=sem, vmem_limit_bytes=VMEM_LIMIT)


def _sigmoid(x):
    return 1.0 / (1.0 + jnp.exp(-x))


def _softplus(x):
    return jnp.maximum(x, 0.0) + jnp.log(1.0 + jnp.exp(-jnp.abs(x)))


def _split2(x):
    hi = x.astype(BF16)
    lo = (x - hi.astype(F32)).astype(BF16)
    return hi, lo


def _dot(a, b, dims=NN, passes=1):
    dg = functools.partial(lax.dot_general, dimension_numbers=dims, preferred_element_type=F32)
    if passes == 1:
        return dg(a.astype(BF16), b.astype(BF16))
    (ka,), (kb,) = dims[0]
    if passes == "a":
        ah, al = _split2(a)
        bh = b.astype(BF16)
        return dg(jnp.concatenate([ah, al], axis=ka), jnp.concatenate([bh, bh], axis=kb))
    if passes == "b":
        bh, bl = _split2(b)
        n = b.shape[1 - kb]
        res = dg(a.astype(BF16), jnp.concatenate([bh, bl], axis=1 - kb))
        return res[:, 0:n] + res[:, n:2 * n]
    ah, al = _split2(a)
    bh, bl = _split2(b)
    return dg(ah, bh) + (dg(ah, bl) + dg(al, bh))


def _group_sum(x, ones_bd):
    hi, lo = _split2(x)
    dg = functools.partial(lax.dot_general, dimension_numbers=NN, preferred_element_type=F32)
    return dg(hi, ones_bd) + dg(lo, ones_bd)


def _cast_kernel(w_ref, o_ref):
    o_ref[...] = w_ref[...].astype(o_ref.dtype)


def _to_bf16(w, layer):
    _, n, r, c = w.shape
    return pl.pallas_call(_cast_kernel, grid=(n,),
                          in_specs=[pl.BlockSpec((1, 1, r, c), lambda e: (layer, e, 0, 0))],
                          out_specs=pl.BlockSpec((1, 1, r, c), lambda e: (0, e, 0, 0)),
                          out_shape=jax.ShapeDtypeStruct((1, n, r, c), BF16),
                          compiler_params=_cparams(("parallel",)), name="cast_bf16")(w)[0]


def _adaln_kernel(c_ref, w_ref, b_ref, o_ref):
    c = c_ref[...]
    s = c * _sigmoid(c)
    o_ref[0] = jnp.dot(s, w_ref[0], precision=HIGHEST, preferred_element_type=F32) + b_ref[0]


def _adaln(cc, ada_w, ada_b):
    depth, _, n = ada_w.shape
    rows = cc.shape[0]
    tn = 768
    return pl.pallas_call(
        _adaln_kernel,
        grid=(depth, n // tn),
        in_specs=[pl.BlockSpec((rows, D), lambda i, j: (0, 0)),
                  pl.BlockSpec((1, D, tn), lambda i, j: (i, 0, j)),
                  pl.BlockSpec((1, 1, tn), lambda i, j: (i, 0, j))],
        out_specs=pl.BlockSpec((1, rows, tn), lambda i, j: (i, 0, j)),
        out_shape=jax.ShapeDtypeStruct((depth, rows, n), F32),
        compiler_params=_cparams(("parallel", "parallel")),
        name="adaln",
    )(cc, ada_w, ada_b.reshape(depth, 1, n))


def _norm_kernel(*refs, k, kres, has_res, has_mod, has_router):
    it = iter(refs)
    x_ref = next(it)
    d_ref = next(it) if has_res else None
    mres_ref = next(it) if has_res else None
    gain_ref = next(it)
    mod_ref = next(it) if has_mod else None
    rt_ref = next(it) if has_router else None
    xo_ref = next(it) if has_res else None
    h_ref = next(it)
    lg_ref = next(it) if has_router else None

    x = x_ref[0]
    if has_res:
        x = x + mres_ref[0, kres:kres + 1, :] * d_ref[0]
        xo_ref[0] = x
    xn = x * lax.rsqrt(jnp.mean(x * x, axis=-1, keepdims=True) + NORM_EPS) * gain_ref[...]
    if has_mod:
        xn = xn * (1.0 + mod_ref[0, k + 1:k + 2, :]) + mod_ref[0, k:k + 1, :]
    h_ref[0] = xn.astype(h_ref.dtype)
    if has_router:
        lg_ref[0] = lax.dot_general(rt_ref[...], xn, NT, precision=HIGHEST, preferred_element_type=F32)


def _norm(x, gain, mod=None, *, k=0, res=None, router_t=None, tt, out_dtype=BF16):
    g, t, _ = x.shape
    has_res, has_mod, has_router = res is not None, mod is not None, router_t is not None
    kres = res[2] if has_res else 0
    tile = pl.BlockSpec((1, tt, D), lambda gi, i: (gi, i, 0))
    modspec = pl.BlockSpec((1, MOD_ROWS, D), lambda gi, i: (gi, 0, 0))
    ins, specs = [x], [tile]
    if has_res:
        ins += [res[0], res[1]]
        specs += [tile, modspec]
    ins.append(gain.reshape(1, D))
    specs.append(pl.BlockSpec((1, D), lambda gi, i: (0, 0)))
    if has_mod:
        ins.append(mod)
        specs.append(modspec)
    if has_router:
        ins.append(router_t)
        specs.append(pl.BlockSpec((N_EXP, D), lambda gi, i: (0, 0)))
    outs, ospecs = [], []
    if has_res:
        outs.append(jax.ShapeDtypeStruct((g, t, D), F32))
        ospecs.append(tile)
    outs.append(jax.ShapeDtypeStruct((g, t, D), out_dtype))
    ospecs.append(tile)
    if has_router:
        outs.append(jax.ShapeDtypeStruct((g, N_EXP, t), F32))
        ospecs.append(pl.BlockSpec((1, N_EXP, tt), lambda gi, i: (gi, 0, i)))
    res_out = list(pl.pallas_call(
        functools.partial(_norm_kernel, k=k, kres=kres, has_res=has_res, has_mod=has_mod, has_router=has_router),
        grid=(g, t // tt), in_specs=specs, out_specs=ospecs, out_shape=outs,
        compiler_params=_cparams(("parallel", "parallel")), name="norm_mod",
    )(*ins))
    x_new = res_out.pop(0) if has_res else None
    h = res_out.pop(0)
    lg = res_out.pop(0) if has_router else None
    return x_new, h, lg


def _mm_split_kernel(x_ref, w_ref, *o_refs, widths):
    x = x_ref[0]
    start = 0
    for o_ref, width in zip(o_refs, widths):
        o_ref[0] = jnp.dot(x, w_ref[:, start:start + width], preferred_element_type=F32).astype(o_ref.dtype)
        start += width


def _mm_split(x, w, widths, out_dtype, tm):
    g, t, kdim = x.shape
    assert sum(widths) == w.shape[1] and all(wd % LANE == 0 for wd in widths)
    return pl.pallas_call(
        functools.partial(_mm_split_kernel, widths=widths),
        grid=(g, t // tm),
        in_specs=[pl.BlockSpec((1, tm, kdim), lambda gi, i: (gi, i, 0)),
                  pl.BlockSpec(w.shape, lambda gi, i: (0, 0))],
        out_specs=[pl.BlockSpec((1, tm, wd), lambda gi, i: (gi, i, 0)) for wd in widths],
        out_shape=[jax.ShapeDtypeStruct((g, t, wd), out_dtype) for wd in widths],
        compiler_params=_cparams(("parallel", "parallel")), name="proj",
    )(x, w)


def _mm_res_kernel(y_ref, w_ref, x_ref, mod_ref, o_ref, *, k):
    acc = jnp.dot(y_ref[0], w_ref[...], preferred_element_type=F32)
    o_ref[0] = x_ref[0] + mod_ref[0, k:k + 1, :] * acc


def _mm_res(y, w, x, mod, k, tm, groups=None):
    g_all, t, kdim = y.shape
    g = g_all if groups is None else groups
    return pl.pallas_call(
        functools.partial(_mm_res_kernel, k=k),
        grid=(g, t // tm),
        in_specs=[pl.BlockSpec((1, tm, kdim), lambda gi, i: (gi, i, 0)),
                  pl.BlockSpec((kdim, D), lambda gi, i: (0, 0)),
                  pl.BlockSpec((1, tm, D), lambda gi, i: (gi, i, 0)),
                  pl.BlockSpec((1, MOD_ROWS, D), lambda gi, i: (gi, 0, 0))],
        out_specs=pl.BlockSpec((1, tm, D), lambda gi, i: (gi, i, 0)),
        out_shape=jax.ShapeDtypeStruct((g, t, D), F32),
        compiler_params=_cparams(("parallel", "parallel")), name="out_proj_res",
    )(y, w, x, mod)


def _fill_halo(buf, x, pv_ref, nx_ref, has_prev, has_next, tt):
    buf[0:SUB, :] = jnp.where(has_prev, pv_ref[0], 0.0)
    buf[SUB:SUB + tt, :] = x
    buf[SUB + tt:2 * SUB + tt, :] = jnp.where(has_next, nx_ref[0], 0.0)


def _rwkv_prep_kernel(pa_ref, pv_ref, nx_ref, mu_ref, w0_ref, w2_ref, a0_ref, a2_ref, g2_ref, kk_ref, ka_ref,
                      rk_ref, bd_ref, rvk_ref, cum_ref, kd_ref, bdir_ref, bg_ref, buf, *, tt, lat_groups, ntile):
    gi, i = pl.program_id(0), pl.program_id(1)
    is_lat = gi < lat_groups
    x = pa_ref[0]
    _fill_halo(buf, x, pv_ref, nx_ref, jnp.logical_and(is_lat, i > 0), jnp.logical_and(is_lat, i < ntile - 1), tt)
    xm = buf[SUB - 1:SUB - 1 + tt, :]
    xp = buf[SUB + 1:SUB + 1 + tt, :]
    x = x + mu_ref[...] * (0.5 * (xm + xp) - x)
    w = A_W
    r, kx, v = x[:, 0:w], x[:, w:2 * w], x[:, 2 * w:3 * w]
    wd, ad, gd = x[:, 3 * w:3 * w + LANE], x[:, 3 * w + LANE:3 * w + 2 * LANE], x[:, 3 * w + 2 * LANE:3 * w + 3 * LANE]
    w_log = -_softplus(-(w0_ref[...] + _dot(jnp.tanh(wd), w2_ref[...], NN, 3))) - 0.5
    lw = -jnp.exp(w_log)
    a = _sigmoid(a0_ref[...] + _dot(ad, a2_ref[...], NN, 3))
    gate = _dot(_sigmoid(gd), g2_ref[...], NN, 3)
    ones_bd = bd_ref[...]
    kq = kx * kk_ref[...]
    kk = kq * lax.rsqrt(_group_sum(kq * kq, ones_bd) + 1e-6)
    k2 = jnp.concatenate([kx, kx], axis=1)
    kdir = k2 * (1.0 + (a - 1.0) * ka_ref[...])
    bdir = jnp.concatenate([kk, kk], axis=1) * a
    rr = r * rk_ref[...]
    bonus = _group_sum(rr * (kdir[:, 0:w] + kdir[:, w:2 * w]), ones_bd) * v
    rvk_ref[0] = jnp.concatenate([r, v, kk], axis=1)
    cum_ref[0] = jnp.concatenate([_chunk_cumsum(lw[:, 0:w], True), _chunk_cumsum(lw[:, w:2 * w], False)], axis=1)
    kd_ref[0] = kdir.astype(kd_ref.dtype)
    bdir_ref[0] = bdir.astype(bdir_ref.dtype)
    bg_ref[0] = jnp.concatenate([bonus, gate], axis=1).astype(bg_ref.dtype)


def _chunk_cumsum(x, fwd):
    _, _, incl = _order_masks(x.shape[0], CHUNK, fwd)
    mask = jnp.where(incl, 1.0, 0.0).astype(BF16)
    p1 = x.astype(BF16)
    r1 = x - p1.astype(F32)
    p2 = r1.astype(BF16)
    p3 = (r1 - p2.astype(F32)).astype(BF16)
    dg = functools.partial(lax.dot_general, dimension_numbers=NN, preferred_element_type=F32)
    return dg(mask, p1) + (dg(mask, p2) + dg(mask, p3))


def _halo_specs(width, tt, t):
    nb = tt // SUB
    last = t // SUB - 1
    prev = pl.BlockSpec((1, SUB, width), lambda gi, i: (gi, jnp.maximum(i * nb - 1, 0), 0))
    nxt = pl.BlockSpec((1, SUB, width), lambda gi, i: (gi, jnp.minimum((i + 1) * nb, last), 0))
    return prev, nxt


def _rwkv_prep(pa, p, tt, lat_groups):
    g, t, wa = pa.shape
    full = lambda arr: pl.BlockSpec(arr.shape, lambda gi, i: (0,) * arr.ndim)
    tile = lambda width: pl.BlockSpec((1, tt, width), lambda gi, i: (gi, i, 0))
    prev, nxt = _halo_specs(wa, tt, t)
    consts = [p["mu"], p["w0"], p["w2"], p["a0"], p["a2"], p["g2"], p["k_k"], p["k_a"], p["r_k"], p["ones_bd"]]
    widths = (3 * A_W, 2 * A_W, 2 * A_W, 2 * A_W, 2 * A_W)
    return pl.pallas_call(
        functools.partial(_rwkv_prep_kernel, tt=tt, lat_groups=lat_groups, ntile=t // tt),
        grid=(g, t // tt),
        in_specs=[tile(wa), prev, nxt] + [full(c) for c in consts],
        out_specs=[tile(wd) for wd in widths],
        out_shape=[jax.ShapeDtypeStruct((g, t, wd), dt) for wd, dt in zip(widths, (F32, F32, BF16, BF16, BF16))],
        scratch_shapes=[pltpu.VMEM((tt + 2 * SUB, wa), F32)],
        compiler_params=_cparams(("parallel", "parallel")), name="rwkv_prep",
    )(pa, pa, pa, *consts)


def _order_masks(n_rows, blk, fwd):
    ii = lax.broadcasted_iota(jnp.int32, (n_rows, n_rows), 0)
    jj = lax.broadcasted_iota(jnp.int32, (n_rows, n_rows), 1)
    same = (ii // blk) == (jj // blk)
    before = (jj < ii) if fwd else (jj > ii)
    strict = jnp.logical_and(same, before)
    incl = jnp.logical_and(same, jnp.logical_or(before, ii == jj))
    return same, strict, incl


def _even_scan_kernel(rvkf_ref, cumf_ref, kdf_ref, bdf_ref, qkvf_ref, gbf_ref,
                      rvkb_ref, cumb_ref, kdb_ref, bdb_ref, qkvb_ref, gbb_ref,
                      yf_ref, of_ref, yb_ref, ob_ref, sa_ref, sb_ref):
    @pl.when(pl.program_id(1) == 0)
    def _():
        sa_ref[...] = jnp.zeros_like(sa_ref)
        sb_ref[...] = jnp.zeros_like(sb_ref)

    na, nb = A_W // LANE, B_H // 2
    sa, sb = sa_ref[...], sb_ref[...]
    chains_a = (_rwkv_operands(rvkf_ref[0], cumf_ref[0], kdf_ref[0], bdf_ref[0], True)
                + _rwkv_operands(rvkb_ref[0], cumb_ref[0], kdb_ref[0], bdb_ref[0], False))
    chains_b = _gdn_operands(qkvf_ref[0], gbf_ref[0], 0) + _gdn_operands(qkvb_ref[0], gbb_ref[0], 1)
    states_a = [sa[d, p] for d in range(2) for p in range(na)]
    states_b = [(sb[d, 2 * p], sb[d, 2 * p + 1]) for d in range(2) for p in range(nb)]
    res_a, res_b = {}, {}
    stages = [_rwkv_solve(chains_a, states_a, RWKV_PASSES, res_a), _gdn_solve(chains_b, states_b, GDN_PASSES, res_b)]
    while stages:
        stages = [s for s in stages if next(s, "done") != "done"]
    yf_ref[0] = jnp.concatenate(res_a["y"][0:na], axis=1)
    yb_ref[0] = jnp.concatenate(res_a["y"][na:2 * na], axis=1)
    of_ref[0] = jnp.concatenate([o for pair in res_b["o"][0:nb] for o in pair], axis=1)
    ob_ref[0] = jnp.concatenate([o for pair in res_b["o"][nb:2 * nb] for o in pair], axis=1)
    for i, s_i in enumerate(res_a["s"]):
        sa_ref[i // na, i % na] = s_i
    for i, pair in enumerate(res_b["s"]):
        for j, s_h in enumerate(pair):
            sb_ref[i // nb, 2 * (i % nb) + j] = s_h


def _rwkv_operands(rvk, cum, kd, bd, fwd):
    L = CHUNK
    L2 = 2 * L
    _, strict, incl = _order_masks(L2, L, fwd)
    row = lax.broadcasted_iota(jnp.int32, (L, 1), 0)
    if fwd:
        tot = cum[L - 1:L]
        cum_ex = jnp.where(row == 0, 0.0, pltpu.roll(cum, 1, axis=0))
    else:
        tot = cum[0:1]
        cum_ex = jnp.where(row == L - 1, 0.0, pltpu.roll(cum, L - 1, axis=0))
    g_inv = jnp.exp(-cum)
    g_rest = jnp.exp(tot - cum)
    g_tot = jnp.exp(tot)
    w = A_W
    r_t = rvk[:, 0:w] * jnp.exp(cum)
    v = rvk[:, w:2 * w]
    a_t = rvk[:, 2 * w:3 * w] * jnp.exp(cum_ex)
    b_t, k_t = bd * g_inv, kd * g_inv
    b_g, k_g = bd * g_rest, kd * g_rest
    low = lax.broadcasted_iota(jnp.int32, (1, LANE), 1) < A_HD

    def expand(x):
        return jnp.concatenate([jnp.where(low, x, 0.0), jnp.where(low, 0.0, x)], axis=0)

    chains = []
    for p in range(A_W // LANE):
        sl = slice(p * LANE, (p + 1) * LANE)
        chains.append(dict(
            ev=expand(v[:, sl]),
            lhs=jnp.concatenate([expand(a_t[:, sl]), expand(r_t[:, sl])], axis=0),
            rhs=jnp.concatenate([expand(b_t[:, sl]), expand(k_t[:, sl])], axis=0),
            bk=jnp.concatenate([expand(b_g[:, sl]), expand(k_g[:, sl])], axis=0),
            g_col=jnp.transpose(jnp.broadcast_to(g_tot[:, sl], (LANE, LANE))),
            strict=strict, incl=incl))
    return chains


def _square_pairs(mats, passes):
    if passes != 1 or len(mats) % 2:
        return [_dot(m, m, NN, passes) for m in mats]
    out = []
    for m0, m1 in zip(mats[0::2], mats[1::2]):
        n = m0.shape[0]
        m0, m1 = m0.astype(BF16), m1.astype(BF16)
        zero = jnp.zeros_like(m0)
        diag = jnp.concatenate([jnp.concatenate([m0, zero], axis=1), jnp.concatenate([zero, m1], axis=1)], axis=0)
        sq = lax.dot_general(jnp.concatenate([m0, m1], axis=1), diag, NN, preferred_element_type=F32)
        out += [sq[:, 0:n], sq[:, n:2 * n]]
    return out


def _rwkv_solve(chains, states, passes, res):
    L = CHUNK
    L2 = 2 * L
    each = lambda fn, *cols: [fn(*args) for args in zip(*cols)]
    quad = each(lambda c: _dot(c["lhs"], c["rhs"], NT, passes["quad"]), chains)
    yield
    es = each(lambda c, s: _dot(c["lhs"], s, NN, passes["state_in"]), chains, states)
    yield
    n_ab = each(lambda c, q: jnp.where(c["strict"], q[0:L2, 0:L2], 0.0), chains, quad)
    x = each(lambda c, q, e: e[0:L2] + _dot(jnp.where(c["strict"], q[0:L2, L2:2 * L2], 0.0), c["ev"], NN,
                                            passes["apply"]), chains, quad, es)
    yield
    x = each(lambda n, xi: xi - _dot(n, xi, NN, passes["solve"]), n_ab, x)
    yield
    npow = n_ab
    for _ in range(int(math.log2(L)) - 1):
        npow = _square_pairs(npow, passes["square"])
        yield
        x = each(lambda n, xi: xi + _dot(n, xi, NN, passes["solve"]), npow, x)
        yield
    pv = each(lambda c, xi: jnp.concatenate([-xi, c["ev"]], axis=0), chains, x)
    y2 = each(lambda c, q, e, pvi: e[L2:2 * L2] + _dot(
        jnp.concatenate([jnp.where(c["incl"], q[L2:2 * L2, 0:L2], 0.0),
                         jnp.where(c["incl"], q[L2:2 * L2, L2:2 * L2], 0.0)], axis=1), pvi, NN, passes["apply"]),
        chains, quad, es, pv)
    res["y"] = [y[0:L] + y[L:L2] for y in y2]
    yield
    res["s"] = each(lambda c, s, pvi: s * c["g_col"] + _dot(c["bk"], pvi, TN, passes["state_out"]),
                    chains, states, pv)


def _chunk_index(b, c, fwd, n_ctx_chunk, n_lat_chunk, lat_groups):
    in_ctx = c < n_ctx_chunk
    cc = c if fwd else n_ctx_chunk - 1 - c
    lc = c - n_ctx_chunk if fwd else n_lat_chunk - 1 - (c - n_ctx_chunk)
    grp = jnp.where(in_ctx, lat_groups, b)
    chunk = jnp.where(in_ctx, b * n_ctx_chunk + cc, lc)
    return grp, chunk


def _scan_specs(widths, bsz, ctx_len, t):
    ncc, nlc = ctx_len // CHUNK, t // CHUNK

    def spec(width, lane_blk, fwd):
        def index(b, c):
            grp, ch = _chunk_index(b, c, fwd, ncc, nlc, bsz)
            return grp, ch, lane_blk
        return pl.BlockSpec((1, CHUNK, width), index)

    return [[spec(w, (0 if fwd else 1) if blk is None else blk, fwd) for w, blk in widths] for fwd in (True, False)]


def _even_scan(rvk, cum, kd, bd, qkv, gb, bsz, ctx_len):
    g, t, _ = rvk.shape
    ins_f, ins_b = _scan_specs([(3 * A_W, 0), (A_W, None), (A_W, None), (A_W, None), (3 * B_W, 0), (LANE, 0)],
                               bsz, ctx_len, t)
    outs_f, outs_b = _scan_specs([(A_W, 0), (B_W, 0)], bsz, ctx_len, t)
    shapes = [jax.ShapeDtypeStruct((g, t, A_W), F32), jax.ShapeDtypeStruct((g, t, B_W), F32)]
    return pl.pallas_call(
        _even_scan_kernel,
        grid=(bsz, (ctx_len + t) // CHUNK),
        in_specs=ins_f + ins_b,
        out_specs=outs_f + outs_b,
        out_shape=shapes + shapes,
        scratch_shapes=[pltpu.VMEM((2, A_W // LANE, LANE, LANE), F32), pltpu.VMEM((2, B_H, B_HD, B_HD), F32)],
        compiler_params=_cparams(("parallel", "arbitrary")), name="even_scan",
    )(rvk, cum, kd, bd, qkv, gb, rvk, cum, kd, bd, qkv, gb)


def _gdn_prep_kernel(pb_ref, pv_ref, nx_ref, ab_ref, cw_ref, alog_ref, dtb_ref, qkv_ref, gb_ref, buf,
                     *, tt, lat_groups, ntile):
    gi, i = pl.program_id(0), pl.program_id(1)
    is_lat = gi < lat_groups
    _fill_halo(buf, pb_ref[0], pv_ref, nx_ref, jnp.logical_and(is_lat, i > 0),
               jnp.logical_and(is_lat, i < ntile - 1), tt)
    half = B_CONV // 2
    acc = buf[SUB - half:SUB - half + tt, :] * cw_ref[0:1, :]
    for j in range(1, B_CONV):
        acc = acc + buf[SUB - half + j:SUB - half + j + tt, :] * cw_ref[j:j + 1, :]
    act = acc * _sigmoid(acc)
    outs = []
    for h in range(2 * B_H):
        xh = act[:, h * B_HD:(h + 1) * B_HD]
        xh = xh * lax.rsqrt(jnp.sum(xh * xh, axis=-1, keepdims=True) + 1e-6)
        if h < B_H:
            xh = xh * (B_HD ** -0.5)
        outs.append(xh)
    outs.append(act[:, 2 * B_W:3 * B_W])
    qkv_ref[0] = jnp.concatenate(outs, axis=1)
    ab = ab_ref[0]
    lane = lax.broadcasted_iota(jnp.int32, ab.shape, 1)
    gval = -jnp.exp(alog_ref[...]) * _softplus(ab + dtb_ref[...])
    gb_ref[0] = jnp.where(lane < 2 * B_H, gval, _sigmoid(ab))


def _gdn_prep(pb, zab, conv_w, alog_row, dtb_row, tt, lat_groups):
    g, t, wb = pb.shape
    tile = lambda width: pl.BlockSpec((1, tt, width), lambda gi, i: (gi, i, 0))
    full = lambda arr: pl.BlockSpec(arr.shape, lambda gi, i: (0,) * arr.ndim)
    prev, nxt = _halo_specs(wb, tt, t)
    ab_spec = pl.BlockSpec((1, tt, LANE), lambda gi, i: (gi, i, B_W // LANE))
    return pl.pallas_call(
        functools.partial(_gdn_prep_kernel, tt=tt, lat_groups=lat_groups, ntile=t // tt),
        grid=(g, t // tt),
        in_specs=[tile(wb), prev, nxt, ab_spec, full(conv_w), full(alog_row), full(dtb_row)],
        out_specs=[tile(wb), tile(LANE)],
        out_shape=[jax.ShapeDtypeStruct((g, t, wb), F32), jax.ShapeDtypeStruct((g, t, LANE), F32)],
        scratch_shapes=[pltpu.VMEM((tt + 2 * SUB, wb), F32)],
        compiler_params=_cparams(("parallel", "parallel")), name="gdn_prep",
    )(pb, pb, pb, zab, conv_w, alog_row, dtb_row)


def _gdn_operands(qkv, gb, d):
    fwd = d == 0
    L = CHUNK
    L2 = 2 * L
    same, strict, incl = _order_masks(L2, L, fwd)
    _, _, incl_rev = _order_masks(L2, L, not fwd)
    incl_bf = jnp.where(incl, 1.0, 0.0).astype(BF16)
    incl_t_bf = jnp.where(incl_rev, 1.0, 0.0).astype(BF16)
    same_bf = jnp.where(same, 1.0, 0.0).astype(BF16)
    lane = lax.broadcasted_iota(jnp.int32, gb.shape, 1)

    def column(idx):
        return jnp.sum(jnp.where(lane == idx, gb, 0.0), axis=-1, keepdims=True)

    chains = []
    for pr in range(B_H // 2):
        heads = (2 * pr, 2 * pr + 1)
        stack = lambda off: jnp.concatenate([qkv[:, off + h * B_HD:off + (h + 1) * B_HD] for h in heads], axis=0)
        q2, k2, v2 = stack(0), stack(B_W), stack(2 * B_W)
        gcol = jnp.concatenate([column(d * B_H + h) for h in heads], axis=0)
        bcol = jnp.concatenate([column(2 * B_H + d * B_H + h) for h in heads], axis=0)
        chains.append(dict(q2=q2, k2=k2, v2=v2, bcol=bcol, g3=_three_pieces(gcol), strict=strict, incl=incl,
                           sum_bf=jnp.concatenate([incl_bf, same_bf], axis=0), incl_t_bf=incl_t_bf))
    return chains


def _three_pieces(col):
    lane = lax.broadcasted_iota(jnp.int32, (col.shape[0], LANE), 1)
    p1 = col.astype(BF16).astype(F32)
    p2 = (col - p1).astype(BF16).astype(F32)
    p3 = col - p1 - p2
    return jnp.where(lane == 0, p1, jnp.where(lane == 1, p2, jnp.where(lane == 2, p3, 0.0))).astype(BF16)


def _gdn_solve(chains, states, passes, res):
    L = CHUNK
    L2 = 2 * L
    each = lambda fn, *cols: [fn(*args) for args in zip(*cols)]
    dg = functools.partial(lax.dot_general, preferred_element_type=F32)
    sums = each(lambda c: dg(c["sum_bf"], c["g3"], NN), chains)
    sums_t = each(lambda c: dg(c["g3"], c["incl_t_bf"], TN), chains)
    k2b = each(lambda c: c["k2"] * c["bcol"], chains)
    kq = each(lambda c, kb: _dot(jnp.concatenate([kb, c["q2"]], axis=0), c["k2"], NT, passes["quad"]), chains, k2b)
    yield
    gc_row = each(lambda m: jnp.sum(m[0:L2], axis=-1, keepdims=True), sums)
    g_tot = each(lambda m: jnp.sum(m[L2:2 * L2], axis=-1, keepdims=True), sums)
    gc_col = each(lambda m: m[0:1] + m[1:2] + m[2:3], sums_t)
    decay = each(lambda c, r, cl: jnp.exp(jnp.where(c["incl"], r - cl, -1e30)), chains, gc_row, gc_col)
    egc = each(jnp.exp, gc_row)
    l_mat = each(lambda c, m, dc: jnp.where(c["strict"], m[0:L2] * dc, 0.0), chains, kq, decay)
    a_int = each(lambda c, m, dc: jnp.where(c["incl"], m[L2:2 * L2] * dc, 0.0), chains, kq, decay)
    x = each(lambda c, kb, eg: jnp.concatenate([c["v2"] * c["bcol"], kb * eg], axis=1), chains, k2b, egc)
    yield
    x = each(lambda n, xi: xi - _dot(n, xi, NN, passes["solve"]), l_mat, x)
    yield
    npow = l_mat
    for _ in range(int(math.log2(L)) - 1):
        npow = _square_pairs(npow, passes["square"])
        yield
        x = each(lambda n, xi: xi + _dot(n, xi, NN, passes["solve"]), npow, x)
        yield
    qe = each(lambda c, eg: c["q2"] * eg, chains, egc)
    rows = (slice(0, L), slice(L, L2))
    vnew = each(lambda xi, st: [xi[r, 0:B_HD] - _dot(xi[r, B_HD:2 * B_HD], s, NN, passes["state"])
                                for r, s in zip(rows, st)], x, states)
    o1 = each(lambda qi, st: [_dot(qi[r], s, NN, passes["state"]) for r, s in zip(rows, st)], qe, states)
    yield
    o2 = each(lambda a, vn, o: jnp.concatenate(o, axis=0) + _dot(a, jnp.concatenate(vn, axis=0), NN,
                                                                 passes["apply"]), a_int, vnew, o1)
    res["o"] = [[o[r] for r in rows] for o in o2]
    k_rest = each(lambda c, gt, r: c["k2"] * jnp.exp(gt - r), chains, g_tot, gc_row)
    res["s"] = each(lambda st, gt, kr, vn: [s * jnp.exp(gt[r][0:1, :]) + _dot(kr[r], v, TN, passes["state"])
                                            for r, s, v in zip(rows, st, vn)], states, g_tot, k_rest, vnew)


def _mix_out_kernel(yaf_ref, yab_ref, bg_ref, lnw_ref, lnb_ref, bd_ref, obf_ref, obb_ref, z_ref, nw_ref,
                    w_ref, x_ref, mod_ref, o_ref, *, k):
    y = yaf_ref[0] + yab_ref[0]
    ones_bd = bd_ref[...]
    mean = _group_sum(y, ones_bd) * (1.0 / A_HD)
    yc = y - mean
    var = _group_sum(yc * yc, ones_bd) * (1.0 / A_HD)
    yn = yc * lax.rsqrt(var + A_LN_EPS) * lnw_ref[...] + lnb_ref[...]
    bg = bg_ref[0].astype(F32)
    out_a = (yn + bg[:, 0:A_W]) * bg[:, A_W:2 * A_W]
    o = obf_ref[0] + obb_ref[0]
    z = z_ref[0]
    outs = [out_a.astype(BF16)]
    for h in range(B_H):
        oh = o[:, h * B_HD:(h + 1) * B_HD]
        on = oh * lax.rsqrt(jnp.mean(oh * oh, axis=-1, keepdims=True) + NORM_EPS) * nw_ref[...]
        zh = z[:, h * B_HD:(h + 1) * B_HD]
        outs.append((on * (zh * _sigmoid(zh))).astype(BF16))
    mixed = jnp.concatenate(outs, axis=1)
    o_ref[0] = x_ref[0] + mod_ref[0, k:k + 1, :] * jnp.dot(mixed, w_ref[...], preferred_element_type=F32)


def _mix_out(ya, bg, ln_w, ln_b, ones_bd, ob, zab, norm_w, w_out, x, mod, k, tt):
    g, t, _ = bg.shape
    tile = lambda width: pl.BlockSpec((1, tt, width), lambda gi, i: (gi, i, 0))
    full = lambda arr: pl.BlockSpec(arr.shape, lambda gi, i: (0,) * arr.ndim)
    return pl.pallas_call(
        functools.partial(_mix_out_kernel, k=k),
        grid=(g, t // tt),
        in_specs=[tile(A_W), tile(A_W), tile(2 * A_W), full(ln_w), full(ln_b), full(ones_bd), tile(B_W),
                  tile(B_W), tile(B_W), full(norm_w), full(w_out), tile(D),
                  pl.BlockSpec((1, MOD_ROWS, D), lambda gi, i: (gi, 0, 0))],
        out_specs=tile(D),
        out_shape=jax.ShapeDtypeStruct((g, t, D), F32),
        compiler_params=_cparams(("parallel", "parallel")), name="mix_out_proj",
    )(ya[0], ya[1], bg, ln_w, ln_b, ones_bd, ob[0], ob[1], zab, norm_w, w_out, x, mod)


def _qkv_rope_kernel(x_ref, w_ref, cos_ref, sin_ref, o_ref, *, lat_groups, q_cols, rope_cols, seg):
    is_lat = pl.program_id(0) < lat_groups
    x = x_ref[0]
    lane = lax.broadcasted_iota(jnp.int32, (1, LANE), 1)
    first = (lane % (C_HD // 2)) < (C_HD // 4)
    quarter = C_HD // 4
    cos_l = jnp.where(is_lat, cos_ref[...], 1.0)
    sin_l = jnp.where(is_lat, sin_ref[...], 0.0)
    for start in range(0, w_ref.shape[1], seg):
        acc = jnp.dot(x, w_ref[:, start:start + seg], preferred_element_type=F32)
        if start >= rope_cols:
            o_ref[0, :, start:start + seg] = acc.astype(o_ref.dtype)
            continue
        scale = C_HD ** -0.5 if start < q_cols else 1.0
        cos, sin = cos_l * scale, sin_l * scale
        for blk in range(seg // LANE):
            xb = acc[:, blk * LANE:(blk + 1) * LANE]
            partner = jnp.where(first, pltpu.roll(xb, LANE - quarter, axis=1), pltpu.roll(xb, quarter, axis=1))
            col = start + blk * LANE
            o_ref[0, :, col:col + LANE] = (xb * cos + partner * sin).astype(o_ref.dtype)


def _qkv_rope(h, w, cos, sin, lat_groups, tm, seg, q_cols, rope_cols):
    g, t, kdim = h.shape
    n = w.shape[1]
    assert q_cols % seg == 0 and rope_cols % seg == 0 and n % seg == 0
    return pl.pallas_call(
        functools.partial(_qkv_rope_kernel, lat_groups=lat_groups, q_cols=q_cols, rope_cols=rope_cols, seg=seg),
        grid=(g, t // tm),
        in_specs=[pl.BlockSpec((1, tm, kdim), lambda gi, i: (gi, i, 0)),
                  pl.BlockSpec((kdim, n), lambda gi, i: (0, 0)),
                  pl.BlockSpec((tm, LANE), lambda gi, i: (i, 0)),
                  pl.BlockSpec((tm, LANE), lambda gi, i: (i, 0))],
        out_specs=pl.BlockSpec((1, tm, n), lambda gi, i: (gi, i, 0)),
        out_shape=jax.ShapeDtypeStruct((g, t, n), BF16),
        compiler_params=_cparams(("parallel", "parallel")), name="qkv_rope",
    )(h, w, cos, sin)


def _attn_kernel(sink_ref, q_ref, kp_ref, km_ref, kn_ref, kc_ref, o_ref, *, n_blocks, kv_w):
    n = pl.program_id(1)
    blk = C_BLOCK
    ii = lax.broadcasted_iota(jnp.int32, (blk, blk), 0)
    jj = lax.broadcasted_iota(jnp.int32, (blk, blk), 1)
    ok_prev = jnp.logical_and(jj >= ii, n > 0)
    ok_next = jnp.logical_and(jj <= ii, n < n_blocks - 1)
    n_ctx = kc_ref.shape[1]
    valid1 = jnp.concatenate([ok_prev, jnp.full((blk, blk), True), ok_next, jnp.full((blk, n_ctx), True)], axis=1)
    valid = jnp.concatenate([valid1] * C_GROUP, axis=0)
    low = lax.broadcasted_iota(jnp.int32, (1, LANE), 1) < C_HD
    row_head = lax.broadcasted_iota(jnp.int32, (C_GROUP * blk, 1), 0) // blk
    q = q_ref[0]
    kv = jnp.concatenate([kp_ref[0], km_ref[0], kn_ref[0], kc_ref[0]], axis=0)
    zero = jnp.zeros((), q.dtype)
    outs = []
    for h in range(C_KVH):
        kh = kv[:, h * LANE:(h + 1) * LANE]
        vh = kv[:, kv_w + h * LANE:kv_w + (h + 1) * LANE]
        qa = q[:, (2 * h) * LANE:(2 * h + 1) * LANE]
        qb = q[:, (2 * h + 1) * LANE:(2 * h + 2) * LANE]
        qs = jnp.concatenate([jnp.where(low, qa, zero), jnp.where(low, zero, qa),
                              jnp.where(low, qb, zero), jnp.where(low, zero, qb)], axis=0)
        s = lax.dot_general(qs, kh, NT, preferred_element_type=F32)
        s = jnp.where(valid, s, -1e30)
        sk = jnp.full((C_GROUP * blk, 1), 0.0, F32)
        for gq in range(C_GROUP):
            sk = jnp.where(row_head == gq, sink_ref[h * C_GROUP + gq], sk)
        m = jnp.maximum(jnp.max(s, axis=-1, keepdims=True), sk)
        p = jnp.exp(s - m)
        denom = jnp.sum(p, axis=-1, keepdims=True) + jnp.exp(sk - m)
        o = lax.dot_general(p.astype(vh.dtype), vh, NN, preferred_element_type=F32) / denom
        outs.append(jnp.where(low, o[0:blk], o[blk:2 * blk]).astype(o_ref.dtype))
        outs.append(jnp.where(low, o[2 * blk:3 * blk], o[3 * blk:4 * blk]).astype(o_ref.dtype))
    o_ref[0] = jnp.concatenate(outs, axis=1)


def _attention(qkv, sink, bsz, ctx_len):
    g, t, _ = qkv.shape
    q_w = C_QH * C_HD
    kv_w = C_KVH * LANE
    nq = q_w // (2 * kv_w)
    assert q_w % (2 * kv_w) == 0
    n_blocks = t // C_BLOCK
    cpb = ctx_len // C_BLOCK
    assert ctx_len % C_BLOCK == 0
    kvspec = lambda fn: pl.BlockSpec((1, C_BLOCK, 2 * kv_w), fn)
    return pl.pallas_call(
        functools.partial(_attn_kernel, n_blocks=n_blocks, kv_w=kv_w),
        grid=(bsz, n_blocks),
        in_specs=[pl.BlockSpec(memory_space=pltpu.SMEM),
                  pl.BlockSpec((1, C_BLOCK, q_w), lambda b, n: (b, n, 0)),
                  kvspec(lambda b, n: (b, jnp.maximum(n - 1, 0), nq)),
                  kvspec(lambda b, n: (b, n, nq)),
                  kvspec(lambda b, n: (b, jnp.minimum(n + 1, n_blocks - 1), nq)),
                  pl.BlockSpec((1, ctx_len, 2 * kv_w), lambda b, n: (bsz, b, nq))],
        out_specs=pl.BlockSpec((1, C_BLOCK, q_w), lambda b, n: (b, n, 0)),
        out_shape=jax.ShapeDtypeStruct((bsz, t, q_w), BF16),
        compiler_params=_cparams(("parallel", "parallel")), name="window_attn",
    )(sink, qkv, qkv, qkv, qkv, qkv)


def _lane_cumsum(x):
    n = x.shape[-1]
    lane = lax.broadcasted_iota(jnp.int32, x.shape, x.ndim - 1)
    k = 1
    while k < n:
        x = x + jnp.where(lane >= k, pltpu.roll(x, k, axis=x.ndim - 1), 0)
        k *= 2
    return x


def _route_kernel(lg_ref, rank_ref, aff_ref, *, cap, slot_stride):
    lg = lg_ref[0]
    m = jnp.max(lg, axis=0, keepdims=True)
    e = jnp.exp(lg - m)
    z = jnp.sum(e, axis=0, keepdims=True)
    aff = e / z
    key = (lg - m) - jnp.log(z)
    count_ge = lambda v: jnp.sum(jnp.where(key >= v, 1, 0), axis=-1, keepdims=True)

    def body(_, carry):
        lo, hi = carry
        mid = 0.5 * (lo + hi)
        ok = count_ge(mid) >= cap
        return jnp.where(ok, mid, lo), jnp.where(ok, hi, mid)

    lo0 = jnp.min(key, axis=-1, keepdims=True)
    lo, hi = lax.fori_loop(0, ROUTE_BISECT, body, (lo0, jnp.ones_like(lo0)))
    thr, found = lo, jnp.zeros(lo.shape, jnp.int32)
    for _ in range(ROUTE_SNAP):
        v = jnp.max(jnp.where(key < hi, key, -3e38), axis=-1, keepdims=True)
        ok = jnp.where(count_ge(v) >= cap, 1, 0)
        thr = jnp.where(jnp.logical_and(found == 0, ok == 1), v, thr)
        hi = jnp.where(jnp.logical_or(found == 1, ok == 1), hi, v)
        found = jnp.maximum(found, ok)
    gt = key > thr
    eq = key == thr
    need = cap - jnp.sum(jnp.where(gt, 1, 0), axis=-1, keepdims=True)
    take_eq = jnp.logical_and(eq, _lane_cumsum(jnp.where(eq, 1, 0)) <= need)
    sel = jnp.logical_or(gt, take_eq)
    slot = _lane_cumsum(jnp.where(sel, 1, 0)) - 1 + pl.program_id(0) * slot_stride
    rank_ref[0] = jnp.where(sel, slot, -1)
    aff_ref[0] = aff


def _route(lg, cap, slot_stride):
    g, e, t = lg.shape
    spec = pl.BlockSpec((1, e, t), lambda gi: (gi, 0, 0))
    return pl.pallas_call(
        functools.partial(_route_kernel, cap=cap, slot_stride=slot_stride),
        grid=(g,), in_specs=[spec], out_specs=[spec, spec],
        out_shape=[jax.ShapeDtypeStruct((g, e, t), jnp.int32), jax.ShapeDtypeStruct((g, e, t), F32)],
        compiler_params=_cparams(("parallel",)), name="route",
    )(lg)


def _expert_kernel(lo_ref, hi_ref, h_ref, rank_ref, aff_ref, w1_ref, w3_ref, w2_ref, o_ref, pick_ref, xe_ref,
                   *, cap, n_tile, win):
    gi, e = pl.program_id(0), pl.program_id(1)

    @pl.when(e == 0)
    def _():
        o_ref[...] = jnp.zeros_like(o_ref)

    rank = rank_ref[0, 0]
    t = rank.shape[-1]
    ts = t // n_tile
    hit = lax.broadcasted_iota(jnp.int32, (cap, t), 0) == rank
    pick_ref[...] = jnp.where(hit, 1.0, 0.0).astype(BF16)
    gate = jnp.sum(jnp.where(hit, aff_ref[0, 0], 0.0), axis=-1, keepdims=True)
    xe_ref[...] = jnp.zeros_like(xe_ref)
    base = (gi * pl.num_programs(1) + e) * n_tile
    for j in range(n_tile):
        lo, hi = lo_ref[base + j], hi_ref[base + j]
        start = pl.multiple_of(jnp.minimum(lo // GATHER_ALIGN * GATHER_ALIGN, cap - win), GATHER_ALIGN)
        cols = slice(j * ts, (j + 1) * ts)
        fits = hi < start + win

        @pl.when(jnp.logical_and(hi >= lo, fits))
        def _():
            xe_ref[pl.ds(start, win), :] += jnp.dot(pick_ref[pl.ds(start, win), cols], h_ref[0, cols, :],
                                                    preferred_element_type=F32)

        @pl.when(jnp.logical_and(hi >= lo, jnp.logical_not(fits)))
        def _():
            xe_ref[...] += jnp.dot(pick_ref[:, cols], h_ref[0, cols, :], preferred_element_type=F32)

    xe = xe_ref[...].astype(BF16)
    h1 = jnp.dot(xe, w1_ref[0, 0].astype(BF16), preferred_element_type=F32)
    h3 = jnp.dot(xe, w3_ref[0, 0].astype(BF16), preferred_element_type=F32)
    hid = (h1 * _sigmoid(h1) * h3).astype(BF16)
    ye = (jnp.dot(hid, w2_ref[0], preferred_element_type=F32) * gate).astype(BF16)
    o_ref[0] += lax.dot_general(pick_ref[...], ye, TN, preferred_element_type=F32)


def _experts(h, rank, aff, w1, w3, w2, layer, cap):
    g, t, _ = h.shape
    _, n_exp, _, f = w1.shape
    n_tile = t // GATHER_TILE
    win = min(GATHER_WIN, cap)
    assert cap % GATHER_ALIGN == 0 and win % GATHER_ALIGN == 0
    tiles = rank.reshape(g, n_exp, n_tile, GATHER_TILE)
    hi = jnp.max(tiles, axis=-1).reshape(-1)
    lo = jnp.min(jnp.where(tiles >= 0, tiles, cap), axis=-1).reshape(-1)
    sel = pl.BlockSpec((1, 1, 1, t), lambda gi, e, lo_r, hi_r: (gi, e, 0, 0))
    grid_spec = pltpu.PrefetchScalarGridSpec(
        num_scalar_prefetch=2, grid=(g, n_exp),
        in_specs=[pl.BlockSpec((1, t, D), lambda gi, e, lo_r, hi_r: (gi, 0, 0)), sel, sel,
                  pl.BlockSpec((1, 1, D, f), lambda gi, e, lo_r, hi_r: (layer, e, 0, 0)),
                  pl.BlockSpec((1, 1, D, f), lambda gi, e, lo_r, hi_r: (layer, e, 0, 0)),
                  pl.BlockSpec((1, f, D), lambda gi, e, lo_r, hi_r: (e, 0, 0))],
        out_specs=pl.BlockSpec((1, t, D), lambda gi, e, lo_r, hi_r: (gi, 0, 0)),
        scratch_shapes=[pltpu.VMEM((cap, t), BF16), pltpu.VMEM((cap, D), F32)])
    return pl.pallas_call(
        functools.partial(_expert_kernel, cap=cap, n_tile=n_tile, win=win),
        grid_spec=grid_spec,
        out_shape=jax.ShapeDtypeStruct((g, t, D), F32),
        compiler_params=_cparams(("parallel", "arbitrary")), name="experts",
    )(lo, hi, h, rank.reshape(g, n_exp, 1, t), aff.reshape(g, n_exp, 1, t), w1, w3, w2)


def _moe(h, lg, w1, w3, w2, layer, bsz, ctx_len, lat_groups_only):
    g, t, _ = h.shape
    cap_lat = EC_CAP * t // N_EXP
    rank, aff = _route(lg[:bsz], cap_lat, 0)
    if not lat_groups_only:
        cap_ctx = EC_CAP * ctx_len // N_EXP
        lg_c = lg[bsz].reshape(N_EXP, bsz, ctx_len).transpose(1, 0, 2)
        rank_c, aff_c = _route(lg_c, cap_ctx, cap_ctx)
        back = lambda a: a.transpose(1, 0, 2).reshape(1, N_EXP, t)
        rank = jnp.concatenate([rank, back(rank_c)], axis=0)
        aff = jnp.concatenate([aff, back(aff_c)], axis=0)
        assert cap_ctx * bsz == cap_lat
    else:
        h = h[:bsz] if h.shape[0] != bsz else h
    return _experts(h, rank, aff, w1, w3, w2, layer, cap_lat)


def _block_diag2(w):
    z = jnp.zeros_like(w[0])
    return jnp.concatenate([jnp.concatenate([w[0], z], axis=1), jnp.concatenate([z, w[1]], axis=1)], axis=0)


def _rope_tables(t):
    quarter = C_HD // 4
    inv = ROPE_BASE ** (-jnp.arange(quarter, dtype=F32) / quarter)
    pos = jnp.arange(t)
    row = (pos // GRID_W).astype(F32)
    col = (pos % GRID_W).astype(F32)
    lane = np.arange(LANE)
    use_col = ((lane % C_HD) >= C_HD // 2)
    ang = jnp.where(use_col[None, :], col[:, None], row[:, None]) * inv[lane % quarter][None, :]
    sign = np.where((lane % (C_HD // 2)) < quarter, -1.0, 1.0).astype(np.float32)
    return jnp.cos(ang), jnp.sin(ang) * sign[None, :]


def kernel(x, c, ctx, c_ctx, ada_w, ada_b, norm_mix, norm_ffn, e_w_in, e_w_out, a_mu, a_w0, a_w2, a_a0, a_a2, a_g2,
           a_k_k, a_k_a, a_r_k, a_ln_w, a_ln_b, b_conv, b_a_log, b_dt_bias, b_norm, o_w_in, o_w_out, o_sink,
           moe_router, moe_w1, moe_w3, moe_w2, final_norm):
    bsz, t, _ = x.shape
    ctx_len = ctx.shape[1]
    depth = ada_w.shape[0]
    assert bsz * ctx_len == t and ctx_len % C_BLOCK == 0 and t % C_BLOCK == 0
    tt = ctx_len

    rows = 2 * SUB * ((bsz + 1 + 2 * SUB - 1) // (2 * SUB))
    cc = jnp.zeros((rows, D), F32).at[:bsz].set(c).at[bsz].set(c_ctx)
    mods = _adaln(cc, ada_w, ada_b)[:, :bsz + 1].reshape(depth, bsz + 1, N_MOD, D)
    mods = jnp.pad(mods, ((0, 0), (0, 0), (0, MOD_ROWS - N_MOD), (0, 0)))

    xs = jnp.concatenate([x, ctx.reshape(1, t, D)], axis=0)
    ones_bd = jnp.asarray(np.kron(np.eye(A_H), np.ones((A_HD, A_HD))), BF16)
    a_cols = e_w_in.shape[2] - (4 * B_W + 4 * B_H)
    cos_t, sin_t = _rope_tables(t)

    res = None
    for i in range(depth):
        j = i // 2
        mod = mods[i]
        x_new, h, _ = _norm(xs, norm_mix[i], mod, k=0, res=res, tt=tt)
        xs = xs if res is None else x_new
        if i % 2 == 0:
            w_in = e_w_in[j]
            split_ab = a_cols + 3 * B_W
            w_all = jnp.concatenate([w_in[:, :split_ab], w_in[:, split_ab + 4 * B_H:],
                                     w_in[:, split_ab:split_ab + 4 * B_H],
                                     jnp.zeros((D, LANE - 4 * B_H), F32)], axis=1).astype(BF16)
            pa, pb, zab = _mm_split(h, w_all, (a_cols, 3 * B_W, B_W + LANE), F32, 2 * tt)
            prm = {
                "mu": a_mu[j].reshape(1, -1), "w0": a_w0[j].reshape(1, -1), "w2": _block_diag2(a_w2[j]),
                "a0": a_a0[j].reshape(1, -1), "a2": _block_diag2(a_a2[j]), "g2": a_g2[j],
                "k_k": a_k_k[j].reshape(1, -1), "k_a": jnp.tile(a_k_a[j].reshape(1, -1), (1, 2)),
                "r_k": a_r_k[j].reshape(1, -1), "ones_bd": ones_bd,
            }
            rvk, cum, kd, bd, bg = _rwkv_prep(pa, prm, tt, bsz)
            pad_row = lambda v: jnp.pad(v.reshape(1, -1), ((0, 0), (0, LANE - v.size)))
            qkv_c, gb = _gdn_prep(pb, zab, b_conv[j], pad_row(b_a_log[j]), pad_row(b_dt_bias[j]), tt, bsz)
            ya_f, ob_f, ya_b, ob_b = _even_scan(rvk, cum, kd, bd, qkv_c, gb, bsz, ctx_len)
            xs = _mix_out((ya_f, ya_b), bg, a_ln_w[j].reshape(1, -1), a_ln_b[j].reshape(1, -1), ones_bd,
                          (ob_f, ob_b), zab, b_norm[j].reshape(1, -1), e_w_out[j].astype(BF16), xs, mod, 2, tt)
        else:
            w_in = o_w_in[j]
            q_w = C_QH * C_HD
            kv_cols = C_KVH * C_HD
            dup = lambda w: jnp.repeat(w.reshape(D, C_KVH, 1, C_HD), 2, axis=2).reshape(D, 2 * kv_cols)
            w_all = jnp.concatenate([w_in[:, :q_w], dup(w_in[:, q_w:q_w + kv_cols]), dup(w_in[:, q_w + kv_cols:])],
                                    axis=1).astype(BF16)
            tn = 2 * kv_cols
            qkv = _qkv_rope(h, w_all, cos_t, sin_t, bsz, 2 * tt, tn, q_w, q_w + tn)
            att = _attention(qkv, o_sink[j], bsz, ctx_len)
            x_lat = _mm_res(att, o_w_out[j].astype(BF16), xs, mod, 2, tt, groups=bsz)
            xs = x_lat if xs.shape[0] == bsz else jnp.concatenate([x_lat, xs[bsz:]], axis=0)
        last = i == depth - 1
        if last:
            xs = xs[:bsz]
            mod = mod[:bsz]
        _, hf, lg = _norm(xs, norm_ffn[i], mod, k=3, router_t=moe_router[i].T, tt=tt)
        delta = _moe(hf, lg, moe_w1, moe_w3, _to_bf16(moe_w2, i), i, bsz, ctx_len, last)
        res = (delta, mod, 5)
    _, out, _ = _norm(xs, final_norm, None, res=res, tt=tt, out_dtype=F32)
    return out
```

```python
import functools
import math

import jax
import jax.numpy as jnp
import numpy as np
from jax import lax
from jax.experimental import pallas as pl
from jax.experimental.pallas import tpu as pltpu

F32 = jnp.float32
BF16 = jnp.bfloat16
HIGHEST = lax.Precision.HIGHEST

D = 1024
N_MOD = 6
NORM_EPS = 1e-6
A_HD, A_H, A_W = 64, 8, 512
A_LN_EPS = 64e-5
B_HD, B_H, B_W = 128, 4, 512
B_CONV = 5
CHUNK = 64
C_HD, C_QH, C_KVH, C_GROUP = 64, 16, 4, 4
C_BLOCK = 128
ROPE_BASE = 10000.0
GRID_W = 64
N_EXP = 16
EC_CAP = 2
ROUTE_BISECT = 40
ROUTE_SNAP = 3
RWKV_PASSES = {"quad": 1, "apply": 1, "solve": "b", "square": 1, "state_in": "a", "state_out": "b"}
GDN_PASSES = {"quad": 1, "apply": 1, "solve": "b", "square": 1, "state": 1}
MOD_ROWS = 8
LANE = 128
SUB = 8
VMEM_LIMIT = 56 * 1024 * 1024

NT = (((1,), (1,)), ((), ()))
NN = (((1,), (0,)), ((), ()))
TN = (((0,), (0,)), ((), ()))


def _cparams(sem):
    return pltpu.CompilerParams(dimension_semantics=sem, vmem_limit_bytes=VMEM_LIMIT)


def _sigmoid(x):
    return 1.0 / (1.0 + jnp.exp(-x))


def _softplus(x):
    return jnp.maximum(x, 0.0) + jnp.log(1.0 + jnp.exp(-jnp.abs(x)))


def _split2(x):
    hi = x.astype(BF16)
    lo = (x - hi.astype(F32)).astype(BF16)
    return hi, lo


def _dot(a, b, dims=NN, passes=1):
    dg = functools.partial(lax.dot_general, dimension_numbers=dims, preferred_element_type=F32)
    if passes == 1:
        return dg(a.astype(BF16), b.astype(BF16))
    (ka,), (kb,) = dims[0]
    if passes == "a":
        ah, al = _split2(a)
        bh = b.astype(BF16)
        return dg(jnp.concatenate([ah, al], axis=ka), jnp.concatenate([bh, bh], axis=kb))
    if passes == "b":
        bh, bl = _split2(b)
        n = b.shape[1 - kb]
        res = dg(a.astype(BF16), jnp.concatenate([bh, bl], axis=1 - kb))
        return res[:, 0:n] + res[:, n:2 * n]
    ah, al = _split2(a)
    bh, bl = _split2(b)
    return dg(ah, bh) + (dg(ah, bl) + dg(al, bh))


def _group_sum(x, ones_bd):
    hi, lo = _split2(x)
    dg = functools.partial(lax.dot_general, dimension_numbers=NN, preferred_element_type=F32)
    return dg(hi, ones_bd) + dg(lo, ones_bd)


def _cast_kernel(w_ref, o_ref):
    o_ref[...] = w_ref[...].astype(o_ref.dtype)


def _to_bf16(w, layer):
    _, n, r, c = w.shape
    return pl.pallas_call(_cast_kernel, grid=(n,),
                          in_specs=[pl.BlockSpec((1, 1, r, c), lambda e: (layer, e, 0, 0))],
                          out_specs=pl.BlockSpec((1, 1, r, c), lambda e: (0, e, 0, 0)),
                          out_shape=jax.ShapeDtypeStruct((1, n, r, c), BF16),
                          compiler_params=_cparams(("parallel",)), name="cast_bf16")(w)[0]


def _adaln_kernel(c_ref, w_ref, b_ref, o_ref):
    c = c_ref[...]
    s = c * _sigmoid(c)
    o_ref[0] = jnp.dot(s, w_ref[0], precision=HIGHEST, preferred_element_type=F32) + b_ref[0]


def _adaln(cc, ada_w, ada_b):
    depth, _, n = ada_w.shape
    rows = cc.shape[0]
    tn = 768
    return pl.pallas_call(
        _adaln_kernel,
        grid=(depth, n // tn),
        in_specs=[pl.BlockSpec((rows, D), lambda i, j: (0, 0)),
                  pl.BlockSpec((1, D, tn), lambda i, j: (i, 0, j)),
                  pl.BlockSpec((1, 1, tn), lambda i, j: (i, 0, j))],
        out_specs=pl.BlockSpec((1, rows, tn), lambda i, j: (i, 0, j)),
        out_shape=jax.ShapeDtypeStruct((depth, rows, n), F32),
        compiler_params=_cparams(("parallel", "parallel")),
        name="adaln",
    )(cc, ada_w, ada_b.reshape(depth, 1, n))


def _norm_kernel(*refs, k, kres, has_res, has_mod, has_router):
    it = iter(refs)
    x_ref = next(it)
    d_ref = next(it) if has_res else None
    mres_ref = next(it) if has_res else None
    gain_ref = next(it)
    mod_ref = next(it) if has_mod else None
    rt_ref = next(it) if has_router else None
    xo_ref = next(it) if has_res else None
    h_ref = next(it)
    lg_ref = next(it) if has_router else None

    x = x_ref[0]
    if has_res:
        x = x + mres_ref[0, kres:kres + 1, :] * d_ref[0]
        xo_ref[0] = x
    xn = x * lax.rsqrt(jnp.mean(x * x, axis=-1, keepdims=True) + NORM_EPS) * gain_ref[...]
    if has_mod:
        xn = xn * (1.0 + mod_ref[0, k + 1:k + 2, :]) + mod_ref[0, k:k + 1, :]
    h_ref[0] = xn.astype(h_ref.dtype)
    if has_router:
        lg_ref[0] = lax.dot_general(rt_ref[...], xn, NT, precision=HIGHEST, preferred_element_type=F32)


def _norm(x, gain, mod=None, *, k=0, res=None, router_t=None, tt, out_dtype=BF16):
    g, t, _ = x.shape
    has_res, has_mod, has_router = res is not None, mod is not None, router_t is not None
    kres = res[2] if has_res else 0
    tile = pl.BlockSpec((1, tt, D), lambda gi, i: (gi, i, 0))
    modspec = pl.BlockSpec((1, MOD_ROWS, D), lambda gi, i: (gi, 0, 0))
    ins, specs = [x], [tile]
    if has_res:
        ins += [res[0], res[1]]
        specs += [tile, modspec]
    ins.append(gain.reshape(1, D))
    specs.append(pl.BlockSpec((1, D), lambda gi, i: (0, 0)))
    if has_mod:
        ins.append(mod)
        specs.append(modspec)
    if has_router:
        ins.append(router_t)
        specs.append(pl.BlockSpec((N_EXP, D), lambda gi, i: (0, 0)))
    outs, ospecs = [], []
    if has_res:
        outs.append(jax.ShapeDtypeStruct((g, t, D), F32))
        ospecs.append(tile)
    outs.append(jax.ShapeDtypeStruct((g, t, D), out_dtype))
    ospecs.append(tile)
    if has_router:
        outs.append(jax.ShapeDtypeStruct((g, N_EXP, t), F32))
        ospecs.append(pl.BlockSpec((1, N_EXP, tt), lambda gi, i: (gi, 0, i)))
    res_out = list(pl.pallas_call(
        functools.partial(_norm_kernel, k=k, kres=kres, has_res=has_res, has_mod=has_mod, has_router=has_router),
        grid=(g, t // tt), in_specs=specs, out_specs=ospecs, out_shape=outs,
        compiler_params=_cparams(("parallel", "parallel")), name="norm_mod",
    )(*ins))
    x_new = res_out.pop(0) if has_res else None
    h = res_out.pop(0)
    lg = res_out.pop(0) if has_router else None
    return x_new, h, lg


def _mixer_input(refs, has_res, k, kres):
    x = next(refs)[0]
    if has_res:
        d_ref, mres_ref = next(refs), next(refs)
        x = x + mres_ref[0, kres:kres + 1, :] * d_ref[0]
    gain_ref, mod_ref = next(refs), next(refs)
    xn = x * lax.rsqrt(jnp.mean(x * x, axis=-1, keepdims=True) + NORM_EPS) * gain_ref[...]
    h = (xn * (1.0 + mod_ref[0, k + 1:k + 2, :]) + mod_ref[0, k:k + 1, :]).astype(BF16)
    return (x if has_res else None), h


def _mixer_input_specs(x, gain, mod, res, tm):
    tile = pl.BlockSpec((1, tm, D), lambda gi, i: (gi, i, 0))
    modspec = pl.BlockSpec((1, MOD_ROWS, D), lambda gi, i: (gi, 0, 0))
    ins, specs = [x], [tile]
    if res is not None:
        ins += [res[0], res[1]]
        specs += [tile, modspec]
    ins += [gain.reshape(1, D), mod]
    specs += [pl.BlockSpec((1, D), lambda gi, i: (0, 0)), modspec]
    return ins, specs, tile


def _mm_split_kernel(*refs, widths, has_res, k, kres):
    it = iter(refs)
    x_new, h = _mixer_input(it, has_res, k, kres)
    w_ref = next(it)
    if has_res:
        next(it)[0] = x_new
    start = 0
    for o_ref, width in zip(it, widths):
        o_ref[0] = jnp.dot(h, w_ref[:, start:start + width], preferred_element_type=F32).astype(o_ref.dtype)
        start += width


def _mm_split(x, gain, mod, k, res, w, widths, out_dtype, tm):
    g, t, _ = x.shape
    assert sum(widths) == w.shape[1] and all(wd % LANE == 0 for wd in widths)
    ins, specs, tile = _mixer_input_specs(x, gain, mod, res, tm)
    has_res = res is not None
    outs = list(pl.pallas_call(
        functools.partial(_mm_split_kernel, widths=widths, has_res=has_res, k=k, kres=res[2] if has_res else 0),
        grid=(g, t // tm),
        in_specs=specs + [pl.BlockSpec(w.shape, lambda gi, i: (0, 0))],
        out_specs=([tile] if has_res else []) + [pl.BlockSpec((1, tm, wd), lambda gi, i: (gi, i, 0)) for wd in widths],
        out_shape=([jax.ShapeDtypeStruct((g, t, D), F32)] if has_res else [])
        + [jax.ShapeDtypeStruct((g, t, wd), out_dtype) for wd in widths],
        compiler_params=_cparams(("parallel", "parallel")), name="proj",
    )(*ins, w))
    return (outs.pop(0) if has_res else None), outs


def _mm_res_kernel(y_ref, w_ref, x_ref, mod_ref, o_ref, *, k):
    acc = jnp.dot(y_ref[0], w_ref[...], preferred_element_type=F32)
    o_ref[0] = x_ref[0] + mod_ref[0, k:k + 1, :] * acc


def _mm_res(y, w, x, mod, k, tm, groups=None):
    g_all, t, kdim = y.shape
    g = g_all if groups is None else groups
    return pl.pallas_call(
        functools.partial(_mm_res_kernel, k=k),
        grid=(g, t // tm),
        in_specs=[pl.BlockSpec((1, tm, kdim), lambda gi, i: (gi, i, 0)),
                  pl.BlockSpec((kdim, D), lambda gi, i: (0, 0)),
                  pl.BlockSpec((1, tm, D), lambda gi, i: (gi, i, 0)),
                  pl.BlockSpec((1, MOD_ROWS, D), lambda gi, i: (gi, 0, 0))],
        out_specs=pl.BlockSpec((1, tm, D), lambda gi, i: (gi, i, 0)),
        out_shape=jax.ShapeDtypeStruct((g, t, D), F32),
        compiler_params=_cparams(("parallel", "parallel")), name="out_proj_res",
    )(y, w, x, mod)


def _fill_halo(buf, x, pv_ref, nx_ref, has_prev, has_next, tt):
    buf[0:SUB, :] = jnp.where(has_prev, pv_ref[0], 0.0)
    buf[SUB:SUB + tt, :] = x
    buf[SUB + tt:2 * SUB + tt, :] = jnp.where(has_next, nx_ref[0], 0.0)


def _rwkv_prep_kernel(pa_ref, pv_ref, nx_ref, mu_ref, w0_ref, w2_ref, a0_ref, a2_ref, g2_ref, kk_ref, ka_ref,
                      rk_ref, bd_ref, rvk_ref, cum_ref, kd_ref, bdir_ref, bg_ref, buf, *, tt, lat_groups, ntile):
    gi, i = pl.program_id(0), pl.program_id(1)
    is_lat = gi < lat_groups
    x = pa_ref[0]
    _fill_halo(buf, x, pv_ref, nx_ref, jnp.logical_and(is_lat, i > 0), jnp.logical_and(is_lat, i < ntile - 1), tt)
    xm = buf[SUB - 1:SUB - 1 + tt, :]
    xp = buf[SUB + 1:SUB + 1 + tt, :]
    x = x + mu_ref[...] * (0.5 * (xm + xp) - x)
    w = A_W
    r, kx, v = x[:, 0:w], x[:, w:2 * w], x[:, 2 * w:3 * w]
    wd, ad, gd = x[:, 3 * w:3 * w + LANE], x[:, 3 * w + LANE:3 * w + 2 * LANE], x[:, 3 * w + 2 * LANE:3 * w + 3 * LANE]
    w_log = -_softplus(-(w0_ref[...] + _dot(jnp.tanh(wd), w2_ref[...], NN, 3))) - 0.5
    lw = -jnp.exp(w_log)
    a = _sigmoid(a0_ref[...] + _dot(ad, a2_ref[...], NN, 3))
    gate = _dot(_sigmoid(gd), g2_ref[...], NN, 3)
    ones_bd = bd_ref[...]
    kq = kx * kk_ref[...]
    kk = kq * lax.rsqrt(_group_sum(kq * kq, ones_bd) + 1e-6)
    k2 = jnp.concatenate([kx, kx], axis=1)
    kdir = k2 * (1.0 + (a - 1.0) * ka_ref[...])
    bdir = jnp.concatenate([kk, kk], axis=1) * a
    rr = r * rk_ref[...]
    bonus = _group_sum(rr * (kdir[:, 0:w] + kdir[:, w:2 * w]), ones_bd) * v
    rvk_ref[0] = jnp.concatenate([r, v, kk], axis=1)
    cum_ref[0] = jnp.concatenate([_chunk_cumsum(lw[:, 0:w], True), _chunk_cumsum(lw[:, w:2 * w], False)], axis=1)
    kd_ref[0] = kdir.astype(kd_ref.dtype)
    bdir_ref[0] = bdir.astype(bdir_ref.dtype)
    bg_ref[0] = jnp.concatenate([bonus, gate], axis=1).astype(bg_ref.dtype)


def _chunk_cumsum(x, fwd):
    _, _, incl = _order_masks(x.shape[0], CHUNK, fwd)
    mask = jnp.where(incl, 1.0, 0.0).astype(BF16)
    p1 = x.astype(BF16)
    r1 = x - p1.astype(F32)
    p2 = r1.astype(BF16)
    p3 = (r1 - p2.astype(F32)).astype(BF16)
    dg = functools.partial(lax.dot_general, dimension_numbers=NN, preferred_element_type=F32)
    return dg(mask, p1) + (dg(mask, p2) + dg(mask, p3))


def _halo_specs(width, tt, t):
    nb = tt // SUB
    last = t // SUB - 1
    prev = pl.BlockSpec((1, SUB, width), lambda gi, i: (gi, jnp.maximum(i * nb - 1, 0), 0))
    nxt = pl.BlockSpec((1, SUB, width), lambda gi, i: (gi, jnp.minimum((i + 1) * nb, last), 0))
    return prev, nxt


def _rwkv_prep(pa, p, tt, lat_groups):
    g, t, wa = pa.shape
    full = lambda arr: pl.BlockSpec(arr.shape, lambda gi, i: (0,) * arr.ndim)
    tile = lambda width: pl.BlockSpec((1, tt, width), lambda gi, i: (gi, i, 0))
    prev, nxt = _halo_specs(wa, tt, t)
    consts = [p["mu"], p["w0"], p["w2"], p["a0"], p["a2"], p["g2"], p["k_k"], p["k_a"], p["r_k"], p["ones_bd"]]
    widths = (3 * A_W, 2 * A_W, 2 * A_W, 2 * A_W, 2 * A_W)
    return pl.pallas_call(
        functools.partial(_rwkv_prep_kernel, tt=tt, lat_groups=lat_groups, ntile=t // tt),
        grid=(g, t // tt),
        in_specs=[tile(wa), prev, nxt] + [full(c) for c in consts],
        out_specs=[tile(wd) for wd in widths],
        out_shape=[jax.ShapeDtypeStruct((g, t, wd), dt) for wd, dt in zip(widths, (F32, F32, BF16, BF16, BF16))],
        scratch_shapes=[pltpu.VMEM((tt + 2 * SUB, wa), F32)],
        compiler_params=_cparams(("parallel", "parallel")), name="rwkv_prep",
    )(pa, pa, pa, *consts)


def _order_masks(n_rows, blk, fwd):
    ii = lax.broadcasted_iota(jnp.int32, (n_rows, n_rows), 0)
    jj = lax.broadcasted_iota(jnp.int32, (n_rows, n_rows), 1)
    same = (ii // blk) == (jj // blk)
    before = (jj < ii) if fwd else (jj > ii)
    strict = jnp.logical_and(same, before)
    incl = jnp.logical_and(same, jnp.logical_or(before, ii == jj))
    return same, strict, incl


def _even_scan_kernel(rvkf_ref, cumf_ref, kdf_ref, bdf_ref, qkvf_ref, gbf_ref,
                      rvkb_ref, cumb_ref, kdb_ref, bdb_ref, qkvb_ref, gbb_ref,
                      yf_ref, of_ref, yb_ref, ob_ref, sa_ref, sb_ref):
    @pl.when(pl.program_id(1) == 0)
    def _():
        sa_ref[...] = jnp.zeros_like(sa_ref)
        sb_ref[...] = jnp.zeros_like(sb_ref)

    na, nb = A_W // LANE, B_H // 2
    sa, sb = sa_ref[...], sb_ref[...]
    chains_a = (_rwkv_operands(rvkf_ref[0], cumf_ref[0], kdf_ref[0], bdf_ref[0], True)
                + _rwkv_operands(rvkb_ref[0], cumb_ref[0], kdb_ref[0], bdb_ref[0], False))
    chains_b = _gdn_operands(qkvf_ref[0], gbf_ref[0], 0) + _gdn_operands(qkvb_ref[0], gbb_ref[0], 1)
    states_a = [sa[d, p] for d in range(2) for p in range(na)]
    states_b = [(sb[d, 2 * p], sb[d, 2 * p + 1]) for d in range(2) for p in range(nb)]
    res_a, res_b = {}, {}
    stages = [_rwkv_solve(chains_a, states_a, RWKV_PASSES, res_a), _gdn_solve(chains_b, states_b, GDN_PASSES, res_b)]
    while stages:
        stages = [s for s in stages if next(s, "done") != "done"]
    yf_ref[0] = jnp.concatenate(res_a["y"][0:na], axis=1)
    yb_ref[0] = jnp.concatenate(res_a["y"][na:2 * na], axis=1)
    of_ref[0] = jnp.concatenate([o for pair in res_b["o"][0:nb] for o in pair], axis=1)
    ob_ref[0] = jnp.concatenate([o for pair in res_b["o"][nb:2 * nb] for o in pair], axis=1)
    for i, s_i in enumerate(res_a["s"]):
        sa_ref[i // na, i % na] = s_i
    for i, pair in enumerate(res_b["s"]):
        for j, s_h in enumerate(pair):
            sb_ref[i // nb, 2 * (i % nb) + j] = s_h


def _rwkv_operands(rvk, cum, kd, bd, fwd):
    L = CHUNK
    L2 = 2 * L
    _, strict, incl = _order_masks(L2, L, fwd)
    row = lax.broadcasted_iota(jnp.int32, (L, 1), 0)
    if fwd:
        tot = cum[L - 1:L]
        cum_ex = jnp.where(row == 0, 0.0, pltpu.roll(cum, 1, axis=0))
    else:
        tot = cum[0:1]
        cum_ex = jnp.where(row == L - 1, 0.0, pltpu.roll(cum, L - 1, axis=0))
    g_inv = jnp.exp(-cum)
    g_rest = jnp.exp(tot - cum)
    g_tot = jnp.exp(tot)
    w = A_W
    r_t = rvk[:, 0:w] * jnp.exp(cum)
    v = rvk[:, w:2 * w]
    a_t = rvk[:, 2 * w:3 * w] * jnp.exp(cum_ex)
    b_t, k_t = bd * g_inv, kd * g_inv
    b_g, k_g = bd * g_rest, kd * g_rest
    low = lax.broadcasted_iota(jnp.int32, (1, LANE), 1) < A_HD

    def expand(x):
        return jnp.concatenate([jnp.where(low, x, 0.0), jnp.where(low, 0.0, x)], axis=0)

    chains = []
    for p in range(A_W // LANE):
        sl = slice(p * LANE, (p + 1) * LANE)
        chains.append(dict(
            ev=expand(v[:, sl]),
            lhs=jnp.concatenate([expand(a_t[:, sl]), expand(r_t[:, sl])], axis=0),
            rhs=jnp.concatenate([expand(b_t[:, sl]), expand(k_t[:, sl])], axis=0),
            bk=jnp.concatenate([expand(b_g[:, sl]), expand(k_g[:, sl])], axis=0),
            g_col=jnp.transpose(jnp.broadcast_to(g_tot[:, sl], (LANE, LANE))),
            strict=strict, incl=incl))
    return chains


def _square_pairs(mats, passes):
    if passes != 1 or len(mats) % 2:
        return [_dot(m, m, NN, passes) for m in mats]
    out = []
    for m0, m1 in zip(mats[0::2], mats[1::2]):
        n = m0.shape[0]
        m0, m1 = m0.astype(BF16), m1.astype(BF16)
        zero = jnp.zeros_like(m0)
        diag = jnp.concatenate([jnp.concatenate([m0, zero], axis=1), jnp.concatenate([zero, m1], axis=1)], axis=0)
        sq = lax.dot_general(jnp.concatenate([m0, m1], axis=1), diag, NN, preferred_element_type=F32)
        out += [sq[:, 0:n], sq[:, n:2 * n]]
    return out


def _rwkv_solve(chains, states, passes, res):
    L = CHUNK
    L2 = 2 * L
    each = lambda fn, *cols: [fn(*args) for args in zip(*cols)]
    quad = each(lambda c: _dot(c["lhs"], c["rhs"], NT, passes["quad"]), chains)
    yield
    es = each(lambda c, s: _dot(c["lhs"], s, NN, passes["state_in"]), chains, states)
    yield
    n_ab = each(lambda c, q: jnp.where(c["strict"], q[0:L2, 0:L2], 0.0), chains, quad)
    x = each(lambda c, q, e: e[0:L2] + _dot(jnp.where(c["strict"], q[0:L2, L2:2 * L2], 0.0), c["ev"], NN,
                                            passes["apply"]), chains, quad, es)
    yield
    x = each(lambda n, xi: xi - _dot(n, xi, NN, passes["solve"]), n_ab, x)
    yield
    npow = n_ab
    for _ in range(int(math.log2(L)) - 1):
        npow = _square_pairs(npow, passes["square"])
        yield
        x = each(lambda n, xi: xi + _dot(n, xi, NN, passes["solve"]), npow, x)
        yield
    pv = each(lambda c, xi: jnp.concatenate([-xi, c["ev"]], axis=0), chains, x)
    y2 = each(lambda c, q, e, pvi: e[L2:2 * L2] + _dot(
        jnp.concatenate([jnp.where(c["incl"], q[L2:2 * L2, 0:L2], 0.0),
                         jnp.where(c["incl"], q[L2:2 * L2, L2:2 * L2], 0.0)], axis=1), pvi, NN, passes["apply"]),
        chains, quad, es, pv)
    res["y"] = [y[0:L] + y[L:L2] for y in y2]
    yield
    res["s"] = each(lambda c, s, pvi: s * c["g_col"] + _dot(c["bk"], pvi, TN, passes["state_out"]),
                    chains, states, pv)


def _chunk_index(b, c, fwd, n_ctx_chunk, n_lat_chunk, lat_groups):
    in_ctx = c < n_ctx_chunk
    cc = c if fwd else n_ctx_chunk - 1 - c
    lc = c - n_ctx_chunk if fwd else n_lat_chunk - 1 - (c - n_ctx_chunk)
    grp = jnp.where(in_ctx, lat_groups, b)
    chunk = jnp.where(in_ctx, b * n_ctx_chunk + cc, lc)
    return grp, chunk


def _scan_specs(widths, bsz, ctx_len, t):
    ncc, nlc = ctx_len // CHUNK, t // CHUNK

    def spec(width, lane_blk, fwd):
        def index(b, c):
            grp, ch = _chunk_index(b, c, fwd, ncc, nlc, bsz)
            return grp, ch, lane_blk
        return pl.BlockSpec((1, CHUNK, width), index)

    return [[spec(w, (0 if fwd else 1) if blk is None else blk, fwd) for w, blk in widths] for fwd in (True, False)]


def _even_scan(rvk, cum, kd, bd, qkv, gb, bsz, ctx_len):
    g, t, _ = rvk.shape
    ins_f, ins_b = _scan_specs([(3 * A_W, 0), (A_W, None), (A_W, None), (A_W, None), (3 * B_W, 0), (LANE, 0)],
                               bsz, ctx_len, t)
    outs_f, outs_b = _scan_specs([(A_W, 0), (B_W, 0)], bsz, ctx_len, t)
    shapes = [jax.ShapeDtypeStruct((g, t, A_W), F32), jax.ShapeDtypeStruct((g, t, B_W), F32)]
    return pl.pallas_call(
        _even_scan_kernel,
        grid=(bsz, (ctx_len + t) // CHUNK),
        in_specs=ins_f + ins_b,
        out_specs=outs_f + outs_b,
        out_shape=shapes + shapes,
        scratch_shapes=[pltpu.VMEM((2, A_W // LANE, LANE, LANE), F32), pltpu.VMEM((2, B_H, B_HD, B_HD), F32)],
        compiler_params=_cparams(("parallel", "arbitrary")), name="even_scan",
    )(rvk, cum, kd, bd, qkv, gb, rvk, cum, kd, bd, qkv, gb)


def _gdn_prep_kernel(pb_ref, pv_ref, nx_ref, ab_ref, cw_ref, alog_ref, dtb_ref, qkv_ref, gb_ref, buf,
                     *, tt, lat_groups, ntile):
    gi, i = pl.program_id(0), pl.program_id(1)
    is_lat = gi < lat_groups
    _fill_halo(buf, pb_ref[0], pv_ref, nx_ref, jnp.logical_and(is_lat, i > 0),
               jnp.logical_and(is_lat, i < ntile - 1), tt)
    half = B_CONV // 2
    acc = buf[SUB - half:SUB - half + tt, :] * cw_ref[0:1, :]
    for j in range(1, B_CONV):
        acc = acc + buf[SUB - half + j:SUB - half + j + tt, :] * cw_ref[j:j + 1, :]
    act = acc * _sigmoid(acc)
    outs = []
    for h in range(2 * B_H):
        xh = act[:, h * B_HD:(h + 1) * B_HD]
        xh = xh * lax.rsqrt(jnp.sum(xh * xh, axis=-1, keepdims=True) + 1e-6)
        if h < B_H:
            xh = xh * (B_HD ** -0.5)
        outs.append(xh)
    outs.append(act[:, 2 * B_W:3 * B_W])
    qkv_ref[0] = jnp.concatenate(outs, axis=1)
    ab = ab_ref[0]
    lane = lax.broadcasted_iota(jnp.int32, ab.shape, 1)
    gval = -jnp.exp(alog_ref[...]) * _softplus(ab + dtb_ref[...])
    gb_ref[0] = jnp.where(lane < 2 * B_H, gval, _sigmoid(ab))


def _gdn_prep(pb, zab, conv_w, alog_row, dtb_row, tt, lat_groups):
    g, t, wb = pb.shape
    tile = lambda width: pl.BlockSpec((1, tt, width), lambda gi, i: (gi, i, 0))
    full = lambda arr: pl.BlockSpec(arr.shape, lambda gi, i: (0,) * arr.ndim)
    prev, nxt = _halo_specs(wb, tt, t)
    ab_spec = pl.BlockSpec((1, tt, LANE), lambda gi, i: (gi, i, B_W // LANE))
    return pl.pallas_call(
        functools.partial(_gdn_prep_kernel, tt=tt, lat_groups=lat_groups, ntile=t // tt),
        grid=(g, t // tt),
        in_specs=[tile(wb), prev, nxt, ab_spec, full(conv_w), full(alog_row), full(dtb_row)],
        out_specs=[tile(wb), tile(LANE)],
        out_shape=[jax.ShapeDtypeStruct((g, t, wb), F32), jax.ShapeDtypeStruct((g, t, LANE), F32)],
        scratch_shapes=[pltpu.VMEM((tt + 2 * SUB, wb), F32)],
        compiler_params=_cparams(("parallel", "parallel")), name="gdn_prep",
    )(pb, pb, pb, zab, conv_w, alog_row, dtb_row)


def _gdn_operands(qkv, gb, d):
    fwd = d == 0
    L = CHUNK
    L2 = 2 * L
    same, strict, incl = _order_masks(L2, L, fwd)
    _, _, incl_rev = _order_masks(L2, L, not fwd)
    incl_bf = jnp.where(incl, 1.0, 0.0).astype(BF16)
    incl_t_bf = jnp.where(incl_rev, 1.0, 0.0).astype(BF16)
    same_bf = jnp.where(same, 1.0, 0.0).astype(BF16)
    lane = lax.broadcasted_iota(jnp.int32, gb.shape, 1)

    def column(idx):
        return jnp.sum(jnp.where(lane == idx, gb, 0.0), axis=-1, keepdims=True)

    chains = []
    for pr in range(B_H // 2):
        heads = (2 * pr, 2 * pr + 1)
        stack = lambda off: jnp.concatenate([qkv[:, off + h * B_HD:off + (h + 1) * B_HD] for h in heads], axis=0)
        q2, k2, v2 = stack(0), stack(B_W), stack(2 * B_W)
        gcol = jnp.concatenate([column(d * B_H + h) for h in heads], axis=0)
        bcol = jnp.concatenate([column(2 * B_H + d * B_H + h) for h in heads], axis=0)
        chains.append(dict(q2=q2, k2=k2, v2=v2, bcol=bcol, g3=_three_pieces(gcol), strict=strict, incl=incl,
                           sum_bf=jnp.concatenate([incl_bf, same_bf], axis=0), incl_t_bf=incl_t_bf))
    return chains


def _three_pieces(col):
    lane = lax.broadcasted_iota(jnp.int32, (col.shape[0], LANE), 1)
    p1 = col.astype(BF16).astype(F32)
    p2 = (col - p1).astype(BF16).astype(F32)
    p3 = col - p1 - p2
    return jnp.where(lane == 0, p1, jnp.where(lane == 1, p2, jnp.where(lane == 2, p3, 0.0))).astype(BF16)


def _gdn_solve(chains, states, passes, res):
    L = CHUNK
    L2 = 2 * L
    each = lambda fn, *cols: [fn(*args) for args in zip(*cols)]
    dg = functools.partial(lax.dot_general, preferred_element_type=F32)
    sums = each(lambda c: dg(c["sum_bf"], c["g3"], NN), chains)
    sums_t = each(lambda c: dg(c["g3"], c["incl_t_bf"], TN), chains)
    k2b = each(lambda c: c["k2"] * c["bcol"], chains)
    kq = each(lambda c, kb: _dot(jnp.concatenate([kb, c["q2"]], axis=0), c["k2"], NT, passes["quad"]), chains, k2b)
    yield
    gc_row = each(lambda m: jnp.sum(m[0:L2], axis=-1, keepdims=True), sums)
    g_tot = each(lambda m: jnp.sum(m[L2:2 * L2], axis=-1, keepdims=True), sums)
    gc_col = each(lambda m: m[0:1] + m[1:2] + m[2:3], sums_t)
    decay = each(lambda c, r, cl: jnp.exp(jnp.where(c["incl"], r - cl, -1e30)), chains, gc_row, gc_col)
    egc = each(jnp.exp, gc_row)
    l_mat = each(lambda c, m, dc: jnp.where(c["strict"], m[0:L2] * dc, 0.0), chains, kq, decay)
    a_int = each(lambda c, m, dc: jnp.where(c["incl"], m[L2:2 * L2] * dc, 0.0), chains, kq, decay)
    x = each(lambda c, kb, eg: jnp.concatenate([c["v2"] * c["bcol"], kb * eg], axis=1), chains, k2b, egc)
    yield
    x = each(lambda n, xi: xi - _dot(n, xi, NN, passes["solve"]), l_mat, x)
    yield
    npow = l_mat
    for _ in range(int(math.log2(L)) - 1):
        npow = _square_pairs(npow, passes["square"])
        yield
        x = each(lambda n, xi: xi + _dot(n, xi, NN, passes["solve"]), npow, x)
        yield
    qe = each(lambda c, eg: c["q2"] * eg, chains, egc)
    rows = (slice(0, L), slice(L, L2))
    vnew = each(lambda xi, st: [xi[r, 0:B_HD] - _dot(xi[r, B_HD:2 * B_HD], s, NN, passes["state"])
                                for r, s in zip(rows, st)], x, states)
    o1 = each(lambda qi, st: [_dot(qi[r], s, NN, passes["state"]) for r, s in zip(rows, st)], qe, states)
    yield
    o2 = each(lambda a, vn, o: jnp.concatenate(o, axis=0) + _dot(a, jnp.concatenate(vn, axis=0), NN,
                                                                 passes["apply"]), a_int, vnew, o1)
    res["o"] = [[o[r] for r in rows] for o in o2]
    k_rest = each(lambda c, gt, r: c["k2"] * jnp.exp(gt - r), chains, g_tot, gc_row)
    res["s"] = each(lambda st, gt, kr, vn: [s * jnp.exp(gt[r][0:1, :]) + _dot(kr[r], v, TN, passes["state"])
                                            for r, s, v in zip(rows, st, vn)], states, g_tot, k_rest, vnew)


def _mix_out_kernel(yaf_ref, yab_ref, bg_ref, lnw_ref, lnb_ref, bd_ref, obf_ref, obb_ref, z_ref, nw_ref,
                    w_ref, x_ref, mod_ref, o_ref, *, k):
    y = yaf_ref[0] + yab_ref[0]
    ones_bd = bd_ref[...]
    mean = _group_sum(y, ones_bd) * (1.0 / A_HD)
    yc = y - mean
    var = _group_sum(yc * yc, ones_bd) * (1.0 / A_HD)
    yn = yc * lax.rsqrt(var + A_LN_EPS) * lnw_ref[...] + lnb_ref[...]
    bg = bg_ref[0].astype(F32)
    out_a = (yn + bg[:, 0:A_W]) * bg[:, A_W:2 * A_W]
    o = obf_ref[0] + obb_ref[0]
    z = z_ref[0]
    outs = [out_a.astype(BF16)]
    for h in range(B_H):
        oh = o[:, h * B_HD:(h + 1) * B_HD]
        on = oh * lax.rsqrt(jnp.mean(oh * oh, axis=-1, keepdims=True) + NORM_EPS) * nw_ref[...]
        zh = z[:, h * B_HD:(h + 1) * B_HD]
        outs.append((on * (zh * _sigmoid(zh))).astype(BF16))
    mixed = jnp.concatenate(outs, axis=1)
    o_ref[0] = x_ref[0] + mod_ref[0, k:k + 1, :] * jnp.dot(mixed, w_ref[...], preferred_element_type=F32)


def _mix_out(ya, bg, ln_w, ln_b, ones_bd, ob, zab, norm_w, w_out, x, mod, k, tt):
    g, t, _ = bg.shape
    tile = lambda width: pl.BlockSpec((1, tt, width), lambda gi, i: (gi, i, 0))
    full = lambda arr: pl.BlockSpec(arr.shape, lambda gi, i: (0,) * arr.ndim)
    return pl.pallas_call(
        functools.partial(_mix_out_kernel, k=k),
        grid=(g, t // tt),
        in_specs=[tile(A_W), tile(A_W), tile(2 * A_W), full(ln_w), full(ln_b), full(ones_bd), tile(B_W),
                  tile(B_W), tile(B_W), full(norm_w), full(w_out), tile(D),
                  pl.BlockSpec((1, MOD_ROWS, D), lambda gi, i: (gi, 0, 0))],
        out_specs=tile(D),
        out_shape=jax.ShapeDtypeStruct((g, t, D), F32),
        compiler_params=_cparams(("parallel", "parallel")), name="mix_out_proj",
    )(ya[0], ya[1], bg, ln_w, ln_b, ones_bd, ob[0], ob[1], zab, norm_w, w_out, x, mod)


def _qkv_rope_kernel(*refs, lat_groups, q_cols, rope_cols, seg, has_res, k, kres):
    it = iter(refs)
    x_new, x = _mixer_input(it, has_res, k, kres)
    w_ref, cos_ref, sin_ref = next(it), next(it), next(it)
    if has_res:
        next(it)[0] = x_new
    o_ref = next(it)
    is_lat = pl.program_id(0) < lat_groups
    lane = lax.broadcasted_iota(jnp.int32, (1, LANE), 1)
    first = (lane % (C_HD // 2)) < (C_HD // 4)
    quarter = C_HD // 4
    cos_l = jnp.where(is_lat, cos_ref[...], 1.0)
    sin_l = jnp.where(is_lat, sin_ref[...], 0.0)
    for start in range(0, w_ref.shape[1], seg):
        acc = jnp.dot(x, w_ref[:, start:start + seg], preferred_element_type=F32)
        if start >= rope_cols:
            o_ref[0, :, start:start + seg] = acc.astype(o_ref.dtype)
            continue
        scale = C_HD ** -0.5 if start < q_cols else 1.0
        cos, sin = cos_l * scale, sin_l * scale
        for blk in range(seg // LANE):
            xb = acc[:, blk * LANE:(blk + 1) * LANE]
            partner = jnp.where(first, pltpu.roll(xb, LANE - quarter, axis=1), pltpu.roll(xb, quarter, axis=1))
            col = start + blk * LANE
            o_ref[0, :, col:col + LANE] = (xb * cos + partner * sin).astype(o_ref.dtype)


def _qkv_rope(x, gain, mod, k, res, w, cos, sin, lat_groups, tm, seg, q_cols, rope_cols):
    g, t, kdim = x.shape
    n = w.shape[1]
    assert q_cols % seg == 0 and rope_cols % seg == 0 and n % seg == 0
    ins, specs, tile = _mixer_input_specs(x, gain, mod, res, tm)
    has_res = res is not None
    outs = list(pl.pallas_call(
        functools.partial(_qkv_rope_kernel, lat_groups=lat_groups, q_cols=q_cols, rope_cols=rope_cols, seg=seg,
                          has_res=has_res, k=k, kres=res[2] if has_res else 0),
        grid=(g, t // tm),
        in_specs=specs + [pl.BlockSpec((kdim, n), lambda gi, i: (0, 0)),
                          pl.BlockSpec((tm, LANE), lambda gi, i: (i, 0)),
                          pl.BlockSpec((tm, LANE), lambda gi, i: (i, 0))],
        out_specs=([tile] if has_res else []) + [pl.BlockSpec((1, tm, n), lambda gi, i: (gi, i, 0))],
        out_shape=([jax.ShapeDtypeStruct((g, t, D), F32)] if has_res else [])
        + [jax.ShapeDtypeStruct((g, t, n), BF16)],
        compiler_params=_cparams(("parallel", "parallel")), name="qkv_rope",
    )(*ins, w, cos, sin))
    return (outs.pop(0) if has_res else None), outs[0]


def _attn_kernel(sink_ref, q_ref, kp_ref, km_ref, kn_ref, kc_ref, o_ref, *, n_blocks, kv_w):
    n = pl.program_id(1)
    blk = C_BLOCK
    ii = lax.broadcasted_iota(jnp.int32, (blk, blk), 0)
    jj = lax.broadcasted_iota(jnp.int32, (blk, blk), 1)
    ok_prev = jnp.logical_and(jj >= ii, n > 0)
    ok_next = jnp.logical_and(jj <= ii, n < n_blocks - 1)
    n_ctx = kc_ref.shape[1]
    valid1 = jnp.concatenate([ok_prev, jnp.full((blk, blk), True), ok_next, jnp.full((blk, n_ctx), True)], axis=1)
    valid = jnp.concatenate([valid1] * C_GROUP, axis=0)
    low = lax.broadcasted_iota(jnp.int32, (1, LANE), 1) < C_HD
    row_head = lax.broadcasted_iota(jnp.int32, (C_GROUP * blk, 1), 0) // blk
    q = q_ref[0]
    kv = jnp.concatenate([kp_ref[0], km_ref[0], kn_ref[0], kc_ref[0]], axis=0)
    zero = jnp.zeros((), q.dtype)
    outs = []
    for h in range(C_KVH):
        kh = kv[:, h * LANE:(h + 1) * LANE]
        vh = kv[:, kv_w + h * LANE:kv_w + (h + 1) * LANE]
        qa = q[:, (2 * h) * LANE:(2 * h + 1) * LANE]
        qb = q[:, (2 * h + 1) * LANE:(2 * h + 2) * LANE]
        qs = jnp.concatenate([jnp.where(low, qa, zero), jnp.where(low, zero, qa),
                              jnp.where(low, qb, zero), jnp.where(low, zero, qb)], axis=0)
        s = lax.dot_general(qs, kh, NT, preferred_element_type=F32)
        s = jnp.where(valid, s, -1e30)
        sk = jnp.full((C_GROUP * blk, 1), 0.0, F32)
        for gq in range(C_GROUP):
            sk = jnp.where(row_head == gq, sink_ref[h * C_GROUP + gq], sk)
        m = jnp.maximum(jnp.max(s, axis=-1, keepdims=True), sk)
        p = jnp.exp(s - m)
        denom = jnp.sum(p, axis=-1, keepdims=True) + jnp.exp(sk - m)
        o = lax.dot_general(p.astype(vh.dtype), vh, NN, preferred_element_type=F32) / denom
        outs.append(jnp.where(low, o[0:blk], o[blk:2 * blk]).astype(o_ref.dtype))
        outs.append(jnp.where(low, o[2 * blk:3 * blk], o[3 * blk:4 * blk]).astype(o_ref.dtype))
    o_ref[0] = jnp.concatenate(outs, axis=1)


def _attention(qkv, sink, bsz, ctx_len):
    g, t, _ = qkv.shape
    q_w = C_QH * C_HD
    kv_w = C_KVH * LANE
    nq = q_w // (2 * kv_w)
    assert q_w % (2 * kv_w) == 0
    n_blocks = t // C_BLOCK
    cpb = ctx_len // C_BLOCK
    assert ctx_len % C_BLOCK == 0
    kvspec = lambda fn: pl.BlockSpec((1, C_BLOCK, 2 * kv_w), fn)
    return pl.pallas_call(
        functools.partial(_attn_kernel, n_blocks=n_blocks, kv_w=kv_w),
        grid=(bsz, n_blocks),
        in_specs=[pl.BlockSpec(memory_space=pltpu.SMEM),
                  pl.BlockSpec((1, C_BLOCK, q_w), lambda b, n: (b, n, 0)),
                  kvspec(lambda b, n: (b, jnp.maximum(n - 1, 0), nq)),
                  kvspec(lambda b, n: (b, n, nq)),
                  kvspec(lambda b, n: (b, jnp.minimum(n + 1, n_blocks - 1), nq)),
                  pl.BlockSpec((1, ctx_len, 2 * kv_w), lambda b, n: (bsz, b, nq))],
        out_specs=pl.BlockSpec((1, C_BLOCK, q_w), lambda b, n: (b, n, 0)),
        out_shape=jax.ShapeDtypeStruct((bsz, t, q_w), BF16),
        compiler_params=_cparams(("parallel", "parallel")), name="window_attn",
    )(sink, qkv, qkv, qkv, qkv, qkv)


def _lane_cumsum(x):
    n = x.shape[-1]
    lane = lax.broadcasted_iota(jnp.int32, x.shape, x.ndim - 1)
    k = 1
    while k < n:
        x = x + jnp.where(lane >= k, pltpu.roll(x, k, axis=x.ndim - 1), 0)
        k *= 2
    return x


def _route_kernel(lg_ref, rank_ref, aff_ref, *, cap, slot_stride):
    lg = lg_ref[0]
    m = jnp.max(lg, axis=0, keepdims=True)
    e = jnp.exp(lg - m)
    z = jnp.sum(e, axis=0, keepdims=True)
    aff = e / z
    key = (lg - m) - jnp.log(z)
    count_ge = lambda v: jnp.sum(jnp.where(key >= v, 1, 0), axis=-1, keepdims=True)

    def body(_, carry):
        lo, hi = carry
        mid = 0.5 * (lo + hi)
        ok = count_ge(mid) >= cap
        return jnp.where(ok, mid, lo), jnp.where(ok, hi, mid)

    lo0 = jnp.min(key, axis=-1, keepdims=True)
    lo, hi = lax.fori_loop(0, ROUTE_BISECT, body, (lo0, jnp.ones_like(lo0)))
    thr, found = lo, jnp.zeros(lo.shape, jnp.int32)
    for _ in range(ROUTE_SNAP):
        v = jnp.max(jnp.where(key < hi, key, -3e38), axis=-1, keepdims=True)
        ok = jnp.where(count_ge(v) >= cap, 1, 0)
        thr = jnp.where(jnp.logical_and(found == 0, ok == 1), v, thr)
        hi = jnp.where(jnp.logical_or(found == 1, ok == 1), hi, v)
        found = jnp.maximum(found, ok)
    gt = key > thr
    eq = key == thr
    need = cap - jnp.sum(jnp.where(gt, 1, 0), axis=-1, keepdims=True)
    take_eq = jnp.logical_and(eq, _lane_cumsum(jnp.where(eq, 1, 0)) <= need)
    sel = jnp.logical_or(gt, take_eq)
    slot = _lane_cumsum(jnp.where(sel, 1, 0)) - 1 + pl.program_id(0) * slot_stride
    rank_ref[0] = jnp.where(sel, slot, -1)
    aff_ref[0] = aff


def _route(lg, cap, slot_stride):
    g, e, t = lg.shape
    spec = pl.BlockSpec((1, e, t), lambda gi: (gi, 0, 0))
    return pl.pallas_call(
        functools.partial(_route_kernel, cap=cap, slot_stride=slot_stride),
        grid=(g,), in_specs=[spec], out_specs=[spec, spec],
        out_shape=[jax.ShapeDtypeStruct((g, e, t), jnp.int32), jax.ShapeDtypeStruct((g, e, t), F32)],
        compiler_params=_cparams(("parallel",)), name="route",
    )(lg)


def _expert_kernel(h_ref, rank_ref, aff_ref, w1_ref, w3_ref, w2_ref, o_ref, *, cap):
    e = pl.program_id(1)

    @pl.when(e == 0)
    def _():
        o_ref[...] = jnp.zeros_like(o_ref)

    rank = rank_ref[0, 0]
    t = rank.shape[-1]
    hit = lax.broadcasted_iota(jnp.int32, (cap, t), 0) == rank
    pick = jnp.where(hit, 1.0, 0.0).astype(BF16)
    gate = jnp.sum(jnp.where(hit, aff_ref[0, 0], 0.0), axis=-1, keepdims=True)
    xe = jnp.dot(pick, h_ref[0], preferred_element_type=F32).astype(BF16)
    h1 = jnp.dot(xe, w1_ref[0, 0].astype(BF16), preferred_element_type=F32)
    h3 = jnp.dot(xe, w3_ref[0, 0].astype(BF16), preferred_element_type=F32)
    hid = (h1 * _sigmoid(h1) * h3).astype(BF16)
    ye = (jnp.dot(hid, w2_ref[0], preferred_element_type=F32) * gate).astype(BF16)
    o_ref[0] += lax.dot_general(pick, ye, TN, preferred_element_type=F32)


def _experts(h, rank, aff, w1, w3, w2, layer, cap):
    g, t, _ = h.shape
    _, n_exp, _, f = w1.shape
    sel = pl.BlockSpec((1, 1, 1, t), lambda gi, e: (gi, e, 0, 0))
    return pl.pallas_call(
        functools.partial(_expert_kernel, cap=cap),
        grid=(g, n_exp),
        in_specs=[pl.BlockSpec((1, t, D), lambda gi, e: (gi, 0, 0)), sel, sel,
                  pl.BlockSpec((1, 1, D, f), lambda gi, e: (layer, e, 0, 0)),
                  pl.BlockSpec((1, 1, D, f), lambda gi, e: (layer, e, 0, 0)),
                  pl.BlockSpec((1, f, D), lambda gi, e: (e, 0, 0))],
        out_specs=pl.BlockSpec((1, t, D), lambda gi, e: (gi, 0, 0)),
        out_shape=jax.ShapeDtypeStruct((g, t, D), F32),
        compiler_params=_cparams(("parallel", "arbitrary")), name="experts",
    )(h, rank.reshape(g, n_exp, 1, t), aff.reshape(g, n_exp, 1, t), w1, w3, w2)


def _moe(h, lg, w1, w3, w2, layer, bsz, ctx_len, lat_groups_only):
    g, t, _ = h.shape
    cap_lat = EC_CAP * t // N_EXP
    rank, aff = _route(lg[:bsz], cap_lat, 0)
    if not lat_groups_only:
        cap_ctx = EC_CAP * ctx_len // N_EXP
        lg_c = lg[bsz].reshape(N_EXP, bsz, ctx_len).transpose(1, 0, 2)
        rank_c, aff_c = _route(lg_c, cap_ctx, cap_ctx)
        back = lambda a: a.transpose(1, 0, 2).reshape(1, N_EXP, t)
        rank = jnp.concatenate([rank, back(rank_c)], axis=0)
        aff = jnp.concatenate([aff, back(aff_c)], axis=0)
        assert cap_ctx * bsz == cap_lat
    else:
        h = h[:bsz] if h.shape[0] != bsz else h
    return _experts(h, rank, aff, w1, w3, w2, layer, cap_lat)


def _block_diag2(w):
    z = jnp.zeros_like(w[0])
    return jnp.concatenate([jnp.concatenate([w[0], z], axis=1), jnp.concatenate([z, w[1]], axis=1)], axis=0)


def _rope_tables(t):
    quarter = C_HD // 4
    inv = ROPE_BASE ** (-jnp.arange(quarter, dtype=F32) / quarter)
    pos = jnp.arange(t)
    row = (pos // GRID_W).astype(F32)
    col = (pos % GRID_W).astype(F32)
    lane = np.arange(LANE)
    use_col = ((lane % C_HD) >= C_HD // 2)
    ang = jnp.where(use_col[None, :], col[:, None], row[:, None]) * inv[lane % quarter][None, :]
    sign = np.where((lane % (C_HD // 2)) < quarter, -1.0, 1.0).astype(np.float32)
    return jnp.cos(ang), jnp.sin(ang) * sign[None, :]


def kernel(x, c, ctx, c_ctx, ada_w, ada_b, norm_mix, norm_ffn, e_w_in, e_w_out, a_mu, a_w0, a_w2, a_a0, a_a2, a_g2,
           a_k_k, a_k_a, a_r_k, a_ln_w, a_ln_b, b_conv, b_a_log, b_dt_bias, b_norm, o_w_in, o_w_out, o_sink,
           moe_router, moe_w1, moe_w3, moe_w2, final_norm):
    bsz, t, _ = x.shape
    ctx_len = ctx.shape[1]
    depth = ada_w.shape[0]
    assert bsz * ctx_len == t and ctx_len % C_BLOCK == 0 and t % C_BLOCK == 0
    tt = ctx_len

    rows = 2 * SUB * ((bsz + 1 + 2 * SUB - 1) // (2 * SUB))
    cc = jnp.zeros((rows, D), F32).at[:bsz].set(c).at[bsz].set(c_ctx)
    mods = _adaln(cc, ada_w, ada_b)[:, :bsz + 1].reshape(depth, bsz + 1, N_MOD, D)
    mods = jnp.pad(mods, ((0, 0), (0, 0), (0, MOD_ROWS - N_MOD), (0, 0)))

    xs = jnp.concatenate([x, ctx.reshape(1, t, D)], axis=0)
    ones_bd = jnp.asarray(np.kron(np.eye(A_H), np.ones((A_HD, A_HD))), BF16)
    a_cols = e_w_in.shape[2] - (4 * B_W + 4 * B_H)
    cos_t, sin_t = _rope_tables(t)

    res = None
    for i in range(depth):
        j = i // 2
        mod = mods[i]
        if i % 2 == 0:
            w_in = e_w_in[j]
            split_ab = a_cols + 3 * B_W
            w_all = jnp.concatenate([w_in[:, :split_ab], w_in[:, split_ab + 4 * B_H:],
                                     w_in[:, split_ab:split_ab + 4 * B_H],
                                     jnp.zeros((D, LANE - 4 * B_H), F32)], axis=1).astype(BF16)
            x_new, (pa, pb, zab) = _mm_split(xs, norm_mix[i], mod, 0, res, w_all, (a_cols, 3 * B_W, B_W + LANE),
                                             F32, 2 * tt)
            xs = xs if res is None else x_new
            prm = {
                "mu": a_mu[j].reshape(1, -1), "w0": a_w0[j].reshape(1, -1), "w2": _block_diag2(a_w2[j]),
                "a0": a_a0[j].reshape(1, -1), "a2": _block_diag2(a_a2[j]), "g2": a_g2[j],
                "k_k": a_k_k[j].reshape(1, -1), "k_a": jnp.tile(a_k_a[j].reshape(1, -1), (1, 2)),
                "r_k": a_r_k[j].reshape(1, -1), "ones_bd": ones_bd,
            }
            rvk, cum, kd, bd, bg = _rwkv_prep(pa, prm, tt, bsz)
            pad_row = lambda v: jnp.pad(v.reshape(1, -1), ((0, 0), (0, LANE - v.size)))
            qkv_c, gb = _gdn_prep(pb, zab, b_conv[j], pad_row(b_a_log[j]), pad_row(b_dt_bias[j]), tt, bsz)
            ya_f, ob_f, ya_b, ob_b = _even_scan(rvk, cum, kd, bd, qkv_c, gb, bsz, ctx_len)
            xs = _mix_out((ya_f, ya_b), bg, a_ln_w[j].reshape(1, -1), a_ln_b[j].reshape(1, -1), ones_bd,
                          (ob_f, ob_b), zab, b_norm[j].reshape(1, -1), e_w_out[j].astype(BF16), xs, mod, 2, tt)
        else:
            w_in = o_w_in[j]
            q_w = C_QH * C_HD
            kv_cols = C_KVH * C_HD
            dup = lambda w: jnp.repeat(w.reshape(D, C_KVH, 1, C_HD), 2, axis=2).reshape(D, 2 * kv_cols)
            w_all = jnp.concatenate([w_in[:, :q_w], dup(w_in[:, q_w:q_w + kv_cols]), dup(w_in[:, q_w + kv_cols:])],
                                    axis=1).astype(BF16)
            tn = 2 * kv_cols
            x_new, qkv = _qkv_rope(xs, norm_mix[i], mod, 0, res, w_all, cos_t, sin_t, bsz, 2 * tt, tn, q_w, q_w + tn)
            xs = xs if res is None else x_new
            att = _attention(qkv, o_sink[j], bsz, ctx_len)
            x_lat = _mm_res(att, o_w_out[j].astype(BF16), xs, mod, 2, tt, groups=bsz)
            xs = x_lat if xs.shape[0] == bsz else jnp.concatenate([x_lat, xs[bsz:]], axis=0)
        last = i == depth - 1
        if last:
            xs = xs[:bsz]
            mod = mod[:bsz]
        _, hf, lg = _norm(xs, norm_ffn[i], mod, k=3, router_t=moe_router[i].T, tt=tt)
        delta = _moe(hf, lg, moe_w1, moe_w3, _to_bf16(moe_w2, i), i, bsz, ctx_len, last)
        res = (delta, mod, 5)
    _, out, _ = _norm(xs, final_norm, None, res=res, tt=tt, out_dtype=F32)
    return out
```

```python
import functools
import math

import jax
import jax.numpy as jnp
import numpy as np
from jax import lax
from jax.experimental import pallas as pl
from jax.experimental.pallas import tpu as pltpu

F32 = jnp.float32
BF16 = jnp.bfloat16
HIGHEST = lax.Precision.HIGHEST

D = 1024
N_MOD = 6
NORM_EPS = 1e-6
A_HD, A_H, A_W = 64, 8, 512
A_LN_EPS = 64e-5
B_HD, B_H, B_W = 128, 4, 512
B_CONV = 5
CHUNK = 64
C_HD, C_QH, C_KVH, C_GROUP = 64, 16, 4, 4
C_BLOCK = 128
ROPE_BASE = 10000.0
GRID_W = 64
N_EXP = 16
EC_CAP = 2
ROUTE_BISECT = 40
ROUTE_SNAP = 3
RWKV_PASSES = {"quad": 1, "apply": 1, "solve": "b", "square": 1, "state_in": "a", "state_out": "b"}
GDN_PASSES = {"quad": 1, "apply": 1, "solve": "b", "square": 1, "state": 1}
MOD_ROWS = 8
LANE = 128
SUB = 8
VMEM_LIMIT = 56 * 1024 * 1024

NT = (((1,), (1,)), ((), ()))
NN = (((1,), (0,)), ((), ()))
TN = (((0,), (0,)), ((), ()))


def _cparams(sem):
    return pltpu.CompilerParams(dimension_semantics=sem, vmem_limit_bytes=VMEM_LIMIT)


def _sigmoid(x):
    return 1.0 / (1.0 + jnp.exp(-x))


def _softplus(x):
    return jnp.maximum(x, 0.0) + jnp.log(1.0 + jnp.exp(-jnp.abs(x)))


def _split2(x):
    hi = x.astype(BF16)
    lo = (x - hi.astype(F32)).astype(BF16)
    return hi, lo


def _dot(a, b, dims=NN, passes=1):
    dg = functools.partial(lax.dot_general, dimension_numbers=dims, preferred_element_type=F32)
    if passes == 1:
        return dg(a.astype(BF16), b.astype(BF16))
    (ka,), (kb,) = dims[0]
    if passes == "a":
        ah, al = _split2(a)
        bh = b.astype(BF16)
        return dg(jnp.concatenate([ah, al], axis=ka), jnp.concatenate([bh, bh], axis=kb))
    if passes == "b":
        bh, bl = _split2(b)
        n = b.shape[1 - kb]
        res = dg(a.astype(BF16), jnp.concatenate([bh, bl], axis=1 - kb))
        return res[:, 0:n] + res[:, n:2 * n]
    ah, al = _split2(a)
    bh, bl = _split2(b)
    return dg(ah, bh) + (dg(ah, bl) + dg(al, bh))


def _group_sum(x, ones_bd):
    hi, lo = _split2(x)
    dg = functools.partial(lax.dot_general, dimension_numbers=NN, preferred_element_type=F32)
    return dg(hi, ones_bd) + dg(lo, ones_bd)


def _cast_kernel(w_ref, o_ref):
    o_ref[...] = w_ref[...].astype(o_ref.dtype)


def _to_bf16(w, layer):
    _, n, r, c = w.shape
    return pl.pallas_call(_cast_kernel, grid=(n,),
                          in_specs=[pl.BlockSpec((1, 1, r, c), lambda e: (layer, e, 0, 0))],
                          out_specs=pl.BlockSpec((1, 1, r, c), lambda e: (0, e, 0, 0)),
                          out_shape=jax.ShapeDtypeStruct((1, n, r, c), BF16),
                          compiler_params=_cparams(("parallel",)), name="cast_bf16")(w)[0]


def _adaln_kernel(c_ref, w_ref, b_ref, o_ref):
    c = c_ref[...]
    s = c * _sigmoid(c)
    o_ref[0] = jnp.dot(s, w_ref[0], precision=HIGHEST, preferred_element_type=F32) + b_ref[0]


def _adaln(cc, ada_w, ada_b):
    depth, _, n = ada_w.shape
    rows = cc.shape[0]
    tn = 768
    return pl.pallas_call(
        _adaln_kernel,
        grid=(depth, n // tn),
        in_specs=[pl.BlockSpec((rows, D), lambda i, j: (0, 0)),
                  pl.BlockSpec((1, D, tn), lambda i, j: (i, 0, j)),
                  pl.BlockSpec((1, 1, tn), lambda i, j: (i, 0, j))],
        out_specs=pl.BlockSpec((1, rows, tn), lambda i, j: (i, 0, j)),
        out_shape=jax.ShapeDtypeStruct((depth, rows, n), F32),
        compiler_params=_cparams(("parallel", "parallel")),
        name="adaln",
    )(cc, ada_w, ada_b.reshape(depth, 1, n))


def _norm_kernel(*refs, k, kres, has_res, has_mod, has_router):
    it = iter(refs)
    x_ref = next(it)
    d_ref = next(it) if has_res else None
    mres_ref = next(it) if has_res else None
    gain_ref = next(it)
    mod_ref = next(it) if has_mod else None
    rt_ref = next(it) if has_router else None
    xo_ref = next(it) if has_res else None
    h_ref = next(it)
    lg_ref = next(it) if has_router else None

    x = x_ref[0]
    if has_res:
        x = x + mres_ref[0, kres:kres + 1, :] * d_ref[0]
        xo_ref[0] = x
    xn = x * lax.rsqrt(jnp.mean(x * x, axis=-1, keepdims=True) + NORM_EPS) * gain_ref[...]
    if has_mod:
        xn = xn * (1.0 + mod_ref[0, k + 1:k + 2, :]) + mod_ref[0, k:k + 1, :]
    h_ref[0] = xn.astype(h_ref.dtype)
    if has_router:
        lg_ref[0] = lax.dot_general(rt_ref[...], xn, NT, precision=HIGHEST, preferred_element_type=F32)


def _norm(x, gain, mod=None, *, k=0, res=None, router_t=None, tt, out_dtype=BF16):
    g, t, _ = x.shape
    has_res, has_mod, has_router = res is not None, mod is not None, router_t is not None
    kres = res[2] if has_res else 0
    tile = pl.BlockSpec((1, tt, D), lambda gi, i: (gi, i, 0))
    modspec = pl.BlockSpec((1, MOD_ROWS, D), lambda gi, i: (gi, 0, 0))
    ins, specs = [x], [tile]
    if has_res:
        ins += [res[0], res[1]]
        specs += [tile, modspec]
    ins.append(gain.reshape(1, D))
    specs.append(pl.BlockSpec((1, D), lambda gi, i: (0, 0)))
    if has_mod:
        ins.append(mod)
        specs.append(modspec)
    if has_router:
        ins.append(router_t)
        specs.append(pl.BlockSpec((N_EXP, D), lambda gi, i: (0, 0)))
    outs, ospecs = [], []
    if has_res:
        outs.append(jax.ShapeDtypeStruct((g, t, D), F32))
        ospecs.append(tile)
    outs.append(jax.ShapeDtypeStruct((g, t, D), out_dtype))
    ospecs.append(tile)
    if has_router:
        outs.append(jax.ShapeDtypeStruct((g, N_EXP, t), F32))
        ospecs.append(pl.BlockSpec((1, N_EXP, tt), lambda gi, i: (gi, 0, i)))
    res_out = list(pl.pallas_call(
        functools.partial(_norm_kernel, k=k, kres=kres, has_res=has_res, has_mod=has_mod, has_router=has_router),
        grid=(g, t // tt), in_specs=specs, out_specs=ospecs, out_shape=outs,
        compiler_params=_cparams(("parallel", "parallel")), name="norm_mod",
    )(*ins))
    x_new = res_out.pop(0) if has_res else None
    h = res_out.pop(0)
    lg = res_out.pop(0) if has_router else None
    return x_new, h, lg


def _mixer_input(refs, has_res, k, kres):
    x = next(refs)[0]
    if has_res:
        d_ref, mres_ref = next(refs), next(refs)
        x = x + mres_ref[0, kres:kres + 1, :] * d_ref[0]
    gain_ref, mod_ref = next(refs), next(refs)
    xn = x * lax.rsqrt(jnp.mean(x * x, axis=-1, keepdims=True) + NORM_EPS) * gain_ref[...]
    h = (xn * (1.0 + mod_ref[0, k + 1:k + 2, :]) + mod_ref[0, k:k + 1, :]).astype(BF16)
    return (x if has_res else None), h


def _mixer_input_specs(x, gain, mod, res, tm):
    tile = pl.BlockSpec((1, tm, D), lambda gi, i: (gi, i, 0))
    modspec = pl.BlockSpec((1, MOD_ROWS, D), lambda gi, i: (gi, 0, 0))
    ins, specs = [x], [tile]
    if res is not None:
        ins += [res[0], res[1]]
        specs += [tile, modspec]
    ins += [gain.reshape(1, D), mod]
    specs += [pl.BlockSpec((1, D), lambda gi, i: (0, 0)), modspec]
    return ins, specs, tile


def _mm_split_kernel(*refs, widths, has_res, k, kres):
    it = iter(refs)
    x_new, h = _mixer_input(it, has_res, k, kres)
    w_ref = next(it)
    if has_res:
        next(it)[0] = x_new
    start = 0
    for o_ref, width in zip(it, widths):
        o_ref[0] = jnp.dot(h, w_ref[:, start:start + width], preferred_element_type=F32).astype(o_ref.dtype)
        start += width


def _mm_split(x, gain, mod, k, res, w, widths, out_dtype, tm):
    g, t, _ = x.shape
    assert sum(widths) == w.shape[1] and all(wd % LANE == 0 for wd in widths)
    ins, specs, tile = _mixer_input_specs(x, gain, mod, res, tm)
    has_res = res is not None
    outs = list(pl.pallas_call(
        functools.partial(_mm_split_kernel, widths=widths, has_res=has_res, k=k, kres=res[2] if has_res else 0),
        grid=(g, t // tm),
        in_specs=specs + [pl.BlockSpec(w.shape, lambda gi, i: (0, 0))],
        out_specs=([tile] if has_res else []) + [pl.BlockSpec((1, tm, wd), lambda gi, i: (gi, i, 0)) for wd in widths],
        out_shape=([jax.ShapeDtypeStruct((g, t, D), F32)] if has_res else [])
        + [jax.ShapeDtypeStruct((g, t, wd), out_dtype) for wd in widths],
        compiler_params=_cparams(("parallel", "parallel")), name="proj",
    )(*ins, w))
    return (outs.pop(0) if has_res else None), outs


def _ffn_input(x, gain_ref, mod_ref, rt_ref, k, h_ref, lg_ref):
    xn = x * lax.rsqrt(jnp.mean(x * x, axis=-1, keepdims=True) + NORM_EPS) * gain_ref[...]
    xn = xn * (1.0 + mod_ref[0, k + 1:k + 2, :]) + mod_ref[0, k:k + 1, :]
    h_ref[0] = xn.astype(h_ref.dtype)
    lg_ref[0] = lax.dot_general(rt_ref[...], xn, NT, precision=HIGHEST, preferred_element_type=F32)


def _ffn_input_specs(g, t, tm, gain, router_t):
    ins = [gain.reshape(1, D), router_t]
    specs = [pl.BlockSpec((1, D), lambda gi, i: (0, 0)), pl.BlockSpec((N_EXP, D), lambda gi, i: (0, 0))]
    ospecs = [pl.BlockSpec((1, tm, D), lambda gi, i: (gi, i, 0)), pl.BlockSpec((1, N_EXP, tm), lambda gi, i: (gi, 0, i))]
    oshapes = [jax.ShapeDtypeStruct((g, t, D), BF16), jax.ShapeDtypeStruct((g, N_EXP, t), F32)]
    return ins, specs, ospecs, oshapes


def _mm_res_kernel(y_ref, w_ref, x_ref, mod_ref, gain_ref, rt_ref, o_ref, h_ref, lg_ref, *, k, kffn):
    acc = jnp.dot(y_ref[0], w_ref[...], preferred_element_type=F32)
    x = x_ref[0] + mod_ref[0, k:k + 1, :] * acc
    o_ref[0] = x
    _ffn_input(x, gain_ref, mod_ref, rt_ref, kffn, h_ref, lg_ref)


def _mm_res(y, w, x, mod, k, gain_ffn, router_t, kffn, tm, groups):
    _, t, kdim = y.shape
    g = groups
    ins, specs, ospecs, oshapes = _ffn_input_specs(g, t, tm, gain_ffn, router_t)
    return pl.pallas_call(
        functools.partial(_mm_res_kernel, k=k, kffn=kffn),
        grid=(g, t // tm),
        in_specs=[pl.BlockSpec((1, tm, kdim), lambda gi, i: (gi, i, 0)),
                  pl.BlockSpec((kdim, D), lambda gi, i: (0, 0)),
                  pl.BlockSpec((1, tm, D), lambda gi, i: (gi, i, 0)),
                  pl.BlockSpec((1, MOD_ROWS, D), lambda gi, i: (gi, 0, 0))] + specs,
        out_specs=[pl.BlockSpec((1, tm, D), lambda gi, i: (gi, i, 0))] + ospecs,
        out_shape=[jax.ShapeDtypeStruct((g, t, D), F32)] + oshapes,
        compiler_params=_cparams(("parallel", "parallel")), name="out_proj_res",
    )(y, w, x, mod, *ins)


def _fill_halo(buf, x, pv_ref, nx_ref, has_prev, has_next, tt):
    buf[0:SUB, :] = jnp.where(has_prev, pv_ref[0], 0.0)
    buf[SUB:SUB + tt, :] = x
    buf[SUB + tt:2 * SUB + tt, :] = jnp.where(has_next, nx_ref[0], 0.0)


def _rwkv_prep_kernel(pa_ref, pv_ref, nx_ref, mu_ref, w0_ref, w2_ref, a0_ref, a2_ref, g2_ref, kk_ref, ka_ref,
                      rk_ref, bd_ref, rvk_ref, cum_ref, kd_ref, bdir_ref, bg_ref, buf, *, tt, lat_groups, ntile):
    gi, i = pl.program_id(0), pl.program_id(1)
    is_lat = gi < lat_groups
    x = pa_ref[0]
    _fill_halo(buf, x, pv_ref, nx_ref, jnp.logical_and(is_lat, i > 0), jnp.logical_and(is_lat, i < ntile - 1), tt)
    xm = buf[SUB - 1:SUB - 1 + tt, :]
    xp = buf[SUB + 1:SUB + 1 + tt, :]
    x = x + mu_ref[...] * (0.5 * (xm + xp) - x)
    w = A_W
    r, kx, v = x[:, 0:w], x[:, w:2 * w], x[:, 2 * w:3 * w]
    wd, ad, gd = x[:, 3 * w:3 * w + LANE], x[:, 3 * w + LANE:3 * w + 2 * LANE], x[:, 3 * w + 2 * LANE:3 * w + 3 * LANE]
    w_log = -_softplus(-(w0_ref[...] + _dot(jnp.tanh(wd), w2_ref[...], NN, 3))) - 0.5
    lw = -jnp.exp(w_log)
    a = _sigmoid(a0_ref[...] + _dot(ad, a2_ref[...], NN, 3))
    gate = _dot(_sigmoid(gd), g2_ref[...], NN, 3)
    ones_bd = bd_ref[...]
    kq = kx * kk_ref[...]
    kk = kq * lax.rsqrt(_group_sum(kq * kq, ones_bd) + 1e-6)
    k2 = jnp.concatenate([kx, kx], axis=1)
    kdir = k2 * (1.0 + (a - 1.0) * ka_ref[...])
    bdir = jnp.concatenate([kk, kk], axis=1) * a
    rr = r * rk_ref[...]
    bonus = _group_sum(rr * (kdir[:, 0:w] + kdir[:, w:2 * w]), ones_bd) * v
    rvk_ref[0] = jnp.concatenate([r, v, kk], axis=1)
    cum_ref[0] = jnp.concatenate([_chunk_cumsum(lw[:, 0:w], True), _chunk_cumsum(lw[:, w:2 * w], False)], axis=1)
    kd_ref[0] = kdir.astype(kd_ref.dtype)
    bdir_ref[0] = bdir.astype(bdir_ref.dtype)
    bg_ref[0] = jnp.concatenate([bonus, gate], axis=1).astype(bg_ref.dtype)


def _chunk_cumsum(x, fwd):
    _, _, incl = _order_masks(x.shape[0], CHUNK, fwd)
    mask = jnp.where(incl, 1.0, 0.0).astype(BF16)
    p1 = x.astype(BF16)
    r1 = x - p1.astype(F32)
    p2 = r1.astype(BF16)
    p3 = (r1 - p2.astype(F32)).astype(BF16)
    dg = functools.partial(lax.dot_general, dimension_numbers=NN, preferred_element_type=F32)
    return dg(mask, p1) + (dg(mask, p2) + dg(mask, p3))


def _halo_specs(width, tt, t):
    nb = tt // SUB
    last = t // SUB - 1
    prev = pl.BlockSpec((1, SUB, width), lambda gi, i: (gi, jnp.maximum(i * nb - 1, 0), 0))
    nxt = pl.BlockSpec((1, SUB, width), lambda gi, i: (gi, jnp.minimum((i + 1) * nb, last), 0))
    return prev, nxt


def _rwkv_prep(pa, p, tt, lat_groups):
    g, t, wa = pa.shape
    full = lambda arr: pl.BlockSpec(arr.shape, lambda gi, i: (0,) * arr.ndim)
    tile = lambda width: pl.BlockSpec((1, tt, width), lambda gi, i: (gi, i, 0))
    prev, nxt = _halo_specs(wa, tt, t)
    consts = [p["mu"], p["w0"], p["w2"], p["a0"], p["a2"], p["g2"], p["k_k"], p["k_a"], p["r_k"], p["ones_bd"]]
    widths = (3 * A_W, 2 * A_W, 2 * A_W, 2 * A_W, 2 * A_W)
    return pl.pallas_call(
        functools.partial(_rwkv_prep_kernel, tt=tt, lat_groups=lat_groups, ntile=t // tt),
        grid=(g, t // tt),
        in_specs=[tile(wa), prev, nxt] + [full(c) for c in consts],
        out_specs=[tile(wd) for wd in widths],
        out_shape=[jax.ShapeDtypeStruct((g, t, wd), dt) for wd, dt in zip(widths, (F32, F32, BF16, BF16, BF16))],
        scratch_shapes=[pltpu.VMEM((tt + 2 * SUB, wa), F32)],
        compiler_params=_cparams(("parallel", "parallel")), name="rwkv_prep",
    )(pa, pa, pa, *consts)


def _order_masks(n_rows, blk, fwd):
    ii = lax.broadcasted_iota(jnp.int32, (n_rows, n_rows), 0)
    jj = lax.broadcasted_iota(jnp.int32, (n_rows, n_rows), 1)
    same = (ii // blk) == (jj // blk)
    before = (jj < ii) if fwd else (jj > ii)
    strict = jnp.logical_and(same, before)
    incl = jnp.logical_and(same, jnp.logical_or(before, ii == jj))
    return same, strict, incl


def _even_scan_kernel(rvkf_ref, cumf_ref, kdf_ref, bdf_ref, qkvf_ref, gbf_ref,
                      rvkb_ref, cumb_ref, kdb_ref, bdb_ref, qkvb_ref, gbb_ref,
                      yf_ref, of_ref, yb_ref, ob_ref, sa_ref, sb_ref):
    @pl.when(pl.program_id(1) == 0)
    def _():
        sa_ref[...] = jnp.zeros_like(sa_ref)
        sb_ref[...] = jnp.zeros_like(sb_ref)

    na, nb = A_W // LANE, B_H // 2
    sa, sb = sa_ref[...], sb_ref[...]
    chains_a = (_rwkv_operands(rvkf_ref[0], cumf_ref[0], kdf_ref[0], bdf_ref[0], True)
                + _rwkv_operands(rvkb_ref[0], cumb_ref[0], kdb_ref[0], bdb_ref[0], False))
    chains_b = _gdn_operands(qkvf_ref[0], gbf_ref[0], 0) + _gdn_operands(qkvb_ref[0], gbb_ref[0], 1)
    states_a = [sa[d, p] for d in range(2) for p in range(na)]
    states_b = [(sb[d, 2 * p], sb[d, 2 * p + 1]) for d in range(2) for p in range(nb)]
    res_a, res_b = {}, {}
    stages = [_rwkv_solve(chains_a, states_a, RWKV_PASSES, res_a), _gdn_solve(chains_b, states_b, GDN_PASSES, res_b)]
    while stages:
        stages = [s for s in stages if next(s, "done") != "done"]
    yf_ref[0] = jnp.concatenate(res_a["y"][0:na], axis=1)
    yb_ref[0] = jnp.concatenate(res_a["y"][na:2 * na], axis=1)
    of_ref[0] = jnp.concatenate([o for pair in res_b["o"][0:nb] for o in pair], axis=1)
    ob_ref[0] = jnp.concatenate([o for pair in res_b["o"][nb:2 * nb] for o in pair], axis=1)
    for i, s_i in enumerate(res_a["s"]):
        sa_ref[i // na, i % na] = s_i
    for i, pair in enumerate(res_b["s"]):
        for j, s_h in enumerate(pair):
            sb_ref[i // nb, 2 * (i % nb) + j] = s_h


def _rwkv_operands(rvk, cum, kd, bd, fwd):
    L = CHUNK
    L2 = 2 * L
    _, strict, incl = _order_masks(L2, L, fwd)
    row = lax.broadcasted_iota(jnp.int32, (L, 1), 0)
    if fwd:
        tot = cum[L - 1:L]
        cum_ex = jnp.where(row == 0, 0.0, pltpu.roll(cum, 1, axis=0))
    else:
        tot = cum[0:1]
        cum_ex = jnp.where(row == L - 1, 0.0, pltpu.roll(cum, L - 1, axis=0))
    g_inv = jnp.exp(-cum)
    g_rest = jnp.exp(tot - cum)
    g_tot = jnp.exp(tot)
    w = A_W
    r_t = rvk[:, 0:w] * jnp.exp(cum)
    v = rvk[:, w:2 * w]
    a_t = rvk[:, 2 * w:3 * w] * jnp.exp(cum_ex)
    b_t, k_t = bd * g_inv, kd * g_inv
    b_g, k_g = bd * g_rest, kd * g_rest
    low = lax.broadcasted_iota(jnp.int32, (1, LANE), 1) < A_HD

    def expand(x):
        return jnp.concatenate([jnp.where(low, x, 0.0), jnp.where(low, 0.0, x)], axis=0)

    chains = []
    for p in range(A_W // LANE):
        sl = slice(p * LANE, (p + 1) * LANE)
        chains.append(dict(
            ev=expand(v[:, sl]),
            lhs=jnp.concatenate([expand(a_t[:, sl]), expand(r_t[:, sl])], axis=0),
            rhs=jnp.concatenate([expand(b_t[:, sl]), expand(k_t[:, sl])], axis=0),
            bk=jnp.concatenate([expand(b_g[:, sl]), expand(k_g[:, sl])], axis=0),
            g_col=jnp.transpose(jnp.broadcast_to(g_tot[:, sl], (LANE, LANE))),
            strict=strict, incl=incl))
    return chains


def _square_pairs(mats, passes):
    if passes != 1 or len(mats) % 2:
        return [_dot(m, m, NN, passes) for m in mats]
    out = []
    for m0, m1 in zip(mats[0::2], mats[1::2]):
        n = m0.shape[0]
        m0, m1 = m0.astype(BF16), m1.astype(BF16)
        zero = jnp.zeros_like(m0)
        diag = jnp.concatenate([jnp.concatenate([m0, zero], axis=1), jnp.concatenate([zero, m1], axis=1)], axis=0)
        sq = lax.dot_general(jnp.concatenate([m0, m1], axis=1), diag, NN, preferred_element_type=F32)
        out += [sq[:, 0:n], sq[:, n:2 * n]]
    return out


def _rwkv_solve(chains, states, passes, res):
    L = CHUNK
    L2 = 2 * L
    each = lambda fn, *cols: [fn(*args) for args in zip(*cols)]
    quad = each(lambda c: _dot(c["lhs"], c["rhs"], NT, passes["quad"]), chains)
    yield
    es = each(lambda c, s: _dot(c["lhs"], s, NN, passes["state_in"]), chains, states)
    yield
    n_ab = each(lambda c, q: jnp.where(c["strict"], q[0:L2, 0:L2], 0.0), chains, quad)
    x = each(lambda c, q, e: e[0:L2] + _dot(jnp.where(c["strict"], q[0:L2, L2:2 * L2], 0.0), c["ev"], NN,
                                            passes["apply"]), chains, quad, es)
    yield
    x = each(lambda n, xi: xi - _dot(n, xi, NN, passes["solve"]), n_ab, x)
    yield
    npow = n_ab
    for _ in range(int(math.log2(L)) - 1):
        npow = _square_pairs(npow, passes["square"])
        yield
        x = each(lambda n, xi: xi + _dot(n, xi, NN, passes["solve"]), npow, x)
        yield
    pv = each(lambda c, xi: jnp.concatenate([-xi, c["ev"]], axis=0), chains, x)
    y2 = each(lambda c, q, e, pvi: e[L2:2 * L2] + _dot(
        jnp.concatenate([jnp.where(c["incl"], q[L2:2 * L2, 0:L2], 0.0),
                         jnp.where(c["incl"], q[L2:2 * L2, L2:2 * L2], 0.0)], axis=1), pvi, NN, passes["apply"]),
        chains, quad, es, pv)
    res["y"] = [y[0:L] + y[L:L2] for y in y2]
    yield
    res["s"] = each(lambda c, s, pvi: s * c["g_col"] + _dot(c["bk"], pvi, TN, passes["state_out"]),
                    chains, states, pv)


def _chunk_index(b, c, fwd, n_ctx_chunk, n_lat_chunk, lat_groups):
    in_ctx = c < n_ctx_chunk
    cc = c if fwd else n_ctx_chunk - 1 - c
    lc = c - n_ctx_chunk if fwd else n_lat_chunk - 1 - (c - n_ctx_chunk)
    grp = jnp.where(in_ctx, lat_groups, b)
    chunk = jnp.where(in_ctx, b * n_ctx_chunk + cc, lc)
    return grp, chunk


def _scan_specs(widths, bsz, ctx_len, t):
    ncc, nlc = ctx_len // CHUNK, t // CHUNK

    def spec(width, lane_blk, fwd):
        def index(b, c):
            grp, ch = _chunk_index(b, c, fwd, ncc, nlc, bsz)
            return grp, ch, lane_blk
        return pl.BlockSpec((1, CHUNK, width), index)

    return [[spec(w, (0 if fwd else 1) if blk is None else blk, fwd) for w, blk in widths] for fwd in (True, False)]


def _even_scan(rvk, cum, kd, bd, qkv, gb, bsz, ctx_len):
    g, t, _ = rvk.shape
    ins_f, ins_b = _scan_specs([(3 * A_W, 0), (A_W, None), (A_W, None), (A_W, None), (3 * B_W, 0), (LANE, 0)],
                               bsz, ctx_len, t)
    outs_f, outs_b = _scan_specs([(A_W, 0), (B_W, 0)], bsz, ctx_len, t)
    shapes = [jax.ShapeDtypeStruct((g, t, A_W), F32), jax.ShapeDtypeStruct((g, t, B_W), F32)]
    return pl.pallas_call(
        _even_scan_kernel,
        grid=(bsz, (ctx_len + t) // CHUNK),
        in_specs=ins_f + ins_b,
        out_specs=outs_f + outs_b,
        out_shape=shapes + shapes,
        scratch_shapes=[pltpu.VMEM((2, A_W // LANE, LANE, LANE), F32), pltpu.VMEM((2, B_H, B_HD, B_HD), F32)],
        compiler_params=_cparams(("parallel", "arbitrary")), name="even_scan",
    )(rvk, cum, kd, bd, qkv, gb, rvk, cum, kd, bd, qkv, gb)


def _gdn_prep_kernel(pb_ref, pv_ref, nx_ref, ab_ref, cw_ref, alog_ref, dtb_ref, qkv_ref, gb_ref, buf,
                     *, tt, lat_groups, ntile):
    gi, i = pl.program_id(0), pl.program_id(1)
    is_lat = gi < lat_groups
    _fill_halo(buf, pb_ref[0], pv_ref, nx_ref, jnp.logical_and(is_lat, i > 0),
               jnp.logical_and(is_lat, i < ntile - 1), tt)
    half = B_CONV // 2
    acc = buf[SUB - half:SUB - half + tt, :] * cw_ref[0:1, :]
    for j in range(1, B_CONV):
        acc = acc + buf[SUB - half + j:SUB - half + j + tt, :] * cw_ref[j:j + 1, :]
    act = acc * _sigmoid(acc)
    outs = []
    for h in range(2 * B_H):
        xh = act[:, h * B_HD:(h + 1) * B_HD]
        xh = xh * lax.rsqrt(jnp.sum(xh * xh, axis=-1, keepdims=True) + 1e-6)
        if h < B_H:
            xh = xh * (B_HD ** -0.5)
        outs.append(xh)
    outs.append(act[:, 2 * B_W:3 * B_W])
    qkv_ref[0] = jnp.concatenate(outs, axis=1)
    ab = ab_ref[0]
    lane = lax.broadcasted_iota(jnp.int32, ab.shape, 1)
    gval = -jnp.exp(alog_ref[...]) * _softplus(ab + dtb_ref[...])
    gb_ref[0] = jnp.where(lane < 2 * B_H, gval, _sigmoid(ab))


def _gdn_prep(pb, zab, conv_w, alog_row, dtb_row, tt, lat_groups):
    g, t, wb = pb.shape
    tile = lambda width: pl.BlockSpec((1, tt, width), lambda gi, i: (gi, i, 0))
    full = lambda arr: pl.BlockSpec(arr.shape, lambda gi, i: (0,) * arr.ndim)
    prev, nxt = _halo_specs(wb, tt, t)
    ab_spec = pl.BlockSpec((1, tt, LANE), lambda gi, i: (gi, i, B_W // LANE))
    return pl.pallas_call(
        functools.partial(_gdn_prep_kernel, tt=tt, lat_groups=lat_groups, ntile=t // tt),
        grid=(g, t // tt),
        in_specs=[tile(wb), prev, nxt, ab_spec, full(conv_w), full(alog_row), full(dtb_row)],
        out_specs=[tile(wb), tile(LANE)],
        out_shape=[jax.ShapeDtypeStruct((g, t, wb), F32), jax.ShapeDtypeStruct((g, t, LANE), F32)],
        scratch_shapes=[pltpu.VMEM((tt + 2 * SUB, wb), F32)],
        compiler_params=_cparams(("parallel", "parallel")), name="gdn_prep",
    )(pb, pb, pb, zab, conv_w, alog_row, dtb_row)


def _gdn_operands(qkv, gb, d):
    fwd = d == 0
    L = CHUNK
    L2 = 2 * L
    same, strict, incl = _order_masks(L2, L, fwd)
    _, _, incl_rev = _order_masks(L2, L, not fwd)
    incl_bf = jnp.where(incl, 1.0, 0.0).astype(BF16)
    incl_t_bf = jnp.where(incl_rev, 1.0, 0.0).astype(BF16)
    same_bf = jnp.where(same, 1.0, 0.0).astype(BF16)
    lane = lax.broadcasted_iota(jnp.int32, gb.shape, 1)

    def column(idx):
        return jnp.sum(jnp.where(lane == idx, gb, 0.0), axis=-1, keepdims=True)

    chains = []
    for pr in range(B_H // 2):
        heads = (2 * pr, 2 * pr + 1)
        stack = lambda off: jnp.concatenate([qkv[:, off + h * B_HD:off + (h + 1) * B_HD] for h in heads], axis=0)
        q2, k2, v2 = stack(0), stack(B_W), stack(2 * B_W)
        gcol = jnp.concatenate([column(d * B_H + h) for h in heads], axis=0)
        bcol = jnp.concatenate([column(2 * B_H + d * B_H + h) for h in heads], axis=0)
        chains.append(dict(q2=q2, k2=k2, v2=v2, bcol=bcol, g3=_three_pieces(gcol), strict=strict, incl=incl,
                           sum_bf=jnp.concatenate([incl_bf, same_bf], axis=0), incl_t_bf=incl_t_bf))
    return chains


def _three_pieces(col):
    lane = lax.broadcasted_iota(jnp.int32, (col.shape[0], LANE), 1)
    p1 = col.astype(BF16).astype(F32)
    p2 = (col - p1).astype(BF16).astype(F32)
    p3 = col - p1 - p2
    return jnp.where(lane == 0, p1, jnp.where(lane == 1, p2, jnp.where(lane == 2, p3, 0.0))).astype(BF16)


def _gdn_solve(chains, states, passes, res):
    L = CHUNK
    L2 = 2 * L
    each = lambda fn, *cols: [fn(*args) for args in zip(*cols)]
    dg = functools.partial(lax.dot_general, preferred_element_type=F32)
    sums = each(lambda c: dg(c["sum_bf"], c["g3"], NN), chains)
    sums_t = each(lambda c: dg(c["g3"], c["incl_t_bf"], TN), chains)
    k2b = each(lambda c: c["k2"] * c["bcol"], chains)
    kq = each(lambda c, kb: _dot(jnp.concatenate([kb, c["q2"]], axis=0), c["k2"], NT, passes["quad"]), chains, k2b)
    yield
    gc_row = each(lambda m: jnp.sum(m[0:L2], axis=-1, keepdims=True), sums)
    g_tot = each(lambda m: jnp.sum(m[L2:2 * L2], axis=-1, keepdims=True), sums)
    gc_col = each(lambda m: m[0:1] + m[1:2] + m[2:3], sums_t)
    decay = each(lambda c, r, cl: jnp.exp(jnp.where(c["incl"], r - cl, -1e30)), chains, gc_row, gc_col)
    egc = each(jnp.exp, gc_row)
    l_mat = each(lambda c, m, dc: jnp.where(c["strict"], m[0:L2] * dc, 0.0), chains, kq, decay)
    a_int = each(lambda c, m, dc: jnp.where(c["incl"], m[L2:2 * L2] * dc, 0.0), chains, kq, decay)
    x = each(lambda c, kb, eg: jnp.concatenate([c["v2"] * c["bcol"], kb * eg], axis=1), chains, k2b, egc)
    yield
    x = each(lambda n, xi: xi - _dot(n, xi, NN, passes["solve"]), l_mat, x)
    yield
    npow = l_mat
    for _ in range(int(math.log2(L)) - 1):
        npow = _square_pairs(npow, passes["square"])
        yield
        x = each(lambda n, xi: xi + _dot(n, xi, NN, passes["solve"]), npow, x)
        yield
    qe = each(lambda c, eg: c["q2"] * eg, chains, egc)
    rows = (slice(0, L), slice(L, L2))
    vnew = each(lambda xi, st: [xi[r, 0:B_HD] - _dot(xi[r, B_HD:2 * B_HD], s, NN, passes["state"])
                                for r, s in zip(rows, st)], x, states)
    o1 = each(lambda qi, st: [_dot(qi[r], s, NN, passes["state"]) for r, s in zip(rows, st)], qe, states)
    yield
    o2 = each(lambda a, vn, o: jnp.concatenate(o, axis=0) + _dot(a, jnp.concatenate(vn, axis=0), NN,
                                                                 passes["apply"]), a_int, vnew, o1)
    res["o"] = [[o[r] for r in rows] for o in o2]
    k_rest = each(lambda c, gt, r: c["k2"] * jnp.exp(gt - r), chains, g_tot, gc_row)
    res["s"] = each(lambda st, gt, kr, vn: [s * jnp.exp(gt[r][0:1, :]) + _dot(kr[r], v, TN, passes["state"])
                                            for r, s, v in zip(rows, st, vn)], states, g_tot, k_rest, vnew)


def _mix_out_kernel(yaf_ref, yab_ref, bg_ref, lnw_ref, lnb_ref, bd_ref, obf_ref, obb_ref, z_ref, nw_ref,
                    w_ref, x_ref, mod_ref, gain_ref, rt_ref, o_ref, h_ref, lg_ref, *, k, kffn):
    y = yaf_ref[0] + yab_ref[0]
    ones_bd = bd_ref[...]
    mean = _group_sum(y, ones_bd) * (1.0 / A_HD)
    yc = y - mean
    var = _group_sum(yc * yc, ones_bd) * (1.0 / A_HD)
    yn = yc * lax.rsqrt(var + A_LN_EPS) * lnw_ref[...] + lnb_ref[...]
    bg = bg_ref[0].astype(F32)
    out_a = (yn + bg[:, 0:A_W]) * bg[:, A_W:2 * A_W]
    o = obf_ref[0] + obb_ref[0]
    z = z_ref[0]
    outs = [out_a.astype(BF16)]
    for h in range(B_H):
        oh = o[:, h * B_HD:(h + 1) * B_HD]
        on = oh * lax.rsqrt(jnp.mean(oh * oh, axis=-1, keepdims=True) + NORM_EPS) * nw_ref[...]
        zh = z[:, h * B_HD:(h + 1) * B_HD]
        outs.append((on * (zh * _sigmoid(zh))).astype(BF16))
    mixed = jnp.concatenate(outs, axis=1)
    x = x_ref[0] + mod_ref[0, k:k + 1, :] * jnp.dot(mixed, w_ref[...], preferred_element_type=F32)
    o_ref[0] = x
    _ffn_input(x, gain_ref, mod_ref, rt_ref, kffn, h_ref, lg_ref)


def _mix_out(ya, bg, ln_w, ln_b, ones_bd, ob, zab, norm_w, w_out, x, mod, k, gain_ffn, router_t, kffn, tt):
    g, t, _ = bg.shape
    tile = lambda width: pl.BlockSpec((1, tt, width), lambda gi, i: (gi, i, 0))
    full = lambda arr: pl.BlockSpec(arr.shape, lambda gi, i: (0,) * arr.ndim)
    ins, specs, ospecs, oshapes = _ffn_input_specs(g, t, tt, gain_ffn, router_t)
    return pl.pallas_call(
        functools.partial(_mix_out_kernel, k=k, kffn=kffn),
        grid=(g, t // tt),
        in_specs=[tile(A_W), tile(A_W), tile(2 * A_W), full(ln_w), full(ln_b), full(ones_bd), tile(B_W),
                  tile(B_W), tile(B_W), full(norm_w), full(w_out), tile(D),
                  pl.BlockSpec((1, MOD_ROWS, D), lambda gi, i: (gi, 0, 0))] + specs,
        out_specs=[tile(D)] + ospecs,
        out_shape=[jax.ShapeDtypeStruct((g, t, D), F32)] + oshapes,
        compiler_params=_cparams(("parallel", "parallel")), name="mix_out_proj",
    )(ya[0], ya[1], bg, ln_w, ln_b, ones_bd, ob[0], ob[1], zab, norm_w, w_out, x, mod, *ins)


def _qkv_rope_kernel(*refs, lat_groups, q_cols, rope_cols, seg, has_res, k, kres):
    it = iter(refs)
    x_new, x = _mixer_input(it, has_res, k, kres)
    w_ref, cos_ref, sin_ref = next(it), next(it), next(it)
    if has_res:
        next(it)[0] = x_new
    o_ref = next(it)
    is_lat = pl.program_id(0) < lat_groups
    lane = lax.broadcasted_iota(jnp.int32, (1, LANE), 1)
    first = (lane % (C_HD // 2)) < (C_HD // 4)
    quarter = C_HD // 4
    cos_l = jnp.where(is_lat, cos_ref[...], 1.0)
    sin_l = jnp.where(is_lat, sin_ref[...], 0.0)
    for start in range(0, w_ref.shape[1], seg):
        acc = jnp.dot(x, w_ref[:, start:start + seg], preferred_element_type=F32)
        if start >= rope_cols:
            o_ref[0, :, start:start + seg] = acc.astype(o_ref.dtype)
            continue
        scale = C_HD ** -0.5 if start < q_cols else 1.0
        cos, sin = cos_l * scale, sin_l * scale
        for blk in range(seg // LANE):
            xb = acc[:, blk * LANE:(blk + 1) * LANE]
            partner = jnp.where(first, pltpu.roll(xb, LANE - quarter, axis=1), pltpu.roll(xb, quarter, axis=1))
            col = start + blk * LANE
            o_ref[0, :, col:col + LANE] = (xb * cos + partner * sin).astype(o_ref.dtype)


def _qkv_rope(x, gain, mod, k, res, w, cos, sin, lat_groups, tm, seg, q_cols, rope_cols):
    g, t, kdim = x.shape
    n = w.shape[1]
    assert q_cols % seg == 0 and rope_cols % seg == 0 and n % seg == 0
    ins, specs, tile = _mixer_input_specs(x, gain, mod, res, tm)
    has_res = res is not None
    outs = list(pl.pallas_call(
        functools.partial(_qkv_rope_kernel, lat_groups=lat_groups, q_cols=q_cols, rope_cols=rope_cols, seg=seg,
                          has_res=has_res, k=k, kres=res[2] if has_res else 0),
        grid=(g, t // tm),
        in_specs=specs + [pl.BlockSpec((kdim, n), lambda gi, i: (0, 0)),
                          pl.BlockSpec((tm, LANE), lambda gi, i: (i, 0)),
                          pl.BlockSpec((tm, LANE), lambda gi, i: (i, 0))],
        out_specs=([tile] if has_res else []) + [pl.BlockSpec((1, tm, n), lambda gi, i: (gi, i, 0))],
        out_shape=([jax.ShapeDtypeStruct((g, t, D), F32)] if has_res else [])
        + [jax.ShapeDtypeStruct((g, t, n), BF16)],
        compiler_params=_cparams(("parallel", "parallel")), name="qkv_rope",
    )(*ins, w, cos, sin))
    return (outs.pop(0) if has_res else None), outs[0]


def _attn_kernel(sink_ref, q_ref, kp_ref, km_ref, kn_ref, kc_ref, o_ref, *, n_blocks, kv_w):
    n = pl.program_id(1)
    blk = C_BLOCK
    ii = lax.broadcasted_iota(jnp.int32, (blk, blk), 0)
    jj = lax.broadcasted_iota(jnp.int32, (blk, blk), 1)
    ok_prev = jnp.logical_and(jj >= ii, n > 0)
    ok_next = jnp.logical_and(jj <= ii, n < n_blocks - 1)
    n_ctx = kc_ref.shape[1]
    valid1 = jnp.concatenate([ok_prev, jnp.full((blk, blk), True), ok_next, jnp.full((blk, n_ctx), True)], axis=1)
    valid = jnp.concatenate([valid1] * C_GROUP, axis=0)
    low = lax.broadcasted_iota(jnp.int32, (1, LANE), 1) < C_HD
    row_head = lax.broadcasted_iota(jnp.int32, (C_GROUP * blk, 1), 0) // blk
    q = q_ref[0]
    kv = jnp.concatenate([kp_ref[0], km_ref[0], kn_ref[0], kc_ref[0]], axis=0)
    zero = jnp.zeros((), q.dtype)
    outs = []
    for h in range(C_KVH):
        kh = kv[:, h * LANE:(h + 1) * LANE]
        vh = kv[:, kv_w + h * LANE:kv_w + (h + 1) * LANE]
        qa = q[:, (2 * h) * LANE:(2 * h + 1) * LANE]
        qb = q[:, (2 * h + 1) * LANE:(2 * h + 2) * LANE]
        qs = jnp.concatenate([jnp.where(low, qa, zero), jnp.where(low, zero, qa),
                              jnp.where(low, qb, zero), jnp.where(low, zero, qb)], axis=0)
        s = lax.dot_general(qs, kh, NT, preferred_element_type=F32)
        s = jnp.where(valid, s, -1e30)
        sk = jnp.full((C_GROUP * blk, 1), 0.0, F32)
        for gq in range(C_GROUP):
            sk = jnp.where(row_head == gq, sink_ref[h * C_GROUP + gq], sk)
        m = jnp.maximum(jnp.max(s, axis=-1, keepdims=True), sk)
        p = jnp.exp(s - m)
        denom = jnp.sum(p, axis=-1, keepdims=True) + jnp.exp(sk - m)
        o = lax.dot_general(p.astype(vh.dtype), vh, NN, preferred_element_type=F32) / denom
        outs.append(jnp.where(low, o[0:blk], o[blk:2 * blk]).astype(o_ref.dtype))
        outs.append(jnp.where(low, o[2 * blk:3 * blk], o[3 * blk:4 * blk]).astype(o_ref.dtype))
    o_ref[0] = jnp.concatenate(outs, axis=1)


def _attention(qkv, sink, bsz, ctx_len):
    g, t, _ = qkv.shape
    q_w = C_QH * C_HD
    kv_w = C_KVH * LANE
    nq = q_w // (2 * kv_w)
    assert q_w % (2 * kv_w) == 0
    n_blocks = t // C_BLOCK
    cpb = ctx_len // C_BLOCK
    assert ctx_len % C_BLOCK == 0
    kvspec = lambda fn: pl.BlockSpec((1, C_BLOCK, 2 * kv_w), fn)
    return pl.pallas_call(
        functools.partial(_attn_kernel, n_blocks=n_blocks, kv_w=kv_w),
        grid=(bsz, n_blocks),
        in_specs=[pl.BlockSpec(memory_space=pltpu.SMEM),
                  pl.BlockSpec((1, C_BLOCK, q_w), lambda b, n: (b, n, 0)),
                  kvspec(lambda b, n: (b, jnp.maximum(n - 1, 0), nq)),
                  kvspec(lambda b, n: (b, n, nq)),
                  kvspec(lambda b, n: (b, jnp.minimum(n + 1, n_blocks - 1), nq)),
                  pl.BlockSpec((1, ctx_len, 2 * kv_w), lambda b, n: (bsz, b, nq))],
        out_specs=pl.BlockSpec((1, C_BLOCK, q_w), lambda b, n: (b, n, 0)),
        out_shape=jax.ShapeDtypeStruct((bsz, t, q_w), BF16),
        compiler_params=_cparams(("parallel", "parallel")), name="window_attn",
    )(sink, qkv, qkv, qkv, qkv, qkv)


def _lane_cumsum(x):
    n = x.shape[-1]
    lane = lax.broadcasted_iota(jnp.int32, x.shape, x.ndim - 1)
    k = 1
    while k < n:
        x = x + jnp.where(lane >= k, pltpu.roll(x, k, axis=x.ndim - 1), 0)
        k *= 2
    return x


def _route_kernel(lg_ref, rank_ref, aff_ref, *, cap, slot_stride):
    lg = lg_ref[0]
    m = jnp.max(lg, axis=0, keepdims=True)
    e = jnp.exp(lg - m)
    z = jnp.sum(e, axis=0, keepdims=True)
    aff = e / z
    key = (lg - m) - jnp.log(z)
    count_ge = lambda v: jnp.sum(jnp.where(key >= v, 1, 0), axis=-1, keepdims=True)

    def body(_, carry):
        lo, hi = carry
        mid = 0.5 * (lo + hi)
        ok = count_ge(mid) >= cap
        return jnp.where(ok, mid, lo), jnp.where(ok, hi, mid)

    lo0 = jnp.min(key, axis=-1, keepdims=True)
    lo, hi = lax.fori_loop(0, ROUTE_BISECT, body, (lo0, jnp.ones_like(lo0)))
    thr, found = lo, jnp.zeros(lo.shape, jnp.int32)
    for _ in range(ROUTE_SNAP):
        v = jnp.max(jnp.where(key < hi, key, -3e38), axis=-1, keepdims=True)
        ok = jnp.where(count_ge(v) >= cap, 1, 0)
        thr = jnp.where(jnp.logical_and(found == 0, ok == 1), v, thr)
        hi = jnp.where(jnp.logical_or(found == 1, ok == 1), hi, v)
        found = jnp.maximum(found, ok)
    gt = key > thr
    eq = key == thr
    need = cap - jnp.sum(jnp.where(gt, 1, 0), axis=-1, keepdims=True)
    take_eq = jnp.logical_and(eq, _lane_cumsum(jnp.where(eq, 1, 0)) <= need)
    sel = jnp.logical_or(gt, take_eq)
    slot = _lane_cumsum(jnp.where(sel, 1, 0)) - 1 + pl.program_id(0) * slot_stride
    rank_ref[0] = jnp.where(sel, slot, -1)
    aff_ref[0] = aff


def _route(lg, cap, slot_stride):
    g, e, t = lg.shape
    spec = pl.BlockSpec((1, e, t), lambda gi: (gi, 0, 0))
    return pl.pallas_call(
        functools.partial(_route_kernel, cap=cap, slot_stride=slot_stride),
        grid=(g,), in_specs=[spec], out_specs=[spec, spec],
        out_shape=[jax.ShapeDtypeStruct((g, e, t), jnp.int32), jax.ShapeDtypeStruct((g, e, t), F32)],
        compiler_params=_cparams(("parallel",)), name="route",
    )(lg)


def _expert_kernel(h_ref, rank_ref, aff_ref, w1_ref, w3_ref, w2_ref, o_ref, *, cap):
    e = pl.program_id(1)

    @pl.when(e == 0)
    def _():
        o_ref[...] = jnp.zeros_like(o_ref)

    rank = rank_ref[0, 0]
    t = rank.shape[-1]
    hit = lax.broadcasted_iota(jnp.int32, (cap, t), 0) == rank
    pick = jnp.where(hit, 1.0, 0.0).astype(BF16)
    gate = jnp.sum(jnp.where(hit, aff_ref[0, 0], 0.0), axis=-1, keepdims=True)
    xe = jnp.dot(pick, h_ref[0], preferred_element_type=F32).astype(BF16)
    h1 = jnp.dot(xe, w1_ref[0, 0].astype(BF16), preferred_element_type=F32)
    h3 = jnp.dot(xe, w3_ref[0, 0].astype(BF16), preferred_element_type=F32)
    hid = (h1 * _sigmoid(h1) * h3).astype(BF16)
    ye = (jnp.dot(hid, w2_ref[0], preferred_element_type=F32) * gate).astype(BF16)
    o_ref[0] += lax.dot_general(pick, ye, TN, preferred_element_type=F32)


def _experts(h, rank, aff, w1, w3, w2, layer, cap):
    g, t, _ = h.shape
    _, n_exp, _, f = w1.shape
    sel = pl.BlockSpec((1, 1, 1, t), lambda gi, e: (gi, e, 0, 0))
    return pl.pallas_call(
        functools.partial(_expert_kernel, cap=cap),
        grid=(g, n_exp),
        in_specs=[pl.BlockSpec((1, t, D), lambda gi, e: (gi, 0, 0)), sel, sel,
                  pl.BlockSpec((1, 1, D, f), lambda gi, e: (layer, e, 0, 0)),
                  pl.BlockSpec((1, 1, D, f), lambda gi, e: (layer, e, 0, 0)),
                  pl.BlockSpec((1, f, D), lambda gi, e: (e, 0, 0))],
        out_specs=pl.BlockSpec((1, t, D), lambda gi, e: (gi, 0, 0)),
        out_shape=jax.ShapeDtypeStruct((g, t, D), F32),
        compiler_params=_cparams(("parallel", "arbitrary")), name="experts",
    )(h, rank.reshape(g, n_exp, 1, t), aff.reshape(g, n_exp, 1, t), w1, w3, w2)


def _moe(h, lg, w1, w3, w2, layer, bsz, ctx_len, lat_groups_only):
    g, t, _ = h.shape
    cap_lat = EC_CAP * t // N_EXP
    rank, aff = _route(lg[:bsz], cap_lat, 0)
    if not lat_groups_only:
        cap_ctx = EC_CAP * ctx_len // N_EXP
        lg_c = lg[bsz].reshape(N_EXP, bsz, ctx_len).transpose(1, 0, 2)
        rank_c, aff_c = _route(lg_c, cap_ctx, cap_ctx)
        back = lambda a: a.transpose(1, 0, 2).reshape(1, N_EXP, t)
        rank = jnp.concatenate([rank, back(rank_c)], axis=0)
        aff = jnp.concatenate([aff, back(aff_c)], axis=0)
        assert cap_ctx * bsz == cap_lat
    else:
        h = h[:bsz] if h.shape[0] != bsz else h
    return _experts(h, rank, aff, w1, w3, w2, layer, cap_lat)


def _block_diag2(w):
    z = jnp.zeros_like(w[0])
    return jnp.concatenate([jnp.concatenate([w[0], z], axis=1), jnp.concatenate([z, w[1]], axis=1)], axis=0)


def _rope_tables(t):
    quarter = C_HD // 4
    inv = ROPE_BASE ** (-jnp.arange(quarter, dtype=F32) / quarter)
    pos = jnp.arange(t)
    row = (pos // GRID_W).astype(F32)
    col = (pos % GRID_W).astype(F32)
    lane = np.arange(LANE)
    use_col = ((lane % C_HD) >= C_HD // 2)
    ang = jnp.where(use_col[None, :], col[:, None], row[:, None]) * inv[lane % quarter][None, :]
    sign = np.where((lane % (C_HD // 2)) < quarter, -1.0, 1.0).astype(np.float32)
    return jnp.cos(ang), jnp.sin(ang) * sign[None, :]


def kernel(x, c, ctx, c_ctx, ada_w, ada_b, norm_mix, norm_ffn, e_w_in, e_w_out, a_mu, a_w0, a_w2, a_a0, a_a2, a_g2,
           a_k_k, a_k_a, a_r_k, a_ln_w, a_ln_b, b_conv, b_a_log, b_dt_bias, b_norm, o_w_in, o_w_out, o_sink,
           moe_router, moe_w1, moe_w3, moe_w2, final_norm):
    bsz, t, _ = x.shape
    ctx_len = ctx.shape[1]
    depth = ada_w.shape[0]
    assert bsz * ctx_len == t and ctx_len % C_BLOCK == 0 and t % C_BLOCK == 0
    tt = ctx_len

    rows = 2 * SUB * ((bsz + 1 + 2 * SUB - 1) // (2 * SUB))
    cc = jnp.zeros((rows, D), F32).at[:bsz].set(c).at[bsz].set(c_ctx)
    mods = _adaln(cc, ada_w, ada_b)[:, :bsz + 1].reshape(depth, bsz + 1, N_MOD, D)
    mods = jnp.pad(mods, ((0, 0), (0, 0), (0, MOD_ROWS - N_MOD), (0, 0)))

    xs = jnp.concatenate([x, ctx.reshape(1, t, D)], axis=0)
    ones_bd = jnp.asarray(np.kron(np.eye(A_H), np.ones((A_HD, A_HD))), BF16)
    a_cols = e_w_in.shape[2] - (4 * B_W + 4 * B_H)
    cos_t, sin_t = _rope_tables(t)

    res = None
    for i in range(depth):
        j = i // 2
        mod = mods[i]
        last = i == depth - 1
        if i % 2 == 0:
            w_in = e_w_in[j]
            split_ab = a_cols + 3 * B_W
            w_all = jnp.concatenate([w_in[:, :split_ab], w_in[:, split_ab + 4 * B_H:],
                                     w_in[:, split_ab:split_ab + 4 * B_H],
                                     jnp.zeros((D, LANE - 4 * B_H), F32)], axis=1).astype(BF16)
            x_new, (pa, pb, zab) = _mm_split(xs, norm_mix[i], mod, 0, res, w_all, (a_cols, 3 * B_W, B_W + LANE),
                                             F32, 2 * tt)
            xs = xs if res is None else x_new
            prm = {
                "mu": a_mu[j].reshape(1, -1), "w0": a_w0[j].reshape(1, -1), "w2": _block_diag2(a_w2[j]),
                "a0": a_a0[j].reshape(1, -1), "a2": _block_diag2(a_a2[j]), "g2": a_g2[j],
                "k_k": a_k_k[j].reshape(1, -1), "k_a": jnp.tile(a_k_a[j].reshape(1, -1), (1, 2)),
                "r_k": a_r_k[j].reshape(1, -1), "ones_bd": ones_bd,
            }
            rvk, cum, kd, bd, bg = _rwkv_prep(pa, prm, tt, bsz)
            pad_row = lambda v: jnp.pad(v.reshape(1, -1), ((0, 0), (0, LANE - v.size)))
            qkv_c, gb = _gdn_prep(pb, zab, b_conv[j], pad_row(b_a_log[j]), pad_row(b_dt_bias[j]), tt, bsz)
            ya_f, ob_f, ya_b, ob_b = _even_scan(rvk, cum, kd, bd, qkv_c, gb, bsz, ctx_len)
            xs, hf, lg = _mix_out((ya_f, ya_b), bg, a_ln_w[j].reshape(1, -1), a_ln_b[j].reshape(1, -1), ones_bd,
                                  (ob_f, ob_b), zab, b_norm[j].reshape(1, -1), e_w_out[j].astype(BF16), xs, mod, 2,
                                  norm_ffn[i], moe_router[i].T, 3, tt)
        else:
            assert last
            w_in = o_w_in[j]
            q_w = C_QH * C_HD
            kv_cols = C_KVH * C_HD
            dup = lambda w: jnp.repeat(w.reshape(D, C_KVH, 1, C_HD), 2, axis=2).reshape(D, 2 * kv_cols)
            w_all = jnp.concatenate([w_in[:, :q_w], dup(w_in[:, q_w:q_w + kv_cols]), dup(w_in[:, q_w + kv_cols:])],
                                    axis=1).astype(BF16)
            tn = 2 * kv_cols
            x_new, qkv = _qkv_rope(xs, norm_mix[i], mod, 0, res, w_all, cos_t, sin_t, bsz, 2 * tt, tn, q_w, q_w + tn)
            xs = xs if res is None else x_new
            att = _attention(qkv, o_sink[j], bsz, ctx_len)
            xs, hf, lg = _mm_res(att, o_w_out[j].astype(BF16), xs, mod, 2, norm_ffn[i], moe_router[i].T, 3, tt,
                                 groups=bsz)
        if last:
            xs, hf, lg, mod = xs[:bsz], hf[:bsz], lg[:bsz], mod[:bsz]
        delta = _moe(hf, lg, moe_w1, moe_w3, _to_bf16(moe_w2, i), i, bsz, ctx_len, last)
        res = (delta, mod, 5)
    _, out, _ = _norm(xs, final_norm, None, res=res, tt=tt, out_dtype=F32)
    return out
```

```python
import functools
import math

import jax
import jax.numpy as jnp
import numpy as np
from jax import lax
from jax.experimental import pallas as pl
from jax.experimental.pallas import tpu as pltpu

F32 = jnp.float32
BF16 = jnp.bfloat16
HIGHEST = lax.Precision.HIGHEST

D = 1024
N_MOD = 6
NORM_EPS = 1e-6
A_HD, A_H, A_W = 64, 8, 512
A_LN_EPS = 64e-5
B_HD, B_H, B_W = 128, 4, 512
B_CONV = 5
CHUNK = 64
C_HD, C_QH, C_KVH, C_GROUP = 64, 16, 4, 4
C_BLOCK = 128
ROPE_BASE = 10000.0
GRID_W = 64
N_EXP = 16
EC_CAP = 2
ROUTE_BISECT = 40
ROUTE_SNAP = 3
RWKV_PASSES = {"quad": 1, "apply": 1, "solve": "b", "square": 1, "state_in": "a", "state_out": "b"}
GDN_PASSES = {"quad": 1, "apply": 1, "solve": "b", "square": 1, "state": 1}
MOD_ROWS = 8
LANE = 128
SUB = 8
VMEM_LIMIT = 56 * 1024 * 1024

NT = (((1,), (1,)), ((), ()))
NN = (((1,), (0,)), ((), ()))
TN = (((0,), (0,)), ((), ()))


def _cparams(sem):
    return pltpu.CompilerParams(dimension_semantics=sem, vmem_limit_bytes=VMEM_LIMIT)


def _sigmoid(x):
    return 1.0 / (1.0 + jnp.exp(-x))


def _softplus(x):
    return jnp.maximum(x, 0.0) + jnp.log(1.0 + jnp.exp(-jnp.abs(x)))


def _split2(x):
    hi = x.astype(BF16)
    lo = (x - hi.astype(F32)).astype(BF16)
    return hi, lo


def _dot(a, b, dims=NN, passes=1):
    dg = functools.partial(lax.dot_general, dimension_numbers=dims, preferred_element_type=F32)
    if passes == 1:
        return dg(a.astype(BF16), b.astype(BF16))
    (ka,), (kb,) = dims[0]
    if passes == "a":
        ah, al = _split2(a)
        bh = b.astype(BF16)
        return dg(jnp.concatenate([ah, al], axis=ka), jnp.concatenate([bh, bh], axis=kb))
    if passes == "b":
        bh, bl = _split2(b)
        n = b.shape[1 - kb]
        res = dg(a.astype(BF16), jnp.concatenate([bh, bl], axis=1 - kb))
        return res[:, 0:n] + res[:, n:2 * n]
    ah, al = _split2(a)
    bh, bl = _split2(b)
    return dg(ah, bh) + (dg(ah, bl) + dg(al, bh))


def _group_sum(x, ones_bd):
    hi, lo = _split2(x)
    dg = functools.partial(lax.dot_general, dimension_numbers=NN, preferred_element_type=F32)
    return dg(hi, ones_bd) + dg(lo, ones_bd)


def _cast_kernel(w_ref, o_ref):
    o_ref[...] = w_ref[...].astype(o_ref.dtype)


def _to_bf16(w, layer):
    _, n, r, c = w.shape
    return pl.pallas_call(_cast_kernel, grid=(n,),
                          in_specs=[pl.BlockSpec((1, 1, r, c), lambda e: (layer, e, 0, 0))],
                          out_specs=pl.BlockSpec((1, 1, r, c), lambda e: (0, e, 0, 0)),
                          out_shape=jax.ShapeDtypeStruct((1, n, r, c), BF16),
                          compiler_params=_cparams(("parallel",)), name="cast_bf16")(w)[0]


def _adaln_kernel(c_ref, w_ref, b_ref, o_ref):
    c = c_ref[...]
    s = c * _sigmoid(c)
    o_ref[0] = jnp.dot(s, w_ref[0], precision=HIGHEST, preferred_element_type=F32) + b_ref[0]


def _adaln(cc, ada_w, ada_b):
    depth, _, n = ada_w.shape
    rows = cc.shape[0]
    tn = 768
    return pl.pallas_call(
        _adaln_kernel,
        grid=(depth, n // tn),
        in_specs=[pl.BlockSpec((rows, D), lambda i, j: (0, 0)),
                  pl.BlockSpec((1, D, tn), lambda i, j: (i, 0, j)),
                  pl.BlockSpec((1, 1, tn), lambda i, j: (i, 0, j))],
        out_specs=pl.BlockSpec((1, rows, tn), lambda i, j: (i, 0, j)),
        out_shape=jax.ShapeDtypeStruct((depth, rows, n), F32),
        compiler_params=_cparams(("parallel", "parallel")),
        name="adaln",
    )(cc, ada_w, ada_b.reshape(depth, 1, n))


def _norm_kernel(*refs, k, kres, has_res, has_mod, has_router):
    it = iter(refs)
    x_ref = next(it)
    d_ref = next(it) if has_res else None
    mres_ref = next(it) if has_res else None
    gain_ref = next(it)
    mod_ref = next(it) if has_mod else None
    rt_ref = next(it) if has_router else None
    xo_ref = next(it) if has_res else None
    h_ref = next(it)
    lg_ref = next(it) if has_router else None

    x = x_ref[0]
    if has_res:
        x = x + mres_ref[0, kres:kres + 1, :] * d_ref[0]
        xo_ref[0] = x
    xn = x * lax.rsqrt(jnp.mean(x * x, axis=-1, keepdims=True) + NORM_EPS) * gain_ref[...]
    if has_mod:
        xn = xn * (1.0 + mod_ref[0, k + 1:k + 2, :]) + mod_ref[0, k:k + 1, :]
    h_ref[0] = xn.astype(h_ref.dtype)
    if has_router:
        lg_ref[0] = lax.dot_general(rt_ref[...], xn, NT, precision=HIGHEST, preferred_element_type=F32)


def _norm(x, gain, mod=None, *, k=0, res=None, router_t=None, tt, out_dtype=BF16):
    g, t, _ = x.shape
    has_res, has_mod, has_router = res is not None, mod is not None, router_t is not None
    kres = res[2] if has_res else 0
    tile = pl.BlockSpec((1, tt, D), lambda gi, i: (gi, i, 0))
    modspec = pl.BlockSpec((1, MOD_ROWS, D), lambda gi, i: (gi, 0, 0))
    ins, specs = [x], [tile]
    if has_res:
        ins += [res[0], res[1]]
        specs += [tile, modspec]
    ins.append(gain.reshape(1, D))
    specs.append(pl.BlockSpec((1, D), lambda gi, i: (0, 0)))
    if has_mod:
        ins.append(mod)
        specs.append(modspec)
    if has_router:
        ins.append(router_t)
        specs.append(pl.BlockSpec((N_EXP, D), lambda gi, i: (0, 0)))
    outs, ospecs = [], []
    if has_res:
        outs.append(jax.ShapeDtypeStruct((g, t, D), F32))
        ospecs.append(tile)
    outs.append(jax.ShapeDtypeStruct((g, t, D), out_dtype))
    ospecs.append(tile)
    if has_router:
        outs.append(jax.ShapeDtypeStruct((g, N_EXP, t), F32))
        ospecs.append(pl.BlockSpec((1, N_EXP, tt), lambda gi, i: (gi, 0, i)))
    res_out = list(pl.pallas_call(
        functools.partial(_norm_kernel, k=k, kres=kres, has_res=has_res, has_mod=has_mod, has_router=has_router),
        grid=(g, t // tt), in_specs=specs, out_specs=ospecs, out_shape=outs,
        compiler_params=_cparams(("parallel", "parallel")), name="norm_mod",
    )(*ins))
    x_new = res_out.pop(0) if has_res else None
    h = res_out.pop(0)
    lg = res_out.pop(0) if has_router else None
    return x_new, h, lg


def _mixer_input(refs, has_res, k, kres):
    x = next(refs)[0]
    if has_res:
        d_ref, mres_ref = next(refs), next(refs)
        x = x + mres_ref[0, kres:kres + 1, :] * d_ref[0]
    gain_ref, mod_ref = next(refs), next(refs)
    xn = x * lax.rsqrt(jnp.mean(x * x, axis=-1, keepdims=True) + NORM_EPS) * gain_ref[...]
    h = (xn * (1.0 + mod_ref[0, k + 1:k + 2, :]) + mod_ref[0, k:k + 1, :]).astype(BF16)
    return (x if has_res else None), h


def _mixer_input_specs(x, gain, mod, res, tm):
    tile = pl.BlockSpec((1, tm, D), lambda gi, i: (gi, i, 0))
    modspec = pl.BlockSpec((1, MOD_ROWS, D), lambda gi, i: (gi, 0, 0))
    ins, specs = [x], [tile]
    if res is not None:
        ins += [res[0], res[1]]
        specs += [tile, modspec]
    ins += [gain.reshape(1, D), mod]
    specs += [pl.BlockSpec((1, D), lambda gi, i: (0, 0)), modspec]
    return ins, specs, tile


def _mm_split_kernel(*refs, widths, has_res, k, kres):
    it = iter(refs)
    x_new, h = _mixer_input(it, has_res, k, kres)
    w_ref = next(it)
    if has_res:
        next(it)[0] = x_new
    start = 0
    for o_ref, width in zip(it, widths):
        o_ref[0] = jnp.dot(h, w_ref[:, start:start + width], preferred_element_type=F32).astype(o_ref.dtype)
        start += width


def _mm_split(x, gain, mod, k, res, w, widths, out_dtype, tm):
    g, t, _ = x.shape
    assert sum(widths) == w.shape[1] and all(wd % LANE == 0 for wd in widths)
    ins, specs, tile = _mixer_input_specs(x, gain, mod, res, tm)
    has_res = res is not None
    outs = list(pl.pallas_call(
        functools.partial(_mm_split_kernel, widths=widths, has_res=has_res, k=k, kres=res[2] if has_res else 0),
        grid=(g, t // tm),
        in_specs=specs + [pl.BlockSpec(w.shape, lambda gi, i: (0, 0))],
        out_specs=([tile] if has_res else []) + [pl.BlockSpec((1, tm, wd), lambda gi, i: (gi, i, 0)) for wd in widths],
        out_shape=([jax.ShapeDtypeStruct((g, t, D), F32)] if has_res else [])
        + [jax.ShapeDtypeStruct((g, t, wd), out_dtype) for wd in widths],
        compiler_params=_cparams(("parallel", "parallel")), name="proj",
    )(*ins, w))
    return (outs.pop(0) if has_res else None), outs


def _mm_res_kernel(y_ref, w_ref, x_ref, mod_ref, o_ref, *, k):
    acc = jnp.dot(y_ref[0], w_ref[...], preferred_element_type=F32)
    o_ref[0] = x_ref[0] + mod_ref[0, k:k + 1, :] * acc


def _mm_res(y, w, x, mod, k, tm, groups=None):
    g_all, t, kdim = y.shape
    g = g_all if groups is None else groups
    return pl.pallas_call(
        functools.partial(_mm_res_kernel, k=k),
        grid=(g, t // tm),
        in_specs=[pl.BlockSpec((1, tm, kdim), lambda gi, i: (gi, i, 0)),
                  pl.BlockSpec((kdim, D), lambda gi, i: (0, 0)),
                  pl.BlockSpec((1, tm, D), lambda gi, i: (gi, i, 0)),
                  pl.BlockSpec((1, MOD_ROWS, D), lambda gi, i: (gi, 0, 0))],
        out_specs=pl.BlockSpec((1, tm, D), lambda gi, i: (gi, i, 0)),
        out_shape=jax.ShapeDtypeStruct((g, t, D), F32),
        compiler_params=_cparams(("parallel", "parallel")), name="out_proj_res",
    )(y, w, x, mod)


def _fill_halo(buf, x, pv_ref, nx_ref, has_prev, has_next, tt):
    buf[0:SUB, :] = jnp.where(has_prev, pv_ref[0], 0.0)
    buf[SUB:SUB + tt, :] = x
    buf[SUB + tt:2 * SUB + tt, :] = jnp.where(has_next, nx_ref[0], 0.0)


def _rwkv_prep_kernel(pa_ref, pv_ref, nx_ref, mu_ref, w0_ref, w2_ref, a0_ref, a2_ref, g2_ref, kk_ref, ka_ref,
                      rk_ref, bd_ref, rvk_ref, cum_ref, kd_ref, bdir_ref, bg_ref, buf, *, tt, lat_groups, ntile):
    gi, i = pl.program_id(0), pl.program_id(1)
    is_lat = gi < lat_groups
    x = pa_ref[0]
    _fill_halo(buf, x, pv_ref, nx_ref, jnp.logical_and(is_lat, i > 0), jnp.logical_and(is_lat, i < ntile - 1), tt)
    xm = buf[SUB - 1:SUB - 1 + tt, :]
    xp = buf[SUB + 1:SUB + 1 + tt, :]
    x = x + mu_ref[...] * (0.5 * (xm + xp) - x)
    w = A_W
    r, kx, v = x[:, 0:w], x[:, w:2 * w], x[:, 2 * w:3 * w]
    wd, ad, gd = x[:, 3 * w:3 * w + LANE], x[:, 3 * w + LANE:3 * w + 2 * LANE], x[:, 3 * w + 2 * LANE:3 * w + 3 * LANE]
    w_log = -_softplus(-(w0_ref[...] + _dot(jnp.tanh(wd), w2_ref[...], NN, 3))) - 0.5
    lw = -jnp.exp(w_log)
    a = _sigmoid(a0_ref[...] + _dot(ad, a2_ref[...], NN, 3))
    gate = _dot(_sigmoid(gd), g2_ref[...], NN, 3)
    ones_bd = bd_ref[...]
    kq = kx * kk_ref[...]
    kk = kq * lax.rsqrt(_group_sum(kq * kq, ones_bd) + 1e-6)
    k2 = jnp.concatenate([kx, kx], axis=1)
    kdir = k2 * (1.0 + (a - 1.0) * ka_ref[...])
    bdir = jnp.concatenate([kk, kk], axis=1) * a
    rr = r * rk_ref[...]
    bonus = _group_sum(rr * (kdir[:, 0:w] + kdir[:, w:2 * w]), ones_bd) * v
    rvk_ref[0] = jnp.concatenate([r, v, kk], axis=1)
    cum_ref[0] = jnp.concatenate([_chunk_cumsum(lw[:, 0:w], True), _chunk_cumsum(lw[:, w:2 * w], False)], axis=1)
    kd_ref[0] = kdir.astype(kd_ref.dtype)
    bdir_ref[0] = bdir.astype(bdir_ref.dtype)
    bg_ref[0] = jnp.concatenate([bonus, gate], axis=1).astype(bg_ref.dtype)


def _chunk_cumsum(x, fwd):
    _, _, incl = _order_masks(x.shape[0], CHUNK, fwd)
    mask = jnp.where(incl, 1.0, 0.0).astype(BF16)
    p1 = x.astype(BF16)
    r1 = x - p1.astype(F32)
    p2 = r1.astype(BF16)
    p3 = (r1 - p2.astype(F32)).astype(BF16)
    dg = functools.partial(lax.dot_general, dimension_numbers=NN, preferred_element_type=F32)
    return dg(mask, p1) + (dg(mask, p2) + dg(mask, p3))


def _halo_specs(width, tt, t):
    nb = tt // SUB
    last = t // SUB - 1
    prev = pl.BlockSpec((1, SUB, width), lambda gi, i: (gi, jnp.maximum(i * nb - 1, 0), 0))
    nxt = pl.BlockSpec((1, SUB, width), lambda gi, i: (gi, jnp.minimum((i + 1) * nb, last), 0))
    return prev, nxt


def _rwkv_prep(pa, p, tt, lat_groups):
    g, t, wa = pa.shape
    full = lambda arr: pl.BlockSpec(arr.shape, lambda gi, i: (0,) * arr.ndim)
    tile = lambda width: pl.BlockSpec((1, tt, width), lambda gi, i: (gi, i, 0))
    prev, nxt = _halo_specs(wa, tt, t)
    consts = [p["mu"], p["w0"], p["w2"], p["a0"], p["a2"], p["g2"], p["k_k"], p["k_a"], p["r_k"], p["ones_bd"]]
    widths = (3 * A_W, 2 * A_W, 2 * A_W, 2 * A_W, 2 * A_W)
    return pl.pallas_call(
        functools.partial(_rwkv_prep_kernel, tt=tt, lat_groups=lat_groups, ntile=t // tt),
        grid=(g, t // tt),
        in_specs=[tile(wa), prev, nxt] + [full(c) for c in consts],
        out_specs=[tile(wd) for wd in widths],
        out_shape=[jax.ShapeDtypeStruct((g, t, wd), dt) for wd, dt in zip(widths, (F32, F32, BF16, BF16, BF16))],
        scratch_shapes=[pltpu.VMEM((tt + 2 * SUB, wa), F32)],
        compiler_params=_cparams(("parallel", "parallel")), name="rwkv_prep",
    )(pa, pa, pa, *consts)


def _order_masks(n_rows, blk, fwd):
    ii = lax.broadcasted_iota(jnp.int32, (n_rows, n_rows), 0)
    jj = lax.broadcasted_iota(jnp.int32, (n_rows, n_rows), 1)
    same = (ii // blk) == (jj // blk)
    before = (jj < ii) if fwd else (jj > ii)
    strict = jnp.logical_and(same, before)
    incl = jnp.logical_and(same, jnp.logical_or(before, ii == jj))
    return same, strict, incl


def _even_scan_kernel(rvkf_ref, cumf_ref, kdf_ref, bdf_ref, qkvf_ref, gbf_ref,
                      rvkb_ref, cumb_ref, kdb_ref, bdb_ref, qkvb_ref, gbb_ref,
                      yf_ref, of_ref, yb_ref, ob_ref, sa_ref, sb_ref):
    @pl.when(pl.program_id(1) == 0)
    def _():
        sa_ref[...] = jnp.zeros_like(sa_ref)
        sb_ref[...] = jnp.zeros_like(sb_ref)

    na, nb = A_W // LANE, B_H // 2
    sa, sb = sa_ref[...], sb_ref[...]
    chains_a = (_rwkv_operands(rvkf_ref[0], cumf_ref[0], kdf_ref[0], bdf_ref[0], True)
                + _rwkv_operands(rvkb_ref[0], cumb_ref[0], kdb_ref[0], bdb_ref[0], False))
    chains_b = _gdn_operands(qkvf_ref[0], gbf_ref[0], 0) + _gdn_operands(qkvb_ref[0], gbb_ref[0], 1)
    states_a = [sa[d, p] for d in range(2) for p in range(na)]
    states_b = [(sb[d, 2 * p], sb[d, 2 * p + 1]) for d in range(2) for p in range(nb)]
    res_a, res_b = {}, {}
    stages = [_rwkv_solve(chains_a, states_a, RWKV_PASSES, res_a), _gdn_solve(chains_b, states_b, GDN_PASSES, res_b)]
    while stages:
        stages = [s for s in stages if next(s, "done") != "done"]
    yf_ref[0] = jnp.concatenate(res_a["y"][0:na], axis=1)
    yb_ref[0] = jnp.concatenate(res_a["y"][na:2 * na], axis=1)
    of_ref[0] = jnp.concatenate([o for pair in res_b["o"][0:nb] for o in pair], axis=1)
    ob_ref[0] = jnp.concatenate([o for pair in res_b["o"][nb:2 * nb] for o in pair], axis=1)
    for i, s_i in enumerate(res_a["s"]):
        sa_ref[i // na, i % na] = s_i
    for i, pair in enumerate(res_b["s"]):
        for j, s_h in enumerate(pair):
            sb_ref[i // nb, 2 * (i % nb) + j] = s_h


def _rwkv_operands(rvk, cum, kd, bd, fwd):
    L = CHUNK
    L2 = 2 * L
    _, strict, incl = _order_masks(L2, L, fwd)
    row = lax.broadcasted_iota(jnp.int32, (L, 1), 0)
    if fwd:
        tot = cum[L - 1:L]
        cum_ex = jnp.where(row == 0, 0.0, pltpu.roll(cum, 1, axis=0))
    else:
        tot = cum[0:1]
        cum_ex = jnp.where(row == L - 1, 0.0, pltpu.roll(cum, L - 1, axis=0))
    g_inv = jnp.exp(-cum)
    g_rest = jnp.exp(tot - cum)
    g_tot = jnp.exp(tot)
    w = A_W
    r_t = rvk[:, 0:w] * jnp.exp(cum)
    v = rvk[:, w:2 * w]
    a_t = rvk[:, 2 * w:3 * w] * jnp.exp(cum_ex)
    b_t, k_t = bd * g_inv, kd * g_inv
    b_g, k_g = bd * g_rest, kd * g_rest
    low = lax.broadcasted_iota(jnp.int32, (1, LANE), 1) < A_HD

    def expand(x):
        return jnp.concatenate([jnp.where(low, x, 0.0), jnp.where(low, 0.0, x)], axis=0)

    chains = []
    for p in range(A_W // LANE):
        sl = slice(p * LANE, (p + 1) * LANE)
        chains.append(dict(
            ev=expand(v[:, sl]),
            lhs=jnp.concatenate([expand(a_t[:, sl]), expand(r_t[:, sl])], axis=0),
            rhs=jnp.concatenate([expand(b_t[:, sl]), expand(k_t[:, sl])], axis=0),
            bk=jnp.concatenate([expand(b_g[:, sl]), expand(k_g[:, sl])], axis=0),
            g_col=jnp.transpose(jnp.broadcast_to(g_tot[:, sl], (LANE, LANE))),
            strict=strict, incl=incl))
    return chains


def _square_pairs(mats, passes):
    if passes != 1 or len(mats) % 2:
        return [_dot(m, m, NN, passes) for m in mats]
    out = []
    for m0, m1 in zip(mats[0::2], mats[1::2]):
        n = m0.shape[0]
        m0, m1 = m0.astype(BF16), m1.astype(BF16)
        zero = jnp.zeros_like(m0)
        diag = jnp.concatenate([jnp.concatenate([m0, zero], axis=1), jnp.concatenate([zero, m1], axis=1)], axis=0)
        sq = lax.dot_general(jnp.concatenate([m0, m1], axis=1), diag, NN, preferred_element_type=F32)
        out += [sq[:, 0:n], sq[:, n:2 * n]]
    return out


def _rwkv_solve(chains, states, passes, res):
    L = CHUNK
    L2 = 2 * L
    each = lambda fn, *cols: [fn(*args) for args in zip(*cols)]
    quad = each(lambda c: _dot(c["lhs"], c["rhs"], NT, passes["quad"]), chains)
    yield
    es = each(lambda c, s: _dot(c["lhs"], s, NN, passes["state_in"]), chains, states)
    yield
    n_ab = each(lambda c, q: jnp.where(c["strict"], q[0:L2, 0:L2], 0.0), chains, quad)
    x = each(lambda c, q, e: e[0:L2] + _dot(jnp.where(c["strict"], q[0:L2, L2:2 * L2], 0.0), c["ev"], NN,
                                            passes["apply"]), chains, quad, es)
    yield
    x = each(lambda n, xi: xi - _dot(n, xi, NN, passes["solve"]), n_ab, x)
    yield
    npow = n_ab
    for _ in range(int(math.log2(L)) - 1):
        npow = _square_pairs(npow, passes["square"])
        yield
        x = each(lambda n, xi: xi + _dot(n, xi, NN, passes["solve"]), npow, x)
        yield
    pv = each(lambda c, xi: jnp.concatenate([-xi, c["ev"]], axis=0), chains, x)
    y2 = each(lambda c, q, e, pvi: e[L2:2 * L2] + _dot(
        jnp.concatenate([jnp.where(c["incl"], q[L2:2 * L2, 0:L2], 0.0),
                         jnp.where(c["incl"], q[L2:2 * L2, L2:2 * L2], 0.0)], axis=1), pvi, NN, passes["apply"]),
        chains, quad, es, pv)
    res["y"] = [y[0:L] + y[L:L2] for y in y2]
    yield
    res["s"] = each(lambda c, s, pvi: s * c["g_col"] + _dot(c["bk"], pvi, TN, passes["state_out"]),
                    chains, states, pv)


def _chunk_index(b, c, fwd, n_ctx_chunk, n_lat_chunk, lat_groups):
    in_ctx = c < n_ctx_chunk
    cc = c if fwd else n_ctx_chunk - 1 - c
    lc = c - n_ctx_chunk if fwd else n_lat_chunk - 1 - (c - n_ctx_chunk)
    grp = jnp.where(in_ctx, lat_groups, b)
    chunk = jnp.where(in_ctx, b * n_ctx_chunk + cc, lc)
    return grp, chunk


def _scan_specs(widths, bsz, ctx_len, t):
    ncc, nlc = ctx_len // CHUNK, t // CHUNK

    def spec(width, lane_blk, fwd):
        def index(b, c):
            grp, ch = _chunk_index(b, c, fwd, ncc, nlc, bsz)
            return grp, ch, lane_blk
        return pl.BlockSpec((1, CHUNK, width), index)

    return [[spec(w, (0 if fwd else 1) if blk is None else blk, fwd) for w, blk in widths] for fwd in (True, False)]


def _even_scan(rvk, cum, kd, bd, qkv, gb, bsz, ctx_len):
    g, t, _ = rvk.shape
    ins_f, ins_b = _scan_specs([(3 * A_W, 0), (A_W, None), (A_W, None), (A_W, None), (3 * B_W, 0), (LANE, 0)],
                               bsz, ctx_len, t)
    outs_f, outs_b = _scan_specs([(A_W, 0), (B_W, 0)], bsz, ctx_len, t)
    shapes = [jax.ShapeDtypeStruct((g, t, A_W), F32), jax.ShapeDtypeStruct((g, t, B_W), F32)]
    return pl.pallas_call(
        _even_scan_kernel,
        grid=(bsz, (ctx_len + t) // CHUNK),
        in_specs=ins_f + ins_b,
        out_specs=outs_f + outs_b,
        out_shape=shapes + shapes,
        scratch_shapes=[pltpu.VMEM((2, A_W // LANE, LANE, LANE), F32), pltpu.VMEM((2, B_H, B_HD, B_HD), F32)],
        compiler_params=_cparams(("parallel", "arbitrary")), name="even_scan",
    )(rvk, cum, kd, bd, qkv, gb, rvk, cum, kd, bd, qkv, gb)


def _gdn_prep_kernel(pb_ref, pv_ref, nx_ref, ab_ref, cw_ref, alog_ref, dtb_ref, qkv_ref, gb_ref, buf,
                     *, tt, lat_groups, ntile):
    gi, i = pl.program_id(0), pl.program_id(1)
    is_lat = gi < lat_groups
    _fill_halo(buf, pb_ref[0], pv_ref, nx_ref, jnp.logical_and(is_lat, i > 0),
               jnp.logical_and(is_lat, i < ntile - 1), tt)
    half = B_CONV // 2
    acc = buf[SUB - half:SUB - half + tt, :] * cw_ref[0:1, :]
    for j in range(1, B_CONV):
        acc = acc + buf[SUB - half + j:SUB - half + j + tt, :] * cw_ref[j:j + 1, :]
    act = acc * _sigmoid(acc)
    outs = []
    for h in range(2 * B_H):
        xh = act[:, h * B_HD:(h + 1) * B_HD]
        xh = xh * lax.rsqrt(jnp.sum(xh * xh, axis=-1, keepdims=True) + 1e-6)
        if h < B_H:
            xh = xh * (B_HD ** -0.5)
        outs.append(xh)
    outs.append(act[:, 2 * B_W:3 * B_W])
    qkv_ref[0] = jnp.concatenate(outs, axis=1)
    ab = ab_ref[0]
    lane = lax.broadcasted_iota(jnp.int32, ab.shape, 1)
    gval = -jnp.exp(alog_ref[...]) * _softplus(ab + dtb_ref[...])
    gb_ref[0] = jnp.where(lane < 2 * B_H, gval, _sigmoid(ab))


def _gdn_prep(pb, zab, conv_w, alog_row, dtb_row, tt, lat_groups):
    g, t, wb = pb.shape
    tile = lambda width: pl.BlockSpec((1, tt, width), lambda gi, i: (gi, i, 0))
    full = lambda arr: pl.BlockSpec(arr.shape, lambda gi, i: (0,) * arr.ndim)
    prev, nxt = _halo_specs(wb, tt, t)
    ab_spec = pl.BlockSpec((1, tt, LANE), lambda gi, i: (gi, i, B_W // LANE))
    return pl.pallas_call(
        functools.partial(_gdn_prep_kernel, tt=tt, lat_groups=lat_groups, ntile=t // tt),
        grid=(g, t // tt),
        in_specs=[tile(wb), prev, nxt, ab_spec, full(conv_w), full(alog_row), full(dtb_row)],
        out_specs=[tile(wb), tile(LANE)],
        out_shape=[jax.ShapeDtypeStruct((g, t, wb), F32), jax.ShapeDtypeStruct((g, t, LANE), F32)],
        scratch_shapes=[pltpu.VMEM((tt + 2 * SUB, wb), F32)],
        compiler_params=_cparams(("parallel", "parallel")), name="gdn_prep",
    )(pb, pb, pb, zab, conv_w, alog_row, dtb_row)


def _gdn_operands(qkv, gb, d):
    fwd = d == 0
    L = CHUNK
    L2 = 2 * L
    same, strict, incl = _order_masks(L2, L, fwd)
    _, _, incl_rev = _order_masks(L2, L, not fwd)
    incl_bf = jnp.where(incl, 1.0, 0.0).astype(BF16)
    incl_t_bf = jnp.where(incl_rev, 1.0, 0.0).astype(BF16)
    same_bf = jnp.where(same, 1.0, 0.0).astype(BF16)
    lane = lax.broadcasted_iota(jnp.int32, gb.shape, 1)

    def column(idx):
        return jnp.sum(jnp.where(lane == idx, gb, 0.0), axis=-1, keepdims=True)

    chains = []
    for pr in range(B_H // 2):
        heads = (2 * pr, 2 * pr + 1)
        stack = lambda off: jnp.concatenate([qkv[:, off + h * B_HD:off + (h + 1) * B_HD] for h in heads], axis=0)
        q2, k2, v2 = stack(0), stack(B_W), stack(2 * B_W)
        gcol = jnp.concatenate([column(d * B_H + h) for h in heads], axis=0)
        bcol = jnp.concatenate([column(2 * B_H + d * B_H + h) for h in heads], axis=0)
        chains.append(dict(q2=q2, k2=k2, v2=v2, bcol=bcol, g3=_three_pieces(gcol), strict=strict, incl=incl,
                           sum_bf=jnp.concatenate([incl_bf, same_bf], axis=0), incl_t_bf=incl_t_bf))
    return chains


def _three_pieces(col):
    lane = lax.broadcasted_iota(jnp.int32, (col.shape[0], LANE), 1)
    p1 = col.astype(BF16).astype(F32)
    p2 = (col - p1).astype(BF16).astype(F32)
    p3 = col - p1 - p2
    return jnp.where(lane == 0, p1, jnp.where(lane == 1, p2, jnp.where(lane == 2, p3, 0.0))).astype(BF16)


def _gdn_solve(chains, states, passes, res):
    L = CHUNK
    L2 = 2 * L
    each = lambda fn, *cols: [fn(*args) for args in zip(*cols)]
    dg = functools.partial(lax.dot_general, preferred_element_type=F32)
    sums = each(lambda c: dg(c["sum_bf"], c["g3"], NN), chains)
    sums_t = each(lambda c: dg(c["g3"], c["incl_t_bf"], TN), chains)
    k2b = each(lambda c: c["k2"] * c["bcol"], chains)
    kq = each(lambda c, kb: _dot(jnp.concatenate([kb, c["q2"]], axis=0), c["k2"], NT, passes["quad"]), chains, k2b)
    yield
    gc_row = each(lambda m: jnp.sum(m[0:L2], axis=-1, keepdims=True), sums)
    g_tot = each(lambda m: jnp.sum(m[L2:2 * L2], axis=-1, keepdims=True), sums)
    gc_col = each(lambda m: m[0:1] + m[1:2] + m[2:3], sums_t)
    decay = each(lambda c, r, cl: jnp.exp(jnp.where(c["incl"], r - cl, -1e30)), chains, gc_row, gc_col)
    egc = each(jnp.exp, gc_row)
    l_mat = each(lambda c, m, dc: jnp.where(c["strict"], m[0:L2] * dc, 0.0), chains, kq, decay)
    a_int = each(lambda c, m, dc: jnp.where(c["incl"], m[L2:2 * L2] * dc, 0.0), chains, kq, decay)
    x = each(lambda c, kb, eg: jnp.concatenate([c["v2"] * c["bcol"], kb * eg], axis=1), chains, k2b, egc)
    yield
    x = each(lambda n, xi: xi - _dot(n, xi, NN, passes["solve"]), l_mat, x)
    yield
    npow = l_mat
    for _ in range(int(math.log2(L)) - 1):
        npow = _square_pairs(npow, passes["square"])
        yield
        x = each(lambda n, xi: xi + _dot(n, xi, NN, passes["solve"]), npow, x)
        yield
    qe = each(lambda c, eg: c["q2"] * eg, chains, egc)
    rows = (slice(0, L), slice(L, L2))
    vnew = each(lambda xi, st: [xi[r, 0:B_HD] - _dot(xi[r, B_HD:2 * B_HD], s, NN, passes["state"])
                                for r, s in zip(rows, st)], x, states)
    o1 = each(lambda qi, st: [_dot(qi[r], s, NN, passes["state"]) for r, s in zip(rows, st)], qe, states)
    yield
    o2 = each(lambda a, vn, o: jnp.concatenate(o, axis=0) + _dot(a, jnp.concatenate(vn, axis=0), NN,
                                                                 passes["apply"]), a_int, vnew, o1)
    res["o"] = [[o[r] for r in rows] for o in o2]
    k_rest = each(lambda c, gt, r: c["k2"] * jnp.exp(gt - r), chains, g_tot, gc_row)
    res["s"] = each(lambda st, gt, kr, vn: [s * jnp.exp(gt[r][0:1, :]) + _dot(kr[r], v, TN, passes["state"])
                                            for r, s, v in zip(rows, st, vn)], states, g_tot, k_rest, vnew)


def _mix_out_kernel(yaf_ref, yab_ref, bg_ref, lnw_ref, lnb_ref, bd_ref, obf_ref, obb_ref, z_ref, nw_ref,
                    w_ref, x_ref, mod_ref, o_ref, *, k):
    y = yaf_ref[0] + yab_ref[0]
    ones_bd = bd_ref[...]
    mean = _group_sum(y, ones_bd) * (1.0 / A_HD)
    yc = y - mean
    var = _group_sum(yc * yc, ones_bd) * (1.0 / A_HD)
    yn = yc * lax.rsqrt(var + A_LN_EPS) * lnw_ref[...] + lnb_ref[...]
    bg = bg_ref[0].astype(F32)
    out_a = (yn + bg[:, 0:A_W]) * bg[:, A_W:2 * A_W]
    o = obf_ref[0] + obb_ref[0]
    z = z_ref[0]
    outs = [out_a.astype(BF16)]
    for h in range(B_H):
        oh = o[:, h * B_HD:(h + 1) * B_HD]
        on = oh * lax.rsqrt(jnp.mean(oh * oh, axis=-1, keepdims=True) + NORM_EPS) * nw_ref[...]
        zh = z[:, h * B_HD:(h + 1) * B_HD]
        outs.append((on * (zh * _sigmoid(zh))).astype(BF16))
    mixed = jnp.concatenate(outs, axis=1)
    o_ref[0] = x_ref[0] + mod_ref[0, k:k + 1, :] * jnp.dot(mixed, w_ref[...], preferred_element_type=F32)


def _mix_out(ya, bg, ln_w, ln_b, ones_bd, ob, zab, norm_w, w_out, x, mod, k, tt):
    g, t, _ = bg.shape
    tile = lambda width: pl.BlockSpec((1, tt, width), lambda gi, i: (gi, i, 0))
    full = lambda arr: pl.BlockSpec(arr.shape, lambda gi, i: (0,) * arr.ndim)
    return pl.pallas_call(
        functools.partial(_mix_out_kernel, k=k),
        grid=(g, t // tt),
        in_specs=[tile(A_W), tile(A_W), tile(2 * A_W), full(ln_w), full(ln_b), full(ones_bd), tile(B_W),
                  tile(B_W), tile(B_W), full(norm_w), full(w_out), tile(D),
                  pl.BlockSpec((1, MOD_ROWS, D), lambda gi, i: (gi, 0, 0))],
        out_specs=tile(D),
        out_shape=jax.ShapeDtypeStruct((g, t, D), F32),
        compiler_params=_cparams(("parallel", "parallel")), name="mix_out_proj",
    )(ya[0], ya[1], bg, ln_w, ln_b, ones_bd, ob[0], ob[1], zab, norm_w, w_out, x, mod)


def _qkv_rope_kernel(*refs, lat_groups, q_cols, rope_cols, seg, has_res, k, kres):
    it = iter(refs)
    x_new, x = _mixer_input(it, has_res, k, kres)
    w_ref, cos_ref, sin_ref = next(it), next(it), next(it)
    if has_res:
        next(it)[0] = x_new
    o_ref = next(it)
    is_lat = pl.program_id(0) < lat_groups
    lane = lax.broadcasted_iota(jnp.int32, (1, LANE), 1)
    first = (lane % (C_HD // 2)) < (C_HD // 4)
    quarter = C_HD // 4
    cos_l = jnp.where(is_lat, cos_ref[...], 1.0)
    sin_l = jnp.where(is_lat, sin_ref[...], 0.0)
    starts = list(range(0, w_ref.shape[1], seg))
    product = lambda s0: jnp.dot(x, w_ref[:, s0:s0 + seg], preferred_element_type=F32)
    acc_next = product(starts[0])
    for idx, start in enumerate(starts):
        acc = acc_next
        if idx + 1 < len(starts):
            acc_next = product(starts[idx + 1])
        if start >= rope_cols:
            o_ref[0, :, start:start + seg] = acc.astype(o_ref.dtype)
            continue
        scale = C_HD ** -0.5 if start < q_cols else 1.0
        cos, sin = cos_l * scale, sin_l * scale
        for blk in range(seg // LANE):
            xb = acc[:, blk * LANE:(blk + 1) * LANE]
            partner = jnp.where(first, pltpu.roll(xb, LANE - quarter, axis=1), pltpu.roll(xb, quarter, axis=1))
            col = start + blk * LANE
            o_ref[0, :, col:col + LANE] = (xb * cos + partner * sin).astype(o_ref.dtype)


def _qkv_rope(x, gain, mod, k, res, w, cos, sin, lat_groups, tm, seg, q_cols, rope_cols):
    g, t, kdim = x.shape
    n = w.shape[1]
    assert q_cols % seg == 0 and rope_cols % seg == 0 and n % seg == 0
    ins, specs, tile = _mixer_input_specs(x, gain, mod, res, tm)
    has_res = res is not None
    outs = list(pl.pallas_call(
        functools.partial(_qkv_rope_kernel, lat_groups=lat_groups, q_cols=q_cols, rope_cols=rope_cols, seg=seg,
                          has_res=has_res, k=k, kres=res[2] if has_res else 0),
        grid=(g, t // tm),
        in_specs=specs + [pl.BlockSpec((kdim, n), lambda gi, i: (0, 0)),
                          pl.BlockSpec((tm, LANE), lambda gi, i: (i, 0)),
                          pl.BlockSpec((tm, LANE), lambda gi, i: (i, 0))],
        out_specs=([tile] if has_res else []) + [pl.BlockSpec((1, tm, n), lambda gi, i: (gi, i, 0))],
        out_shape=([jax.ShapeDtypeStruct((g, t, D), F32)] if has_res else [])
        + [jax.ShapeDtypeStruct((g, t, n), BF16)],
        compiler_params=_cparams(("parallel", "parallel")), name="qkv_rope",
    )(*ins, w, cos, sin))
    return (outs.pop(0) if has_res else None), outs[0]


def _attn_kernel(sink_ref, q_ref, kp_ref, km_ref, kn_ref, kc_ref, o_ref, *, n_blocks, kv_w):
    n = pl.program_id(1)
    blk = C_BLOCK
    ii = lax.broadcasted_iota(jnp.int32, (blk, blk), 0)
    jj = lax.broadcasted_iota(jnp.int32, (blk, blk), 1)
    ok_prev = jnp.logical_and(jj >= ii, n > 0)
    ok_next = jnp.logical_and(jj <= ii, n < n_blocks - 1)
    n_ctx = kc_ref.shape[1]
    valid1 = jnp.concatenate([ok_prev, jnp.full((blk, blk), True), ok_next, jnp.full((blk, n_ctx), True)], axis=1)
    valid = jnp.concatenate([valid1] * C_GROUP, axis=0)
    low = lax.broadcasted_iota(jnp.int32, (1, LANE), 1) < C_HD
    row_head = lax.broadcasted_iota(jnp.int32, (C_GROUP * blk, 1), 0) // blk
    q = q_ref[0]
    kv = jnp.concatenate([kp_ref[0], km_ref[0], kn_ref[0], kc_ref[0]], axis=0)
    zero = jnp.zeros((), q.dtype)

    def scores(h):
        kh = kv[:, h * LANE:(h + 1) * LANE]
        qa = q[:, (2 * h) * LANE:(2 * h + 1) * LANE]
        qb = q[:, (2 * h + 1) * LANE:(2 * h + 2) * LANE]
        qs = jnp.concatenate([jnp.where(low, qa, zero), jnp.where(low, zero, qa),
                              jnp.where(low, qb, zero), jnp.where(low, zero, qb)], axis=0)
        return lax.dot_general(qs, kh, NT, preferred_element_type=F32)

    def softmax(h, s):
        s = jnp.where(valid, s, -1e30)
        sk = jnp.full((C_GROUP * blk, 1), 0.0, F32)
        for gq in range(C_GROUP):
            sk = jnp.where(row_head == gq, sink_ref[h * C_GROUP + gq], sk)
        m = jnp.maximum(jnp.max(s, axis=-1, keepdims=True), sk)
        p = jnp.exp(s - m)
        return p.astype(kv.dtype), jnp.sum(p, axis=-1, keepdims=True) + jnp.exp(sk - m)

    def values(h, p, denom):
        vh = kv[:, kv_w + h * LANE:kv_w + (h + 1) * LANE]
        o = lax.dot_general(p, vh, NN, preferred_element_type=F32) / denom
        return [jnp.where(low, o[0:blk], o[blk:2 * blk]).astype(o_ref.dtype),
                jnp.where(low, o[2 * blk:3 * blk], o[3 * blk:4 * blk]).astype(o_ref.dtype)]

    outs = []
    s_next = scores(0)
    for h in range(C_KVH):
        s_cur = s_next
        if h + 1 < C_KVH:
            s_next = scores(h + 1)
        outs += values(h, *softmax(h, s_cur))
    o_ref[0] = jnp.concatenate(outs, axis=1)


def _attention(qkv, sink, bsz, ctx_len):
    g, t, _ = qkv.shape
    q_w = C_QH * C_HD
    kv_w = C_KVH * LANE
    nq = q_w // (2 * kv_w)
    assert q_w % (2 * kv_w) == 0
    n_blocks = t // C_BLOCK
    cpb = ctx_len // C_BLOCK
    assert ctx_len % C_BLOCK == 0
    kvspec = lambda fn: pl.BlockSpec((1, C_BLOCK, 2 * kv_w), fn)
    return pl.pallas_call(
        functools.partial(_attn_kernel, n_blocks=n_blocks, kv_w=kv_w),
        grid=(bsz, n_blocks),
        in_specs=[pl.BlockSpec(memory_space=pltpu.SMEM),
                  pl.BlockSpec((1, C_BLOCK, q_w), lambda b, n: (b, n, 0)),
                  kvspec(lambda b, n: (b, jnp.maximum(n - 1, 0), nq)),
                  kvspec(lambda b, n: (b, n, nq)),
                  kvspec(lambda b, n: (b, jnp.minimum(n + 1, n_blocks - 1), nq)),
                  pl.BlockSpec((1, ctx_len, 2 * kv_w), lambda b, n: (bsz, b, nq))],
        out_specs=pl.BlockSpec((1, C_BLOCK, q_w), lambda b, n: (b, n, 0)),
        out_shape=jax.ShapeDtypeStruct((bsz, t, q_w), BF16),
        compiler_params=_cparams(("parallel", "parallel")), name="window_attn",
    )(sink, qkv, qkv, qkv, qkv, qkv)


def _lane_cumsum(x):
    n = x.shape[-1]
    lane = lax.broadcasted_iota(jnp.int32, x.shape, x.ndim - 1)
    k = 1
    while k < n:
        x = x + jnp.where(lane >= k, pltpu.roll(x, k, axis=x.ndim - 1), 0)
        k *= 2
    return x


def _route_kernel(lg_ref, rank_ref, aff_ref, *, cap, slot_stride):
    lg = lg_ref[0]
    m = jnp.max(lg, axis=0, keepdims=True)
    e = jnp.exp(lg - m)
    z = jnp.sum(e, axis=0, keepdims=True)
    aff = e / z
    key = (lg - m) - jnp.log(z)
    count_ge = lambda v: jnp.sum(jnp.where(key >= v, 1, 0), axis=-1, keepdims=True)

    def body(_, carry):
        lo, hi = carry
        span = hi - lo
        q1, q2, q3 = lo + 0.25 * span, lo + 0.5 * span, lo + 0.75 * span
        ok1, ok2, ok3 = count_ge(q1) >= cap, count_ge(q2) >= cap, count_ge(q3) >= cap
        new_lo = jnp.where(ok3, q3, jnp.where(ok2, q2, jnp.where(ok1, q1, lo)))
        new_hi = jnp.where(ok3, hi, jnp.where(ok2, q3, jnp.where(ok1, q2, q1)))
        return new_lo, new_hi

    lo0 = jnp.min(key, axis=-1, keepdims=True)
    lo, hi = lax.fori_loop(0, ROUTE_BISECT // 2, body, (lo0, jnp.ones_like(lo0)))
    thr, found = lo, jnp.zeros(lo.shape, jnp.int32)
    for _ in range(ROUTE_SNAP):
        v = jnp.max(jnp.where(key < hi, key, -3e38), axis=-1, keepdims=True)
        ok = jnp.where(count_ge(v) >= cap, 1, 0)
        thr = jnp.where(jnp.logical_and(found == 0, ok == 1), v, thr)
        hi = jnp.where(jnp.logical_or(found == 1, ok == 1), hi, v)
        found = jnp.maximum(found, ok)
    gt = key > thr
    eq = key == thr
    need = cap - jnp.sum(jnp.where(gt, 1, 0), axis=-1, keepdims=True)
    take_eq = jnp.logical_and(eq, _lane_cumsum(jnp.where(eq, 1, 0)) <= need)
    sel = jnp.logical_or(gt, take_eq)
    slot = _lane_cumsum(jnp.where(sel, 1, 0)) - 1 + pl.program_id(0) * slot_stride
    rank_ref[0] = jnp.where(sel, slot, -1)
    aff_ref[0] = aff


def _route(lg, cap, slot_stride):
    g, e, t = lg.shape
    spec = pl.BlockSpec((1, e, t), lambda gi: (gi, 0, 0))
    return pl.pallas_call(
        functools.partial(_route_kernel, cap=cap, slot_stride=slot_stride),
        grid=(g,), in_specs=[spec], out_specs=[spec, spec],
        out_shape=[jax.ShapeDtypeStruct((g, e, t), jnp.int32), jax.ShapeDtypeStruct((g, e, t), F32)],
        compiler_params=_cparams(("parallel",)), name="route",
    )(lg)


def _expert_kernel(h_ref, rank_ref, aff_ref, w1_ref, w3_ref, w2_ref, o_ref, *, cap):
    e = pl.program_id(1)

    @pl.when(e == 0)
    def _():
        o_ref[...] = jnp.zeros_like(o_ref)

    rank = rank_ref[0, 0]
    t = rank.shape[-1]
    hit = lax.broadcasted_iota(jnp.int32, (cap, t), 0) == rank
    pick = jnp.where(hit, 1.0, 0.0).astype(BF16)
    gate = jnp.sum(jnp.where(hit, aff_ref[0, 0], 0.0), axis=-1, keepdims=True)
    xe = jnp.dot(pick, h_ref[0], preferred_element_type=F32).astype(BF16)
    h1 = jnp.dot(xe, w1_ref[0, 0].astype(BF16), preferred_element_type=F32)
    h3 = jnp.dot(xe, w3_ref[0, 0].astype(BF16), preferred_element_type=F32)
    hid = (h1 * _sigmoid(h1) * h3).astype(BF16)
    ye = (jnp.dot(hid, w2_ref[0], preferred_element_type=F32) * gate).astype(BF16)
    o_ref[0] += lax.dot_general(pick, ye, TN, preferred_element_type=F32)


def _experts(h, rank, aff, w1, w3, w2, layer, cap):
    g, t, _ = h.shape
    _, n_exp, _, f = w1.shape
    sel = pl.BlockSpec((1, 1, 1, t), lambda gi, e: (gi, e, 0, 0))
    return pl.pallas_call(
        functools.partial(_expert_kernel, cap=cap),
        grid=(g, n_exp),
        in_specs=[pl.BlockSpec((1, t, D), lambda gi, e: (gi, 0, 0)), sel, sel,
                  pl.BlockSpec((1, 1, D, f), lambda gi, e: (layer, e, 0, 0)),
                  pl.BlockSpec((1, 1, D, f), lambda gi, e: (layer, e, 0, 0)),
                  pl.BlockSpec((1, f, D), lambda gi, e: (e, 0, 0))],
        out_specs=pl.BlockSpec((1, t, D), lambda gi, e: (gi, 0, 0)),
        out_shape=jax.ShapeDtypeStruct((g, t, D), F32),
        compiler_params=_cparams(("parallel", "arbitrary")), name="experts",
    )(h, rank.reshape(g, n_exp, 1, t), aff.reshape(g, n_exp, 1, t), w1, w3, w2)


def _moe(h, lg, w1, w3, w2, layer, bsz, ctx_len, lat_groups_only):
    g, t, _ = h.shape
    cap_lat = EC_CAP * t // N_EXP
    rank, aff = _route(lg[:bsz], cap_lat, 0)
    if not lat_groups_only:
        cap_ctx = EC_CAP * ctx_len // N_EXP
        lg_c = lg[bsz].reshape(N_EXP, bsz, ctx_len).transpose(1, 0, 2)
        rank_c, aff_c = _route(lg_c, cap_ctx, cap_ctx)
        back = lambda a: a.transpose(1, 0, 2).reshape(1, N_EXP, t)
        rank = jnp.concatenate([rank, back(rank_c)], axis=0)
        aff = jnp.concatenate([aff, back(aff_c)], axis=0)
        assert cap_ctx * bsz == cap_lat
    else:
        h = h[:bsz] if h.shape[0] != bsz else h
    return _experts(h, rank, aff, w1, w3, w2, layer, cap_lat)


def _block_diag2(w):
    z = jnp.zeros_like(w[0])
    return jnp.concatenate([jnp.concatenate([w[0], z], axis=1), jnp.concatenate([z, w[1]], axis=1)], axis=0)


def _rope_tables(t):
    quarter = C_HD // 4
    inv = ROPE_BASE ** (-jnp.arange(quarter, dtype=F32) / quarter)
    pos = jnp.arange(t)
    row = (pos // GRID_W).astype(F32)
    col = (pos % GRID_W).astype(F32)
    lane = np.arange(LANE)
    use_col = ((lane % C_HD) >= C_HD // 2)
    ang = jnp.where(use_col[None, :], col[:, None], row[:, None]) * inv[lane % quarter][None, :]
    sign = np.where((lane % (C_HD // 2)) < quarter, -1.0, 1.0).astype(np.float32)
    return jnp.cos(ang), jnp.sin(ang) * sign[None, :]


def kernel(x, c, ctx, c_ctx, ada_w, ada_b, norm_mix, norm_ffn, e_w_in, e_w_out, a_mu, a_w0, a_w2, a_a0, a_a2, a_g2,
           a_k_k, a_k_a, a_r_k, a_ln_w, a_ln_b, b_conv, b_a_log, b_dt_bias, b_norm, o_w_in, o_w_out, o_sink,
           moe_router, moe_w1, moe_w3, moe_w2, final_norm):
    bsz, t, _ = x.shape
    ctx_len = ctx.shape[1]
    depth = ada_w.shape[0]
    assert bsz * ctx_len == t and ctx_len % C_BLOCK == 0 and t % C_BLOCK == 0
    tt = ctx_len

    rows = 2 * SUB * ((bsz + 1 + 2 * SUB - 1) // (2 * SUB))
    cc = jnp.zeros((rows, D), F32).at[:bsz].set(c).at[bsz].set(c_ctx)
    mods = _adaln(cc, ada_w, ada_b)[:, :bsz + 1].reshape(depth, bsz + 1, N_MOD, D)
    mods = jnp.pad(mods, ((0, 0), (0, 0), (0, MOD_ROWS - N_MOD), (0, 0)))

    xs = jnp.concatenate([x, ctx.reshape(1, t, D)], axis=0)
    ones_bd = jnp.asarray(np.kron(np.eye(A_H), np.ones((A_HD, A_HD))), BF16)
    a_cols = e_w_in.shape[2] - (4 * B_W + 4 * B_H)
    cos_t, sin_t = _rope_tables(t)

    res = None
    for i in range(depth):
        j = i // 2
        mod = mods[i]
        if i % 2 == 0:
            w_in = e_w_in[j]
            split_ab = a_cols + 3 * B_W
            w_all = jnp.concatenate([w_in[:, :split_ab], w_in[:, split_ab + 4 * B_H:],
                                     w_in[:, split_ab:split_ab + 4 * B_H],
                                     jnp.zeros((D, LANE - 4 * B_H), F32)], axis=1).astype(BF16)
            x_new, (pa, pb, zab) = _mm_split(xs, norm_mix[i], mod, 0, res, w_all, (a_cols, 3 * B_W, B_W + LANE),
                                             F32, 2 * tt)
            xs = xs if res is None else x_new
            prm = {
                "mu": a_mu[j].reshape(1, -1), "w0": a_w0[j].reshape(1, -1), "w2": _block_diag2(a_w2[j]),
                "a0": a_a0[j].reshape(1, -1), "a2": _block_diag2(a_a2[j]), "g2": a_g2[j],
                "k_k": a_k_k[j].reshape(1, -1), "k_a": jnp.tile(a_k_a[j].reshape(1, -1), (1, 2)),
                "r_k": a_r_k[j].reshape(1, -1), "ones_bd": ones_bd,
            }
            rvk, cum, kd, bd, bg = _rwkv_prep(pa, prm, tt, bsz)
            pad_row = lambda v: jnp.pad(v.reshape(1, -1), ((0, 0), (0, LANE - v.size)))
            qkv_c, gb = _gdn_prep(pb, zab, b_conv[j], pad_row(b_a_log[j]), pad_row(b_dt_bias[j]), tt, bsz)
            ya_f, ob_f, ya_b, ob_b = _even_scan(rvk, cum, kd, bd, qkv_c, gb, bsz, ctx_len)
            xs = _mix_out((ya_f, ya_b), bg, a_ln_w[j].reshape(1, -1), a_ln_b[j].reshape(1, -1), ones_bd,
                          (ob_f, ob_b), zab, b_norm[j].reshape(1, -1), e_w_out[j].astype(BF16), xs, mod, 2, tt)
        else:
            w_in = o_w_in[j]
            q_w = C_QH * C_HD
            kv_cols = C_KVH * C_HD
            dup = lambda w: jnp.repeat(w.reshape(D, C_KVH, 1, C_HD), 2, axis=2).reshape(D, 2 * kv_cols)
            w_all = jnp.concatenate([w_in[:, :q_w], dup(w_in[:, q_w:q_w + kv_cols]), dup(w_in[:, q_w + kv_cols:])],
                                    axis=1).astype(BF16)
            tn = 2 * kv_cols
            x_new, qkv = _qkv_rope(xs, norm_mix[i], mod, 0, res, w_all, cos_t, sin_t, bsz, 2 * tt, tn, q_w, q_w + tn)
            xs = xs if res is None else x_new
            att = _attention(qkv, o_sink[j], bsz, ctx_len)
            x_lat = _mm_res(att, o_w_out[j].astype(BF16), xs, mod, 2, tt, groups=bsz)
            xs = x_lat if xs.shape[0] == bsz else jnp.concatenate([x_lat, xs[bsz:]], axis=0)
        last = i == depth - 1
        if last:
            xs = xs[:bsz]
            mod = mod[:bsz]
        _, hf, lg = _norm(xs, norm_ffn[i], mod, k=3, router_t=moe_router[i].T, tt=tt)
        delta = _moe(hf, lg, moe_w1, moe_w3, _to_bf16(moe_w2, i), i, bsz, ctx_len, last)
        res = (delta, mod, 5)
    _, out, _ = _norm(xs, final_norm, None, res=res, tt=tt, out_dtype=F32)
    return out
```

```python
import functools
import math

import jax
import jax.numpy as jnp
import numpy as np
from jax import lax
from jax.experimental import pallas as pl
from jax.experimental.pallas import tpu as pltpu

F32 = jnp.float32
BF16 = jnp.bfloat16
HIGHEST = lax.Precision.HIGHEST

D = 1024
N_MOD = 6
NORM_EPS = 1e-6
A_HD, A_H, A_W = 64, 8, 512
A_LN_EPS = 64e-5
B_HD, B_H, B_W = 128, 4, 512
B_CONV = 5
CHUNK = 64
C_HD, C_QH, C_KVH, C_GROUP = 64, 16, 4, 4
C_BLOCK = 128
ROPE_BASE = 10000.0
GRID_W = 64
N_EXP = 16
EC_CAP = 2
ROUTE_BISECT = 40
ROUTE_SNAP = 3
RWKV_PASSES = {"quad": 1, "apply": 1, "solve": "b", "square": 1, "state_in": "a", "state_out": "b"}
GDN_PASSES = {"quad": 1, "apply": 1, "solve": "b", "square": 1, "state": 1}
MOD_ROWS = 8
LANE = 128
SUB = 8
VMEM_LIMIT = 56 * 1024 * 1024

NT = (((1,), (1,)), ((), ()))
NN = (((1,), (0,)), ((), ()))
TN = (((0,), (0,)), ((), ()))


def _cparams(sem):
    return pltpu.CompilerParams(dimension_semantics=sem, vmem_limit_bytes=VMEM_LIMIT)


def _sigmoid(x):
    return 1.0 / (1.0 + jnp.exp(-x))


def _softplus(x):
    return jnp.maximum(x, 0.0) + jnp.log(1.0 + jnp.exp(-jnp.abs(x)))


def _split2(x):
    hi = x.astype(BF16)
    lo = (x - hi.astype(F32)).astype(BF16)
    return hi, lo


def _dot(a, b, dims=NN, passes=1):
    dg = functools.partial(lax.dot_general, dimension_numbers=dims, preferred_element_type=F32)
    if passes == 1:
        return dg(a.astype(BF16), b.astype(BF16))
    (ka,), (kb,) = dims[0]
    if passes == "a":
        ah, al = _split2(a)
        bh = b.astype(BF16)
        return dg(jnp.concatenate([ah, al], axis=ka), jnp.concatenate([bh, bh], axis=kb))
    if passes == "b":
        bh, bl = _split2(b)
        n = b.shape[1 - kb]
        res = dg(a.astype(BF16), jnp.concatenate([bh, bl], axis=1 - kb))
        return res[:, 0:n] + res[:, n:2 * n]
    ah, al = _split2(a)
    bh, bl = _split2(b)
    return dg(ah, bh) + (dg(ah, bl) + dg(al, bh))


def _group_sum(x, ones_bd):
    hi, lo = _split2(x)
    dg = functools.partial(lax.dot_general, dimension_numbers=NN, preferred_element_type=F32)
    return dg(hi, ones_bd) + dg(lo, ones_bd)


def _cast_kernel(w_ref, o_ref):
    o_ref[...] = w_ref[...].astype(o_ref.dtype)


def _to_bf16(w, layer):
    _, n, r, c = w.shape
    return pl.pallas_call(_cast_kernel, grid=(n,),
                          in_specs=[pl.BlockSpec((1, 1, r, c), lambda e: (layer, e, 0, 0))],
                          out_specs=pl.BlockSpec((1, 1, r, c), lambda e: (0, e, 0, 0)),
                          out_shape=jax.ShapeDtypeStruct((1, n, r, c), BF16),
                          compiler_params=_cparams(("parallel",)), name="cast_bf16")(w)[0]


def _adaln_kernel(c_ref, w_ref, b_ref, o_ref):
    c = c_ref[...]
    s = c * _sigmoid(c)
    o_ref[0] = jnp.dot(s, w_ref[0], precision=HIGHEST, preferred_element_type=F32) + b_ref[0]


def _adaln(cc, ada_w, ada_b):
    depth, _, n = ada_w.shape
    rows = cc.shape[0]
    tn = 768
    return pl.pallas_call(
        _adaln_kernel,
        grid=(depth, n // tn),
        in_specs=[pl.BlockSpec((rows, D), lambda i, j: (0, 0)),
                  pl.BlockSpec((1, D, tn), lambda i, j: (i, 0, j)),
                  pl.BlockSpec((1, 1, tn), lambda i, j: (i, 0, j))],
        out_specs=pl.BlockSpec((1, rows, tn), lambda i, j: (i, 0, j)),
        out_shape=jax.ShapeDtypeStruct((depth, rows, n), F32),
        compiler_params=_cparams(("parallel", "parallel")),
        name="adaln",
    )(cc, ada_w, ada_b.reshape(depth, 1, n))


def _norm_kernel(*refs, k, kres, has_res, has_mod, has_router):
    it = iter(refs)
    x_ref = next(it)
    d_ref = next(it) if has_res else None
    mres_ref = next(it) if has_res else None
    gain_ref = next(it)
    mod_ref = next(it) if has_mod else None
    rt_ref = next(it) if has_router else None
    xo_ref = next(it) if has_res else None
    h_ref = next(it)
    lg_ref = next(it) if has_router else None

    x = x_ref[0]
    if has_res:
        x = x + mres_ref[0, kres:kres + 1, :] * d_ref[0]
        xo_ref[0] = x
    xn = x * lax.rsqrt(jnp.mean(x * x, axis=-1, keepdims=True) + NORM_EPS) * gain_ref[...]
    if has_mod:
        xn = xn * (1.0 + mod_ref[0, k + 1:k + 2, :]) + mod_ref[0, k:k + 1, :]
    h_ref[0] = xn.astype(h_ref.dtype)
    if has_router:
        lg_ref[0] = lax.dot_general(rt_ref[...], xn, NT, precision=HIGHEST, preferred_element_type=F32)


def _norm(x, gain, mod=None, *, k=0, res=None, router_t=None, tt, out_dtype=BF16):
    g, t, _ = x.shape
    has_res, has_mod, has_router = res is not None, mod is not None, router_t is not None
    kres = res[2] if has_res else 0
    tile = pl.BlockSpec((1, tt, D), lambda gi, i: (gi, i, 0))
    modspec = pl.BlockSpec((1, MOD_ROWS, D), lambda gi, i: (gi, 0, 0))
    ins, specs = [x], [tile]
    if has_res:
        ins += [res[0], res[1]]
        specs += [tile, modspec]
    ins.append(gain.reshape(1, D))
    specs.append(pl.BlockSpec((1, D), lambda gi, i: (0, 0)))
    if has_mod:
        ins.append(mod)
        specs.append(modspec)
    if has_router:
        ins.append(router_t)
        specs.append(pl.BlockSpec((N_EXP, D), lambda gi, i: (0, 0)))
    outs, ospecs = [], []
    if has_res:
        outs.append(jax.ShapeDtypeStruct((g, t, D), F32))
        ospecs.append(tile)
    outs.append(jax.ShapeDtypeStruct((g, t, D), out_dtype))
    ospecs.append(tile)
    if has_router:
        outs.append(jax.ShapeDtypeStruct((g, N_EXP, t), F32))
        ospecs.append(pl.BlockSpec((1, N_EXP, tt), lambda gi, i: (gi, 0, i)))
    res_out = list(pl.pallas_call(
        functools.partial(_norm_kernel, k=k, kres=kres, has_res=has_res, has_mod=has_mod, has_router=has_router),
        grid=(g, t // tt), in_specs=specs, out_specs=ospecs, out_shape=outs,
        compiler_params=_cparams(("parallel", "parallel")), name="norm_mod",
    )(*ins))
    x_new = res_out.pop(0) if has_res else None
    h = res_out.pop(0)
    lg = res_out.pop(0) if has_router else None
    return x_new, h, lg


def _mixer_input(refs, has_res, k, kres):
    x = next(refs)[0]
    if has_res:
        d_ref, mres_ref = next(refs), next(refs)
        x = x + mres_ref[0, kres:kres + 1, :] * d_ref[0]
    gain_ref, mod_ref = next(refs), next(refs)
    xn = x * lax.rsqrt(jnp.mean(x * x, axis=-1, keepdims=True) + NORM_EPS) * gain_ref[...]
    h = (xn * (1.0 + mod_ref[0, k + 1:k + 2, :]) + mod_ref[0, k:k + 1, :]).astype(BF16)
    return (x if has_res else None), h


def _mixer_input_specs(x, gain, mod, res, tm):
    tile = pl.BlockSpec((1, tm, D), lambda gi, i: (gi, i, 0))
    modspec = pl.BlockSpec((1, MOD_ROWS, D), lambda gi, i: (gi, 0, 0))
    ins, specs = [x], [tile]
    if res is not None:
        ins += [res[0], res[1]]
        specs += [tile, modspec]
    ins += [gain.reshape(1, D), mod]
    specs += [pl.BlockSpec((1, D), lambda gi, i: (0, 0)), modspec]
    return ins, specs, tile


def _mm_split_kernel(*refs, widths, has_res, k, kres):
    it = iter(refs)
    x_new, h = _mixer_input(it, has_res, k, kres)
    w_ref = next(it)
    if has_res:
        next(it)[0] = x_new
    start = 0
    for o_ref, width in zip(it, widths):
        o_ref[0] = jnp.dot(h, w_ref[:, start:start + width], preferred_element_type=F32).astype(o_ref.dtype)
        start += width


def _mm_split(x, gain, mod, k, res, w, widths, out_dtype, tm):
    g, t, _ = x.shape
    assert sum(widths) == w.shape[1] and all(wd % LANE == 0 for wd in widths)
    ins, specs, tile = _mixer_input_specs(x, gain, mod, res, tm)
    has_res = res is not None
    outs = list(pl.pallas_call(
        functools.partial(_mm_split_kernel, widths=widths, has_res=has_res, k=k, kres=res[2] if has_res else 0),
        grid=(g, t // tm),
        in_specs=specs + [pl.BlockSpec(w.shape, lambda gi, i: (0, 0))],
        out_specs=([tile] if has_res else []) + [pl.BlockSpec((1, tm, wd), lambda gi, i: (gi, i, 0)) for wd in widths],
        out_shape=([jax.ShapeDtypeStruct((g, t, D), F32)] if has_res else [])
        + [jax.ShapeDtypeStruct((g, t, wd), out_dtype) for wd in widths],
        compiler_params=_cparams(("parallel", "parallel")), name="proj",
    )(*ins, w))
    return (outs.pop(0) if has_res else None), outs


def _ffn_input(x, gain_ref, mod_ref, rt_ref, k, h_ref, lg_ref):
    xn = x * lax.rsqrt(jnp.mean(x * x, axis=-1, keepdims=True) + NORM_EPS) * gain_ref[...]
    xn = xn * (1.0 + mod_ref[0, k + 1:k + 2, :]) + mod_ref[0, k:k + 1, :]
    h_ref[0] = xn.astype(h_ref.dtype)
    lg_ref[0] = _dot(xn, rt_ref[...], NN, 3)


def _ffn_input_specs(g, t, tm, gain, router):
    ins = [gain.reshape(1, D), jnp.pad(router, ((0, 0), (0, LANE - router.shape[1])))]
    specs = [pl.BlockSpec((1, D), lambda gi, i: (0, 0)), pl.BlockSpec((D, LANE), lambda gi, i: (0, 0))]
    ospecs = [pl.BlockSpec((1, tm, D), lambda gi, i: (gi, i, 0)), pl.BlockSpec((1, tm, LANE), lambda gi, i: (gi, i, 0))]
    oshapes = [jax.ShapeDtypeStruct((g, t, D), BF16), jax.ShapeDtypeStruct((g, t, LANE), F32)]
    return ins, specs, ospecs, oshapes


def _mm_res_kernel(y_ref, w_ref, x_ref, mod_ref, gain_ref, rt_ref, o_ref, h_ref, lg_ref, *, k, kffn):
    acc = jnp.dot(y_ref[0], w_ref[...], preferred_element_type=F32)
    x = x_ref[0] + mod_ref[0, k:k + 1, :] * acc
    o_ref[0] = x
    _ffn_input(x, gain_ref, mod_ref, rt_ref, kffn, h_ref, lg_ref)


def _mm_res(y, w, x, mod, k, gain_ffn, router, kffn, tm, groups):
    _, t, kdim = y.shape
    g = groups
    ins, specs, ospecs, oshapes = _ffn_input_specs(g, t, tm, gain_ffn, router)
    return pl.pallas_call(
        functools.partial(_mm_res_kernel, k=k, kffn=kffn),
        grid=(g, t // tm),
        in_specs=[pl.BlockSpec((1, tm, kdim), lambda gi, i: (gi, i, 0)),
                  pl.BlockSpec((kdim, D), lambda gi, i: (0, 0)),
                  pl.BlockSpec((1, tm, D), lambda gi, i: (gi, i, 0)),
                  pl.BlockSpec((1, MOD_ROWS, D), lambda gi, i: (gi, 0, 0))] + specs,
        out_specs=[pl.BlockSpec((1, tm, D), lambda gi, i: (gi, i, 0))] + ospecs,
        out_shape=[jax.ShapeDtypeStruct((g, t, D), F32)] + oshapes,
        compiler_params=_cparams(("parallel", "parallel")), name="out_proj_res",
    )(y, w, x, mod, *ins)


def _fill_halo(buf, x, pv_ref, nx_ref, has_prev, has_next, tt):
    buf[0:SUB, :] = jnp.where(has_prev, pv_ref[0], 0.0)
    buf[SUB:SUB + tt, :] = x
    buf[SUB + tt:2 * SUB + tt, :] = jnp.where(has_next, nx_ref[0], 0.0)


def _rwkv_prep_kernel(pa_ref, pv_ref, nx_ref, mu_ref, w0_ref, w2_ref, a0_ref, a2_ref, g2_ref, kk_ref, ka_ref,
                      rk_ref, bd_ref, rvk_ref, cum_ref, kd_ref, bdir_ref, bg_ref, buf, *, tt, lat_groups, ntile):
    gi, i = pl.program_id(0), pl.program_id(1)
    is_lat = gi < lat_groups
    x = pa_ref[0]
    _fill_halo(buf, x, pv_ref, nx_ref, jnp.logical_and(is_lat, i > 0), jnp.logical_and(is_lat, i < ntile - 1), tt)
    xm = buf[SUB - 1:SUB - 1 + tt, :]
    xp = buf[SUB + 1:SUB + 1 + tt, :]
    x = x + mu_ref[...] * (0.5 * (xm + xp) - x)
    w = A_W
    r, kx, v = x[:, 0:w], x[:, w:2 * w], x[:, 2 * w:3 * w]
    wd, ad, gd = x[:, 3 * w:3 * w + LANE], x[:, 3 * w + LANE:3 * w + 2 * LANE], x[:, 3 * w + 2 * LANE:3 * w + 3 * LANE]
    w_log = -_softplus(-(w0_ref[...] + _dot(jnp.tanh(wd), w2_ref[...], NN, 3))) - 0.5
    lw = -jnp.exp(w_log)
    a = _sigmoid(a0_ref[...] + _dot(ad, a2_ref[...], NN, 3))
    gate = _dot(_sigmoid(gd), g2_ref[...], NN, 3)
    ones_bd = bd_ref[...]
    kq = kx * kk_ref[...]
    kk = kq * lax.rsqrt(_group_sum(kq * kq, ones_bd) + 1e-6)
    k2 = jnp.concatenate([kx, kx], axis=1)
    kdir = k2 * (1.0 + (a - 1.0) * ka_ref[...])
    bdir = jnp.concatenate([kk, kk], axis=1) * a
    rr = r * rk_ref[...]
    bonus = _group_sum(rr * (kdir[:, 0:w] + kdir[:, w:2 * w]), ones_bd) * v
    rvk_ref[0] = jnp.concatenate([r, v, kk], axis=1)
    cum_ref[0] = jnp.concatenate([_chunk_cumsum(lw[:, 0:w], True), _chunk_cumsum(lw[:, w:2 * w], False)], axis=1)
    kd_ref[0] = kdir.astype(kd_ref.dtype)
    bdir_ref[0] = bdir.astype(bdir_ref.dtype)
    bg_ref[0] = jnp.concatenate([bonus, gate], axis=1).astype(bg_ref.dtype)


def _chunk_cumsum(x, fwd):
    _, _, incl = _order_masks(x.shape[0], CHUNK, fwd)
    mask = jnp.where(incl, 1.0, 0.0).astype(BF16)
    p1 = x.astype(BF16)
    r1 = x - p1.astype(F32)
    p2 = r1.astype(BF16)
    p3 = (r1 - p2.astype(F32)).astype(BF16)
    dg = functools.partial(lax.dot_general, dimension_numbers=NN, preferred_element_type=F32)
    return dg(mask, p1) + (dg(mask, p2) + dg(mask, p3))


def _halo_specs(width, tt, t):
    nb = tt // SUB
    last = t // SUB - 1
    prev = pl.BlockSpec((1, SUB, width), lambda gi, i: (gi, jnp.maximum(i * nb - 1, 0), 0))
    nxt = pl.BlockSpec((1, SUB, width), lambda gi, i: (gi, jnp.minimum((i + 1) * nb, last), 0))
    return prev, nxt


def _rwkv_prep(pa, p, tt, lat_groups):
    g, t, wa = pa.shape
    full = lambda arr: pl.BlockSpec(arr.shape, lambda gi, i: (0,) * arr.ndim)
    tile = lambda width: pl.BlockSpec((1, tt, width), lambda gi, i: (gi, i, 0))
    prev, nxt = _halo_specs(wa, tt, t)
    consts = [p["mu"], p["w0"], p["w2"], p["a0"], p["a2"], p["g2"], p["k_k"], p["k_a"], p["r_k"], p["ones_bd"]]
    widths = (3 * A_W, 2 * A_W, 2 * A_W, 2 * A_W, 2 * A_W)
    return pl.pallas_call(
        functools.partial(_rwkv_prep_kernel, tt=tt, lat_groups=lat_groups, ntile=t // tt),
        grid=(g, t // tt),
        in_specs=[tile(wa), prev, nxt] + [full(c) for c in consts],
        out_specs=[tile(wd) for wd in widths],
        out_shape=[jax.ShapeDtypeStruct((g, t, wd), dt) for wd, dt in zip(widths, (F32, F32, BF16, BF16, BF16))],
        scratch_shapes=[pltpu.VMEM((tt + 2 * SUB, wa), F32)],
        compiler_params=_cparams(("parallel", "parallel")), name="rwkv_prep",
    )(pa, pa, pa, *consts)


def _order_masks(n_rows, blk, fwd):
    ii = lax.broadcasted_iota(jnp.int32, (n_rows, n_rows), 0)
    jj = lax.broadcasted_iota(jnp.int32, (n_rows, n_rows), 1)
    same = (ii // blk) == (jj // blk)
    before = (jj < ii) if fwd else (jj > ii)
    strict = jnp.logical_and(same, before)
    incl = jnp.logical_and(same, jnp.logical_or(before, ii == jj))
    return same, strict, incl


def _even_scan_kernel(rvkf_ref, cumf_ref, kdf_ref, bdf_ref, qkvf_ref, gbf_ref,
                      rvkb_ref, cumb_ref, kdb_ref, bdb_ref, qkvb_ref, gbb_ref,
                      yf_ref, of_ref, yb_ref, ob_ref, sa_ref, sb_ref):
    @pl.when(pl.program_id(1) == 0)
    def _():
        sa_ref[...] = jnp.zeros_like(sa_ref)
        sb_ref[...] = jnp.zeros_like(sb_ref)

    na, nb = A_W // LANE, B_H // 2
    sa, sb = sa_ref[...], sb_ref[...]
    chains_a = (_rwkv_operands(rvkf_ref[0], cumf_ref[0], kdf_ref[0], bdf_ref[0], True)
                + _rwkv_operands(rvkb_ref[0], cumb_ref[0], kdb_ref[0], bdb_ref[0], False))
    chains_b = _gdn_operands(qkvf_ref[0], gbf_ref[0], 0) + _gdn_operands(qkvb_ref[0], gbb_ref[0], 1)
    states_a = [sa[d, p] for d in range(2) for p in range(na)]
    states_b = [(sb[d, 2 * p], sb[d, 2 * p + 1]) for d in range(2) for p in range(nb)]
    res_a, res_b = {}, {}
    stages = [_rwkv_solve(chains_a, states_a, RWKV_PASSES, res_a), _gdn_solve(chains_b, states_b, GDN_PASSES, res_b)]
    while stages:
        stages = [s for s in stages if next(s, "done") != "done"]
    yf_ref[0] = jnp.concatenate(res_a["y"][0:na], axis=1)
    yb_ref[0] = jnp.concatenate(res_a["y"][na:2 * na], axis=1)
    of_ref[0] = jnp.concatenate([o for pair in res_b["o"][0:nb] for o in pair], axis=1)
    ob_ref[0] = jnp.concatenate([o for pair in res_b["o"][nb:2 * nb] for o in pair], axis=1)
    for i, s_i in enumerate(res_a["s"]):
        sa_ref[i // na, i % na] = s_i
    for i, pair in enumerate(res_b["s"]):
        for j, s_h in enumerate(pair):
            sb_ref[i // nb, 2 * (i % nb) + j] = s_h


def _rwkv_operands(rvk, cum, kd, bd, fwd):
    L = CHUNK
    L2 = 2 * L
    _, strict, incl = _order_masks(L2, L, fwd)
    row = lax.broadcasted_iota(jnp.int32, (L, 1), 0)
    if fwd:
        tot = cum[L - 1:L]
        cum_ex = jnp.where(row == 0, 0.0, pltpu.roll(cum, 1, axis=0))
    else:
        tot = cum[0:1]
        cum_ex = jnp.where(row == L - 1, 0.0, pltpu.roll(cum, L - 1, axis=0))
    g_inv = jnp.exp(-cum)
    g_rest = jnp.exp(tot - cum)
    g_tot = jnp.exp(tot)
    w = A_W
    r_t = rvk[:, 0:w] * jnp.exp(cum)
    v = rvk[:, w:2 * w]
    a_t = rvk[:, 2 * w:3 * w] * jnp.exp(cum_ex)
    b_t, k_t = bd * g_inv, kd * g_inv
    b_g, k_g = bd * g_rest, kd * g_rest
    low = lax.broadcasted_iota(jnp.int32, (1, LANE), 1) < A_HD

    def expand(x):
        return jnp.concatenate([jnp.where(low, x, 0.0), jnp.where(low, 0.0, x)], axis=0)

    chains = []
    for p in range(A_W // LANE):
        sl = slice(p * LANE, (p + 1) * LANE)
        chains.append(dict(
            ev=expand(v[:, sl]),
            lhs=jnp.concatenate([expand(a_t[:, sl]), expand(r_t[:, sl])], axis=0),
            rhs=jnp.concatenate([expand(b_t[:, sl]), expand(k_t[:, sl])], axis=0),
            bk=jnp.concatenate([expand(b_g[:, sl]), expand(k_g[:, sl])], axis=0),
            g_col=jnp.transpose(jnp.broadcast_to(g_tot[:, sl], (LANE, LANE))),
            strict=strict, incl=incl))
    return chains


def _square_pairs(mats, passes):
    if passes != 1 or len(mats) % 2:
        return [_dot(m, m, NN, passes) for m in mats]
    out = []
    for m0, m1 in zip(mats[0::2], mats[1::2]):
        n = m0.shape[0]
        m0, m1 = m0.astype(BF16), m1.astype(BF16)
        zero = jnp.zeros_like(m0)
        diag = jnp.concatenate([jnp.concatenate([m0, zero], axis=1), jnp.concatenate([zero, m1], axis=1)], axis=0)
        sq = lax.dot_general(jnp.concatenate([m0, m1], axis=1), diag, NN, preferred_element_type=F32)
        out += [sq[:, 0:n], sq[:, n:2 * n]]
    return out


def _rwkv_solve(chains, states, passes, res):
    L = CHUNK
    L2 = 2 * L
    each = lambda fn, *cols: [fn(*args) for args in zip(*cols)]
    quad = each(lambda c: _dot(c["lhs"], c["rhs"], NT, passes["quad"]), chains)
    yield
    es = each(lambda c, s: _dot(c["lhs"], s, NN, passes["state_in"]), chains, states)
    yield
    n_ab = each(lambda c, q: jnp.where(c["strict"], q[0:L2, 0:L2], 0.0), chains, quad)
    x = each(lambda c, q, e: e[0:L2] + _dot(jnp.where(c["strict"], q[0:L2, L2:2 * L2], 0.0), c["ev"], NN,
                                            passes["apply"]), chains, quad, es)
    yield
    x = each(lambda n, xi: xi - _dot(n, xi, NN, passes["solve"]), n_ab, x)
    yield
    npow = n_ab
    for _ in range(int(math.log2(L)) - 1):
        npow = _square_pairs(npow, passes["square"])
        yield
        x = each(lambda n, xi: xi + _dot(n, xi, NN, passes["solve"]), npow, x)
        yield
    pv = each(lambda c, xi: jnp.concatenate([-xi, c["ev"]], axis=0), chains, x)
    y2 = each(lambda c, q, e, pvi: e[L2:2 * L2] + _dot(
        jnp.concatenate([jnp.where(c["incl"], q[L2:2 * L2, 0:L2], 0.0),
                         jnp.where(c["incl"], q[L2:2 * L2, L2:2 * L2], 0.0)], axis=1), pvi, NN, passes["apply"]),
        chains, quad, es, pv)
    res["y"] = [y[0:L] + y[L:L2] for y in y2]
    yield
    res["s"] = each(lambda c, s, pvi: s * c["g_col"] + _dot(c["bk"], pvi, TN, passes["state_out"]),
                    chains, states, pv)


def _chunk_index(b, c, fwd, n_ctx_chunk, n_lat_chunk, lat_groups):
    in_ctx = c < n_ctx_chunk
    cc = c if fwd else n_ctx_chunk - 1 - c
    lc = c - n_ctx_chunk if fwd else n_lat_chunk - 1 - (c - n_ctx_chunk)
    grp = jnp.where(in_ctx, lat_groups, b)
    chunk = jnp.where(in_ctx, b * n_ctx_chunk + cc, lc)
    return grp, chunk


def _scan_specs(widths, bsz, ctx_len, t):
    ncc, nlc = ctx_len // CHUNK, t // CHUNK

    def spec(width, lane_blk, fwd):
        def index(b, c):
            grp, ch = _chunk_index(b, c, fwd, ncc, nlc, bsz)
            return grp, ch, lane_blk
        return pl.BlockSpec((1, CHUNK, width), index)

    return [[spec(w, (0 if fwd else 1) if blk is None else blk, fwd) for w, blk in widths] for fwd in (True, False)]


def _even_scan(rvk, cum, kd, bd, qkv, gb, bsz, ctx_len):
    g, t, _ = rvk.shape
    ins_f, ins_b = _scan_specs([(3 * A_W, 0), (A_W, None), (A_W, None), (A_W, None), (3 * B_W, 0), (LANE, 0)],
                               bsz, ctx_len, t)
    outs_f, outs_b = _scan_specs([(A_W, 0), (B_W, 0)], bsz, ctx_len, t)
    shapes = [jax.ShapeDtypeStruct((g, t, A_W), F32), jax.ShapeDtypeStruct((g, t, B_W), F32)]
    return pl.pallas_call(
        _even_scan_kernel,
        grid=(bsz, (ctx_len + t) // CHUNK),
        in_specs=ins_f + ins_b,
        out_specs=outs_f + outs_b,
        out_shape=shapes + shapes,
        scratch_shapes=[pltpu.VMEM((2, A_W // LANE, LANE, LANE), F32), pltpu.VMEM((2, B_H, B_HD, B_HD), F32)],
        compiler_params=_cparams(("parallel", "arbitrary")), name="even_scan",
    )(rvk, cum, kd, bd, qkv, gb, rvk, cum, kd, bd, qkv, gb)


def _gdn_prep_kernel(pb_ref, pv_ref, nx_ref, ab_ref, cw_ref, alog_ref, dtb_ref, qkv_ref, gb_ref, buf,
                     *, tt, lat_groups, ntile):
    gi, i = pl.program_id(0), pl.program_id(1)
    is_lat = gi < lat_groups
    _fill_halo(buf, pb_ref[0], pv_ref, nx_ref, jnp.logical_and(is_lat, i > 0),
               jnp.logical_and(is_lat, i < ntile - 1), tt)
    half = B_CONV // 2
    acc = buf[SUB - half:SUB - half + tt, :] * cw_ref[0:1, :]
    for j in range(1, B_CONV):
        acc = acc + buf[SUB - half + j:SUB - half + j + tt, :] * cw_ref[j:j + 1, :]
    act = acc * _sigmoid(acc)
    outs = []
    for h in range(2 * B_H):
        xh = act[:, h * B_HD:(h + 1) * B_HD]
        xh = xh * lax.rsqrt(jnp.sum(xh * xh, axis=-1, keepdims=True) + 1e-6)
        if h < B_H:
            xh = xh * (B_HD ** -0.5)
        outs.append(xh)
    outs.append(act[:, 2 * B_W:3 * B_W])
    qkv_ref[0] = jnp.concatenate(outs, axis=1)
    ab = ab_ref[0]
    lane = lax.broadcasted_iota(jnp.int32, ab.shape, 1)
    gval = -jnp.exp(alog_ref[...]) * _softplus(ab + dtb_ref[...])
    gb_ref[0] = jnp.where(lane < 2 * B_H, gval, _sigmoid(ab))


def _gdn_prep(pb, zab, conv_w, alog_row, dtb_row, tt, lat_groups):
    g, t, wb = pb.shape
    tile = lambda width: pl.BlockSpec((1, tt, width), lambda gi, i: (gi, i, 0))
    full = lambda arr: pl.BlockSpec(arr.shape, lambda gi, i: (0,) * arr.ndim)
    prev, nxt = _halo_specs(wb, tt, t)
    ab_spec = pl.BlockSpec((1, tt, LANE), lambda gi, i: (gi, i, B_W // LANE))
    return pl.pallas_call(
        functools.partial(_gdn_prep_kernel, tt=tt, lat_groups=lat_groups, ntile=t // tt),
        grid=(g, t // tt),
        in_specs=[tile(wb), prev, nxt, ab_spec, full(conv_w), full(alog_row), full(dtb_row)],
        out_specs=[tile(wb), tile(LANE)],
        out_shape=[jax.ShapeDtypeStruct((g, t, wb), F32), jax.ShapeDtypeStruct((g, t, LANE), F32)],
        scratch_shapes=[pltpu.VMEM((tt + 2 * SUB, wb), F32)],
        compiler_params=_cparams(("parallel", "parallel")), name="gdn_prep",
    )(pb, pb, pb, zab, conv_w, alog_row, dtb_row)


def _gdn_operands(qkv, gb, d):
    fwd = d == 0
    L = CHUNK
    L2 = 2 * L
    same, strict, incl = _order_masks(L2, L, fwd)
    _, _, incl_rev = _order_masks(L2, L, not fwd)
    incl_bf = jnp.where(incl, 1.0, 0.0).astype(BF16)
    incl_t_bf = jnp.where(incl_rev, 1.0, 0.0).astype(BF16)
    same_bf = jnp.where(same, 1.0, 0.0).astype(BF16)
    lane = lax.broadcasted_iota(jnp.int32, gb.shape, 1)

    def column(idx):
        return jnp.sum(jnp.where(lane == idx, gb, 0.0), axis=-1, keepdims=True)

    chains = []
    for pr in range(B_H // 2):
        heads = (2 * pr, 2 * pr + 1)
        stack = lambda off: jnp.concatenate([qkv[:, off + h * B_HD:off + (h + 1) * B_HD] for h in heads], axis=0)
        q2, k2, v2 = stack(0), stack(B_W), stack(2 * B_W)
        gcol = jnp.concatenate([column(d * B_H + h) for h in heads], axis=0)
        bcol = jnp.concatenate([column(2 * B_H + d * B_H + h) for h in heads], axis=0)
        chains.append(dict(q2=q2, k2=k2, v2=v2, bcol=bcol, g3=_three_pieces(gcol), strict=strict, incl=incl,
                           sum_bf=jnp.concatenate([incl_bf, same_bf], axis=0), incl_t_bf=incl_t_bf))
    return chains


def _three_pieces(col):
    lane = lax.broadcasted_iota(jnp.int32, (col.shape[0], LANE), 1)
    p1 = col.astype(BF16).astype(F32)
    p2 = (col - p1).astype(BF16).astype(F32)
    p3 = col - p1 - p2
    return jnp.where(lane == 0, p1, jnp.where(lane == 1, p2, jnp.where(lane == 2, p3, 0.0))).astype(BF16)


def _gdn_solve(chains, states, passes, res):
    L = CHUNK
    L2 = 2 * L
    each = lambda fn, *cols: [fn(*args) for args in zip(*cols)]
    dg = functools.partial(lax.dot_general, preferred_element_type=F32)
    sums = each(lambda c: dg(c["sum_bf"], c["g3"], NN), chains)
    sums_t = each(lambda c: dg(c["g3"], c["incl_t_bf"], TN), chains)
    k2b = each(lambda c: c["k2"] * c["bcol"], chains)
    kq = each(lambda c, kb: _dot(jnp.concatenate([kb, c["q2"]], axis=0), c["k2"], NT, passes["quad"]), chains, k2b)
    yield
    gc_row = each(lambda m: jnp.sum(m[0:L2], axis=-1, keepdims=True), sums)
    g_tot = each(lambda m: jnp.sum(m[L2:2 * L2], axis=-1, keepdims=True), sums)
    gc_col = each(lambda m: m[0:1] + m[1:2] + m[2:3], sums_t)
    decay = each(lambda c, r, cl: jnp.exp(jnp.where(c["incl"], r - cl, -1e30)), chains, gc_row, gc_col)
    egc = each(jnp.exp, gc_row)
    l_mat = each(lambda c, m, dc: jnp.where(c["strict"], m[0:L2] * dc, 0.0), chains, kq, decay)
    a_int = each(lambda c, m, dc: jnp.where(c["incl"], m[L2:2 * L2] * dc, 0.0), chains, kq, decay)
    x = each(lambda c, kb, eg: jnp.concatenate([c["v2"] * c["bcol"], kb * eg], axis=1), chains, k2b, egc)
    yield
    x = each(lambda n, xi: xi - _dot(n, xi, NN, passes["solve"]), l_mat, x)
    yield
    npow = l_mat
    for _ in range(int(math.log2(L)) - 1):
        npow = _square_pairs(npow, passes["square"])
        yield
        x = each(lambda n, xi: xi + _dot(n, xi, NN, passes["solve"]), npow, x)
        yield
    qe = each(lambda c, eg: c["q2"] * eg, chains, egc)
    rows = (slice(0, L), slice(L, L2))
    vnew = each(lambda xi, st: [xi[r, 0:B_HD] - _dot(xi[r, B_HD:2 * B_HD], s, NN, passes["state"])
                                for r, s in zip(rows, st)], x, states)
    o1 = each(lambda qi, st: [_dot(qi[r], s, NN, passes["state"]) for r, s in zip(rows, st)], qe, states)
    yield
    o2 = each(lambda a, vn, o: jnp.concatenate(o, axis=0) + _dot(a, jnp.concatenate(vn, axis=0), NN,
                                                                 passes["apply"]), a_int, vnew, o1)
    res["o"] = [[o[r] for r in rows] for o in o2]
    k_rest = each(lambda c, gt, r: c["k2"] * jnp.exp(gt - r), chains, g_tot, gc_row)
    res["s"] = each(lambda st, gt, kr, vn: [s * jnp.exp(gt[r][0:1, :]) + _dot(kr[r], v, TN, passes["state"])
                                            for r, s, v in zip(rows, st, vn)], states, g_tot, k_rest, vnew)


def _mix_out_kernel(yaf_ref, yab_ref, bg_ref, lnw_ref, lnb_ref, bd_ref, obf_ref, obb_ref, z_ref, nw_ref,
                    w_ref, x_ref, mod_ref, gain_ref, rt_ref, o_ref, h_ref, lg_ref, *, k, kffn):
    y = yaf_ref[0] + yab_ref[0]
    ones_bd = bd_ref[...]
    mean = _group_sum(y, ones_bd) * (1.0 / A_HD)
    yc = y - mean
    var = _group_sum(yc * yc, ones_bd) * (1.0 / A_HD)
    yn = yc * lax.rsqrt(var + A_LN_EPS) * lnw_ref[...] + lnb_ref[...]
    bg = bg_ref[0].astype(F32)
    out_a = (yn + bg[:, 0:A_W]) * bg[:, A_W:2 * A_W]
    o = obf_ref[0] + obb_ref[0]
    z = z_ref[0]
    outs = [out_a.astype(BF16)]
    for h in range(B_H):
        oh = o[:, h * B_HD:(h + 1) * B_HD]
        on = oh * lax.rsqrt(jnp.mean(oh * oh, axis=-1, keepdims=True) + NORM_EPS) * nw_ref[...]
        zh = z[:, h * B_HD:(h + 1) * B_HD]
        outs.append((on * (zh * _sigmoid(zh))).astype(BF16))
    mixed = jnp.concatenate(outs, axis=1)
    x = x_ref[0] + mod_ref[0, k:k + 1, :] * jnp.dot(mixed, w_ref[...], preferred_element_type=F32)
    o_ref[0] = x
    _ffn_input(x, gain_ref, mod_ref, rt_ref, kffn, h_ref, lg_ref)


def _mix_out(ya, bg, ln_w, ln_b, ones_bd, ob, zab, norm_w, w_out, x, mod, k, gain_ffn, router, kffn, tt):
    g, t, _ = bg.shape
    tile = lambda width: pl.BlockSpec((1, tt, width), lambda gi, i: (gi, i, 0))
    full = lambda arr: pl.BlockSpec(arr.shape, lambda gi, i: (0,) * arr.ndim)
    ins, specs, ospecs, oshapes = _ffn_input_specs(g, t, tt, gain_ffn, router)
    return pl.pallas_call(
        functools.partial(_mix_out_kernel, k=k, kffn=kffn),
        grid=(g, t // tt),
        in_specs=[tile(A_W), tile(A_W), tile(2 * A_W), full(ln_w), full(ln_b), full(ones_bd), tile(B_W),
                  tile(B_W), tile(B_W), full(norm_w), full(w_out), tile(D),
                  pl.BlockSpec((1, MOD_ROWS, D), lambda gi, i: (gi, 0, 0))] + specs,
        out_specs=[tile(D)] + ospecs,
        out_shape=[jax.ShapeDtypeStruct((g, t, D), F32)] + oshapes,
        compiler_params=_cparams(("parallel", "parallel")), name="mix_out_proj",
    )(ya[0], ya[1], bg, ln_w, ln_b, ones_bd, ob[0], ob[1], zab, norm_w, w_out, x, mod, *ins)


def _qkv_rope_kernel(*refs, lat_groups, q_cols, rope_cols, seg, has_res, k, kres):
    it = iter(refs)
    x_new, x = _mixer_input(it, has_res, k, kres)
    w_ref, cos_ref, sin_ref = next(it), next(it), next(it)
    if has_res:
        next(it)[0] = x_new
    o_ref = next(it)
    is_lat = pl.program_id(0) < lat_groups
    lane = lax.broadcasted_iota(jnp.int32, (1, LANE), 1)
    first = (lane % (C_HD // 2)) < (C_HD // 4)
    quarter = C_HD // 4
    cos_l = jnp.where(is_lat, cos_ref[...], 1.0)
    sin_l = jnp.where(is_lat, sin_ref[...], 0.0)
    starts = list(range(0, w_ref.shape[1], seg))
    product = lambda s0: jnp.dot(x, w_ref[:, s0:s0 + seg], preferred_element_type=F32)
    acc_next = product(starts[0])
    for idx, start in enumerate(starts):
        acc = acc_next
        if idx + 1 < len(starts):
            acc_next = product(starts[idx + 1])
        if start >= rope_cols:
            o_ref[0, :, start:start + seg] = acc.astype(o_ref.dtype)
            continue
        scale = C_HD ** -0.5 if start < q_cols else 1.0
        cos, sin = cos_l * scale, sin_l * scale
        for blk in range(seg // LANE):
            xb = acc[:, blk * LANE:(blk + 1) * LANE]
            partner = jnp.where(first, pltpu.roll(xb, LANE - quarter, axis=1), pltpu.roll(xb, quarter, axis=1))
            col = start + blk * LANE
            o_ref[0, :, col:col + LANE] = (xb * cos + partner * sin).astype(o_ref.dtype)


def _qkv_rope(x, gain, mod, k, res, w, cos, sin, lat_groups, tm, seg, q_cols, rope_cols):
    g, t, kdim = x.shape
    n = w.shape[1]
    assert q_cols % seg == 0 and rope_cols % seg == 0 and n % seg == 0
    ins, specs, tile = _mixer_input_specs(x, gain, mod, res, tm)
    has_res = res is not None
    outs = list(pl.pallas_call(
        functools.partial(_qkv_rope_kernel, lat_groups=lat_groups, q_cols=q_cols, rope_cols=rope_cols, seg=seg,
                          has_res=has_res, k=k, kres=res[2] if has_res else 0),
        grid=(g, t // tm),
        in_specs=specs + [pl.BlockSpec((kdim, n), lambda gi, i: (0, 0)),
                          pl.BlockSpec((tm, LANE), lambda gi, i: (i, 0)),
                          pl.BlockSpec((tm, LANE), lambda gi, i: (i, 0))],
        out_specs=([tile] if has_res else []) + [pl.BlockSpec((1, tm, n), lambda gi, i: (gi, i, 0))],
        out_shape=([jax.ShapeDtypeStruct((g, t, D), F32)] if has_res else [])
        + [jax.ShapeDtypeStruct((g, t, n), BF16)],
        compiler_params=_cparams(("parallel", "parallel")), name="qkv_rope",
    )(*ins, w, cos, sin))
    return (outs.pop(0) if has_res else None), outs[0]


def _attn_kernel(sink_ref, q_ref, kp_ref, km_ref, kn_ref, kc_ref, o_ref, *, n_blocks, kv_w):
    n = pl.program_id(1)
    blk = C_BLOCK
    ii = lax.broadcasted_iota(jnp.int32, (blk, blk), 0)
    jj = lax.broadcasted_iota(jnp.int32, (blk, blk), 1)
    ok_prev = jnp.logical_and(jj >= ii, n > 0)
    ok_next = jnp.logical_and(jj <= ii, n < n_blocks - 1)
    n_ctx = kc_ref.shape[1]
    valid1 = jnp.concatenate([ok_prev, jnp.full((blk, blk), True), ok_next, jnp.full((blk, n_ctx), True)], axis=1)
    valid = jnp.concatenate([valid1] * C_GROUP, axis=0)
    low = lax.broadcasted_iota(jnp.int32, (1, LANE), 1) < C_HD
    row_head = lax.broadcasted_iota(jnp.int32, (C_GROUP * blk, 1), 0) // blk
    q = q_ref[0]
    kv = jnp.concatenate([kp_ref[0], km_ref[0], kn_ref[0], kc_ref[0]], axis=0)
    zero = jnp.zeros((), q.dtype)

    def scores(h):
        kh = kv[:, h * LANE:(h + 1) * LANE]
        qa = q[:, (2 * h) * LANE:(2 * h + 1) * LANE]
        qb = q[:, (2 * h + 1) * LANE:(2 * h + 2) * LANE]
        qs = jnp.concatenate([jnp.where(low, qa, zero), jnp.where(low, zero, qa),
                              jnp.where(low, qb, zero), jnp.where(low, zero, qb)], axis=0)
        return lax.dot_general(qs, kh, NT, preferred_element_type=F32)

    def softmax(h, s):
        s = jnp.where(valid, s, -1e30)
        sk = jnp.full((C_GROUP * blk, 1), 0.0, F32)
        for gq in range(C_GROUP):
            sk = jnp.where(row_head == gq, sink_ref[h * C_GROUP + gq], sk)
        m = jnp.maximum(jnp.max(s, axis=-1, keepdims=True), sk)
        p = jnp.exp(s - m)
        return p.astype(kv.dtype), jnp.sum(p, axis=-1, keepdims=True) + jnp.exp(sk - m)

    def values(h, p, denom):
        vh = kv[:, kv_w + h * LANE:kv_w + (h + 1) * LANE]
        o = lax.dot_general(p, vh, NN, preferred_element_type=F32) / denom
        return [jnp.where(low, o[0:blk], o[blk:2 * blk]).astype(o_ref.dtype),
                jnp.where(low, o[2 * blk:3 * blk], o[3 * blk:4 * blk]).astype(o_ref.dtype)]

    outs = []
    s_next = scores(0)
    for h in range(C_KVH):
        s_cur = s_next
        if h + 1 < C_KVH:
            s_next = scores(h + 1)
        outs += values(h, *softmax(h, s_cur))
    o_ref[0] = jnp.concatenate(outs, axis=1)


def _attention(qkv, sink, bsz, ctx_len):
    g, t, _ = qkv.shape
    q_w = C_QH * C_HD
    kv_w = C_KVH * LANE
    nq = q_w // (2 * kv_w)
    assert q_w % (2 * kv_w) == 0
    n_blocks = t // C_BLOCK
    cpb = ctx_len // C_BLOCK
    assert ctx_len % C_BLOCK == 0
    kvspec = lambda fn: pl.BlockSpec((1, C_BLOCK, 2 * kv_w), fn)
    return pl.pallas_call(
        functools.partial(_attn_kernel, n_blocks=n_blocks, kv_w=kv_w),
        grid=(bsz, n_blocks),
        in_specs=[pl.BlockSpec(memory_space=pltpu.SMEM),
                  pl.BlockSpec((1, C_BLOCK, q_w), lambda b, n: (b, n, 0)),
                  kvspec(lambda b, n: (b, jnp.maximum(n - 1, 0), nq)),
                  kvspec(lambda b, n: (b, n, nq)),
                  kvspec(lambda b, n: (b, jnp.minimum(n + 1, n_blocks - 1), nq)),
                  pl.BlockSpec((1, ctx_len, 2 * kv_w), lambda b, n: (bsz, b, nq))],
        out_specs=pl.BlockSpec((1, C_BLOCK, q_w), lambda b, n: (b, n, 0)),
        out_shape=jax.ShapeDtypeStruct((bsz, t, q_w), BF16),
        compiler_params=_cparams(("parallel", "parallel")), name="window_attn",
    )(sink, qkv, qkv, qkv, qkv, qkv)


def _lane_cumsum(x):
    n = x.shape[-1]
    lane = lax.broadcasted_iota(jnp.int32, x.shape, x.ndim - 1)
    k = 1
    while k < n:
        x = x + jnp.where(lane >= k, pltpu.roll(x, k, axis=x.ndim - 1), 0)
        k *= 2
    return x


def _route_kernel(lg_ref, rank_ref, aff_ref, *, cap, slot_stride):
    lg = lg_ref[0]
    m = jnp.max(lg, axis=0, keepdims=True)
    e = jnp.exp(lg - m)
    z = jnp.sum(e, axis=0, keepdims=True)
    aff = e / z
    key = (lg - m) - jnp.log(z)
    count_ge = lambda v: jnp.sum(jnp.where(key >= v, 1, 0), axis=-1, keepdims=True)

    def body(_, carry):
        lo, hi = carry
        span = hi - lo
        q1, q2, q3 = lo + 0.25 * span, lo + 0.5 * span, lo + 0.75 * span
        ok1, ok2, ok3 = count_ge(q1) >= cap, count_ge(q2) >= cap, count_ge(q3) >= cap
        new_lo = jnp.where(ok3, q3, jnp.where(ok2, q2, jnp.where(ok1, q1, lo)))
        new_hi = jnp.where(ok3, hi, jnp.where(ok2, q3, jnp.where(ok1, q2, q1)))
        return new_lo, new_hi

    lo0 = jnp.min(key, axis=-1, keepdims=True)
    lo, hi = lax.fori_loop(0, ROUTE_BISECT // 2, body, (lo0, jnp.ones_like(lo0)))
    thr, found = lo, jnp.zeros(lo.shape, jnp.int32)
    for _ in range(ROUTE_SNAP):
        v = jnp.max(jnp.where(key < hi, key, -3e38), axis=-1, keepdims=True)
        ok = jnp.where(count_ge(v) >= cap, 1, 0)
        thr = jnp.where(jnp.logical_and(found == 0, ok == 1), v, thr)
        hi = jnp.where(jnp.logical_or(found == 1, ok == 1), hi, v)
        found = jnp.maximum(found, ok)
    gt = key > thr
    eq = key == thr
    need = cap - jnp.sum(jnp.where(gt, 1, 0), axis=-1, keepdims=True)
    take_eq = jnp.logical_and(eq, _lane_cumsum(jnp.where(eq, 1, 0)) <= need)
    sel = jnp.logical_or(gt, take_eq)
    slot = _lane_cumsum(jnp.where(sel, 1, 0)) - 1 + pl.program_id(0) * slot_stride
    rank_ref[0] = jnp.where(sel, slot, -1)
    aff_ref[0] = aff


def _route(lg, cap, slot_stride):
    g, e, t = lg.shape
    spec = pl.BlockSpec((1, e, t), lambda gi: (gi, 0, 0))
    return pl.pallas_call(
        functools.partial(_route_kernel, cap=cap, slot_stride=slot_stride),
        grid=(g,), in_specs=[spec], out_specs=[spec, spec],
        out_shape=[jax.ShapeDtypeStruct((g, e, t), jnp.int32), jax.ShapeDtypeStruct((g, e, t), F32)],
        compiler_params=_cparams(("parallel",)), name="route",
    )(lg)


def _expert_kernel(h_ref, rank_ref, aff_ref, w1_ref, w3_ref, w2_ref, o_ref, *, cap):
    e = pl.program_id(1)

    @pl.when(e == 0)
    def _():
        o_ref[...] = jnp.zeros_like(o_ref)

    rank = rank_ref[0, 0]
    t = rank.shape[-1]
    hit = lax.broadcasted_iota(jnp.int32, (cap, t), 0) == rank
    pick = jnp.where(hit, 1.0, 0.0).astype(BF16)
    gate = jnp.sum(jnp.where(hit, aff_ref[0, 0], 0.0), axis=-1, keepdims=True)
    xe = jnp.dot(pick, h_ref[0], preferred_element_type=F32).astype(BF16)
    h1 = jnp.dot(xe, w1_ref[0, 0].astype(BF16), preferred_element_type=F32)
    h3 = jnp.dot(xe, w3_ref[0, 0].astype(BF16), preferred_element_type=F32)
    hid = (h1 * _sigmoid(h1) * h3).astype(BF16)
    ye = (jnp.dot(hid, w2_ref[0], preferred_element_type=F32) * gate).astype(BF16)
    o_ref[0] += lax.dot_general(pick, ye, TN, preferred_element_type=F32)


def _experts(h, rank, aff, w1, w3, w2, layer, cap):
    g, t, _ = h.shape
    _, n_exp, _, f = w1.shape
    sel = pl.BlockSpec((1, 1, 1, t), lambda gi, e: (gi, e, 0, 0))
    return pl.pallas_call(
        functools.partial(_expert_kernel, cap=cap),
        grid=(g, n_exp),
        in_specs=[pl.BlockSpec((1, t, D), lambda gi, e: (gi, 0, 0)), sel, sel,
                  pl.BlockSpec((1, 1, D, f), lambda gi, e: (layer, e, 0, 0)),
                  pl.BlockSpec((1, 1, D, f), lambda gi, e: (layer, e, 0, 0)),
                  pl.BlockSpec((1, f, D), lambda gi, e: (e, 0, 0))],
        out_specs=pl.BlockSpec((1, t, D), lambda gi, e: (gi, 0, 0)),
        out_shape=jax.ShapeDtypeStruct((g, t, D), F32),
        compiler_params=_cparams(("parallel", "arbitrary")), name="experts",
    )(h, rank.reshape(g, n_exp, 1, t), aff.reshape(g, n_exp, 1, t), w1, w3, w2)


def _moe(h, lg, w1, w3, w2, layer, bsz, ctx_len, lat_groups_only):
    g, t, _ = h.shape
    cap_lat = EC_CAP * t // N_EXP
    rank, aff = _route(lg[:bsz], cap_lat, 0)
    if not lat_groups_only:
        cap_ctx = EC_CAP * ctx_len // N_EXP
        lg_c = lg[bsz].reshape(N_EXP, bsz, ctx_len).transpose(1, 0, 2)
        rank_c, aff_c = _route(lg_c, cap_ctx, cap_ctx)
        back = lambda a: a.transpose(1, 0, 2).reshape(1, N_EXP, t)
        rank = jnp.concatenate([rank, back(rank_c)], axis=0)
        aff = jnp.concatenate([aff, back(aff_c)], axis=0)
        assert cap_ctx * bsz == cap_lat
    else:
        h = h[:bsz] if h.shape[0] != bsz else h
    return _experts(h, rank, aff, w1, w3, w2, layer, cap_lat)


def _block_diag2(w):
    z = jnp.zeros_like(w[0])
    return jnp.concatenate([jnp.concatenate([w[0], z], axis=1), jnp.concatenate([z, w[1]], axis=1)], axis=0)


def _rope_tables(t):
    quarter = C_HD // 4
    inv = ROPE_BASE ** (-jnp.arange(quarter, dtype=F32) / quarter)
    pos = jnp.arange(t)
    row = (pos // GRID_W).astype(F32)
    col = (pos % GRID_W).astype(F32)
    lane = np.arange(LANE)
    use_col = ((lane % C_HD) >= C_HD // 2)
    ang = jnp.where(use_col[None, :], col[:, None], row[:, None]) * inv[lane % quarter][None, :]
    sign = np.where((lane % (C_HD // 2)) < quarter, -1.0, 1.0).astype(np.float32)
    return jnp.cos(ang), jnp.sin(ang) * sign[None, :]


def kernel(x, c, ctx, c_ctx, ada_w, ada_b, norm_mix, norm_ffn, e_w_in, e_w_out, a_mu, a_w0, a_w2, a_a0, a_a2, a_g2,
           a_k_k, a_k_a, a_r_k, a_ln_w, a_ln_b, b_conv, b_a_log, b_dt_bias, b_norm, o_w_in, o_w_out, o_sink,
           moe_router, moe_w1, moe_w3, moe_w2, final_norm):
    bsz, t, _ = x.shape
    ctx_len = ctx.shape[1]
    depth = ada_w.shape[0]
    assert bsz * ctx_len == t and ctx_len % C_BLOCK == 0 and t % C_BLOCK == 0
    tt = ctx_len

    rows = 2 * SUB * ((bsz + 1 + 2 * SUB - 1) // (2 * SUB))
    cc = jnp.zeros((rows, D), F32).at[:bsz].set(c).at[bsz].set(c_ctx)
    mods = _adaln(cc, ada_w, ada_b)[:, :bsz + 1].reshape(depth, bsz + 1, N_MOD, D)
    mods = jnp.pad(mods, ((0, 0), (0, 0), (0, MOD_ROWS - N_MOD), (0, 0)))

    xs = jnp.concatenate([x, ctx.reshape(1, t, D)], axis=0)
    ones_bd = jnp.asarray(np.kron(np.eye(A_H), np.ones((A_HD, A_HD))), BF16)
    a_cols = e_w_in.shape[2] - (4 * B_W + 4 * B_H)
    cos_t, sin_t = _rope_tables(t)

    res = None
    for i in range(depth):
        j = i // 2
        mod = mods[i]
        last = i == depth - 1
        if i % 2 == 0:
            w_in = e_w_in[j]
            split_ab = a_cols + 3 * B_W
            w_all = jnp.concatenate([w_in[:, :split_ab], w_in[:, split_ab + 4 * B_H:],
                                     w_in[:, split_ab:split_ab + 4 * B_H],
                                     jnp.zeros((D, LANE - 4 * B_H), F32)], axis=1).astype(BF16)
            x_new, (pa, pb, zab) = _mm_split(xs, norm_mix[i], mod, 0, res, w_all, (a_cols, 3 * B_W, B_W + LANE),
                                             F32, 2 * tt)
            xs = xs if res is None else x_new
            prm = {
                "mu": a_mu[j].reshape(1, -1), "w0": a_w0[j].reshape(1, -1), "w2": _block_diag2(a_w2[j]),
                "a0": a_a0[j].reshape(1, -1), "a2": _block_diag2(a_a2[j]), "g2": a_g2[j],
                "k_k": a_k_k[j].reshape(1, -1), "k_a": jnp.tile(a_k_a[j].reshape(1, -1), (1, 2)),
                "r_k": a_r_k[j].reshape(1, -1), "ones_bd": ones_bd,
            }
            rvk, cum, kd, bd, bg = _rwkv_prep(pa, prm, tt, bsz)
            pad_row = lambda v: jnp.pad(v.reshape(1, -1), ((0, 0), (0, LANE - v.size)))
            qkv_c, gb = _gdn_prep(pb, zab, b_conv[j], pad_row(b_a_log[j]), pad_row(b_dt_bias[j]), tt, bsz)
            ya_f, ob_f, ya_b, ob_b = _even_scan(rvk, cum, kd, bd, qkv_c, gb, bsz, ctx_len)
            xs, hf, lg = _mix_out((ya_f, ya_b), bg, a_ln_w[j].reshape(1, -1), a_ln_b[j].reshape(1, -1), ones_bd,
                                  (ob_f, ob_b), zab, b_norm[j].reshape(1, -1), e_w_out[j].astype(BF16), xs, mod, 2,
                                  norm_ffn[i], moe_router[i], 3, tt)
        else:
            assert last
            w_in = o_w_in[j]
            q_w = C_QH * C_HD
            kv_cols = C_KVH * C_HD
            dup = lambda w: jnp.repeat(w.reshape(D, C_KVH, 1, C_HD), 2, axis=2).reshape(D, 2 * kv_cols)
            w_all = jnp.concatenate([w_in[:, :q_w], dup(w_in[:, q_w:q_w + kv_cols]), dup(w_in[:, q_w + kv_cols:])],
                                    axis=1).astype(BF16)
            tn = 2 * kv_cols
            x_new, qkv = _qkv_rope(xs, norm_mix[i], mod, 0, res, w_all, cos_t, sin_t, bsz, 2 * tt, tn, q_w, q_w + tn)
            xs = xs if res is None else x_new
            att = _attention(qkv, o_sink[j], bsz, ctx_len)
            xs, hf, lg = _mm_res(att, o_w_out[j].astype(BF16), xs, mod, 2, norm_ffn[i], moe_router[i], 3, tt,
                                 groups=bsz)
        if last:
            xs, hf, lg, mod = xs[:bsz], hf[:bsz], lg[:bsz], mod[:bsz]
        lg = lg[:, :, :N_EXP].transpose(0, 2, 1)
        delta = _moe(hf, lg, moe_w1, moe_w3, _to_bf16(moe_w2, i), i, bsz, ctx_len, last)
        res = (delta, mod, 5)
    _, out, _ = _norm(xs, final_norm, None, res=res, tt=tt, out_dtype=F32)
    return out
```

```python
import functools
import math

import jax
import jax.numpy as jnp
import numpy as np
from jax import lax
from jax.experimental import pallas as pl
from jax.experimental.pallas import tpu as pltpu

F32 = jnp.float32
BF16 = jnp.bfloat16
HIGHEST = lax.Precision.HIGHEST

D = 1024
N_MOD = 6
NORM_EPS = 1e-6
A_HD, A_H, A_W = 64, 8, 512
A_LN_EPS = 64e-5
B_HD, B_H, B_W = 128, 4, 512
B_CONV = 5
CHUNK = 64
C_HD, C_QH, C_KVH, C_GROUP = 64, 16, 4, 4
C_BLOCK = 128
ROPE_BASE = 10000.0
GRID_W = 64
N_EXP = 16
EC_CAP = 2
ROUTE_BISECT = 40
ROUTE_SNAP = 3
RWKV_PASSES = {"quad": 1, "apply": 1, "solve": "b", "square": 1, "state_in": "a", "state_out": "b"}
GDN_PASSES = {"quad": 1, "apply": 1, "solve": "b", "square": 1, "state": 1}
MOD_ROWS = 8
LANE = 128
SUB = 8
VMEM_LIMIT = 56 * 1024 * 1024

NT = (((1,), (1,)), ((), ()))
NN = (((1,), (0,)), ((), ()))
TN = (((0,), (0,)), ((), ()))


def _cparams(sem):
    return pltpu.CompilerParams(dimension_semantics=sem, vmem_limit_bytes=VMEM_LIMIT)


def _sigmoid(x):
    return 1.0 / (1.0 + jnp.exp(-x))


def _softplus(x):
    return jnp.maximum(x, 0.0) + jnp.log(1.0 + jnp.exp(-jnp.abs(x)))


def _split2(x):
    hi = x.astype(BF16)
    lo = (x - hi.astype(F32)).astype(BF16)
    return hi, lo


def _dot(a, b, dims=NN, passes=1):
    dg = functools.partial(lax.dot_general, dimension_numbers=dims, preferred_element_type=F32)
    if passes == 1:
        return dg(a.astype(BF16), b.astype(BF16))
    (ka,), (kb,) = dims[0]
    if passes == "a":
        ah, al = _split2(a)
        bh = b.astype(BF16)
        return dg(jnp.concatenate([ah, al], axis=ka), jnp.concatenate([bh, bh], axis=kb))
    if passes == "b":
        bh, bl = _split2(b)
        n = b.shape[1 - kb]
        res = dg(a.astype(BF16), jnp.concatenate([bh, bl], axis=1 - kb))
        return res[:, 0:n] + res[:, n:2 * n]
    ah, al = _split2(a)
    bh, bl = _split2(b)
    return dg(ah, bh) + (dg(ah, bl) + dg(al, bh))


def _group_sum(x, ones_bd):
    hi, lo = _split2(x)
    dg = functools.partial(lax.dot_general, dimension_numbers=NN, preferred_element_type=F32)
    return dg(hi, ones_bd) + dg(lo, ones_bd)


def _cast_kernel(w_ref, o_ref):
    o_ref[...] = w_ref[...].astype(o_ref.dtype)


def _to_bf16(w, layer):
    _, n, r, c = w.shape
    return pl.pallas_call(_cast_kernel, grid=(n,),
                          in_specs=[pl.BlockSpec((1, 1, r, c), lambda e: (layer, e, 0, 0))],
                          out_specs=pl.BlockSpec((1, 1, r, c), lambda e: (0, e, 0, 0)),
                          out_shape=jax.ShapeDtypeStruct((1, n, r, c), BF16),
                          compiler_params=_cparams(("parallel",)), name="cast_bf16")(w)[0]


def _adaln_kernel(c_ref, w_ref, b_ref, o_ref):
    c = c_ref[...]
    s = c * _sigmoid(c)
    o_ref[0] = jnp.dot(s, w_ref[0], precision=HIGHEST, preferred_element_type=F32) + b_ref[0]


def _adaln(cc, ada_w, ada_b):
    depth, _, n = ada_w.shape
    rows = cc.shape[0]
    tn = 768
    return pl.pallas_call(
        _adaln_kernel,
        grid=(depth, n // tn),
        in_specs=[pl.BlockSpec((rows, D), lambda i, j: (0, 0)),
                  pl.BlockSpec((1, D, tn), lambda i, j: (i, 0, j)),
                  pl.BlockSpec((1, 1, tn), lambda i, j: (i, 0, j))],
        out_specs=pl.BlockSpec((1, rows, tn), lambda i, j: (i, 0, j)),
        out_shape=jax.ShapeDtypeStruct((depth, rows, n), F32),
        compiler_params=_cparams(("parallel", "parallel")),
        name="adaln",
    )(cc, ada_w, ada_b.reshape(depth, 1, n))


def _final_norm_kernel(x_ref, d_ref, mres_ref, gain_ref, o_ref, *, kres):
    x = x_ref[0] + mres_ref[0, kres:kres + 1, :] * d_ref[0]
    o_ref[0] = x * lax.rsqrt(jnp.mean(x * x, axis=-1, keepdims=True) + NORM_EPS) * gain_ref[...]


def _final_norm(x, res, gain, tt):
    g, t, _ = x.shape
    tile = pl.BlockSpec((1, tt, D), lambda gi, i: (gi, i, 0))
    return pl.pallas_call(
        functools.partial(_final_norm_kernel, kres=res[2]),
        grid=(g, t // tt),
        in_specs=[tile, tile, pl.BlockSpec((1, MOD_ROWS, D), lambda gi, i: (gi, 0, 0)),
                  pl.BlockSpec((1, D), lambda gi, i: (0, 0))],
        out_specs=tile, out_shape=jax.ShapeDtypeStruct((g, t, D), F32),
        compiler_params=_cparams(("parallel", "parallel")), name="final_norm",
    )(x, res[0], res[1], gain.reshape(1, D))


def _rows_specs(x, tm):
    if not isinstance(x, tuple):
        return [x], [pl.BlockSpec((1, tm, D), lambda gi, i: (gi, i, 0))], x.shape[0], x.shape[1]
    lat, ctx = x
    n_lat, t = lat.shape[0], lat.shape[1]
    last = t // tm - 1
    specs = [pl.BlockSpec((1, tm, D), lambda gi, i: (jnp.minimum(gi, n_lat - 1), jnp.where(gi < n_lat, i, last), 0)),
             pl.BlockSpec((1, tm, D), lambda gi, i: (0, jnp.where(gi < n_lat, 0, i), 0))]
    return [lat, ctx], specs, n_lat + 1, t


def _rows(refs, n_lat):
    if n_lat is None:
        return next(refs)[0]
    lat_ref, ctx_ref = next(refs), next(refs)
    return jnp.where(pl.program_id(0) < n_lat, lat_ref[0], ctx_ref[0])


def _n_lat(x):
    return x[0].shape[0] if isinstance(x, tuple) else None


def _mixer_input(refs, has_res, k, kres, n_lat=None):
    x = _rows(refs, n_lat)
    if has_res:
        d_ref, mres_ref = next(refs), next(refs)
        x = x + mres_ref[0, kres:kres + 1, :] * d_ref[0]
    gain_ref, mod_ref = next(refs), next(refs)
    xn = x * lax.rsqrt(jnp.mean(x * x, axis=-1, keepdims=True) + NORM_EPS) * gain_ref[...]
    h = (xn * (1.0 + mod_ref[0, k + 1:k + 2, :]) + mod_ref[0, k:k + 1, :]).astype(BF16)
    return (x if has_res else None), h


def _mixer_input_specs(x, gain, mod, res, tm):
    tile = pl.BlockSpec((1, tm, D), lambda gi, i: (gi, i, 0))
    modspec = pl.BlockSpec((1, MOD_ROWS, D), lambda gi, i: (gi, 0, 0))
    ins, specs, g, t = _rows_specs(x, tm)
    if res is not None:
        ins += [res[0], res[1]]
        specs += [tile, modspec]
    ins += [gain.reshape(1, D), mod]
    specs += [pl.BlockSpec((1, D), lambda gi, i: (0, 0)), modspec]
    return ins, specs, tile, g, t


def _mm_split_kernel(*refs, widths, has_res, k, kres, n_lat):
    it = iter(refs)
    x_new, h = _mixer_input(it, has_res, k, kres, n_lat)
    w_ref = next(it)
    if has_res:
        next(it)[0] = x_new
    start = 0
    for o_ref, width in zip(it, widths):
        o_ref[0] = jnp.dot(h, w_ref[:, start:start + width], preferred_element_type=F32).astype(o_ref.dtype)
        start += width


def _mm_split(x, gain, mod, k, res, w, widths, out_dtype, tm):
    assert sum(widths) == w.shape[1] and all(wd % LANE == 0 for wd in widths)
    ins, specs, tile, g, t = _mixer_input_specs(x, gain, mod, res, tm)
    has_res = res is not None
    outs = list(pl.pallas_call(
        functools.partial(_mm_split_kernel, widths=widths, has_res=has_res, k=k, kres=res[2] if has_res else 0,
                          n_lat=_n_lat(x)),
        grid=(g, t // tm),
        in_specs=specs + [pl.BlockSpec(w.shape, lambda gi, i: (0, 0))],
        out_specs=([tile] if has_res else []) + [pl.BlockSpec((1, tm, wd), lambda gi, i: (gi, i, 0)) for wd in widths],
        out_shape=([jax.ShapeDtypeStruct((g, t, D), F32)] if has_res else [])
        + [jax.ShapeDtypeStruct((g, t, wd), out_dtype) for wd in widths],
        compiler_params=_cparams(("parallel", "parallel")), name="proj",
    )(*ins, w))
    return (outs.pop(0) if has_res else None), outs


def _ffn_input(x, gain_ref, mod_ref, rt_ref, k, h_ref, lg_ref):
    xn = x * lax.rsqrt(jnp.mean(x * x, axis=-1, keepdims=True) + NORM_EPS) * gain_ref[...]
    xn = xn * (1.0 + mod_ref[0, k + 1:k + 2, :]) + mod_ref[0, k:k + 1, :]
    h_ref[0] = xn.astype(h_ref.dtype)
    lg_ref[0] = _dot(xn, rt_ref[...], NN, 3)


def _ffn_input_specs(g, t, tm, gain, router):
    ins = [gain.reshape(1, D), jnp.pad(router, ((0, 0), (0, LANE - router.shape[1])))]
    specs = [pl.BlockSpec((1, D), lambda gi, i: (0, 0)), pl.BlockSpec((D, LANE), lambda gi, i: (0, 0))]
    ospecs = [pl.BlockSpec((1, tm, D), lambda gi, i: (gi, i, 0)), pl.BlockSpec((1, tm, LANE), lambda gi, i: (gi, i, 0))]
    oshapes = [jax.ShapeDtypeStruct((g, t, D), BF16), jax.ShapeDtypeStruct((g, t, LANE), F32)]
    return ins, specs, ospecs, oshapes


def _mm_res_kernel(y_ref, w_ref, x_ref, mod_ref, gain_ref, rt_ref, o_ref, h_ref, lg_ref, *, k, kffn):
    acc = jnp.dot(y_ref[0], w_ref[...], preferred_element_type=F32)
    x = x_ref[0] + mod_ref[0, k:k + 1, :] * acc
    o_ref[0] = x
    _ffn_input(x, gain_ref, mod_ref, rt_ref, kffn, h_ref, lg_ref)


def _mm_res(y, w, x, mod, k, gain_ffn, router, kffn, tm, groups):
    _, t, kdim = y.shape
    g = groups
    ins, specs, ospecs, oshapes = _ffn_input_specs(g, t, tm, gain_ffn, router)
    return pl.pallas_call(
        functools.partial(_mm_res_kernel, k=k, kffn=kffn),
        grid=(g, t // tm),
        in_specs=[pl.BlockSpec((1, tm, kdim), lambda gi, i: (gi, i, 0)),
                  pl.BlockSpec((kdim, D), lambda gi, i: (0, 0)),
                  pl.BlockSpec((1, tm, D), lambda gi, i: (gi, i, 0)),
                  pl.BlockSpec((1, MOD_ROWS, D), lambda gi, i: (gi, 0, 0))] + specs,
        out_specs=[pl.BlockSpec((1, tm, D), lambda gi, i: (gi, i, 0))] + ospecs,
        out_shape=[jax.ShapeDtypeStruct((g, t, D), F32)] + oshapes,
        compiler_params=_cparams(("parallel", "parallel")), name="out_proj_res",
    )(y, w, x, mod, *ins)


def _fill_halo(buf, x, pv_ref, nx_ref, has_prev, has_next, tt):
    buf[0:SUB, :] = jnp.where(has_prev, pv_ref[0], 0.0)
    buf[SUB:SUB + tt, :] = x
    buf[SUB + tt:2 * SUB + tt, :] = jnp.where(has_next, nx_ref[0], 0.0)


def _rwkv_prep_kernel(pa_ref, pv_ref, nx_ref, mu_ref, w0_ref, w2_ref, a0_ref, a2_ref, g2_ref, kk_ref, ka_ref,
                      rk_ref, bd_ref, rvk_ref, cum_ref, kd_ref, bdir_ref, bg_ref, buf, *, tt, lat_groups, ntile):
    gi, i = pl.program_id(0), pl.program_id(1)
    is_lat = gi < lat_groups
    x = pa_ref[0]
    _fill_halo(buf, x, pv_ref, nx_ref, jnp.logical_and(is_lat, i > 0), jnp.logical_and(is_lat, i < ntile - 1), tt)
    xm = buf[SUB - 1:SUB - 1 + tt, :]
    xp = buf[SUB + 1:SUB + 1 + tt, :]
    x = x + mu_ref[...] * (0.5 * (xm + xp) - x)
    w = A_W
    r, kx, v = x[:, 0:w], x[:, w:2 * w], x[:, 2 * w:3 * w]
    wd, ad, gd = x[:, 3 * w:3 * w + LANE], x[:, 3 * w + LANE:3 * w + 2 * LANE], x[:, 3 * w + 2 * LANE:3 * w + 3 * LANE]
    w_log = -_softplus(-(w0_ref[...] + _dot(jnp.tanh(wd), w2_ref[...], NN, 3))) - 0.5
    lw = -jnp.exp(w_log)
    a = _sigmoid(a0_ref[...] + _dot(ad, a2_ref[...], NN, 3))
    gate = _dot(_sigmoid(gd), g2_ref[...], NN, 3)
    ones_bd = bd_ref[...]
    kq = kx * kk_ref[...]
    kk = kq * lax.rsqrt(_group_sum(kq * kq, ones_bd) + 1e-6)
    k2 = jnp.concatenate([kx, kx], axis=1)
    kdir = k2 * (1.0 + (a - 1.0) * ka_ref[...])
    bdir = jnp.concatenate([kk, kk], axis=1) * a
    rr = r * rk_ref[...]
    bonus = _group_sum(rr * (kdir[:, 0:w] + kdir[:, w:2 * w]), ones_bd) * v
    rvk_ref[0] = jnp.concatenate([r, v, kk], axis=1)
    cum_ref[0] = jnp.concatenate([_chunk_cumsum(lw[:, 0:w], True), _chunk_cumsum(lw[:, w:2 * w], False)], axis=1)
    kd_ref[0] = kdir.astype(kd_ref.dtype)
    bdir_ref[0] = bdir.astype(bdir_ref.dtype)
    bg_ref[0] = jnp.concatenate([bonus, gate], axis=1).astype(bg_ref.dtype)


def _chunk_cumsum(x, fwd):
    _, _, incl = _order_masks(x.shape[0], CHUNK, fwd)
    mask = jnp.where(incl, 1.0, 0.0).astype(BF16)
    p1 = x.astype(BF16)
    r1 = x - p1.astype(F32)
    p2 = r1.astype(BF16)
    p3 = (r1 - p2.astype(F32)).astype(BF16)
    dg = functools.partial(lax.dot_general, dimension_numbers=NN, preferred_element_type=F32)
    return dg(mask, p1) + (dg(mask, p2) + dg(mask, p3))


def _halo_specs(width, tt, t):
    nb = tt // SUB
    last = t // SUB - 1
    prev = pl.BlockSpec((1, SUB, width), lambda gi, i: (gi, jnp.maximum(i * nb - 1, 0), 0))
    nxt = pl.BlockSpec((1, SUB, width), lambda gi, i: (gi, jnp.minimum((i + 1) * nb, last), 0))
    return prev, nxt


def _rwkv_prep(pa, p, tt, lat_groups):
    g, t, wa = pa.shape
    full = lambda arr: pl.BlockSpec(arr.shape, lambda gi, i: (0,) * arr.ndim)
    tile = lambda width: pl.BlockSpec((1, tt, width), lambda gi, i: (gi, i, 0))
    prev, nxt = _halo_specs(wa, tt, t)
    consts = [p["mu"], p["w0"], p["w2"], p["a0"], p["a2"], p["g2"], p["k_k"], p["k_a"], p["r_k"], p["ones_bd"]]
    widths = (3 * A_W, 2 * A_W, 2 * A_W, 2 * A_W, 2 * A_W)
    return pl.pallas_call(
        functools.partial(_rwkv_prep_kernel, tt=tt, lat_groups=lat_groups, ntile=t // tt),
        grid=(g, t // tt),
        in_specs=[tile(wa), prev, nxt] + [full(c) for c in consts],
        out_specs=[tile(wd) for wd in widths],
        out_shape=[jax.ShapeDtypeStruct((g, t, wd), dt) for wd, dt in zip(widths, (F32, F32, BF16, BF16, BF16))],
        scratch_shapes=[pltpu.VMEM((tt + 2 * SUB, wa), F32)],
        compiler_params=_cparams(("parallel", "parallel")), name="rwkv_prep",
    )(pa, pa, pa, *consts)


def _order_masks(n_rows, blk, fwd):
    ii = lax.broadcasted_iota(jnp.int32, (n_rows, n_rows), 0)
    jj = lax.broadcasted_iota(jnp.int32, (n_rows, n_rows), 1)
    same = (ii // blk) == (jj // blk)
    before = (jj < ii) if fwd else (jj > ii)
    strict = jnp.logical_and(same, before)
    incl = jnp.logical_and(same, jnp.logical_or(before, ii == jj))
    return same, strict, incl


def _even_scan_kernel(rvkf_ref, cumf_ref, kdf_ref, bdf_ref, qkvf_ref, gbf_ref,
                      rvkb_ref, cumb_ref, kdb_ref, bdb_ref, qkvb_ref, gbb_ref,
                      yf_ref, of_ref, yb_ref, ob_ref, sa_ref, sb_ref):
    @pl.when(pl.program_id(1) == 0)
    def _():
        sa_ref[...] = jnp.zeros_like(sa_ref)
        sb_ref[...] = jnp.zeros_like(sb_ref)

    na, nb = A_W // LANE, B_H // 2
    sa, sb = sa_ref[...], sb_ref[...]
    chains_a = (_rwkv_operands(rvkf_ref[0], cumf_ref[0], kdf_ref[0], bdf_ref[0], True)
                + _rwkv_operands(rvkb_ref[0], cumb_ref[0], kdb_ref[0], bdb_ref[0], False))
    chains_b = _gdn_operands(qkvf_ref[0], gbf_ref[0], 0) + _gdn_operands(qkvb_ref[0], gbb_ref[0], 1)
    states_a = [sa[d, p] for d in range(2) for p in range(na)]
    states_b = [(sb[d, 2 * p], sb[d, 2 * p + 1]) for d in range(2) for p in range(nb)]
    res_a, res_b = {}, {}
    stages = [_rwkv_solve(chains_a, states_a, RWKV_PASSES, res_a), _gdn_solve(chains_b, states_b, GDN_PASSES, res_b)]
    while stages:
        stages = [s for s in stages if next(s, "done") != "done"]
    yf_ref[0] = jnp.concatenate(res_a["y"][0:na], axis=1)
    yb_ref[0] = jnp.concatenate(res_a["y"][na:2 * na], axis=1)
    of_ref[0] = jnp.concatenate([o for pair in res_b["o"][0:nb] for o in pair], axis=1)
    ob_ref[0] = jnp.concatenate([o for pair in res_b["o"][nb:2 * nb] for o in pair], axis=1)
    for i, s_i in enumerate(res_a["s"]):
        sa_ref[i // na, i % na] = s_i
    for i, pair in enumerate(res_b["s"]):
        for j, s_h in enumerate(pair):
            sb_ref[i // nb, 2 * (i % nb) + j] = s_h


def _rwkv_operands(rvk, cum, kd, bd, fwd):
    L = CHUNK
    L2 = 2 * L
    _, strict, incl = _order_masks(L2, L, fwd)
    row = lax.broadcasted_iota(jnp.int32, (L, 1), 0)
    if fwd:
        tot = cum[L - 1:L]
        cum_ex = jnp.where(row == 0, 0.0, pltpu.roll(cum, 1, axis=0))
    else:
        tot = cum[0:1]
        cum_ex = jnp.where(row == L - 1, 0.0, pltpu.roll(cum, L - 1, axis=0))
    g_inv = jnp.exp(-cum)
    g_rest = jnp.exp(tot - cum)
    g_tot = jnp.exp(tot)
    w = A_W
    r_t = rvk[:, 0:w] * jnp.exp(cum)
    v = rvk[:, w:2 * w]
    a_t = rvk[:, 2 * w:3 * w] * jnp.exp(cum_ex)
    b_t, k_t = bd * g_inv, kd * g_inv
    b_g, k_g = bd * g_rest, kd * g_rest
    low = lax.broadcasted_iota(jnp.int32, (1, LANE), 1) < A_HD

    def expand(x):
        return jnp.concatenate([jnp.where(low, x, 0.0), jnp.where(low, 0.0, x)], axis=0)

    chains = []
    for p in range(A_W // LANE):
        sl = slice(p * LANE, (p + 1) * LANE)
        chains.append(dict(
            ev=expand(v[:, sl]),
            lhs=jnp.concatenate([expand(a_t[:, sl]), expand(r_t[:, sl])], axis=0),
            rhs=jnp.concatenate([expand(b_t[:, sl]), expand(k_t[:, sl])], axis=0),
            bk=jnp.concatenate([expand(b_g[:, sl]), expand(k_g[:, sl])], axis=0),
            g_col=jnp.transpose(jnp.broadcast_to(g_tot[:, sl], (LANE, LANE))),
            strict=strict, incl=incl))
    return chains


def _square_pairs(mats, passes):
    if passes != 1 or len(mats) % 2:
        return [_dot(m, m, NN, passes) for m in mats]
    out = []
    for m0, m1 in zip(mats[0::2], mats[1::2]):
        n = m0.shape[0]
        m0, m1 = m0.astype(BF16), m1.astype(BF16)
        zero = jnp.zeros_like(m0)
        diag = jnp.concatenate([jnp.concatenate([m0, zero], axis=1), jnp.concatenate([zero, m1], axis=1)], axis=0)
        sq = lax.dot_general(jnp.concatenate([m0, m1], axis=1), diag, NN, preferred_element_type=F32)
        out += [sq[:, 0:n], sq[:, n:2 * n]]
    return out


def _rwkv_solve(chains, states, passes, res):
    L = CHUNK
    L2 = 2 * L
    each = lambda fn, *cols: [fn(*args) for args in zip(*cols)]
    quad = each(lambda c: _dot(c["lhs"], c["rhs"], NT, passes["quad"]), chains)
    yield
    es = each(lambda c, s: _dot(c["lhs"], s, NN, passes["state_in"]), chains, states)
    yield
    n_ab = each(lambda c, q: jnp.where(c["strict"], q[0:L2, 0:L2], 0.0), chains, quad)
    x = each(lambda c, q, e: e[0:L2] + _dot(jnp.where(c["strict"], q[0:L2, L2:2 * L2], 0.0), c["ev"], NN,
                                            passes["apply"]), chains, quad, es)
    yield
    x = each(lambda n, xi: xi - _dot(n, xi, NN, passes["solve"]), n_ab, x)
    yield
    npow = n_ab
    for _ in range(int(math.log2(L)) - 1):
        npow = _square_pairs(npow, passes["square"])
        yield
        x = each(lambda n, xi: xi + _dot(n, xi, NN, passes["solve"]), npow, x)
        yield
    pv = each(lambda c, xi: jnp.concatenate([-xi, c["ev"]], axis=0), chains, x)
    y2 = each(lambda c, q, e, pvi: e[L2:2 * L2] + _dot(
        jnp.concatenate([jnp.where(c["incl"], q[L2:2 * L2, 0:L2], 0.0),
                         jnp.where(c["incl"], q[L2:2 * L2, L2:2 * L2], 0.0)], axis=1), pvi, NN, passes["apply"]),
        chains, quad, es, pv)
    res["y"] = [y[0:L] + y[L:L2] for y in y2]
    yield
    res["s"] = each(lambda c, s, pvi: s * c["g_col"] + _dot(c["bk"], pvi, TN, passes["state_out"]),
                    chains, states, pv)


def _chunk_index(b, c, fwd, n_ctx_chunk, n_lat_chunk, lat_groups):
    in_ctx = c < n_ctx_chunk
    cc = c if fwd else n_ctx_chunk - 1 - c
    lc = c - n_ctx_chunk if fwd else n_lat_chunk - 1 - (c - n_ctx_chunk)
    grp = jnp.where(in_ctx, lat_groups, b)
    chunk = jnp.where(in_ctx, b * n_ctx_chunk + cc, lc)
    return grp, chunk


def _scan_specs(widths, bsz, ctx_len, t):
    ncc, nlc = ctx_len // CHUNK, t // CHUNK

    def spec(width, lane_blk, fwd):
        def index(b, c):
            grp, ch = _chunk_index(b, c, fwd, ncc, nlc, bsz)
            return grp, ch, lane_blk
        return pl.BlockSpec((1, CHUNK, width), index)

    return [[spec(w, (0 if fwd else 1) if blk is None else blk, fwd) for w, blk in widths] for fwd in (True, False)]


def _even_scan(rvk, cum, kd, bd, qkv, gb, bsz, ctx_len):
    g, t, _ = rvk.shape
    ins_f, ins_b = _scan_specs([(3 * A_W, 0), (A_W, None), (A_W, None), (A_W, None), (3 * B_W, 0), (LANE, 0)],
                               bsz, ctx_len, t)
    outs_f, outs_b = _scan_specs([(A_W, 0), (B_W, 0)], bsz, ctx_len, t)
    shapes = [jax.ShapeDtypeStruct((g, t, A_W), F32), jax.ShapeDtypeStruct((g, t, B_W), F32)]
    return pl.pallas_call(
        _even_scan_kernel,
        grid=(bsz, (ctx_len + t) // CHUNK),
        in_specs=ins_f + ins_b,
        out_specs=outs_f + outs_b,
        out_shape=shapes + shapes,
        scratch_shapes=[pltpu.VMEM((2, A_W // LANE, LANE, LANE), F32), pltpu.VMEM((2, B_H, B_HD, B_HD), F32)],
        compiler_params=_cparams(("parallel", "arbitrary")), name="even_scan",
    )(rvk, cum, kd, bd, qkv, gb, rvk, cum, kd, bd, qkv, gb)


def _gdn_prep_kernel(pb_ref, pv_ref, nx_ref, ab_ref, cw_ref, alog_ref, dtb_ref, qkv_ref, gb_ref, buf,
                     *, tt, lat_groups, ntile):
    gi, i = pl.program_id(0), pl.program_id(1)
    is_lat = gi < lat_groups
    _fill_halo(buf, pb_ref[0], pv_ref, nx_ref, jnp.logical_and(is_lat, i > 0),
               jnp.logical_and(is_lat, i < ntile - 1), tt)
    half = B_CONV // 2
    acc = buf[SUB - half:SUB - half + tt, :] * cw_ref[0:1, :]
    for j in range(1, B_CONV):
        acc = acc + buf[SUB - half + j:SUB - half + j + tt, :] * cw_ref[j:j + 1, :]
    act = acc * _sigmoid(acc)
    outs = []
    for h in range(2 * B_H):
        xh = act[:, h * B_HD:(h + 1) * B_HD]
        xh = xh * lax.rsqrt(jnp.sum(xh * xh, axis=-1, keepdims=True) + 1e-6)
        if h < B_H:
            xh = xh * (B_HD ** -0.5)
        outs.append(xh)
    outs.append(act[:, 2 * B_W:3 * B_W])
    qkv_ref[0] = jnp.concatenate(outs, axis=1)
    ab = ab_ref[0]
    lane = lax.broadcasted_iota(jnp.int32, ab.shape, 1)
    gval = -jnp.exp(alog_ref[...]) * _softplus(ab + dtb_ref[...])
    gb_ref[0] = jnp.where(lane < 2 * B_H, gval, _sigmoid(ab))


def _gdn_prep(pb, zab, conv_w, alog_row, dtb_row, tt, lat_groups):
    g, t, wb = pb.shape
    tile = lambda width: pl.BlockSpec((1, tt, width), lambda gi, i: (gi, i, 0))
    full = lambda arr: pl.BlockSpec(arr.shape, lambda gi, i: (0,) * arr.ndim)
    prev, nxt = _halo_specs(wb, tt, t)
    ab_spec = pl.BlockSpec((1, tt, LANE), lambda gi, i: (gi, i, B_W // LANE))
    return pl.pallas_call(
        functools.partial(_gdn_prep_kernel, tt=tt, lat_groups=lat_groups, ntile=t // tt),
        grid=(g, t // tt),
        in_specs=[tile(wb), prev, nxt, ab_spec, full(conv_w), full(alog_row), full(dtb_row)],
        out_specs=[tile(wb), tile(LANE)],
        out_shape=[jax.ShapeDtypeStruct((g, t, wb), F32), jax.ShapeDtypeStruct((g, t, LANE), F32)],
        scratch_shapes=[pltpu.VMEM((tt + 2 * SUB, wb), F32)],
        compiler_params=_cparams(("parallel", "parallel")), name="gdn_prep",
    )(pb, pb, pb, zab, conv_w, alog_row, dtb_row)


def _gdn_operands(qkv, gb, d):
    fwd = d == 0
    L = CHUNK
    L2 = 2 * L
    same, strict, incl = _order_masks(L2, L, fwd)
    _, _, incl_rev = _order_masks(L2, L, not fwd)
    incl_bf = jnp.where(incl, 1.0, 0.0).astype(BF16)
    incl_t_bf = jnp.where(incl_rev, 1.0, 0.0).astype(BF16)
    same_bf = jnp.where(same, 1.0, 0.0).astype(BF16)
    lane = lax.broadcasted_iota(jnp.int32, gb.shape, 1)

    def column(idx):
        return jnp.sum(jnp.where(lane == idx, gb, 0.0), axis=-1, keepdims=True)

    chains = []
    for pr in range(B_H // 2):
        heads = (2 * pr, 2 * pr + 1)
        stack = lambda off: jnp.concatenate([qkv[:, off + h * B_HD:off + (h + 1) * B_HD] for h in heads], axis=0)
        q2, k2, v2 = stack(0), stack(B_W), stack(2 * B_W)
        gcol = jnp.concatenate([column(d * B_H + h) for h in heads], axis=0)
        bcol = jnp.concatenate([column(2 * B_H + d * B_H + h) for h in heads], axis=0)
        chains.append(dict(q2=q2, k2=k2, v2=v2, bcol=bcol, g3=_three_pieces(gcol), strict=strict, incl=incl,
                           sum_bf=jnp.concatenate([incl_bf, same_bf], axis=0), incl_t_bf=incl_t_bf))
    return chains


def _three_pieces(col):
    lane = lax.broadcasted_iota(jnp.int32, (col.shape[0], LANE), 1)
    p1 = col.astype(BF16).astype(F32)
    p2 = (col - p1).astype(BF16).astype(F32)
    p3 = col - p1 - p2
    return jnp.where(lane == 0, p1, jnp.where(lane == 1, p2, jnp.where(lane == 2, p3, 0.0))).astype(BF16)


def _gdn_solve(chains, states, passes, res):
    L = CHUNK
    L2 = 2 * L
    each = lambda fn, *cols: [fn(*args) for args in zip(*cols)]
    dg = functools.partial(lax.dot_general, preferred_element_type=F32)
    sums = each(lambda c: dg(c["sum_bf"], c["g3"], NN), chains)
    sums_t = each(lambda c: dg(c["g3"], c["incl_t_bf"], TN), chains)
    k2b = each(lambda c: c["k2"] * c["bcol"], chains)
    kq = each(lambda c, kb: _dot(jnp.concatenate([kb, c["q2"]], axis=0), c["k2"], NT, passes["quad"]), chains, k2b)
    yield
    gc_row = each(lambda m: jnp.sum(m[0:L2], axis=-1, keepdims=True), sums)
    g_tot = each(lambda m: jnp.sum(m[L2:2 * L2], axis=-1, keepdims=True), sums)
    gc_col = each(lambda m: m[0:1] + m[1:2] + m[2:3], sums_t)
    decay = each(lambda c, r, cl: jnp.exp(jnp.where(c["incl"], r - cl, -1e30)), chains, gc_row, gc_col)
    egc = each(jnp.exp, gc_row)
    l_mat = each(lambda c, m, dc: jnp.where(c["strict"], m[0:L2] * dc, 0.0), chains, kq, decay)
    a_int = each(lambda c, m, dc: jnp.where(c["incl"], m[L2:2 * L2] * dc, 0.0), chains, kq, decay)
    x = each(lambda c, kb, eg: jnp.concatenate([c["v2"] * c["bcol"], kb * eg], axis=1), chains, k2b, egc)
    yield
    x = each(lambda n, xi: xi - _dot(n, xi, NN, passes["solve"]), l_mat, x)
    yield
    npow = l_mat
    for _ in range(int(math.log2(L)) - 1):
        npow = _square_pairs(npow, passes["square"])
        yield
        x = each(lambda n, xi: xi + _dot(n, xi, NN, passes["solve"]), npow, x)
        yield
    qe = each(lambda c, eg: c["q2"] * eg, chains, egc)
    rows = (slice(0, L), slice(L, L2))
    vnew = each(lambda xi, st: [xi[r, 0:B_HD] - _dot(xi[r, B_HD:2 * B_HD], s, NN, passes["state"])
                                for r, s in zip(rows, st)], x, states)
    o1 = each(lambda qi, st: [_dot(qi[r], s, NN, passes["state"]) for r, s in zip(rows, st)], qe, states)
    yield
    o2 = each(lambda a, vn, o: jnp.concatenate(o, axis=0) + _dot(a, jnp.concatenate(vn, axis=0), NN,
                                                                 passes["apply"]), a_int, vnew, o1)
    res["o"] = [[o[r] for r in rows] for o in o2]
    k_rest = each(lambda c, gt, r: c["k2"] * jnp.exp(gt - r), chains, g_tot, gc_row)
    res["s"] = each(lambda st, gt, kr, vn: [s * jnp.exp(gt[r][0:1, :]) + _dot(kr[r], v, TN, passes["state"])
                                            for r, s, v in zip(rows, st, vn)], states, g_tot, k_rest, vnew)


def _mix_out_kernel(yaf_ref, yab_ref, bg_ref, lnw_ref, lnb_ref, bd_ref, obf_ref, obb_ref, z_ref, nw_ref,
                    w_ref, *rest, k, kffn, n_lat):
    it = iter(rest)
    x_in = _rows(it, n_lat)
    mod_ref, gain_ref, rt_ref, o_ref, h_ref, lg_ref = it
    y = yaf_ref[0] + yab_ref[0]
    ones_bd = bd_ref[...]
    mean = _group_sum(y, ones_bd) * (1.0 / A_HD)
    yc = y - mean
    var = _group_sum(yc * yc, ones_bd) * (1.0 / A_HD)
    yn = yc * lax.rsqrt(var + A_LN_EPS) * lnw_ref[...] + lnb_ref[...]
    bg = bg_ref[0].astype(F32)
    out_a = (yn + bg[:, 0:A_W]) * bg[:, A_W:2 * A_W]
    o = obf_ref[0] + obb_ref[0]
    z = z_ref[0]
    outs = [out_a.astype(BF16)]
    for h in range(B_H):
        oh = o[:, h * B_HD:(h + 1) * B_HD]
        on = oh * lax.rsqrt(jnp.mean(oh * oh, axis=-1, keepdims=True) + NORM_EPS) * nw_ref[...]
        zh = z[:, h * B_HD:(h + 1) * B_HD]
        outs.append((on * (zh * _sigmoid(zh))).astype(BF16))
    mixed = jnp.concatenate(outs, axis=1)
    x = x_in + mod_ref[0, k:k + 1, :] * jnp.dot(mixed, w_ref[...], preferred_element_type=F32)
    o_ref[0] = x
    _ffn_input(x, gain_ref, mod_ref, rt_ref, kffn, h_ref, lg_ref)


def _mix_out(ya, bg, ln_w, ln_b, ones_bd, ob, zab, norm_w, w_out, x, mod, k, gain_ffn, router, kffn, tt):
    tile = lambda width: pl.BlockSpec((1, tt, width), lambda gi, i: (gi, i, 0))
    full = lambda arr: pl.BlockSpec(arr.shape, lambda gi, i: (0,) * arr.ndim)
    x_ins, x_specs, g, t = _rows_specs(x, tt)
    ins, specs, ospecs, oshapes = _ffn_input_specs(g, t, tt, gain_ffn, router)
    return pl.pallas_call(
        functools.partial(_mix_out_kernel, k=k, kffn=kffn, n_lat=_n_lat(x)),
        grid=(g, t // tt),
        in_specs=[tile(A_W), tile(A_W), tile(2 * A_W), full(ln_w), full(ln_b), full(ones_bd), tile(B_W),
                  tile(B_W), tile(B_W), full(norm_w), full(w_out)] + x_specs
        + [pl.BlockSpec((1, MOD_ROWS, D), lambda gi, i: (gi, 0, 0))] + specs,
        out_specs=[tile(D)] + ospecs,
        out_shape=[jax.ShapeDtypeStruct((g, t, D), F32)] + oshapes,
        compiler_params=_cparams(("parallel", "parallel")), name="mix_out_proj",
    )(ya[0], ya[1], bg, ln_w, ln_b, ones_bd, ob[0], ob[1], zab, norm_w, w_out, *x_ins, mod, *ins)


def _qkv_rope_kernel(*refs, lat_groups, q_cols, rope_cols, seg, has_res, k, kres):
    it = iter(refs)
    x_new, x = _mixer_input(it, has_res, k, kres)
    w_ref, cos_ref, sin_ref = next(it), next(it), next(it)
    if has_res:
        next(it)[0] = x_new
    o_ref = next(it)
    is_lat = pl.program_id(0) < lat_groups
    lane = lax.broadcasted_iota(jnp.int32, (1, LANE), 1)
    first = (lane % (C_HD // 2)) < (C_HD // 4)
    quarter = C_HD // 4
    cos_l = jnp.where(is_lat, cos_ref[...], 1.0)
    sin_l = jnp.where(is_lat, sin_ref[...], 0.0)
    starts = list(range(0, w_ref.shape[1], seg))
    product = lambda s0: jnp.dot(x, w_ref[:, s0:s0 + seg], preferred_element_type=F32)
    acc_next = product(starts[0])
    for idx, start in enumerate(starts):
        acc = acc_next
        if idx + 1 < len(starts):
            acc_next = product(starts[idx + 1])
        if start >= rope_cols:
            o_ref[0, :, start:start + seg] = acc.astype(o_ref.dtype)
            continue
        scale = C_HD ** -0.5 if start < q_cols else 1.0
        cos, sin = cos_l * scale, sin_l * scale
        for blk in range(seg // LANE):
            xb = acc[:, blk * LANE:(blk + 1) * LANE]
            partner = jnp.where(first, pltpu.roll(xb, LANE - quarter, axis=1), pltpu.roll(xb, quarter, axis=1))
            col = start + blk * LANE
            o_ref[0, :, col:col + LANE] = (xb * cos + partner * sin).astype(o_ref.dtype)


def _qkv_rope(x, gain, mod, k, res, w, cos, sin, lat_groups, tm, seg, q_cols, rope_cols):
    kdim, n = w.shape
    assert q_cols % seg == 0 and rope_cols % seg == 0 and n % seg == 0
    ins, specs, tile, g, t = _mixer_input_specs(x, gain, mod, res, tm)
    has_res = res is not None
    outs = list(pl.pallas_call(
        functools.partial(_qkv_rope_kernel, lat_groups=lat_groups, q_cols=q_cols, rope_cols=rope_cols, seg=seg,
                          has_res=has_res, k=k, kres=res[2] if has_res else 0),
        grid=(g, t // tm),
        in_specs=specs + [pl.BlockSpec((kdim, n), lambda gi, i: (0, 0)),
                          pl.BlockSpec((tm, LANE), lambda gi, i: (i, 0)),
                          pl.BlockSpec((tm, LANE), lambda gi, i: (i, 0))],
        out_specs=([tile] if has_res else []) + [pl.BlockSpec((1, tm, n), lambda gi, i: (gi, i, 0))],
        out_shape=([jax.ShapeDtypeStruct((g, t, D), F32)] if has_res else [])
        + [jax.ShapeDtypeStruct((g, t, n), BF16)],
        compiler_params=_cparams(("parallel", "parallel")), name="qkv_rope",
    )(*ins, w, cos, sin))
    return (outs.pop(0) if has_res else None), outs[0]


def _attn_kernel(sink_ref, q_ref, kp_ref, km_ref, kn_ref, kc_ref, o_ref, *, n_blocks, kv_w):
    n = pl.program_id(1)
    blk = C_BLOCK
    ii = lax.broadcasted_iota(jnp.int32, (blk, blk), 0)
    jj = lax.broadcasted_iota(jnp.int32, (blk, blk), 1)
    ok_prev = jnp.logical_and(jj >= ii, n > 0)
    ok_next = jnp.logical_and(jj <= ii, n < n_blocks - 1)
    n_ctx = kc_ref.shape[1]
    valid1 = jnp.concatenate([ok_prev, jnp.full((blk, blk), True), ok_next, jnp.full((blk, n_ctx), True)], axis=1)
    valid = jnp.concatenate([valid1] * C_GROUP, axis=0)
    low = lax.broadcasted_iota(jnp.int32, (1, LANE), 1) < C_HD
    row_head = lax.broadcasted_iota(jnp.int32, (C_GROUP * blk, 1), 0) // blk
    q = q_ref[0]
    kv = jnp.concatenate([kp_ref[0], km_ref[0], kn_ref[0], kc_ref[0]], axis=0)
    zero = jnp.zeros((), q.dtype)

    def scores(h):
        kh = kv[:, h * LANE:(h + 1) * LANE]
        qa = q[:, (2 * h) * LANE:(2 * h + 1) * LANE]
        qb = q[:, (2 * h + 1) * LANE:(2 * h + 2) * LANE]
        qs = jnp.concatenate([jnp.where(low, qa, zero), jnp.where(low, zero, qa),
                              jnp.where(low, qb, zero), jnp.where(low, zero, qb)], axis=0)
        return lax.dot_general(qs, kh, NT, preferred_element_type=F32)

    def softmax(h, s):
        s = jnp.where(valid, s, -1e30)
        sk = jnp.full((C_GROUP * blk, 1), 0.0, F32)
        for gq in range(C_GROUP):
            sk = jnp.where(row_head == gq, sink_ref[h * C_GROUP + gq], sk)
        m = jnp.maximum(jnp.max(s, axis=-1, keepdims=True), sk)
        p = jnp.exp(s - m)
        return p.astype(kv.dtype), jnp.sum(p, axis=-1, keepdims=True) + jnp.exp(sk - m)

    def values(h, p, denom):
        vh = kv[:, kv_w + h * LANE:kv_w + (h + 1) * LANE]
        o = lax.dot_general(p, vh, NN, preferred_element_type=F32) / denom
        return [jnp.where(low, o[0:blk], o[blk:2 * blk]).astype(o_ref.dtype),
                jnp.where(low, o[2 * blk:3 * blk], o[3 * blk:4 * blk]).astype(o_ref.dtype)]

    outs = []
    s_next = scores(0)
    for h in range(C_KVH):
        s_cur = s_next
        if h + 1 < C_KVH:
            s_next = scores(h + 1)
        outs += values(h, *softmax(h, s_cur))
    o_ref[0] = jnp.concatenate(outs, axis=1)


def _attention(qkv, sink, bsz, ctx_len):
    g, t, _ = qkv.shape
    q_w = C_QH * C_HD
    kv_w = C_KVH * LANE
    nq = q_w // (2 * kv_w)
    assert q_w % (2 * kv_w) == 0
    n_blocks = t // C_BLOCK
    cpb = ctx_len // C_BLOCK
    assert ctx_len % C_BLOCK == 0
    kvspec = lambda fn: pl.BlockSpec((1, C_BLOCK, 2 * kv_w), fn)
    return pl.pallas_call(
        functools.partial(_attn_kernel, n_blocks=n_blocks, kv_w=kv_w),
        grid=(bsz, n_blocks),
        in_specs=[pl.BlockSpec(memory_space=pltpu.SMEM),
                  pl.BlockSpec((1, C_BLOCK, q_w), lambda b, n: (b, n, 0)),
                  kvspec(lambda b, n: (b, jnp.maximum(n - 1, 0), nq)),
                  kvspec(lambda b, n: (b, n, nq)),
                  kvspec(lambda b, n: (b, jnp.minimum(n + 1, n_blocks - 1), nq)),
                  pl.BlockSpec((1, ctx_len, 2 * kv_w), lambda b, n: (bsz, b, nq))],
        out_specs=pl.BlockSpec((1, C_BLOCK, q_w), lambda b, n: (b, n, 0)),
        out_shape=jax.ShapeDtypeStruct((bsz, t, q_w), BF16),
        compiler_params=_cparams(("parallel", "parallel")), name="window_attn",
    )(sink, qkv, qkv, qkv, qkv, qkv)


def _lane_cumsum(x):
    n = x.shape[-1]
    lane = lax.broadcasted_iota(jnp.int32, x.shape, x.ndim - 1)
    k = 1
    while k < n:
        x = x + jnp.where(lane >= k, pltpu.roll(x, k, axis=x.ndim - 1), 0)
        k *= 2
    return x


def _route_kernel(lg_ref, rank_ref, aff_ref, *, cap, slot_stride):
    lg = lg_ref[0]
    m = jnp.max(lg, axis=0, keepdims=True)
    e = jnp.exp(lg - m)
    z = jnp.sum(e, axis=0, keepdims=True)
    aff = e / z
    key = (lg - m) - jnp.log(z)
    count_ge = lambda v: jnp.sum(jnp.where(key >= v, 1, 0), axis=-1, keepdims=True)

    def body(_, carry):
        lo, hi = carry
        span = hi - lo
        q1, q2, q3 = lo + 0.25 * span, lo + 0.5 * span, lo + 0.75 * span
        ok1, ok2, ok3 = count_ge(q1) >= cap, count_ge(q2) >= cap, count_ge(q3) >= cap
        new_lo = jnp.where(ok3, q3, jnp.where(ok2, q2, jnp.where(ok1, q1, lo)))
        new_hi = jnp.where(ok3, hi, jnp.where(ok2, q3, jnp.where(ok1, q2, q1)))
        return new_lo, new_hi

    lo0 = jnp.min(key, axis=-1, keepdims=True)
    lo, hi = lax.fori_loop(0, ROUTE_BISECT // 2, body, (lo0, jnp.ones_like(lo0)))
    thr, found = lo, jnp.zeros(lo.shape, jnp.int32)
    for _ in range(ROUTE_SNAP):
        v = jnp.max(jnp.where(key < hi, key, -3e38), axis=-1, keepdims=True)
        ok = jnp.where(count_ge(v) >= cap, 1, 0)
        thr = jnp.where(jnp.logical_and(found == 0, ok == 1), v, thr)
        hi = jnp.where(jnp.logical_or(found == 1, ok == 1), hi, v)
        found = jnp.maximum(found, ok)
    gt = key > thr
    eq = key == thr
    need = cap - jnp.sum(jnp.where(gt, 1, 0), axis=-1, keepdims=True)
    take_eq = jnp.logical_and(eq, _lane_cumsum(jnp.where(eq, 1, 0)) <= need)
    sel = jnp.logical_or(gt, take_eq)
    slot = _lane_cumsum(jnp.where(sel, 1, 0)) - 1 + pl.program_id(0) * slot_stride
    rank_ref[0] = jnp.where(sel, slot, -1)
    aff_ref[0] = aff


def _route(lg, cap, slot_stride):
    g, e, t = lg.shape
    spec = pl.BlockSpec((1, e, t), lambda gi: (gi, 0, 0))
    return pl.pallas_call(
        functools.partial(_route_kernel, cap=cap, slot_stride=slot_stride),
        grid=(g,), in_specs=[spec], out_specs=[spec, spec],
        out_shape=[jax.ShapeDtypeStruct((g, e, t), jnp.int32), jax.ShapeDtypeStruct((g, e, t), F32)],
        compiler_params=_cparams(("parallel",)), name="route",
    )(lg)


def _expert_kernel(h_ref, rank_ref, aff_ref, w1_ref, w3_ref, w2_ref, o_ref, *, cap):
    e = pl.program_id(1)

    @pl.when(e == 0)
    def _():
        o_ref[...] = jnp.zeros_like(o_ref)

    rank = rank_ref[0, 0]
    t = rank.shape[-1]
    hit = lax.broadcasted_iota(jnp.int32, (cap, t), 0) == rank
    pick = jnp.where(hit, 1.0, 0.0).astype(BF16)
    gate = jnp.sum(jnp.where(hit, aff_ref[0, 0], 0.0), axis=-1, keepdims=True)
    xe = jnp.dot(pick, h_ref[0], preferred_element_type=F32).astype(BF16)
    h1 = jnp.dot(xe, w1_ref[0, 0].astype(BF16), preferred_element_type=F32)
    h3 = jnp.dot(xe, w3_ref[0, 0].astype(BF16), preferred_element_type=F32)
    hid = (h1 * _sigmoid(h1) * h3).astype(BF16)
    ye = (jnp.dot(hid, w2_ref[0], preferred_element_type=F32) * gate).astype(BF16)
    o_ref[0] += lax.dot_general(pick, ye, TN, preferred_element_type=F32)


def _experts(h, rank, aff, w1, w3, w2, layer, cap):
    g, t, _ = h.shape
    _, n_exp, _, f = w1.shape
    sel = pl.BlockSpec((1, 1, 1, t), lambda gi, e: (gi, e, 0, 0))
    return pl.pallas_call(
        functools.partial(_expert_kernel, cap=cap),
        grid=(g, n_exp),
        in_specs=[pl.BlockSpec((1, t, D), lambda gi, e: (gi, 0, 0)), sel, sel,
                  pl.BlockSpec((1, 1, D, f), lambda gi, e: (layer, e, 0, 0)),
                  pl.BlockSpec((1, 1, D, f), lambda gi, e: (layer, e, 0, 0)),
                  pl.BlockSpec((1, f, D), lambda gi, e: (e, 0, 0))],
        out_specs=pl.BlockSpec((1, t, D), lambda gi, e: (gi, 0, 0)),
        out_shape=jax.ShapeDtypeStruct((g, t, D), F32),
        compiler_params=_cparams(("parallel", "arbitrary")), name="experts",
    )(h, rank.reshape(g, n_exp, 1, t), aff.reshape(g, n_exp, 1, t), w1, w3, w2)


def _moe(h, lg, w1, w3, w2, layer, bsz, ctx_len, lat_groups_only):
    g, t, _ = h.shape
    cap_lat = EC_CAP * t // N_EXP
    rank, aff = _route(lg[:bsz], cap_lat, 0)
    if not lat_groups_only:
        cap_ctx = EC_CAP * ctx_len // N_EXP
        lg_c = lg[bsz].reshape(N_EXP, bsz, ctx_len).transpose(1, 0, 2)
        rank_c, aff_c = _route(lg_c, cap_ctx, cap_ctx)
        back = lambda a: a.transpose(1, 0, 2).reshape(1, N_EXP, t)
        rank = jnp.concatenate([rank, back(rank_c)], axis=0)
        aff = jnp.concatenate([aff, back(aff_c)], axis=0)
        assert cap_ctx * bsz == cap_lat
    else:
        h = h[:bsz] if h.shape[0] != bsz else h
    return _experts(h, rank, aff, w1, w3, w2, layer, cap_lat)


def _block_diag2(w):
    z = jnp.zeros_like(w[0])
    return jnp.concatenate([jnp.concatenate([w[0], z], axis=1), jnp.concatenate([z, w[1]], axis=1)], axis=0)


def _rope_tables(t):
    quarter = C_HD // 4
    inv = ROPE_BASE ** (-jnp.arange(quarter, dtype=F32) / quarter)
    pos = jnp.arange(t)
    row = (pos // GRID_W).astype(F32)
    col = (pos % GRID_W).astype(F32)
    lane = np.arange(LANE)
    use_col = ((lane % C_HD) >= C_HD // 2)
    ang = jnp.where(use_col[None, :], col[:, None], row[:, None]) * inv[lane % quarter][None, :]
    sign = np.where((lane % (C_HD // 2)) < quarter, -1.0, 1.0).astype(np.float32)
    return jnp.cos(ang), jnp.sin(ang) * sign[None, :]


def kernel(x, c, ctx, c_ctx, ada_w, ada_b, norm_mix, norm_ffn, e_w_in, e_w_out, a_mu, a_w0, a_w2, a_a0, a_a2, a_g2,
           a_k_k, a_k_a, a_r_k, a_ln_w, a_ln_b, b_conv, b_a_log, b_dt_bias, b_norm, o_w_in, o_w_out, o_sink,
           moe_router, moe_w1, moe_w3, moe_w2, final_norm):
    bsz, t, _ = x.shape
    ctx_len = ctx.shape[1]
    depth = ada_w.shape[0]
    assert bsz * ctx_len == t and ctx_len % C_BLOCK == 0 and t % C_BLOCK == 0
    tt = ctx_len

    rows = 2 * SUB * ((bsz + 1 + 2 * SUB - 1) // (2 * SUB))
    cc = jnp.zeros((rows, D), F32).at[:bsz].set(c).at[bsz].set(c_ctx)
    mods = _adaln(cc, ada_w, ada_b)[:, :bsz + 1].reshape(depth, bsz + 1, N_MOD, D)
    mods = jnp.pad(mods, ((0, 0), (0, 0), (0, MOD_ROWS - N_MOD), (0, 0)))

    xs = (x, ctx.reshape(1, t, D))
    ones_bd = jnp.asarray(np.kron(np.eye(A_H), np.ones((A_HD, A_HD))), BF16)
    a_cols = e_w_in.shape[2] - (4 * B_W + 4 * B_H)
    cos_t, sin_t = _rope_tables(t)

    res = None
    for i in range(depth):
        j = i // 2
        mod = mods[i]
        last = i == depth - 1
        if i % 2 == 0:
            w_in = e_w_in[j]
            split_ab = a_cols + 3 * B_W
            w_all = jnp.concatenate([w_in[:, :split_ab], w_in[:, split_ab + 4 * B_H:],
                                     w_in[:, split_ab:split_ab + 4 * B_H],
                                     jnp.zeros((D, LANE - 4 * B_H), F32)], axis=1).astype(BF16)
            x_new, (pa, pb, zab) = _mm_split(xs, norm_mix[i], mod, 0, res, w_all, (a_cols, 3 * B_W, B_W + LANE),
                                             F32, 2 * tt)
            xs = xs if res is None else x_new
            prm = {
                "mu": a_mu[j].reshape(1, -1), "w0": a_w0[j].reshape(1, -1), "w2": _block_diag2(a_w2[j]),
                "a0": a_a0[j].reshape(1, -1), "a2": _block_diag2(a_a2[j]), "g2": a_g2[j],
                "k_k": a_k_k[j].reshape(1, -1), "k_a": jnp.tile(a_k_a[j].reshape(1, -1), (1, 2)),
                "r_k": a_r_k[j].reshape(1, -1), "ones_bd": ones_bd,
            }
            rvk, cum, kd, bd, bg = _rwkv_prep(pa, prm, tt, bsz)
            pad_row = lambda v: jnp.pad(v.reshape(1, -1), ((0, 0), (0, LANE - v.size)))
            qkv_c, gb = _gdn_prep(pb, zab, b_conv[j], pad_row(b_a_log[j]), pad_row(b_dt_bias[j]), tt, bsz)
            ya_f, ob_f, ya_b, ob_b = _even_scan(rvk, cum, kd, bd, qkv_c, gb, bsz, ctx_len)
            xs, hf, lg = _mix_out((ya_f, ya_b), bg, a_ln_w[j].reshape(1, -1), a_ln_b[j].reshape(1, -1), ones_bd,
                                  (ob_f, ob_b), zab, b_norm[j].reshape(1, -1), e_w_out[j].astype(BF16), xs, mod, 2,
                                  norm_ffn[i], moe_router[i], 3, tt)
        else:
            assert last
            w_in = o_w_in[j]
            q_w = C_QH * C_HD
            kv_cols = C_KVH * C_HD
            dup = lambda w: jnp.repeat(w.reshape(D, C_KVH, 1, C_HD), 2, axis=2).reshape(D, 2 * kv_cols)
            w_all = jnp.concatenate([w_in[:, :q_w], dup(w_in[:, q_w:q_w + kv_cols]), dup(w_in[:, q_w + kv_cols:])],
                                    axis=1).astype(BF16)
            tn = 2 * kv_cols
            x_new, qkv = _qkv_rope(xs, norm_mix[i], mod, 0, res, w_all, cos_t, sin_t, bsz, 2 * tt, tn, q_w, q_w + tn)
            xs = xs if res is None else x_new
            att = _attention(qkv, o_sink[j], bsz, ctx_len)
            xs, hf, lg = _mm_res(att, o_w_out[j].astype(BF16), xs, mod, 2, norm_ffn[i], moe_router[i], 3, tt,
                                 groups=bsz)
        if last:
            xs, hf, lg, mod = xs[:bsz], hf[:bsz], lg[:bsz], mod[:bsz]
        lg = lg[:, :, :N_EXP].transpose(0, 2, 1)
        delta = _moe(hf, lg, moe_w1, moe_w3, _to_bf16(moe_w2, i), i, bsz, ctx_len, last)
        res = (delta, mod, 5)
    return _final_norm(xs, res, final_norm, tt)
```

```python
import functools
import math

import jax
import jax.numpy as jnp
import numpy as np
from jax import lax
from jax.experimental import pallas as pl
from jax.experimental.pallas import tpu as pltpu

F32 = jnp.float32
BF16 = jnp.bfloat16
HIGHEST = lax.Precision.HIGHEST

D = 1024
N_MOD = 6
NORM_EPS = 1e-6
A_HD, A_H, A_W = 64, 8, 512
A_LN_EPS = 64e-5
B_HD, B_H, B_W = 128, 4, 512
B_CONV = 5
CHUNK = 64
C_HD, C_QH, C_KVH, C_GROUP = 64, 16, 4, 4
C_BLOCK = 128
ROPE_BASE = 10000.0
GRID_W = 64
N_EXP = 16
EC_CAP = 2
ROUTE_BISECT = 40
ROUTE_SNAP = 3
RWKV_PASSES = {"quad": 1, "apply": 1, "solve": "b", "square": 1, "state_in": "a", "state_out": "b"}
GDN_PASSES = {"quad": 1, "apply": 1, "solve": "b", "square": 1, "state": 1}
MOD_ROWS = 8
LANE = 128
SUB = 8
VMEM_LIMIT = 56 * 1024 * 1024

NT = (((1,), (1,)), ((), ()))
NN = (((1,), (0,)), ((), ()))
TN = (((0,), (0,)), ((), ()))


def _cparams(sem):
    return pltpu.CompilerParams(dimension_semantics=sem, vmem_limit_bytes=VMEM_LIMIT)


def _sigmoid(x):
    return 1.0 / (1.0 + jnp.exp(-x))


def _softplus(x):
    return jnp.maximum(x, 0.0) + jnp.log(1.0 + jnp.exp(-jnp.abs(x)))


def _split2(x):
    hi = x.astype(BF16)
    lo = (x - hi.astype(F32)).astype(BF16)
    return hi, lo


def _dot(a, b, dims=NN, passes=1):
    dg = functools.partial(lax.dot_general, dimension_numbers=dims, preferred_element_type=F32)
    if passes == 1:
        return dg(a.astype(BF16), b.astype(BF16))
    (ka,), (kb,) = dims[0]
    if passes == "a":
        ah, al = _split2(a)
        bh = b.astype(BF16)
        return dg(jnp.concatenate([ah, al], axis=ka), jnp.concatenate([bh, bh], axis=kb))
    if passes == "b":
        bh, bl = _split2(b)
        n = b.shape[1 - kb]
        res = dg(a.astype(BF16), jnp.concatenate([bh, bl], axis=1 - kb))
        return res[:, 0:n] + res[:, n:2 * n]
    ah, al = _split2(a)
    bh, bl = _split2(b)
    return dg(jnp.concatenate([ah, al, ah], axis=ka), jnp.concatenate([bh, bh, bl], axis=kb))


def _group_sum(x, ones_bd):
    hi, lo = _split2(x)
    return lax.dot_general(jnp.concatenate([hi, lo], axis=1), jnp.concatenate([ones_bd, ones_bd], axis=0), NN,
                           preferred_element_type=F32)


def _cast_kernel(w_ref, o_ref):
    o_ref[...] = w_ref[...].astype(o_ref.dtype)


def _to_bf16(w, layer):
    _, n, r, c = w.shape
    return pl.pallas_call(_cast_kernel, grid=(n,),
                          in_specs=[pl.BlockSpec((1, 1, r, c), lambda e: (layer, e, 0, 0))],
                          out_specs=pl.BlockSpec((1, 1, r, c), lambda e: (0, e, 0, 0)),
                          out_shape=jax.ShapeDtypeStruct((1, n, r, c), BF16),
                          compiler_params=_cparams(("parallel",)), name="cast_bf16")(w)[0]


def _adaln_kernel(c_ref, w_ref, b_ref, o_ref):
    c = c_ref[...]
    s = c * _sigmoid(c)
    o_ref[0] = jnp.dot(s, w_ref[0], precision=HIGHEST, preferred_element_type=F32) + b_ref[0]


def _adaln(cc, ada_w, ada_b):
    depth, _, n = ada_w.shape
    rows = cc.shape[0]
    tn = 768
    return pl.pallas_call(
        _adaln_kernel,
        grid=(depth, n // tn),
        in_specs=[pl.BlockSpec((rows, D), lambda i, j: (0, 0)),
                  pl.BlockSpec((1, D, tn), lambda i, j: (i, 0, j)),
                  pl.BlockSpec((1, 1, tn), lambda i, j: (i, 0, j))],
        out_specs=pl.BlockSpec((1, rows, tn), lambda i, j: (i, 0, j)),
        out_shape=jax.ShapeDtypeStruct((depth, rows, n), F32),
        compiler_params=_cparams(("parallel", "parallel")),
        name="adaln",
    )(cc, ada_w, ada_b.reshape(depth, 1, n))


def _final_norm_kernel(x_ref, d_ref, mres_ref, gain_ref, o_ref, *, kres):
    x = x_ref[0] + mres_ref[0, kres:kres + 1, :] * d_ref[0]
    o_ref[0] = x * lax.rsqrt(jnp.mean(x * x, axis=-1, keepdims=True) + NORM_EPS) * gain_ref[...]


def _final_norm(x, res, gain, tt):
    g, t, _ = x.shape
    tile = pl.BlockSpec((1, tt, D), lambda gi, i: (gi, i, 0))
    return pl.pallas_call(
        functools.partial(_final_norm_kernel, kres=res[2]),
        grid=(g, t // tt),
        in_specs=[tile, tile, pl.BlockSpec((1, MOD_ROWS, D), lambda gi, i: (gi, 0, 0)),
                  pl.BlockSpec((1, D), lambda gi, i: (0, 0))],
        out_specs=tile, out_shape=jax.ShapeDtypeStruct((g, t, D), F32),
        compiler_params=_cparams(("parallel", "parallel")), name="final_norm",
    )(x, res[0], res[1], gain.reshape(1, D))


def _rows_specs(x, tm):
    if not isinstance(x, tuple):
        return [x], [pl.BlockSpec((1, tm, D), lambda gi, i: (gi, i, 0))], x.shape[0], x.shape[1]
    lat, ctx = x
    n_lat, t = lat.shape[0], lat.shape[1]
    last = t // tm - 1
    specs = [pl.BlockSpec((1, tm, D), lambda gi, i: (jnp.minimum(gi, n_lat - 1), jnp.where(gi < n_lat, i, last), 0)),
             pl.BlockSpec((1, tm, D), lambda gi, i: (0, jnp.where(gi < n_lat, 0, i), 0))]
    return [lat, ctx], specs, n_lat + 1, t


def _rows(refs, n_lat):
    if n_lat is None:
        return next(refs)[0]
    lat_ref, ctx_ref = next(refs), next(refs)
    return jnp.where(pl.program_id(0) < n_lat, lat_ref[0], ctx_ref[0])


def _n_lat(x):
    return x[0].shape[0] if isinstance(x, tuple) else None


def _mixer_input(refs, has_res, k, kres, n_lat=None):
    x = _rows(refs, n_lat)
    if has_res:
        d_ref, mres_ref = next(refs), next(refs)
        x = x + mres_ref[0, kres:kres + 1, :] * d_ref[0]
    gain_ref, mod_ref = next(refs), next(refs)
    xn = x * lax.rsqrt(jnp.mean(x * x, axis=-1, keepdims=True) + NORM_EPS) * gain_ref[...]
    h = (xn * (1.0 + mod_ref[0, k + 1:k + 2, :]) + mod_ref[0, k:k + 1, :]).astype(BF16)
    return (x if has_res else None), h


def _mixer_input_specs(x, gain, mod, res, tm):
    tile = pl.BlockSpec((1, tm, D), lambda gi, i: (gi, i, 0))
    modspec = pl.BlockSpec((1, MOD_ROWS, D), lambda gi, i: (gi, 0, 0))
    ins, specs, g, t = _rows_specs(x, tm)
    if res is not None:
        ins += [res[0], res[1]]
        specs += [tile, modspec]
    ins += [gain.reshape(1, D), mod]
    specs += [pl.BlockSpec((1, D), lambda gi, i: (0, 0)), modspec]
    return ins, specs, tile, g, t


def _mm_split_kernel(*refs, widths, has_res, k, kres, n_lat):
    it = iter(refs)
    x_new, h = _mixer_input(it, has_res, k, kres, n_lat)
    w_ref = next(it)
    if has_res:
        next(it)[0] = x_new
    start = 0
    for o_ref, width in zip(it, widths):
        o_ref[0] = jnp.dot(h, w_ref[:, start:start + width], preferred_element_type=F32).astype(o_ref.dtype)
        start += width


def _mm_split(x, gain, mod, k, res, w, widths, out_dtype, tm):
    assert sum(widths) == w.shape[1] and all(wd % LANE == 0 for wd in widths)
    ins, specs, tile, g, t = _mixer_input_specs(x, gain, mod, res, tm)
    has_res = res is not None
    outs = list(pl.pallas_call(
        functools.partial(_mm_split_kernel, widths=widths, has_res=has_res, k=k, kres=res[2] if has_res else 0,
                          n_lat=_n_lat(x)),
        grid=(g, t // tm),
        in_specs=specs + [pl.BlockSpec(w.shape, lambda gi, i: (0, 0))],
        out_specs=([tile] if has_res else []) + [pl.BlockSpec((1, tm, wd), lambda gi, i: (gi, i, 0)) for wd in widths],
        out_shape=([jax.ShapeDtypeStruct((g, t, D), F32)] if has_res else [])
        + [jax.ShapeDtypeStruct((g, t, wd), out_dtype) for wd in widths],
        compiler_params=_cparams(("parallel", "parallel")), name="proj",
    )(*ins, w))
    return (outs.pop(0) if has_res else None), outs


def _ffn_input(x, gain_ref, mod_ref, rt_ref, k, h_ref, lg_ref):
    xn = x * lax.rsqrt(jnp.mean(x * x, axis=-1, keepdims=True) + NORM_EPS) * gain_ref[...]
    xn = xn * (1.0 + mod_ref[0, k + 1:k + 2, :]) + mod_ref[0, k:k + 1, :]
    h_ref[0] = xn.astype(h_ref.dtype)
    lg_ref[0] = _dot(xn, rt_ref[...], NN, 3)


def _ffn_input_specs(g, t, tm, gain, router):
    ins = [gain.reshape(1, D), jnp.pad(router, ((0, 0), (0, LANE - router.shape[1])))]
    specs = [pl.BlockSpec((1, D), lambda gi, i: (0, 0)), pl.BlockSpec((D, LANE), lambda gi, i: (0, 0))]
    ospecs = [pl.BlockSpec((1, tm, D), lambda gi, i: (gi, i, 0)), pl.BlockSpec((1, tm, LANE), lambda gi, i: (gi, i, 0))]
    oshapes = [jax.ShapeDtypeStruct((g, t, D), BF16), jax.ShapeDtypeStruct((g, t, LANE), F32)]
    return ins, specs, ospecs, oshapes


def _mm_res_kernel(y_ref, w_ref, x_ref, mod_ref, gain_ref, rt_ref, o_ref, h_ref, lg_ref, *, k, kffn):
    acc = jnp.dot(y_ref[0], w_ref[...], preferred_element_type=F32)
    x = x_ref[0] + mod_ref[0, k:k + 1, :] * acc
    o_ref[0] = x
    _ffn_input(x, gain_ref, mod_ref, rt_ref, kffn, h_ref, lg_ref)


def _mm_res(y, w, x, mod, k, gain_ffn, router, kffn, tm, groups):
    _, t, kdim = y.shape
    g = groups
    ins, specs, ospecs, oshapes = _ffn_input_specs(g, t, tm, gain_ffn, router)
    return pl.pallas_call(
        functools.partial(_mm_res_kernel, k=k, kffn=kffn),
        grid=(g, t // tm),
        in_specs=[pl.BlockSpec((1, tm, kdim), lambda gi, i: (gi, i, 0)),
                  pl.BlockSpec((kdim, D), lambda gi, i: (0, 0)),
                  pl.BlockSpec((1, tm, D), lambda gi, i: (gi, i, 0)),
                  pl.BlockSpec((1, MOD_ROWS, D), lambda gi, i: (gi, 0, 0))] + specs,
        out_specs=[pl.BlockSpec((1, tm, D), lambda gi, i: (gi, i, 0))] + ospecs,
        out_shape=[jax.ShapeDtypeStruct((g, t, D), F32)] + oshapes,
        compiler_params=_cparams(("parallel", "parallel")), name="out_proj_res",
    )(y, w, x, mod, *ins)


def _fill_halo(buf, x, pv_ref, nx_ref, has_prev, has_next, tt):
    buf[0:SUB, :] = jnp.where(has_prev, pv_ref[0], 0.0)
    buf[SUB:SUB + tt, :] = x
    buf[SUB + tt:2 * SUB + tt, :] = jnp.where(has_next, nx_ref[0], 0.0)


def _rwkv_prep_kernel(pa_ref, pv_ref, nx_ref, mu_ref, w0_ref, w2_ref, a0_ref, a2_ref, g2_ref, kk_ref, ka_ref,
                      rk_ref, bd_ref, rvk_ref, cum_ref, kd_ref, bdir_ref, bg_ref, buf, *, tt, lat_groups, ntile):
    gi, i = pl.program_id(0), pl.program_id(1)
    is_lat = gi < lat_groups
    x = pa_ref[0]
    _fill_halo(buf, x, pv_ref, nx_ref, jnp.logical_and(is_lat, i > 0), jnp.logical_and(is_lat, i < ntile - 1), tt)
    xm = buf[SUB - 1:SUB - 1 + tt, :]
    xp = buf[SUB + 1:SUB + 1 + tt, :]
    x = x + mu_ref[...] * (0.5 * (xm + xp) - x)
    w = A_W
    r, kx, v = x[:, 0:w], x[:, w:2 * w], x[:, 2 * w:3 * w]
    wd, ad, gd = x[:, 3 * w:3 * w + LANE], x[:, 3 * w + LANE:3 * w + 2 * LANE], x[:, 3 * w + 2 * LANE:3 * w + 3 * LANE]
    u = w0_ref[...] + _dot(jnp.tanh(wd), w2_ref[...], NN, 3)
    lw = -math.exp(-0.5) * _sigmoid(u)
    a = _sigmoid(a0_ref[...] + _dot(ad, a2_ref[...], NN, 3))
    gate = _dot(_sigmoid(gd), g2_ref[...], NN, 3)
    ones_bd = bd_ref[...]
    kq = kx * kk_ref[...]
    kk = kq * lax.rsqrt(_group_sum(kq * kq, ones_bd) + 1e-6)
    k2 = jnp.concatenate([kx, kx], axis=1)
    kdir = k2 * (1.0 + (a - 1.0) * ka_ref[...])
    bdir = jnp.concatenate([kk, kk], axis=1) * a
    rr = r * rk_ref[...]
    bonus = _group_sum(rr * (kdir[:, 0:w] + kdir[:, w:2 * w]), ones_bd) * v
    rvk_ref[0] = jnp.concatenate([r, v, kk], axis=1)
    cum_ref[0] = jnp.concatenate([_chunk_cumsum(lw[:, 0:w], True), _chunk_cumsum(lw[:, w:2 * w], False)], axis=1)
    kd_ref[0] = kdir.astype(kd_ref.dtype)
    bdir_ref[0] = bdir.astype(bdir_ref.dtype)
    bg_ref[0] = jnp.concatenate([bonus, gate], axis=1).astype(bg_ref.dtype)


def _chunk_cumsum(x, fwd):
    _, _, incl = _order_masks(x.shape[0], CHUNK, fwd)
    mask = jnp.where(incl, 1.0, 0.0).astype(BF16)
    p1 = x.astype(BF16)
    r1 = x - p1.astype(F32)
    p2 = r1.astype(BF16)
    p3 = (r1 - p2.astype(F32)).astype(BF16)
    return lax.dot_general(jnp.concatenate([mask, mask, mask], axis=1), jnp.concatenate([p1, p2, p3], axis=0), NN,
                           preferred_element_type=F32)


def _halo_specs(width, tt, t):
    nb = tt // SUB
    last = t // SUB - 1
    prev = pl.BlockSpec((1, SUB, width), lambda gi, i: (gi, jnp.maximum(i * nb - 1, 0), 0))
    nxt = pl.BlockSpec((1, SUB, width), lambda gi, i: (gi, jnp.minimum((i + 1) * nb, last), 0))
    return prev, nxt


def _rwkv_prep(pa, p, tt, lat_groups):
    g, t, wa = pa.shape
    full = lambda arr: pl.BlockSpec(arr.shape, lambda gi, i: (0,) * arr.ndim)
    tile = lambda width: pl.BlockSpec((1, tt, width), lambda gi, i: (gi, i, 0))
    prev, nxt = _halo_specs(wa, tt, t)
    consts = [p["mu"], p["w0"], p["w2"], p["a0"], p["a2"], p["g2"], p["k_k"], p["k_a"], p["r_k"], p["ones_bd"]]
    widths = (3 * A_W, 2 * A_W, 2 * A_W, 2 * A_W, 2 * A_W)
    return pl.pallas_call(
        functools.partial(_rwkv_prep_kernel, tt=tt, lat_groups=lat_groups, ntile=t // tt),
        grid=(g, t // tt),
        in_specs=[tile(wa), prev, nxt] + [full(c) for c in consts],
        out_specs=[tile(wd) for wd in widths],
        out_shape=[jax.ShapeDtypeStruct((g, t, wd), dt) for wd, dt in zip(widths, (F32, F32, BF16, BF16, BF16))],
        scratch_shapes=[pltpu.VMEM((tt + 2 * SUB, wa), F32)],
        compiler_params=_cparams(("parallel", "parallel")), name="rwkv_prep",
    )(pa, pa, pa, *consts)


def _order_masks(n_rows, blk, fwd):
    ii = lax.broadcasted_iota(jnp.int32, (n_rows, n_rows), 0)
    jj = lax.broadcasted_iota(jnp.int32, (n_rows, n_rows), 1)
    same = (ii // blk) == (jj // blk)
    before = (jj < ii) if fwd else (jj > ii)
    strict = jnp.logical_and(same, before)
    incl = jnp.logical_and(same, jnp.logical_or(before, ii == jj))
    return same, strict, incl


def _even_scan_kernel(rvkf_ref, cumf_ref, kdf_ref, bdf_ref, qkvf_ref, gbf_ref,
                      rvkb_ref, cumb_ref, kdb_ref, bdb_ref, qkvb_ref, gbb_ref,
                      yf_ref, of_ref, yb_ref, ob_ref, sa_ref, sb_ref):
    @pl.when(pl.program_id(1) == 0)
    def _():
        sa_ref[...] = jnp.zeros_like(sa_ref)
        sb_ref[...] = jnp.zeros_like(sb_ref)

    na, nb = A_W // LANE, B_H // 2
    sa, sb = sa_ref[...], sb_ref[...]
    chains_a = (_rwkv_operands(rvkf_ref[0], cumf_ref[0], kdf_ref[0], bdf_ref[0], True)
                + _rwkv_operands(rvkb_ref[0], cumb_ref[0], kdb_ref[0], bdb_ref[0], False))
    chains_b = _gdn_operands(qkvf_ref[0], gbf_ref[0], 0) + _gdn_operands(qkvb_ref[0], gbb_ref[0], 1)
    states_a = [sa[d, p] for d in range(2) for p in range(na)]
    states_b = [(sb[d, 2 * p], sb[d, 2 * p + 1]) for d in range(2) for p in range(nb)]
    res_a, res_b = {}, {}
    stages = [_rwkv_solve(chains_a, states_a, RWKV_PASSES, res_a), _gdn_solve(chains_b, states_b, GDN_PASSES, res_b)]
    while stages:
        stages = [s for s in stages if next(s, "done") != "done"]
    yf_ref[0] = jnp.concatenate(res_a["y"][0:na], axis=1)
    yb_ref[0] = jnp.concatenate(res_a["y"][na:2 * na], axis=1)
    of_ref[0] = jnp.concatenate([o for pair in res_b["o"][0:nb] for o in pair], axis=1)
    ob_ref[0] = jnp.concatenate([o for pair in res_b["o"][nb:2 * nb] for o in pair], axis=1)
    for i, s_i in enumerate(res_a["s"]):
        sa_ref[i // na, i % na] = s_i
    for i, pair in enumerate(res_b["s"]):
        for j, s_h in enumerate(pair):
            sb_ref[i // nb, 2 * (i % nb) + j] = s_h


def _rwkv_operands(rvk, cum, kd, bd, fwd):
    L = CHUNK
    L2 = 2 * L
    _, strict, incl = _order_masks(L2, L, fwd)
    row = lax.broadcasted_iota(jnp.int32, (L, 1), 0)
    if fwd:
        tot = cum[L - 1:L]
        cum_ex = jnp.where(row == 0, 0.0, pltpu.roll(cum, 1, axis=0))
    else:
        tot = cum[0:1]
        cum_ex = jnp.where(row == L - 1, 0.0, pltpu.roll(cum, L - 1, axis=0))
    g_inv = jnp.exp(-cum)
    g_rest = jnp.exp(tot - cum)
    g_tot = jnp.exp(tot)
    w = A_W
    r_t = rvk[:, 0:w] * jnp.exp(cum)
    v = rvk[:, w:2 * w]
    a_t = rvk[:, 2 * w:3 * w] * jnp.exp(cum_ex)
    b_t, k_t = bd * g_inv, kd * g_inv
    b_g, k_g = bd * g_rest, kd * g_rest
    low = lax.broadcasted_iota(jnp.int32, (1, LANE), 1) < A_HD

    def expand(x):
        return jnp.concatenate([jnp.where(low, x, 0.0), jnp.where(low, 0.0, x)], axis=0)

    chains = []
    for p in range(A_W // LANE):
        sl = slice(p * LANE, (p + 1) * LANE)
        chains.append(dict(
            ev=expand(v[:, sl]),
            lhs=jnp.concatenate([expand(a_t[:, sl]), expand(r_t[:, sl])], axis=0),
            rhs=jnp.concatenate([expand(b_t[:, sl]), expand(k_t[:, sl])], axis=0),
            bk=jnp.concatenate([expand(b_g[:, sl]), expand(k_g[:, sl])], axis=0),
            g_col=jnp.transpose(jnp.broadcast_to(g_tot[:, sl], (LANE, LANE))),
            strict=strict, incl=incl))
    return chains


def _square_pairs(mats, passes):
    if passes != 1 or len(mats) % 2:
        return [_dot(m, m, NN, passes) for m in mats]
    out = []
    for m0, m1 in zip(mats[0::2], mats[1::2]):
        n = m0.shape[0]
        m0, m1 = m0.astype(BF16), m1.astype(BF16)
        zero = jnp.zeros_like(m0)
        diag = jnp.concatenate([jnp.concatenate([m0, zero], axis=1), jnp.concatenate([zero, m1], axis=1)], axis=0)
        sq = lax.dot_general(jnp.concatenate([m0, m1], axis=1), diag, NN, preferred_element_type=F32)
        out += [sq[:, 0:n], sq[:, n:2 * n]]
    return out


def _rwkv_solve(chains, states, passes, res):
    L = CHUNK
    L2 = 2 * L
    each = lambda fn, *cols: [fn(*args) for args in zip(*cols)]
    quad = each(lambda c: _dot(c["lhs"], c["rhs"], NT, passes["quad"]), chains)
    yield
    es = each(lambda c, s: _dot(c["lhs"], s, NN, passes["state_in"]), chains, states)
    yield
    n_ab = each(lambda c, q: jnp.where(c["strict"], q[0:L2, 0:L2], 0.0), chains, quad)
    x = each(lambda c, q, e: e[0:L2] + _dot(jnp.where(c["strict"], q[0:L2, L2:2 * L2], 0.0), c["ev"], NN,
                                            passes["apply"]), chains, quad, es)
    yield
    x = each(lambda n, xi: xi - _dot(n, xi, NN, passes["solve"]), n_ab, x)
    yield
    npow = n_ab
    for _ in range(int(math.log2(L)) - 1):
        npow = _square_pairs(npow, passes["square"])
        yield
        x = each(lambda n, xi: xi + _dot(n, xi, NN, passes["solve"]), npow, x)
        yield
    pv = each(lambda c, xi: jnp.concatenate([-xi, c["ev"]], axis=0), chains, x)
    y2 = each(lambda c, q, e, pvi: e[L2:2 * L2] + _dot(
        jnp.concatenate([jnp.where(c["incl"], q[L2:2 * L2, 0:L2], 0.0),
                         jnp.where(c["incl"], q[L2:2 * L2, L2:2 * L2], 0.0)], axis=1), pvi, NN, passes["apply"]),
        chains, quad, es, pv)
    res["y"] = [y[0:L] + y[L:L2] for y in y2]
    yield
    res["s"] = each(lambda c, s, pvi: s * c["g_col"] + _dot(c["bk"], pvi, TN, passes["state_out"]),
                    chains, states, pv)


def _chunk_index(b, c, fwd, n_ctx_chunk, n_lat_chunk, lat_groups):
    in_ctx = c < n_ctx_chunk
    cc = c if fwd else n_ctx_chunk - 1 - c
    lc = c - n_ctx_chunk if fwd else n_lat_chunk - 1 - (c - n_ctx_chunk)
    grp = jnp.where(in_ctx, lat_groups, b)
    chunk = jnp.where(in_ctx, b * n_ctx_chunk + cc, lc)
    return grp, chunk


def _scan_specs(widths, bsz, ctx_len, t):
    ncc, nlc = ctx_len // CHUNK, t // CHUNK

    def spec(width, lane_blk, fwd):
        def index(b, c):
            grp, ch = _chunk_index(b, c, fwd, ncc, nlc, bsz)
            return grp, ch, lane_blk
        return pl.BlockSpec((1, CHUNK, width), index)

    return [[spec(w, (0 if fwd else 1) if blk is None else blk, fwd) for w, blk in widths] for fwd in (True, False)]


def _even_scan(rvk, cum, kd, bd, qkv, gb, bsz, ctx_len):
    g, t, _ = rvk.shape
    ins_f, ins_b = _scan_specs([(3 * A_W, 0), (A_W, None), (A_W, None), (A_W, None), (3 * B_W, 0), (LANE, 0)],
                               bsz, ctx_len, t)
    outs_f, outs_b = _scan_specs([(A_W, 0), (B_W, 0)], bsz, ctx_len, t)
    shapes = [jax.ShapeDtypeStruct((g, t, A_W), F32), jax.ShapeDtypeStruct((g, t, B_W), F32)]
    return pl.pallas_call(
        _even_scan_kernel,
        grid=(bsz, (ctx_len + t) // CHUNK),
        in_specs=ins_f + ins_b,
        out_specs=outs_f + outs_b,
        out_shape=shapes + shapes,
        scratch_shapes=[pltpu.VMEM((2, A_W // LANE, LANE, LANE), F32), pltpu.VMEM((2, B_H, B_HD, B_HD), F32)],
        compiler_params=_cparams(("parallel", "arbitrary")), name="even_scan",
    )(rvk, cum, kd, bd, qkv, gb, rvk, cum, kd, bd, qkv, gb)


def _gdn_prep_kernel(pb_ref, pv_ref, nx_ref, ab_ref, cw_ref, alog_ref, dtb_ref, qkv_ref, gb_ref, buf,
                     *, tt, lat_groups, ntile):
    gi, i = pl.program_id(0), pl.program_id(1)
    is_lat = gi < lat_groups
    _fill_halo(buf, pb_ref[0], pv_ref, nx_ref, jnp.logical_and(is_lat, i > 0),
               jnp.logical_and(is_lat, i < ntile - 1), tt)
    half = B_CONV // 2
    acc = buf[SUB - half:SUB - half + tt, :] * cw_ref[0:1, :]
    for j in range(1, B_CONV):
        acc = acc + buf[SUB - half + j:SUB - half + j + tt, :] * cw_ref[j:j + 1, :]
    act = acc * _sigmoid(acc)
    outs = []
    for h in range(2 * B_H):
        xh = act[:, h * B_HD:(h + 1) * B_HD]
        xh = xh * lax.rsqrt(jnp.sum(xh * xh, axis=-1, keepdims=True) + 1e-6)
        if h < B_H:
            xh = xh * (B_HD ** -0.5)
        outs.append(xh)
    outs.append(act[:, 2 * B_W:3 * B_W])
    qkv_ref[0] = jnp.concatenate(outs, axis=1)
    ab = ab_ref[0]
    lane = lax.broadcasted_iota(jnp.int32, ab.shape, 1)
    gval = -jnp.exp(alog_ref[...]) * _softplus(ab + dtb_ref[...])
    gb_ref[0] = jnp.where(lane < 2 * B_H, gval, _sigmoid(ab))


def _gdn_prep(pb, zab, conv_w, alog_row, dtb_row, tt, lat_groups):
    g, t, wb = pb.shape
    tile = lambda width: pl.BlockSpec((1, tt, width), lambda gi, i: (gi, i, 0))
    full = lambda arr: pl.BlockSpec(arr.shape, lambda gi, i: (0,) * arr.ndim)
    prev, nxt = _halo_specs(wb, tt, t)
    ab_spec = pl.BlockSpec((1, tt, LANE), lambda gi, i: (gi, i, B_W // LANE))
    return pl.pallas_call(
        functools.partial(_gdn_prep_kernel, tt=tt, lat_groups=lat_groups, ntile=t // tt),
        grid=(g, t // tt),
        in_specs=[tile(wb), prev, nxt, ab_spec, full(conv_w), full(alog_row), full(dtb_row)],
        out_specs=[tile(wb), tile(LANE)],
        out_shape=[jax.ShapeDtypeStruct((g, t, wb), F32), jax.ShapeDtypeStruct((g, t, LANE), F32)],
        scratch_shapes=[pltpu.VMEM((tt + 2 * SUB, wb), F32)],
        compiler_params=_cparams(("parallel", "parallel")), name="gdn_prep",
    )(pb, pb, pb, zab, conv_w, alog_row, dtb_row)


def _gdn_operands(qkv, gb, d):
    fwd = d == 0
    L = CHUNK
    L2 = 2 * L
    same, strict, incl = _order_masks(L2, L, fwd)
    _, _, incl_rev = _order_masks(L2, L, not fwd)
    incl_bf = jnp.where(incl, 1.0, 0.0).astype(BF16)
    incl_t_bf = jnp.where(incl_rev, 1.0, 0.0).astype(BF16)
    same_bf = jnp.where(same, 1.0, 0.0).astype(BF16)
    lane = lax.broadcasted_iota(jnp.int32, gb.shape, 1)

    def column(idx):
        return jnp.sum(jnp.where(lane == idx, gb, 0.0), axis=-1, keepdims=True)

    chains = []
    for pr in range(B_H // 2):
        heads = (2 * pr, 2 * pr + 1)
        stack = lambda off: jnp.concatenate([qkv[:, off + h * B_HD:off + (h + 1) * B_HD] for h in heads], axis=0)
        q2, k2, v2 = stack(0), stack(B_W), stack(2 * B_W)
        gcol = jnp.concatenate([column(d * B_H + h) for h in heads], axis=0)
        bcol = jnp.concatenate([column(2 * B_H + d * B_H + h) for h in heads], axis=0)
        chains.append(dict(q2=q2, k2=k2, v2=v2, bcol=bcol, g3=_three_pieces(gcol), strict=strict, incl=incl,
                           sum_bf=jnp.concatenate([incl_bf, same_bf], axis=0), incl_t_bf=incl_t_bf))
    return chains


def _three_pieces(col):
    lane = lax.broadcasted_iota(jnp.int32, (col.shape[0], LANE), 1)
    p1 = col.astype(BF16).astype(F32)
    p2 = (col - p1).astype(BF16).astype(F32)
    p3 = col - p1 - p2
    return jnp.where(lane == 0, p1, jnp.where(lane == 1, p2, jnp.where(lane == 2, p3, 0.0))).astype(BF16)


def _gdn_solve(chains, states, passes, res):
    L = CHUNK
    L2 = 2 * L
    each = lambda fn, *cols: [fn(*args) for args in zip(*cols)]
    dg = functools.partial(lax.dot_general, preferred_element_type=F32)
    sums = each(lambda c: dg(c["sum_bf"], c["g3"], NN), chains)
    sums_t = each(lambda c: dg(c["g3"], c["incl_t_bf"], TN), chains)
    k2b = each(lambda c: c["k2"] * c["bcol"], chains)
    kq = each(lambda c, kb: _dot(jnp.concatenate([kb, c["q2"]], axis=0), c["k2"], NT, passes["quad"]), chains, k2b)
    yield
    gc_row = each(lambda m: jnp.sum(m[0:L2], axis=-1, keepdims=True), sums)
    g_tot = each(lambda m: jnp.sum(m[L2:2 * L2], axis=-1, keepdims=True), sums)
    gc_col = each(lambda m: m[0:1] + m[1:2] + m[2:3], sums_t)
    decay = each(lambda c, r, cl: jnp.exp(jnp.where(c["incl"], r - cl, -1e30)), chains, gc_row, gc_col)
    egc = each(jnp.exp, gc_row)
    l_mat = each(lambda c, m, dc: jnp.where(c["strict"], m[0:L2] * dc, 0.0), chains, kq, decay)
    a_int = each(lambda c, m, dc: jnp.where(c["incl"], m[L2:2 * L2] * dc, 0.0), chains, kq, decay)
    x = each(lambda c, kb, eg: jnp.concatenate([c["v2"] * c["bcol"], kb * eg], axis=1), chains, k2b, egc)
    yield
    x = each(lambda n, xi: xi - _dot(n, xi, NN, passes["solve"]), l_mat, x)
    yield
    npow = l_mat
    for _ in range(int(math.log2(L)) - 1):
        npow = _square_pairs(npow, passes["square"])
        yield
        x = each(lambda n, xi: xi + _dot(n, xi, NN, passes["solve"]), npow, x)
        yield
    qe = each(lambda c, eg: c["q2"] * eg, chains, egc)
    rows = (slice(0, L), slice(L, L2))
    vnew = each(lambda xi, st: [xi[r, 0:B_HD] - _dot(xi[r, B_HD:2 * B_HD], s, NN, passes["state"])
                                for r, s in zip(rows, st)], x, states)
    o1 = each(lambda qi, st: [_dot(qi[r], s, NN, passes["state"]) for r, s in zip(rows, st)], qe, states)
    yield
    o2 = each(lambda a, vn, o: jnp.concatenate(o, axis=0) + _dot(a, jnp.concatenate(vn, axis=0), NN,
                                                                 passes["apply"]), a_int, vnew, o1)
    res["o"] = [[o[r] for r in rows] for o in o2]
    k_rest = each(lambda c, gt, r: c["k2"] * jnp.exp(gt - r), chains, g_tot, gc_row)
    res["s"] = each(lambda st, gt, kr, vn: [s * jnp.exp(gt[r][0:1, :]) + _dot(kr[r], v, TN, passes["state"])
                                            for r, s, v in zip(rows, st, vn)], states, g_tot, k_rest, vnew)


def _mix_out_kernel(yaf_ref, yab_ref, bg_ref, lnw_ref, lnb_ref, bd_ref, obf_ref, obb_ref, z_ref, nw_ref,
                    w_ref, *rest, k, kffn, n_lat):
    it = iter(rest)
    x_in = _rows(it, n_lat)
    mod_ref, gain_ref, rt_ref, o_ref, h_ref, lg_ref = it
    y = yaf_ref[0] + yab_ref[0]
    ones_bd = bd_ref[...]
    mean = _group_sum(y, ones_bd) * (1.0 / A_HD)
    yc = y - mean
    var = _group_sum(yc * yc, ones_bd) * (1.0 / A_HD)
    yn = yc * lax.rsqrt(var + A_LN_EPS) * lnw_ref[...] + lnb_ref[...]
    bg = bg_ref[0].astype(F32)
    out_a = (yn + bg[:, 0:A_W]) * bg[:, A_W:2 * A_W]
    o = obf_ref[0] + obb_ref[0]
    z = z_ref[0]
    outs = [out_a.astype(BF16)]
    for h in range(B_H):
        oh = o[:, h * B_HD:(h + 1) * B_HD]
        on = oh * lax.rsqrt(jnp.mean(oh * oh, axis=-1, keepdims=True) + NORM_EPS) * nw_ref[...]
        zh = z[:, h * B_HD:(h + 1) * B_HD]
        outs.append((on * (zh * _sigmoid(zh))).astype(BF16))
    mixed = jnp.concatenate(outs, axis=1)
    x = x_in + mod_ref[0, k:k + 1, :] * jnp.dot(mixed, w_ref[...], preferred_element_type=F32)
    o_ref[0] = x
    _ffn_input(x, gain_ref, mod_ref, rt_ref, kffn, h_ref, lg_ref)


def _mix_out(ya, bg, ln_w, ln_b, ones_bd, ob, zab, norm_w, w_out, x, mod, k, gain_ffn, router, kffn, tt):
    tile = lambda width: pl.BlockSpec((1, tt, width), lambda gi, i: (gi, i, 0))
    full = lambda arr: pl.BlockSpec(arr.shape, lambda gi, i: (0,) * arr.ndim)
    x_ins, x_specs, g, t = _rows_specs(x, tt)
    ins, specs, ospecs, oshapes = _ffn_input_specs(g, t, tt, gain_ffn, router)
    return pl.pallas_call(
        functools.partial(_mix_out_kernel, k=k, kffn=kffn, n_lat=_n_lat(x)),
        grid=(g, t // tt),
        in_specs=[tile(A_W), tile(A_W), tile(2 * A_W), full(ln_w), full(ln_b), full(ones_bd), tile(B_W),
                  tile(B_W), tile(B_W), full(norm_w), full(w_out)] + x_specs
        + [pl.BlockSpec((1, MOD_ROWS, D), lambda gi, i: (gi, 0, 0))] + specs,
        out_specs=[tile(D)] + ospecs,
        out_shape=[jax.ShapeDtypeStruct((g, t, D), F32)] + oshapes,
        compiler_params=_cparams(("parallel", "parallel")), name="mix_out_proj",
    )(ya[0], ya[1], bg, ln_w, ln_b, ones_bd, ob[0], ob[1], zab, norm_w, w_out, *x_ins, mod, *ins)


def _qkv_rope_kernel(*refs, lat_groups, q_cols, rope_cols, seg, has_res, k, kres):
    it = iter(refs)
    x_new, x = _mixer_input(it, has_res, k, kres)
    w_ref, cos_ref, sin_ref = next(it), next(it), next(it)
    if has_res:
        next(it)[0] = x_new
    o_ref = next(it)
    is_lat = pl.program_id(0) < lat_groups
    lane = lax.broadcasted_iota(jnp.int32, (1, LANE), 1)
    first = (lane % (C_HD // 2)) < (C_HD // 4)
    quarter = C_HD // 4
    cos_l = jnp.where(is_lat, cos_ref[...], 1.0)
    sin_l = jnp.where(is_lat, sin_ref[...], 0.0)
    starts = list(range(0, w_ref.shape[1], seg))
    product = lambda s0: jnp.dot(x, w_ref[:, s0:s0 + seg], preferred_element_type=F32)
    acc_next = product(starts[0])
    for idx, start in enumerate(starts):
        acc = acc_next
        if idx + 1 < len(starts):
            acc_next = product(starts[idx + 1])
        if start >= rope_cols:
            o_ref[0, :, start:start + seg] = acc.astype(o_ref.dtype)
            continue
        scale = C_HD ** -0.5 if start < q_cols else 1.0
        cos, sin = cos_l * scale, sin_l * scale
        for blk in range(seg // LANE):
            xb = acc[:, blk * LANE:(blk + 1) * LANE]
            partner = jnp.where(first, pltpu.roll(xb, LANE - quarter, axis=1), pltpu.roll(xb, quarter, axis=1))
            col = start + blk * LANE
            o_ref[0, :, col:col + LANE] = (xb * cos + partner * sin).astype(o_ref.dtype)


def _qkv_rope(x, gain, mod, k, res, w, cos, sin, lat_groups, tm, seg, q_cols, rope_cols):
    kdim, n = w.shape
    assert q_cols % seg == 0 and rope_cols % seg == 0 and n % seg == 0
    ins, specs, tile, g, t = _mixer_input_specs(x, gain, mod, res, tm)
    has_res = res is not None
    outs = list(pl.pallas_call(
        functools.partial(_qkv_rope_kernel, lat_groups=lat_groups, q_cols=q_cols, rope_cols=rope_cols, seg=seg,
                          has_res=has_res, k=k, kres=res[2] if has_res else 0),
        grid=(g, t // tm),
        in_specs=specs + [pl.BlockSpec((kdim, n), lambda gi, i: (0, 0)),
                          pl.BlockSpec((tm, LANE), lambda gi, i: (i, 0)),
                          pl.BlockSpec((tm, LANE), lambda gi, i: (i, 0))],
        out_specs=([tile] if has_res else []) + [pl.BlockSpec((1, tm, n), lambda gi, i: (gi, i, 0))],
        out_shape=([jax.ShapeDtypeStruct((g, t, D), F32)] if has_res else [])
        + [jax.ShapeDtypeStruct((g, t, n), BF16)],
        compiler_params=_cparams(("parallel", "parallel")), name="qkv_rope",
    )(*ins, w, cos, sin))
    return (outs.pop(0) if has_res else None), outs[0]


def _attn_kernel(sink_ref, q_ref, kp_ref, km_ref, kn_ref, kc_ref, o_ref, *, n_blocks, kv_w):
    n = pl.program_id(1)
    blk = C_BLOCK
    ii = lax.broadcasted_iota(jnp.int32, (blk, blk), 0)
    jj = lax.broadcasted_iota(jnp.int32, (blk, blk), 1)
    ok_prev = jnp.logical_and(jj >= ii, n > 0)
    ok_next = jnp.logical_and(jj <= ii, n < n_blocks - 1)
    n_ctx = kc_ref.shape[1]
    valid1 = jnp.concatenate([ok_prev, jnp.full((blk, blk), True), ok_next, jnp.full((blk, n_ctx), True)], axis=1)
    valid = jnp.concatenate([valid1] * C_GROUP, axis=0)
    low = lax.broadcasted_iota(jnp.int32, (1, LANE), 1) < C_HD
    row_head = lax.broadcasted_iota(jnp.int32, (C_GROUP * blk, 1), 0) // blk
    q = q_ref[0]
    kv = jnp.concatenate([kp_ref[0], km_ref[0], kn_ref[0], kc_ref[0]], axis=0)
    zero = jnp.zeros((), q.dtype)

    def scores(h):
        kh = kv[:, h * LANE:(h + 1) * LANE]
        qa = q[:, (2 * h) * LANE:(2 * h + 1) * LANE]
        qb = q[:, (2 * h + 1) * LANE:(2 * h + 2) * LANE]
        qs = jnp.concatenate([jnp.where(low, qa, zero), jnp.where(low, zero, qa),
                              jnp.where(low, qb, zero), jnp.where(low, zero, qb)], axis=0)
        return lax.dot_general(qs, kh, NT, preferred_element_type=F32)

    def softmax(h, s):
        s = jnp.where(valid, s, -1e30)
        sk = jnp.full((C_GROUP * blk, 1), 0.0, F32)
        for gq in range(C_GROUP):
            sk = jnp.where(row_head == gq, sink_ref[h * C_GROUP + gq], sk)
        m = jnp.maximum(jnp.max(s, axis=-1, keepdims=True), sk)
        p = jnp.exp(s - m)
        return p.astype(kv.dtype), jnp.sum(p, axis=-1, keepdims=True) + jnp.exp(sk - m)

    def values(h, p, denom):
        vh = kv[:, kv_w + h * LANE:kv_w + (h + 1) * LANE]
        o = lax.dot_general(p, vh, NN, preferred_element_type=F32) / denom
        return [jnp.where(low, o[0:blk], o[blk:2 * blk]).astype(o_ref.dtype),
                jnp.where(low, o[2 * blk:3 * blk], o[3 * blk:4 * blk]).astype(o_ref.dtype)]

    outs = []
    s_next = scores(0)
    for h in range(C_KVH):
        s_cur = s_next
        if h + 1 < C_KVH:
            s_next = scores(h + 1)
        outs += values(h, *softmax(h, s_cur))
    o_ref[0] = jnp.concatenate(outs, axis=1)


def _attention(qkv, sink, bsz, ctx_len):
    g, t, _ = qkv.shape
    q_w = C_QH * C_HD
    kv_w = C_KVH * LANE
    nq = q_w // (2 * kv_w)
    assert q_w % (2 * kv_w) == 0
    n_blocks = t // C_BLOCK
    cpb = ctx_len // C_BLOCK
    assert ctx_len % C_BLOCK == 0
    kvspec = lambda fn: pl.BlockSpec((1, C_BLOCK, 2 * kv_w), fn)
    return pl.pallas_call(
        functools.partial(_attn_kernel, n_blocks=n_blocks, kv_w=kv_w),
        grid=(bsz, n_blocks),
        in_specs=[pl.BlockSpec(memory_space=pltpu.SMEM),
                  pl.BlockSpec((1, C_BLOCK, q_w), lambda b, n: (b, n, 0)),
                  kvspec(lambda b, n: (b, jnp.maximum(n - 1, 0), nq)),
                  kvspec(lambda b, n: (b, n, nq)),
                  kvspec(lambda b, n: (b, jnp.minimum(n + 1, n_blocks - 1), nq)),
                  pl.BlockSpec((1, ctx_len, 2 * kv_w), lambda b, n: (bsz, b, nq))],
        out_specs=pl.BlockSpec((1, C_BLOCK, q_w), lambda b, n: (b, n, 0)),
        out_shape=jax.ShapeDtypeStruct((bsz, t, q_w), BF16),
        compiler_params=_cparams(("parallel", "parallel")), name="window_attn",
    )(sink, qkv, qkv, qkv, qkv, qkv)


def _lane_cumsum(x):
    n = x.shape[-1]
    lane = lax.broadcasted_iota(jnp.int32, x.shape, x.ndim - 1)
    k = 1
    while k < n:
        x = x + jnp.where(lane >= k, pltpu.roll(x, k, axis=x.ndim - 1), 0)
        k *= 2
    return x


def _route_kernel(lg_ref, rank_ref, aff_ref, *, cap, slot_stride):
    lg = lg_ref[0]
    m = jnp.max(lg, axis=0, keepdims=True)
    e = jnp.exp(lg - m)
    z = jnp.sum(e, axis=0, keepdims=True)
    aff = e / z
    key = (lg - m) - jnp.log(z)
    count_ge = lambda v: jnp.sum(jnp.where(key >= v, 1, 0), axis=-1, keepdims=True)

    def body(_, carry):
        lo, hi = carry
        span = hi - lo
        q1, q2, q3 = lo + 0.25 * span, lo + 0.5 * span, lo + 0.75 * span
        ok1, ok2, ok3 = count_ge(q1) >= cap, count_ge(q2) >= cap, count_ge(q3) >= cap
        new_lo = jnp.where(ok3, q3, jnp.where(ok2, q2, jnp.where(ok1, q1, lo)))
        new_hi = jnp.where(ok3, hi, jnp.where(ok2, q3, jnp.where(ok1, q2, q1)))
        return new_lo, new_hi

    lo0 = jnp.min(key, axis=-1, keepdims=True)
    lo, hi = lax.fori_loop(0, ROUTE_BISECT // 2, body, (lo0, jnp.ones_like(lo0)))
    thr, found = lo, jnp.zeros(lo.shape, jnp.int32)
    for _ in range(ROUTE_SNAP):
        v = jnp.max(jnp.where(key < hi, key, -3e38), axis=-1, keepdims=True)
        ok = jnp.where(count_ge(v) >= cap, 1, 0)
        thr = jnp.where(jnp.logical_and(found == 0, ok == 1), v, thr)
        hi = jnp.where(jnp.logical_or(found == 1, ok == 1), hi, v)
        found = jnp.maximum(found, ok)
    gt = key > thr
    eq = key == thr
    need = cap - jnp.sum(jnp.where(gt, 1, 0), axis=-1, keepdims=True)
    take_eq = jnp.logical_and(eq, _lane_cumsum(jnp.where(eq, 1, 0)) <= need)
    sel = jnp.logical_or(gt, take_eq)
    slot = _lane_cumsum(jnp.where(sel, 1, 0)) - 1 + pl.program_id(0) * slot_stride
    rank_ref[0] = jnp.where(sel, slot, -1)
    aff_ref[0] = aff


def _route(lg, cap, slot_stride):
    g, e, t = lg.shape
    spec = pl.BlockSpec((1, e, t), lambda gi: (gi, 0, 0))
    return pl.pallas_call(
        functools.partial(_route_kernel, cap=cap, slot_stride=slot_stride),
        grid=(g,), in_specs=[spec], out_specs=[spec, spec],
        out_shape=[jax.ShapeDtypeStruct((g, e, t), jnp.int32), jax.ShapeDtypeStruct((g, e, t), F32)],
        compiler_params=_cparams(("parallel",)), name="route",
    )(lg)


def _expert_kernel(h_ref, rank_ref, aff_ref, w1_ref, w3_ref, w2_ref, o_ref, *, cap):
    e = pl.program_id(1)

    @pl.when(e == 0)
    def _():
        o_ref[...] = jnp.zeros_like(o_ref)

    rank = rank_ref[0, 0]
    t = rank.shape[-1]
    hit = lax.broadcasted_iota(jnp.int32, (cap, t), 0) == rank
    pick = jnp.where(hit, 1.0, 0.0).astype(BF16)
    gate = jnp.sum(jnp.where(hit, aff_ref[0, 0], 0.0), axis=-1, keepdims=True)
    xe = jnp.dot(pick, h_ref[0], preferred_element_type=F32).astype(BF16)
    h1 = jnp.dot(xe, w1_ref[0, 0].astype(BF16), preferred_element_type=F32)
    h3 = jnp.dot(xe, w3_ref[0, 0].astype(BF16), preferred_element_type=F32)
    hid = (h1 * _sigmoid(h1) * h3).astype(BF16)
    ye = (jnp.dot(hid, w2_ref[0], preferred_element_type=F32) * gate).astype(BF16)
    o_ref[0] += lax.dot_general(pick, ye, TN, preferred_element_type=F32)


def _experts(h, rank, aff, w1, w3, w2, layer, cap):
    g, t, _ = h.shape
    _, n_exp, _, f = w1.shape
    sel = pl.BlockSpec((1, 1, 1, t), lambda gi, e: (gi, e, 0, 0))
    return pl.pallas_call(
        functools.partial(_expert_kernel, cap=cap),
        grid=(g, n_exp),
        in_specs=[pl.BlockSpec((1, t, D), lambda gi, e: (gi, 0, 0)), sel, sel,
                  pl.BlockSpec((1, 1, D, f), lambda gi, e: (layer, e, 0, 0)),
                  pl.BlockSpec((1, 1, D, f), lambda gi, e: (layer, e, 0, 0)),
                  pl.BlockSpec((1, f, D), lambda gi, e: (e, 0, 0))],
        out_specs=pl.BlockSpec((1, t, D), lambda gi, e: (gi, 0, 0)),
        out_shape=jax.ShapeDtypeStruct((g, t, D), F32),
        compiler_params=_cparams(("parallel", "arbitrary")), name="experts",
    )(h, rank.reshape(g, n_exp, 1, t), aff.reshape(g, n_exp, 1, t), w1, w3, w2)


def _moe(h, lg, w1, w3, w2, layer, bsz, ctx_len, lat_groups_only):
    g, t, _ = h.shape
    cap_lat = EC_CAP * t // N_EXP
    rank, aff = _route(lg[:bsz], cap_lat, 0)
    if not lat_groups_only:
        cap_ctx = EC_CAP * ctx_len // N_EXP
        lg_c = lg[bsz].reshape(N_EXP, bsz, ctx_len).transpose(1, 0, 2)
        rank_c, aff_c = _route(lg_c, cap_ctx, cap_ctx)
        back = lambda a: a.transpose(1, 0, 2).reshape(1, N_EXP, t)
        rank = jnp.concatenate([rank, back(rank_c)], axis=0)
        aff = jnp.concatenate([aff, back(aff_c)], axis=0)
        assert cap_ctx * bsz == cap_lat
    else:
        h = h[:bsz] if h.shape[0] != bsz else h
    return _experts(h, rank, aff, w1, w3, w2, layer, cap_lat)


def _block_diag2(w):
    z = jnp.zeros_like(w[0])
    return jnp.concatenate([jnp.concatenate([w[0], z], axis=1), jnp.concatenate([z, w[1]], axis=1)], axis=0)


def _rope_tables(t):
    quarter = C_HD // 4
    inv = ROPE_BASE ** (-jnp.arange(quarter, dtype=F32) / quarter)
    pos = jnp.arange(t)
    row = (pos // GRID_W).astype(F32)
    col = (pos % GRID_W).astype(F32)
    lane = np.arange(LANE)
    use_col = ((lane % C_HD) >= C_HD // 2)
    ang = jnp.where(use_col[None, :], col[:, None], row[:, None]) * inv[lane % quarter][None, :]
    sign = np.where((lane % (C_HD // 2)) < quarter, -1.0, 1.0).astype(np.float32)
    return jnp.cos(ang), jnp.sin(ang) * sign[None, :]


def kernel(x, c, ctx, c_ctx, ada_w, ada_b, norm_mix, norm_ffn, e_w_in, e_w_out, a_mu, a_w0, a_w2, a_a0, a_a2, a_g2,
           a_k_k, a_k_a, a_r_k, a_ln_w, a_ln_b, b_conv, b_a_log, b_dt_bias, b_norm, o_w_in, o_w_out, o_sink,
           moe_router, moe_w1, moe_w3, moe_w2, final_norm):
    bsz, t, _ = x.shape
    ctx_len = ctx.shape[1]
    depth = ada_w.shape[0]
    assert bsz * ctx_len == t and ctx_len % C_BLOCK == 0 and t % C_BLOCK == 0
    tt = ctx_len

    rows = 2 * SUB * ((bsz + 1 + 2 * SUB - 1) // (2 * SUB))
    cc = jnp.zeros((rows, D), F32).at[:bsz].set(c).at[bsz].set(c_ctx)
    mods = _adaln(cc, ada_w, ada_b)[:, :bsz + 1].reshape(depth, bsz + 1, N_MOD, D)
    mods = jnp.pad(mods, ((0, 0), (0, 0), (0, MOD_ROWS - N_MOD), (0, 0)))

    xs = (x, ctx.reshape(1, t, D))
    ones_bd = jnp.asarray(np.kron(np.eye(A_H), np.ones((A_HD, A_HD))), BF16)
    a_cols = e_w_in.shape[2] - (4 * B_W + 4 * B_H)
    cos_t, sin_t = _rope_tables(t)

    res = None
    for i in range(depth):
        j = i // 2
        mod = mods[i]
        last = i == depth - 1
        if i % 2 == 0:
            w_in = e_w_in[j]
            split_ab = a_cols + 3 * B_W
            w_all = jnp.concatenate([w_in[:, :split_ab], w_in[:, split_ab + 4 * B_H:],
                                     w_in[:, split_ab:split_ab + 4 * B_H],
                                     jnp.zeros((D, LANE - 4 * B_H), F32)], axis=1).astype(BF16)
            x_new, (pa, pb, zab) = _mm_split(xs, norm_mix[i], mod, 0, res, w_all, (a_cols, 3 * B_W, B_W + LANE),
                                             F32, 2 * tt)
            xs = xs if res is None else x_new
            prm = {
                "mu": a_mu[j].reshape(1, -1), "w0": a_w0[j].reshape(1, -1), "w2": _block_diag2(a_w2[j]),
                "a0": a_a0[j].reshape(1, -1), "a2": _block_diag2(a_a2[j]), "g2": a_g2[j],
                "k_k": a_k_k[j].reshape(1, -1), "k_a": jnp.tile(a_k_a[j].reshape(1, -1), (1, 2)),
                "r_k": a_r_k[j].reshape(1, -1), "ones_bd": ones_bd,
            }
            rvk, cum, kd, bd, bg = _rwkv_prep(pa, prm, tt, bsz)
            pad_row = lambda v: jnp.pad(v.reshape(1, -1), ((0, 0), (0, LANE - v.size)))
            qkv_c, gb = _gdn_prep(pb, zab, b_conv[j], pad_row(b_a_log[j]), pad_row(b_dt_bias[j]), tt, bsz)
            ya_f, ob_f, ya_b, ob_b = _even_scan(rvk, cum, kd, bd, qkv_c, gb, bsz, ctx_len)
            xs, hf, lg = _mix_out((ya_f, ya_b), bg, a_ln_w[j].reshape(1, -1), a_ln_b[j].reshape(1, -1), ones_bd,
                                  (ob_f, ob_b), zab, b_norm[j].reshape(1, -1), e_w_out[j].astype(BF16), xs, mod, 2,
                                  norm_ffn[i], moe_router[i], 3, tt)
        else:
            assert last
            w_in = o_w_in[j]
            q_w = C_QH * C_HD
            kv_cols = C_KVH * C_HD
            dup = lambda w: jnp.repeat(w.reshape(D, C_KVH, 1, C_HD), 2, axis=2).reshape(D, 2 * kv_cols)
            w_all = jnp.concatenate([w_in[:, :q_w], dup(w_in[:, q_w:q_w + kv_cols]), dup(w_in[:, q_w + kv_cols:])],
                                    axis=1).astype(BF16)
            tn = 2 * kv_cols
            x_new, qkv = _qkv_rope(xs, norm_mix[i], mod, 0, res, w_all, cos_t, sin_t, bsz, 2 * tt, tn, q_w, q_w + tn)
            xs = xs if res is None else x_new
            att = _attention(qkv, o_sink[j], bsz, ctx_len)
            xs, hf, lg = _mm_res(att, o_w_out[j].astype(BF16), xs, mod, 2, norm_ffn[i], moe_router[i], 3, tt,
                                 groups=bsz)
        if last:
            xs, hf, lg, mod = xs[:bsz], hf[:bsz], lg[:bsz], mod[:bsz]
        lg = lg[:, :, :N_EXP].transpose(0, 2, 1)
        delta = _moe(hf, lg, moe_w1, moe_w3, _to_bf16(moe_w2, i), i, bsz, ctx_len, last)
        res = (delta, mod, 5)
    return _final_norm(xs, res, final_norm, tt)
```

```python
import functools
import math

import jax
import jax.numpy as jnp
import numpy as np
from jax import lax
from jax.experimental import pallas as pl
from jax.experimental.pallas import tpu as pltpu

F32 = jnp.float32
BF16 = jnp.bfloat16
HIGHEST = lax.Precision.HIGHEST

D = 1024
N_MOD = 6
NORM_EPS = 1e-6
A_HD, A_H, A_W = 64, 8, 512
A_LN_EPS = 64e-5
B_HD, B_H, B_W = 128, 4, 512
B_CONV = 5
CHUNK = 64
C_HD, C_QH, C_KVH, C_GROUP = 64, 16, 4, 4
C_BLOCK = 128
ROPE_BASE = 10000.0
GRID_W = 64
N_EXP = 16
EC_CAP = 2
ROUTE_BISECT = 40
ROUTE_SNAP = 3
RWKV_PASSES = {"quad": 1, "apply": 1, "solve": "b", "square": 1, "state_in": "a", "state_out": "b"}
GDN_PASSES = {"quad": 1, "apply": 1, "solve": "b", "square": 1, "state": 1}
MOD_ROWS = 8
LANE = 128
SUB = 8
VMEM_LIMIT = 56 * 1024 * 1024

NT = (((1,), (1,)), ((), ()))
NN = (((1,), (0,)), ((), ()))
TN = (((0,), (0,)), ((), ()))


def _cparams(sem):
    return pltpu.CompilerParams(dimension_semantics=sem, vmem_limit_bytes=VMEM_LIMIT)


def _sigmoid(x):
    return 1.0 / (1.0 + jnp.exp(-x))


def _softplus(x):
    return jnp.maximum(x, 0.0) + jnp.log(1.0 + jnp.exp(-jnp.abs(x)))


def _split2(x):
    hi = x.astype(BF16)
    lo = (x - hi.astype(F32)).astype(BF16)
    return hi, lo


def _dot(a, b, dims=NN, passes=1):
    dg = functools.partial(lax.dot_general, dimension_numbers=dims, preferred_element_type=F32)
    if passes == 1:
        return dg(a.astype(BF16), b.astype(BF16))
    (ka,), (kb,) = dims[0]
    if passes == "a":
        ah, al = _split2(a)
        bh = b.astype(BF16)
        return dg(jnp.concatenate([ah, al], axis=ka), jnp.concatenate([bh, bh], axis=kb))
    if passes == "b":
        bh, bl = _split2(b)
        n = b.shape[1 - kb]
        res = dg(a.astype(BF16), jnp.concatenate([bh, bl], axis=1 - kb))
        return res[:, 0:n] + res[:, n:2 * n]
    ah, al = _split2(a)
    bh, bl = _split2(b)
    return dg(jnp.concatenate([ah, al, ah], axis=ka), jnp.concatenate([bh, bh, bl], axis=kb))


def _group_sum(x, ones_bd):
    hi, lo = _split2(x)
    return lax.dot_general(jnp.concatenate([hi, lo], axis=1), jnp.concatenate([ones_bd, ones_bd], axis=0), NN,
                           preferred_element_type=F32)


def _cast_kernel(w_ref, o_ref):
    o_ref[...] = w_ref[...].astype(o_ref.dtype)


def _to_bf16(w, layer):
    _, n, r, c = w.shape
    return pl.pallas_call(_cast_kernel, grid=(n,),
                          in_specs=[pl.BlockSpec((1, 1, r, c), lambda e: (layer, e, 0, 0))],
                          out_specs=pl.BlockSpec((1, 1, r, c), lambda e: (0, e, 0, 0)),
                          out_shape=jax.ShapeDtypeStruct((1, n, r, c), BF16),
                          compiler_params=_cparams(("parallel",)), name="cast_bf16")(w)[0]


def _adaln_kernel(c_ref, w_ref, b_ref, o_ref):
    c = c_ref[...]
    s = c * _sigmoid(c)
    o_ref[0] = jnp.dot(s, w_ref[0], precision=HIGHEST, preferred_element_type=F32) + b_ref[0]


def _adaln(cc, ada_w, ada_b):
    depth, _, n = ada_w.shape
    rows = cc.shape[0]
    tn = 768
    return pl.pallas_call(
        _adaln_kernel,
        grid=(depth, n // tn),
        in_specs=[pl.BlockSpec((rows, D), lambda i, j: (0, 0)),
                  pl.BlockSpec((1, D, tn), lambda i, j: (i, 0, j)),
                  pl.BlockSpec((1, 1, tn), lambda i, j: (i, 0, j))],
        out_specs=pl.BlockSpec((1, rows, tn), lambda i, j: (i, 0, j)),
        out_shape=jax.ShapeDtypeStruct((depth, rows, n), F32),
        compiler_params=_cparams(("parallel", "parallel")),
        name="adaln",
    )(cc, ada_w, ada_b.reshape(depth, 1, n))


def _final_norm_kernel(x_ref, d_ref, mres_ref, gain_ref, o_ref, *, kres):
    x = x_ref[0] + mres_ref[0, kres:kres + 1, :] * d_ref[0]
    o_ref[0] = x * lax.rsqrt(jnp.mean(x * x, axis=-1, keepdims=True) + NORM_EPS) * gain_ref[...]


def _final_norm(x, res, gain, tt):
    g, t, _ = x.shape
    tile = pl.BlockSpec((1, tt, D), lambda gi, i: (gi, i, 0))
    return pl.pallas_call(
        functools.partial(_final_norm_kernel, kres=res[2]),
        grid=(g, t // tt),
        in_specs=[tile, tile, pl.BlockSpec((1, MOD_ROWS, D), lambda gi, i: (gi, 0, 0)),
                  pl.BlockSpec((1, D), lambda gi, i: (0, 0))],
        out_specs=tile, out_shape=jax.ShapeDtypeStruct((g, t, D), F32),
        compiler_params=_cparams(("parallel", "parallel")), name="final_norm",
    )(x, res[0], res[1], gain.reshape(1, D))


def _rows_specs(x, tm):
    if not isinstance(x, tuple):
        return [x], [pl.BlockSpec((1, tm, D), lambda gi, i: (gi, i, 0))], x.shape[0], x.shape[1]
    lat, ctx = x
    n_lat, t = lat.shape[0], lat.shape[1]
    last = t // tm - 1
    specs = [pl.BlockSpec((1, tm, D), lambda gi, i: (jnp.minimum(gi, n_lat - 1), jnp.where(gi < n_lat, i, last), 0)),
             pl.BlockSpec((1, tm, D), lambda gi, i: (0, jnp.where(gi < n_lat, 0, i), 0))]
    return [lat, ctx], specs, n_lat + 1, t


def _rows(refs, n_lat):
    if n_lat is None:
        return next(refs)[0]
    lat_ref, ctx_ref = next(refs), next(refs)
    return jnp.where(pl.program_id(0) < n_lat, lat_ref[0], ctx_ref[0])


def _n_lat(x):
    return x[0].shape[0] if isinstance(x, tuple) else None


def _mixer_input(refs, has_res, k, kres, n_lat=None):
    x = _rows(refs, n_lat)
    if has_res:
        d_ref, mres_ref = next(refs), next(refs)
        x = x + mres_ref[0, kres:kres + 1, :] * d_ref[0]
    gain_ref, mod_ref = next(refs), next(refs)
    xn = x * lax.rsqrt(jnp.mean(x * x, axis=-1, keepdims=True) + NORM_EPS) * gain_ref[...]
    h = (xn * (1.0 + mod_ref[0, k + 1:k + 2, :]) + mod_ref[0, k:k + 1, :]).astype(BF16)
    return (x if has_res else None), h


def _mixer_input_specs(x, gain, mod, res, tm):
    tile = pl.BlockSpec((1, tm, D), lambda gi, i: (gi, i, 0))
    modspec = pl.BlockSpec((1, MOD_ROWS, D), lambda gi, i: (gi, 0, 0))
    ins, specs, g, t = _rows_specs(x, tm)
    if res is not None:
        ins += [res[0], res[1]]
        specs += [tile, modspec]
    ins += [gain.reshape(1, D), mod]
    specs += [pl.BlockSpec((1, D), lambda gi, i: (0, 0)), modspec]
    return ins, specs, tile, g, t


def _mm_split_kernel(*refs, widths, has_res, k, kres, n_lat):
    it = iter(refs)
    x_new, h = _mixer_input(it, has_res, k, kres, n_lat)
    w_ref = next(it)
    if has_res:
        next(it)[0] = x_new
    start = 0
    for o_ref, width in zip(it, widths):
        o_ref[0] = jnp.dot(h, w_ref[:, start:start + width], preferred_element_type=F32).astype(o_ref.dtype)
        start += width


def _mm_split(x, gain, mod, k, res, w, widths, out_dtype, tm):
    assert sum(widths) == w.shape[1] and all(wd % LANE == 0 for wd in widths)
    ins, specs, tile, g, t = _mixer_input_specs(x, gain, mod, res, tm)
    has_res = res is not None
    outs = list(pl.pallas_call(
        functools.partial(_mm_split_kernel, widths=widths, has_res=has_res, k=k, kres=res[2] if has_res else 0,
                          n_lat=_n_lat(x)),
        grid=(g, t // tm),
        in_specs=specs + [pl.BlockSpec(w.shape, lambda gi, i: (0, 0))],
        out_specs=([tile] if has_res else []) + [pl.BlockSpec((1, tm, wd), lambda gi, i: (gi, i, 0)) for wd in widths],
        out_shape=([jax.ShapeDtypeStruct((g, t, D), F32)] if has_res else [])
        + [jax.ShapeDtypeStruct((g, t, wd), out_dtype) for wd in widths],
        compiler_params=_cparams(("parallel", "parallel")), name="proj",
    )(*ins, w))
    return (outs.pop(0) if has_res else None), outs


def _staggered(stages):
    live = []
    for gen in stages:
        live = [g for g in live if next(g, "done") != "done"]
        live.append(gen)
    while live:
        live = [g for g in live if next(g, "done") != "done"]


def _ffn_input(x, gain_ref, mod_ref, rt_ref, k, h_ref, lg_ref, rows):
    xn = x * lax.rsqrt(jnp.mean(x * x, axis=-1, keepdims=True) + NORM_EPS) * gain_ref[...]
    xn = xn * (1.0 + mod_ref[0, k + 1:k + 2, :]) + mod_ref[0, k:k + 1, :]
    h_ref[0, rows, :] = xn.astype(h_ref.dtype)
    yield
    lg_ref[0, rows, :] = _dot(xn, rt_ref[...], NN, 3)


def _ffn_input_specs(g, t, tm, gain, router):
    ins = [gain.reshape(1, D), jnp.pad(router, ((0, 0), (0, LANE - router.shape[1])))]
    specs = [pl.BlockSpec((1, D), lambda gi, i: (0, 0)), pl.BlockSpec((D, LANE), lambda gi, i: (0, 0))]
    ospecs = [pl.BlockSpec((1, tm, D), lambda gi, i: (gi, i, 0)), pl.BlockSpec((1, tm, LANE), lambda gi, i: (gi, i, 0))]
    oshapes = [jax.ShapeDtypeStruct((g, t, D), BF16), jax.ShapeDtypeStruct((g, t, LANE), F32)]
    return ins, specs, ospecs, oshapes


def _mm_res_kernel(y_ref, w_ref, x_ref, mod_ref, gain_ref, rt_ref, o_ref, h_ref, lg_ref, *, k, kffn):
    def half(rows):
        acc = jnp.dot(y_ref[0, rows, :], w_ref[...], preferred_element_type=F32)
        yield
        x = x_ref[0, rows, :] + mod_ref[0, k:k + 1, :] * acc
        o_ref[0, rows, :] = x
        yield from _ffn_input(x, gain_ref, mod_ref, rt_ref, kffn, h_ref, lg_ref, rows)

    n = y_ref.shape[1]
    _staggered([half(slice(0, n // 2)), half(slice(n // 2, n))])


def _mm_res(y, w, x, mod, k, gain_ffn, router, kffn, tm, groups):
    _, t, kdim = y.shape
    g = groups
    ins, specs, ospecs, oshapes = _ffn_input_specs(g, t, tm, gain_ffn, router)
    return pl.pallas_call(
        functools.partial(_mm_res_kernel, k=k, kffn=kffn),
        grid=(g, t // tm),
        in_specs=[pl.BlockSpec((1, tm, kdim), lambda gi, i: (gi, i, 0)),
                  pl.BlockSpec((kdim, D), lambda gi, i: (0, 0)),
                  pl.BlockSpec((1, tm, D), lambda gi, i: (gi, i, 0)),
                  pl.BlockSpec((1, MOD_ROWS, D), lambda gi, i: (gi, 0, 0))] + specs,
        out_specs=[pl.BlockSpec((1, tm, D), lambda gi, i: (gi, i, 0))] + ospecs,
        out_shape=[jax.ShapeDtypeStruct((g, t, D), F32)] + oshapes,
        compiler_params=_cparams(("parallel", "parallel")), name="out_proj_res",
    )(y, w, x, mod, *ins)


def _fill_halo(buf, x, pv_ref, nx_ref, has_prev, has_next, tt):
    buf[0:SUB, :] = jnp.where(has_prev, pv_ref[0], 0.0)
    buf[SUB:SUB + tt, :] = x
    buf[SUB + tt:2 * SUB + tt, :] = jnp.where(has_next, nx_ref[0], 0.0)


def _rwkv_prep_kernel(pa_ref, pv_ref, nx_ref, mu_ref, w0_ref, w2_ref, a0_ref, a2_ref, g2_ref, kk_ref, ka_ref,
                      rk_ref, bd_ref, rvk_ref, cum_ref, kd_ref, bdir_ref, bg_ref, buf, *, tt, lat_groups, ntile):
    gi, i = pl.program_id(0), pl.program_id(1)
    is_lat = gi < lat_groups
    x = pa_ref[0]
    _fill_halo(buf, x, pv_ref, nx_ref, jnp.logical_and(is_lat, i > 0), jnp.logical_and(is_lat, i < ntile - 1), tt)
    xm = buf[SUB - 1:SUB - 1 + tt, :]
    xp = buf[SUB + 1:SUB + 1 + tt, :]
    x = x + mu_ref[...] * (0.5 * (xm + xp) - x)
    w = A_W
    r, kx, v = x[:, 0:w], x[:, w:2 * w], x[:, 2 * w:3 * w]
    wd, ad, gd = x[:, 3 * w:3 * w + LANE], x[:, 3 * w + LANE:3 * w + 2 * LANE], x[:, 3 * w + 2 * LANE:3 * w + 3 * LANE]
    u = w0_ref[...] + _dot(jnp.tanh(wd), w2_ref[...], NN, 3)
    lw = -math.exp(-0.5) * _sigmoid(u)
    a = _sigmoid(a0_ref[...] + _dot(ad, a2_ref[...], NN, 3))
    gate = _dot(_sigmoid(gd), g2_ref[...], NN, 3)
    ones_bd = bd_ref[...]
    kq = kx * kk_ref[...]
    kk = kq * lax.rsqrt(_group_sum(kq * kq, ones_bd) + 1e-6)
    k2 = jnp.concatenate([kx, kx], axis=1)
    kdir = k2 * (1.0 + (a - 1.0) * ka_ref[...])
    bdir = jnp.concatenate([kk, kk], axis=1) * a
    rr = r * rk_ref[...]
    bonus = _group_sum(rr * (kdir[:, 0:w] + kdir[:, w:2 * w]), ones_bd) * v
    rvk_ref[0] = jnp.concatenate([r, v, kk], axis=1)
    cum_ref[0] = jnp.concatenate([_chunk_cumsum(lw[:, 0:w], True), _chunk_cumsum(lw[:, w:2 * w], False)], axis=1)
    kd_ref[0] = kdir.astype(kd_ref.dtype)
    bdir_ref[0] = bdir.astype(bdir_ref.dtype)
    bg_ref[0] = jnp.concatenate([bonus, gate], axis=1).astype(bg_ref.dtype)


def _chunk_cumsum(x, fwd):
    _, _, incl = _order_masks(x.shape[0], CHUNK, fwd)
    mask = jnp.where(incl, 1.0, 0.0).astype(BF16)
    p1 = x.astype(BF16)
    r1 = x - p1.astype(F32)
    p2 = r1.astype(BF16)
    p3 = (r1 - p2.astype(F32)).astype(BF16)
    return lax.dot_general(jnp.concatenate([mask, mask, mask], axis=1), jnp.concatenate([p1, p2, p3], axis=0), NN,
                           preferred_element_type=F32)


def _halo_specs(width, tt, t):
    nb = tt // SUB
    last = t // SUB - 1
    prev = pl.BlockSpec((1, SUB, width), lambda gi, i: (gi, jnp.maximum(i * nb - 1, 0), 0))
    nxt = pl.BlockSpec((1, SUB, width), lambda gi, i: (gi, jnp.minimum((i + 1) * nb, last), 0))
    return prev, nxt


def _rwkv_prep(pa, p, tt, lat_groups):
    g, t, wa = pa.shape
    full = lambda arr: pl.BlockSpec(arr.shape, lambda gi, i: (0,) * arr.ndim)
    tile = lambda width: pl.BlockSpec((1, tt, width), lambda gi, i: (gi, i, 0))
    prev, nxt = _halo_specs(wa, tt, t)
    consts = [p["mu"], p["w0"], p["w2"], p["a0"], p["a2"], p["g2"], p["k_k"], p["k_a"], p["r_k"], p["ones_bd"]]
    widths = (3 * A_W, 2 * A_W, 2 * A_W, 2 * A_W, 2 * A_W)
    return pl.pallas_call(
        functools.partial(_rwkv_prep_kernel, tt=tt, lat_groups=lat_groups, ntile=t // tt),
        grid=(g, t // tt),
        in_specs=[tile(wa), prev, nxt] + [full(c) for c in consts],
        out_specs=[tile(wd) for wd in widths],
        out_shape=[jax.ShapeDtypeStruct((g, t, wd), dt) for wd, dt in zip(widths, (F32, F32, BF16, BF16, BF16))],
        scratch_shapes=[pltpu.VMEM((tt + 2 * SUB, wa), F32)],
        compiler_params=_cparams(("parallel", "parallel")), name="rwkv_prep",
    )(pa, pa, pa, *consts)


def _order_masks(n_rows, blk, fwd):
    ii = lax.broadcasted_iota(jnp.int32, (n_rows, n_rows), 0)
    jj = lax.broadcasted_iota(jnp.int32, (n_rows, n_rows), 1)
    same = (ii // blk) == (jj // blk)
    before = (jj < ii) if fwd else (jj > ii)
    strict = jnp.logical_and(same, before)
    incl = jnp.logical_and(same, jnp.logical_or(before, ii == jj))
    return same, strict, incl


def _even_scan_kernel(rvkf_ref, cumf_ref, kdf_ref, bdf_ref, qkvf_ref, gbf_ref,
                      rvkb_ref, cumb_ref, kdb_ref, bdb_ref, qkvb_ref, gbb_ref,
                      yf_ref, of_ref, yb_ref, ob_ref, sa_ref, sb_ref):
    @pl.when(pl.program_id(1) == 0)
    def _():
        sa_ref[...] = jnp.zeros_like(sa_ref)
        sb_ref[...] = jnp.zeros_like(sb_ref)

    na, nb = A_W // LANE, B_H // 2
    sa, sb = sa_ref[...], sb_ref[...]
    chains_a = (_rwkv_operands(rvkf_ref[0], cumf_ref[0], kdf_ref[0], bdf_ref[0], True)
                + _rwkv_operands(rvkb_ref[0], cumb_ref[0], kdb_ref[0], bdb_ref[0], False))
    chains_b = _gdn_operands(qkvf_ref[0], gbf_ref[0], 0) + _gdn_operands(qkvb_ref[0], gbb_ref[0], 1)
    states_a = [sa[d, p] for d in range(2) for p in range(na)]
    states_b = [(sb[d, 2 * p], sb[d, 2 * p + 1]) for d in range(2) for p in range(nb)]
    res_a, res_b = {}, {}
    stages = [_rwkv_solve(chains_a, states_a, RWKV_PASSES, res_a), _gdn_solve(chains_b, states_b, GDN_PASSES, res_b)]
    while stages:
        stages = [s for s in stages if next(s, "done") != "done"]
    yf_ref[0] = jnp.concatenate(res_a["y"][0:na], axis=1)
    yb_ref[0] = jnp.concatenate(res_a["y"][na:2 * na], axis=1)
    of_ref[0] = jnp.concatenate([o for pair in res_b["o"][0:nb] for o in pair], axis=1)
    ob_ref[0] = jnp.concatenate([o for pair in res_b["o"][nb:2 * nb] for o in pair], axis=1)
    for i, s_i in enumerate(res_a["s"]):
        sa_ref[i // na, i % na] = s_i
    for i, pair in enumerate(res_b["s"]):
        for j, s_h in enumerate(pair):
            sb_ref[i // nb, 2 * (i % nb) + j] = s_h


def _rwkv_operands(rvk, cum, kd, bd, fwd):
    L = CHUNK
    L2 = 2 * L
    _, strict, incl = _order_masks(L2, L, fwd)
    row = lax.broadcasted_iota(jnp.int32, (L, 1), 0)
    if fwd:
        tot = cum[L - 1:L]
        cum_ex = jnp.where(row == 0, 0.0, pltpu.roll(cum, 1, axis=0))
    else:
        tot = cum[0:1]
        cum_ex = jnp.where(row == L - 1, 0.0, pltpu.roll(cum, L - 1, axis=0))
    g_inv = jnp.exp(-cum)
    g_rest = jnp.exp(tot - cum)
    g_tot = jnp.exp(tot)
    w = A_W
    r_t = rvk[:, 0:w] * jnp.exp(cum)
    v = rvk[:, w:2 * w]
    a_t = rvk[:, 2 * w:3 * w] * jnp.exp(cum_ex)
    b_t, k_t = bd * g_inv, kd * g_inv
    b_g, k_g = bd * g_rest, kd * g_rest
    low = lax.broadcasted_iota(jnp.int32, (1, LANE), 1) < A_HD

    def expand(x):
        return jnp.concatenate([jnp.where(low, x, 0.0), jnp.where(low, 0.0, x)], axis=0)

    chains = []
    for p in range(A_W // LANE):
        sl = slice(p * LANE, (p + 1) * LANE)
        chains.append(dict(
            ev=expand(v[:, sl]),
            lhs=jnp.concatenate([expand(a_t[:, sl]), expand(r_t[:, sl])], axis=0),
            rhs=jnp.concatenate([expand(b_t[:, sl]), expand(k_t[:, sl])], axis=0),
            bk=jnp.concatenate([expand(b_g[:, sl]), expand(k_g[:, sl])], axis=0),
            g_col=jnp.transpose(jnp.broadcast_to(g_tot[:, sl], (LANE, LANE))),
            strict=strict, incl=incl))
    return chains


def _square_pairs(mats, passes):
    if passes != 1 or len(mats) % 2:
        return [_dot(m, m, NN, passes) for m in mats]
    out = []
    for m0, m1 in zip(mats[0::2], mats[1::2]):
        n = m0.shape[0]
        m0, m1 = m0.astype(BF16), m1.astype(BF16)
        zero = jnp.zeros_like(m0)
        diag = jnp.concatenate([jnp.concatenate([m0, zero], axis=1), jnp.concatenate([zero, m1], axis=1)], axis=0)
        sq = lax.dot_general(jnp.concatenate([m0, m1], axis=1), diag, NN, preferred_element_type=F32)
        out += [sq[:, 0:n], sq[:, n:2 * n]]
    return out


def _rwkv_solve(chains, states, passes, res):
    L = CHUNK
    L2 = 2 * L
    each = lambda fn, *cols: [fn(*args) for args in zip(*cols)]
    quad = each(lambda c: _dot(c["lhs"], c["rhs"], NT, passes["quad"]), chains)
    yield
    es = each(lambda c, s: _dot(c["lhs"], s, NN, passes["state_in"]), chains, states)
    yield
    n_ab = each(lambda c, q: jnp.where(c["strict"], q[0:L2, 0:L2], 0.0), chains, quad)
    x = each(lambda c, q, e: e[0:L2] + _dot(jnp.where(c["strict"], q[0:L2, L2:2 * L2], 0.0), c["ev"], NN,
                                            passes["apply"]), chains, quad, es)
    yield
    x = each(lambda n, xi: xi - _dot(n, xi, NN, passes["solve"]), n_ab, x)
    yield
    npow = n_ab
    for _ in range(int(math.log2(L)) - 1):
        npow = _square_pairs(npow, passes["square"])
        yield
        x = each(lambda n, xi: xi + _dot(n, xi, NN, passes["solve"]), npow, x)
        yield
    pv = each(lambda c, xi: jnp.concatenate([-xi, c["ev"]], axis=0), chains, x)
    y2 = each(lambda c, q, e, pvi: e[L2:2 * L2] + _dot(
        jnp.concatenate([jnp.where(c["incl"], q[L2:2 * L2, 0:L2], 0.0),
                         jnp.where(c["incl"], q[L2:2 * L2, L2:2 * L2], 0.0)], axis=1), pvi, NN, passes["apply"]),
        chains, quad, es, pv)
    res["y"] = [y[0:L] + y[L:L2] for y in y2]
    yield
    res["s"] = each(lambda c, s, pvi: s * c["g_col"] + _dot(c["bk"], pvi, TN, passes["state_out"]),
                    chains, states, pv)


def _chunk_index(b, c, fwd, n_ctx_chunk, n_lat_chunk, lat_groups):
    in_ctx = c < n_ctx_chunk
    cc = c if fwd else n_ctx_chunk - 1 - c
    lc = c - n_ctx_chunk if fwd else n_lat_chunk - 1 - (c - n_ctx_chunk)
    grp = jnp.where(in_ctx, lat_groups, b)
    chunk = jnp.where(in_ctx, b * n_ctx_chunk + cc, lc)
    return grp, chunk


def _scan_specs(widths, bsz, ctx_len, t):
    ncc, nlc = ctx_len // CHUNK, t // CHUNK

    def spec(width, lane_blk, fwd):
        def index(b, c):
            grp, ch = _chunk_index(b, c, fwd, ncc, nlc, bsz)
            return grp, ch, lane_blk
        return pl.BlockSpec((1, CHUNK, width), index)

    return [[spec(w, (0 if fwd else 1) if blk is None else blk, fwd) for w, blk in widths] for fwd in (True, False)]


def _even_scan(rvk, cum, kd, bd, qkv, gb, bsz, ctx_len):
    g, t, _ = rvk.shape
    ins_f, ins_b = _scan_specs([(3 * A_W, 0), (A_W, None), (A_W, None), (A_W, None), (3 * B_W, 0), (LANE, 0)],
                               bsz, ctx_len, t)
    outs_f, outs_b = _scan_specs([(A_W, 0), (B_W, 0)], bsz, ctx_len, t)
    shapes = [jax.ShapeDtypeStruct((g, t, A_W), F32), jax.ShapeDtypeStruct((g, t, B_W), F32)]
    return pl.pallas_call(
        _even_scan_kernel,
        grid=(bsz, (ctx_len + t) // CHUNK),
        in_specs=ins_f + ins_b,
        out_specs=outs_f + outs_b,
        out_shape=shapes + shapes,
        scratch_shapes=[pltpu.VMEM((2, A_W // LANE, LANE, LANE), F32), pltpu.VMEM((2, B_H, B_HD, B_HD), F32)],
        compiler_params=_cparams(("parallel", "arbitrary")), name="even_scan",
    )(rvk, cum, kd, bd, qkv, gb, rvk, cum, kd, bd, qkv, gb)


def _gdn_prep_kernel(pb_ref, pv_ref, nx_ref, ab_ref, cw_ref, alog_ref, dtb_ref, qkv_ref, gb_ref, buf,
                     *, tt, lat_groups, ntile):
    gi, i = pl.program_id(0), pl.program_id(1)
    is_lat = gi < lat_groups
    _fill_halo(buf, pb_ref[0], pv_ref, nx_ref, jnp.logical_and(is_lat, i > 0),
               jnp.logical_and(is_lat, i < ntile - 1), tt)
    half = B_CONV // 2
    acc = buf[SUB - half:SUB - half + tt, :] * cw_ref[0:1, :]
    for j in range(1, B_CONV):
        acc = acc + buf[SUB - half + j:SUB - half + j + tt, :] * cw_ref[j:j + 1, :]
    act = acc * _sigmoid(acc)
    outs = []
    for h in range(2 * B_H):
        xh = act[:, h * B_HD:(h + 1) * B_HD]
        xh = xh * lax.rsqrt(jnp.sum(xh * xh, axis=-1, keepdims=True) + 1e-6)
        if h < B_H:
            xh = xh * (B_HD ** -0.5)
        outs.append(xh)
    outs.append(act[:, 2 * B_W:3 * B_W])
    qkv_ref[0] = jnp.concatenate(outs, axis=1)
    ab = ab_ref[0]
    lane = lax.broadcasted_iota(jnp.int32, ab.shape, 1)
    gval = -jnp.exp(alog_ref[...]) * _softplus(ab + dtb_ref[...])
    gb_ref[0] = jnp.where(lane < 2 * B_H, gval, _sigmoid(ab))


def _gdn_prep(pb, zab, conv_w, alog_row, dtb_row, tt, lat_groups):
    g, t, wb = pb.shape
    tile = lambda width: pl.BlockSpec((1, tt, width), lambda gi, i: (gi, i, 0))
    full = lambda arr: pl.BlockSpec(arr.shape, lambda gi, i: (0,) * arr.ndim)
    prev, nxt = _halo_specs(wb, tt, t)
    ab_spec = pl.BlockSpec((1, tt, LANE), lambda gi, i: (gi, i, B_W // LANE))
    return pl.pallas_call(
        functools.partial(_gdn_prep_kernel, tt=tt, lat_groups=lat_groups, ntile=t // tt),
        grid=(g, t // tt),
        in_specs=[tile(wb), prev, nxt, ab_spec, full(conv_w), full(alog_row), full(dtb_row)],
        out_specs=[tile(wb), tile(LANE)],
        out_shape=[jax.ShapeDtypeStruct((g, t, wb), F32), jax.ShapeDtypeStruct((g, t, LANE), F32)],
        scratch_shapes=[pltpu.VMEM((tt + 2 * SUB, wb), F32)],
        compiler_params=_cparams(("parallel", "parallel")), name="gdn_prep",
    )(pb, pb, pb, zab, conv_w, alog_row, dtb_row)


def _gdn_operands(qkv, gb, d):
    fwd = d == 0
    L = CHUNK
    L2 = 2 * L
    same, strict, incl = _order_masks(L2, L, fwd)
    _, _, incl_rev = _order_masks(L2, L, not fwd)
    incl_bf = jnp.where(incl, 1.0, 0.0).astype(BF16)
    incl_t_bf = jnp.where(incl_rev, 1.0, 0.0).astype(BF16)
    same_bf = jnp.where(same, 1.0, 0.0).astype(BF16)
    lane = lax.broadcasted_iota(jnp.int32, gb.shape, 1)

    def column(idx):
        return jnp.sum(jnp.where(lane == idx, gb, 0.0), axis=-1, keepdims=True)

    chains = []
    for pr in range(B_H // 2):
        heads = (2 * pr, 2 * pr + 1)
        stack = lambda off: jnp.concatenate([qkv[:, off + h * B_HD:off + (h + 1) * B_HD] for h in heads], axis=0)
        q2, k2, v2 = stack(0), stack(B_W), stack(2 * B_W)
        gcol = jnp.concatenate([column(d * B_H + h) for h in heads], axis=0)
        bcol = jnp.concatenate([column(2 * B_H + d * B_H + h) for h in heads], axis=0)
        chains.append(dict(q2=q2, k2=k2, v2=v2, bcol=bcol, g3=_three_pieces(gcol), strict=strict, incl=incl,
                           sum_bf=jnp.concatenate([incl_bf, same_bf], axis=0), incl_t_bf=incl_t_bf))
    return chains


def _three_pieces(col):
    lane = lax.broadcasted_iota(jnp.int32, (col.shape[0], LANE), 1)
    p1 = col.astype(BF16).astype(F32)
    p2 = (col - p1).astype(BF16).astype(F32)
    p3 = col - p1 - p2
    return jnp.where(lane == 0, p1, jnp.where(lane == 1, p2, jnp.where(lane == 2, p3, 0.0))).astype(BF16)


def _gdn_solve(chains, states, passes, res):
    L = CHUNK
    L2 = 2 * L
    each = lambda fn, *cols: [fn(*args) for args in zip(*cols)]
    dg = functools.partial(lax.dot_general, preferred_element_type=F32)
    sums = each(lambda c: dg(c["sum_bf"], c["g3"], NN), chains)
    sums_t = each(lambda c: dg(c["g3"], c["incl_t_bf"], TN), chains)
    k2b = each(lambda c: c["k2"] * c["bcol"], chains)
    kq = each(lambda c, kb: _dot(jnp.concatenate([kb, c["q2"]], axis=0), c["k2"], NT, passes["quad"]), chains, k2b)
    yield
    gc_row = each(lambda m: jnp.sum(m[0:L2], axis=-1, keepdims=True), sums)
    g_tot = each(lambda m: jnp.sum(m[L2:2 * L2], axis=-1, keepdims=True), sums)
    gc_col = each(lambda m: m[0:1] + m[1:2] + m[2:3], sums_t)
    decay = each(lambda c, r, cl: jnp.exp(jnp.where(c["incl"], r - cl, -1e30)), chains, gc_row, gc_col)
    egc = each(jnp.exp, gc_row)
    l_mat = each(lambda c, m, dc: jnp.where(c["strict"], m[0:L2] * dc, 0.0), chains, kq, decay)
    a_int = each(lambda c, m, dc: jnp.where(c["incl"], m[L2:2 * L2] * dc, 0.0), chains, kq, decay)
    x = each(lambda c, kb, eg: jnp.concatenate([c["v2"] * c["bcol"], kb * eg], axis=1), chains, k2b, egc)
    yield
    x = each(lambda n, xi: xi - _dot(n, xi, NN, passes["solve"]), l_mat, x)
    yield
    npow = l_mat
    for _ in range(int(math.log2(L)) - 1):
        npow = _square_pairs(npow, passes["square"])
        yield
        x = each(lambda n, xi: xi + _dot(n, xi, NN, passes["solve"]), npow, x)
        yield
    qe = each(lambda c, eg: c["q2"] * eg, chains, egc)
    rows = (slice(0, L), slice(L, L2))
    vnew = each(lambda xi, st: [xi[r, 0:B_HD] - _dot(xi[r, B_HD:2 * B_HD], s, NN, passes["state"])
                                for r, s in zip(rows, st)], x, states)
    o1 = each(lambda qi, st: [_dot(qi[r], s, NN, passes["state"]) for r, s in zip(rows, st)], qe, states)
    yield
    o2 = each(lambda a, vn, o: jnp.concatenate(o, axis=0) + _dot(a, jnp.concatenate(vn, axis=0), NN,
                                                                 passes["apply"]), a_int, vnew, o1)
    res["o"] = [[o[r] for r in rows] for o in o2]
    k_rest = each(lambda c, gt, r: c["k2"] * jnp.exp(gt - r), chains, g_tot, gc_row)
    res["s"] = each(lambda st, gt, kr, vn: [s * jnp.exp(gt[r][0:1, :]) + _dot(kr[r], v, TN, passes["state"])
                                            for r, s, v in zip(rows, st, vn)], states, g_tot, k_rest, vnew)


def _mix_out_kernel(yaf_ref, yab_ref, bg_ref, lnw_ref, lnb_ref, bd_ref, obf_ref, obb_ref, z_ref, nw_ref,
                    w_ref, *rest, k, kffn, n_lat):
    it = iter(rest)
    x_in = _rows(it, n_lat)
    mod_ref, gain_ref, rt_ref, o_ref, h_ref, lg_ref = it
    ones_bd = bd_ref[...]

    def half(rows):
        y = yaf_ref[0, rows, :] + yab_ref[0, rows, :]
        mean = _group_sum(y, ones_bd) * (1.0 / A_HD)
        yc = y - mean
        var = _group_sum(yc * yc, ones_bd) * (1.0 / A_HD)
        yn = yc * lax.rsqrt(var + A_LN_EPS) * lnw_ref[...] + lnb_ref[...]
        bg = bg_ref[0, rows, :].astype(F32)
        out_a = (yn + bg[:, 0:A_W]) * bg[:, A_W:2 * A_W]
        o = obf_ref[0, rows, :] + obb_ref[0, rows, :]
        z = z_ref[0, rows, :]
        outs = [out_a.astype(BF16)]
        for h in range(B_H):
            oh = o[:, h * B_HD:(h + 1) * B_HD]
            on = oh * lax.rsqrt(jnp.mean(oh * oh, axis=-1, keepdims=True) + NORM_EPS) * nw_ref[...]
            zh = z[:, h * B_HD:(h + 1) * B_HD]
            outs.append((on * (zh * _sigmoid(zh))).astype(BF16))
        mixed = jnp.concatenate(outs, axis=1)
        yield
        acc = jnp.dot(mixed, w_ref[...], preferred_element_type=F32)
        yield
        x = x_in[rows] + mod_ref[0, k:k + 1, :] * acc
        o_ref[0, rows, :] = x
        yield from _ffn_input(x, gain_ref, mod_ref, rt_ref, kffn, h_ref, lg_ref, rows)

    n = x_in.shape[0]
    _staggered([half(slice(0, n // 2)), half(slice(n // 2, n))])


def _mix_out(ya, bg, ln_w, ln_b, ones_bd, ob, zab, norm_w, w_out, x, mod, k, gain_ffn, router, kffn, tt):
    tile = lambda width: pl.BlockSpec((1, tt, width), lambda gi, i: (gi, i, 0))
    full = lambda arr: pl.BlockSpec(arr.shape, lambda gi, i: (0,) * arr.ndim)
    x_ins, x_specs, g, t = _rows_specs(x, tt)
    ins, specs, ospecs, oshapes = _ffn_input_specs(g, t, tt, gain_ffn, router)
    return pl.pallas_call(
        functools.partial(_mix_out_kernel, k=k, kffn=kffn, n_lat=_n_lat(x)),
        grid=(g, t // tt),
        in_specs=[tile(A_W), tile(A_W), tile(2 * A_W), full(ln_w), full(ln_b), full(ones_bd), tile(B_W),
                  tile(B_W), tile(B_W), full(norm_w), full(w_out)] + x_specs
        + [pl.BlockSpec((1, MOD_ROWS, D), lambda gi, i: (gi, 0, 0))] + specs,
        out_specs=[tile(D)] + ospecs,
        out_shape=[jax.ShapeDtypeStruct((g, t, D), F32)] + oshapes,
        compiler_params=_cparams(("parallel", "parallel")), name="mix_out_proj",
    )(ya[0], ya[1], bg, ln_w, ln_b, ones_bd, ob[0], ob[1], zab, norm_w, w_out, *x_ins, mod, *ins)


def _qkv_rope_kernel(*refs, lat_groups, q_cols, rope_cols, seg, has_res, k, kres):
    it = iter(refs)
    x_new, x = _mixer_input(it, has_res, k, kres)
    w_ref, cos_ref, sin_ref = next(it), next(it), next(it)
    if has_res:
        next(it)[0] = x_new
    o_ref = next(it)
    is_lat = pl.program_id(0) < lat_groups
    lane = lax.broadcasted_iota(jnp.int32, (1, LANE), 1)
    first = (lane % (C_HD // 2)) < (C_HD // 4)
    quarter = C_HD // 4
    cos_l = jnp.where(is_lat, cos_ref[...], 1.0)
    sin_l = jnp.where(is_lat, sin_ref[...], 0.0)
    starts = list(range(0, w_ref.shape[1], seg))
    product = lambda s0: jnp.dot(x, w_ref[:, s0:s0 + seg], preferred_element_type=F32)
    acc_next = product(starts[0])
    for idx, start in enumerate(starts):
        acc = acc_next
        if idx + 1 < len(starts):
            acc_next = product(starts[idx + 1])
        if start >= rope_cols:
            o_ref[0, :, start:start + seg] = acc.astype(o_ref.dtype)
            continue
        scale = C_HD ** -0.5 if start < q_cols else 1.0
        cos, sin = cos_l * scale, sin_l * scale
        for blk in range(seg // LANE):
            xb = acc[:, blk * LANE:(blk + 1) * LANE]
            partner = jnp.where(first, pltpu.roll(xb, LANE - quarter, axis=1), pltpu.roll(xb, quarter, axis=1))
            col = start + blk * LANE
            o_ref[0, :, col:col + LANE] = (xb * cos + partner * sin).astype(o_ref.dtype)


def _qkv_rope(x, gain, mod, k, res, w, cos, sin, lat_groups, tm, seg, q_cols, rope_cols):
    kdim, n = w.shape
    assert q_cols % seg == 0 and rope_cols % seg == 0 and n % seg == 0
    ins, specs, tile, g, t = _mixer_input_specs(x, gain, mod, res, tm)
    has_res = res is not None
    outs = list(pl.pallas_call(
        functools.partial(_qkv_rope_kernel, lat_groups=lat_groups, q_cols=q_cols, rope_cols=rope_cols, seg=seg,
                          has_res=has_res, k=k, kres=res[2] if has_res else 0),
        grid=(g, t // tm),
        in_specs=specs + [pl.BlockSpec((kdim, n), lambda gi, i: (0, 0)),
                          pl.BlockSpec((tm, LANE), lambda gi, i: (i, 0)),
                          pl.BlockSpec((tm, LANE), lambda gi, i: (i, 0))],
        out_specs=([tile] if has_res else []) + [pl.BlockSpec((1, tm, n), lambda gi, i: (gi, i, 0))],
        out_shape=([jax.ShapeDtypeStruct((g, t, D), F32)] if has_res else [])
        + [jax.ShapeDtypeStruct((g, t, n), BF16)],
        compiler_params=_cparams(("parallel", "parallel")), name="qkv_rope",
    )(*ins, w, cos, sin))
    return (outs.pop(0) if has_res else None), outs[0]


def _attn_kernel(sink_ref, q_ref, kp_ref, km_ref, kn_ref, kc_ref, o_ref, *, n_blocks, kv_w):
    n = pl.program_id(1)
    blk = C_BLOCK
    ii = lax.broadcasted_iota(jnp.int32, (blk, blk), 0)
    jj = lax.broadcasted_iota(jnp.int32, (blk, blk), 1)
    ok_prev = jnp.logical_and(jj >= ii, n > 0)
    ok_next = jnp.logical_and(jj <= ii, n < n_blocks - 1)
    n_ctx = kc_ref.shape[1]
    valid1 = jnp.concatenate([ok_prev, jnp.full((blk, blk), True), ok_next, jnp.full((blk, n_ctx), True)], axis=1)
    valid = jnp.concatenate([valid1] * C_GROUP, axis=0)
    low = lax.broadcasted_iota(jnp.int32, (1, LANE), 1) < C_HD
    row_head = lax.broadcasted_iota(jnp.int32, (C_GROUP * blk, 1), 0) // blk
    q = q_ref[0]
    kv = jnp.concatenate([kp_ref[0], km_ref[0], kn_ref[0], kc_ref[0]], axis=0)
    zero = jnp.zeros((), q.dtype)

    def scores(h):
        kh = kv[:, h * LANE:(h + 1) * LANE]
        qa = q[:, (2 * h) * LANE:(2 * h + 1) * LANE]
        qb = q[:, (2 * h + 1) * LANE:(2 * h + 2) * LANE]
        qs = jnp.concatenate([jnp.where(low, qa, zero), jnp.where(low, zero, qa),
                              jnp.where(low, qb, zero), jnp.where(low, zero, qb)], axis=0)
        return lax.dot_general(qs, kh, NT, preferred_element_type=F32)

    def softmax(h, s):
        s = jnp.where(valid, s, -1e30)
        sk = jnp.full((C_GROUP * blk, 1), 0.0, F32)
        for gq in range(C_GROUP):
            sk = jnp.where(row_head == gq, sink_ref[h * C_GROUP + gq], sk)
        m = jnp.maximum(jnp.max(s, axis=-1, keepdims=True), sk)
        p = jnp.exp(s - m)
        return p.astype(kv.dtype), jnp.sum(p, axis=-1, keepdims=True) + jnp.exp(sk - m)

    def values(h, p, denom):
        vh = kv[:, kv_w + h * LANE:kv_w + (h + 1) * LANE]
        o = lax.dot_general(p, vh, NN, preferred_element_type=F32) / denom
        return [jnp.where(low, o[0:blk], o[blk:2 * blk]).astype(o_ref.dtype),
                jnp.where(low, o[2 * blk:3 * blk], o[3 * blk:4 * blk]).astype(o_ref.dtype)]

    outs = []
    s_next = scores(0)
    for h in range(C_KVH):
        s_cur = s_next
        if h + 1 < C_KVH:
            s_next = scores(h + 1)
        outs += values(h, *softmax(h, s_cur))
    o_ref[0] = jnp.concatenate(outs, axis=1)


def _attention(qkv, sink, bsz, ctx_len):
    g, t, _ = qkv.shape
    q_w = C_QH * C_HD
    kv_w = C_KVH * LANE
    nq = q_w // (2 * kv_w)
    assert q_w % (2 * kv_w) == 0
    n_blocks = t // C_BLOCK
    cpb = ctx_len // C_BLOCK
    assert ctx_len % C_BLOCK == 0
    kvspec = lambda fn: pl.BlockSpec((1, C_BLOCK, 2 * kv_w), fn)
    return pl.pallas_call(
        functools.partial(_attn_kernel, n_blocks=n_blocks, kv_w=kv_w),
        grid=(bsz, n_blocks),
        in_specs=[pl.BlockSpec(memory_space=pltpu.SMEM),
                  pl.BlockSpec((1, C_BLOCK, q_w), lambda b, n: (b, n, 0)),
                  kvspec(lambda b, n: (b, jnp.maximum(n - 1, 0), nq)),
                  kvspec(lambda b, n: (b, n, nq)),
                  kvspec(lambda b, n: (b, jnp.minimum(n + 1, n_blocks - 1), nq)),
                  pl.BlockSpec((1, ctx_len, 2 * kv_w), lambda b, n: (bsz, b, nq))],
        out_specs=pl.BlockSpec((1, C_BLOCK, q_w), lambda b, n: (b, n, 0)),
        out_shape=jax.ShapeDtypeStruct((bsz, t, q_w), BF16),
        compiler_params=_cparams(("parallel", "parallel")), name="window_attn",
    )(sink, qkv, qkv, qkv, qkv, qkv)


def _lane_cumsum(x):
    n = x.shape[-1]
    lane = lax.broadcasted_iota(jnp.int32, x.shape, x.ndim - 1)
    k = 1
    while k < n:
        x = x + jnp.where(lane >= k, pltpu.roll(x, k, axis=x.ndim - 1), 0)
        k *= 2
    return x


def _route_kernel(lg_ref, rank_ref, aff_ref, *, cap, slot_stride):
    lg = lg_ref[0]
    m = jnp.max(lg, axis=0, keepdims=True)
    e = jnp.exp(lg - m)
    z = jnp.sum(e, axis=0, keepdims=True)
    aff = e / z
    key = (lg - m) - jnp.log(z)
    count_ge = lambda v: jnp.sum(jnp.where(key >= v, 1, 0), axis=-1, keepdims=True)

    def body(_, carry):
        lo, hi = carry
        span = hi - lo
        q1, q2, q3 = lo + 0.25 * span, lo + 0.5 * span, lo + 0.75 * span
        ok1, ok2, ok3 = count_ge(q1) >= cap, count_ge(q2) >= cap, count_ge(q3) >= cap
        new_lo = jnp.where(ok3, q3, jnp.where(ok2, q2, jnp.where(ok1, q1, lo)))
        new_hi = jnp.where(ok3, hi, jnp.where(ok2, q3, jnp.where(ok1, q2, q1)))
        return new_lo, new_hi

    lo0 = jnp.min(key, axis=-1, keepdims=True)
    lo, hi = lax.fori_loop(0, ROUTE_BISECT // 2, body, (lo0, jnp.ones_like(lo0)))
    thr, found = lo, jnp.zeros(lo.shape, jnp.int32)
    for _ in range(ROUTE_SNAP):
        v = jnp.max(jnp.where(key < hi, key, -3e38), axis=-1, keepdims=True)
        ok = jnp.where(count_ge(v) >= cap, 1, 0)
        thr = jnp.where(jnp.logical_and(found == 0, ok == 1), v, thr)
        hi = jnp.where(jnp.logical_or(found == 1, ok == 1), hi, v)
        found = jnp.maximum(found, ok)
    gt = key > thr
    eq = key == thr
    need = cap - jnp.sum(jnp.where(gt, 1, 0), axis=-1, keepdims=True)
    take_eq = jnp.logical_and(eq, _lane_cumsum(jnp.where(eq, 1, 0)) <= need)
    sel = jnp.logical_or(gt, take_eq)
    slot = _lane_cumsum(jnp.where(sel, 1, 0)) - 1 + pl.program_id(0) * slot_stride
    rank_ref[0] = jnp.where(sel, slot, -1)
    aff_ref[0] = aff


def _route(lg, cap, slot_stride):
    g, e, t = lg.shape
    spec = pl.BlockSpec((1, e, t), lambda gi: (gi, 0, 0))
    return pl.pallas_call(
        functools.partial(_route_kernel, cap=cap, slot_stride=slot_stride),
        grid=(g,), in_specs=[spec], out_specs=[spec, spec],
        out_shape=[jax.ShapeDtypeStruct((g, e, t), jnp.int32), jax.ShapeDtypeStruct((g, e, t), F32)],
        compiler_params=_cparams(("parallel",)), name="route",
    )(lg)


def _expert_kernel(h_ref, rank_ref, aff_ref, w1_ref, w3_ref, w2_ref, o_ref, *, cap):
    e = pl.program_id(1)

    @pl.when(e == 0)
    def _():
        o_ref[...] = jnp.zeros_like(o_ref)

    rank = rank_ref[0, 0]
    t = rank.shape[-1]
    hit = lax.broadcasted_iota(jnp.int32, (cap, t), 0) == rank
    pick = jnp.where(hit, 1.0, 0.0).astype(BF16)
    gate = jnp.sum(jnp.where(hit, aff_ref[0, 0], 0.0), axis=-1, keepdims=True)
    xe = jnp.dot(pick, h_ref[0], preferred_element_type=F32).astype(BF16)
    h1 = jnp.dot(xe, w1_ref[0, 0].astype(BF16), preferred_element_type=F32)
    h3 = jnp.dot(xe, w3_ref[0, 0].astype(BF16), preferred_element_type=F32)
    hid = (h1 * _sigmoid(h1) * h3).astype(BF16)
    ye = (jnp.dot(hid, w2_ref[0], preferred_element_type=F32) * gate).astype(BF16)
    o_ref[0] += lax.dot_general(pick, ye, TN, preferred_element_type=F32)


def _experts(h, rank, aff, w1, w3, w2, layer, cap):
    g, t, _ = h.shape
    _, n_exp, _, f = w1.shape
    sel = pl.BlockSpec((1, 1, 1, t), lambda gi, e: (gi, e, 0, 0))
    return pl.pallas_call(
        functools.partial(_expert_kernel, cap=cap),
        grid=(g, n_exp),
        in_specs=[pl.BlockSpec((1, t, D), lambda gi, e: (gi, 0, 0)), sel, sel,
                  pl.BlockSpec((1, 1, D, f), lambda gi, e: (layer, e, 0, 0)),
                  pl.BlockSpec((1, 1, D, f), lambda gi, e: (layer, e, 0, 0)),
                  pl.BlockSpec((1, f, D), lambda gi, e: (e, 0, 0))],
        out_specs=pl.BlockSpec((1, t, D), lambda gi, e: (gi, 0, 0)),
        out_shape=jax.ShapeDtypeStruct((g, t, D), F32),
        compiler_params=_cparams(("parallel", "arbitrary")), name="experts",
    )(h, rank.reshape(g, n_exp, 1, t), aff.reshape(g, n_exp, 1, t), w1, w3, w2)


def _moe(h, lg, w1, w3, w2, layer, bsz, ctx_len, lat_groups_only):
    g, t, _ = h.shape
    cap_lat = EC_CAP * t // N_EXP
    rank, aff = _route(lg[:bsz], cap_lat, 0)
    if not lat_groups_only:
        cap_ctx = EC_CAP * ctx_len // N_EXP
        lg_c = lg[bsz].reshape(N_EXP, bsz, ctx_len).transpose(1, 0, 2)
        rank_c, aff_c = _route(lg_c, cap_ctx, cap_ctx)
        back = lambda a: a.transpose(1, 0, 2).reshape(1, N_EXP, t)
        rank = jnp.concatenate([rank, back(rank_c)], axis=0)
        aff = jnp.concatenate([aff, back(aff_c)], axis=0)
        assert cap_ctx * bsz == cap_lat
    else:
        h = h[:bsz] if h.shape[0] != bsz else h
    return _experts(h, rank, aff, w1, w3, w2, layer, cap_lat)


def _block_diag2(w):
    z = jnp.zeros_like(w[0])
    return jnp.concatenate([jnp.concatenate([w[0], z], axis=1), jnp.concatenate([z, w[1]], axis=1)], axis=0)


def _rope_tables(t):
    quarter = C_HD // 4
    inv = ROPE_BASE ** (-jnp.arange(quarter, dtype=F32) / quarter)
    pos = jnp.arange(t)
    row = (pos // GRID_W).astype(F32)
    col = (pos % GRID_W).astype(F32)
    lane = np.arange(LANE)
    use_col = ((lane % C_HD) >= C_HD // 2)
    ang = jnp.where(use_col[None, :], col[:, None], row[:, None]) * inv[lane % quarter][None, :]
    sign = np.where((lane % (C_HD // 2)) < quarter, -1.0, 1.0).astype(np.float32)
    return jnp.cos(ang), jnp.sin(ang) * sign[None, :]


def kernel(x, c, ctx, c_ctx, ada_w, ada_b, norm_mix, norm_ffn, e_w_in, e_w_out, a_mu, a_w0, a_w2, a_a0, a_a2, a_g2,
           a_k_k, a_k_a, a_r_k, a_ln_w, a_ln_b, b_conv, b_a_log, b_dt_bias, b_norm, o_w_in, o_w_out, o_sink,
           moe_router, moe_w1, moe_w3, moe_w2, final_norm):
    bsz, t, _ = x.shape
    ctx_len = ctx.shape[1]
    depth = ada_w.shape[0]
    assert bsz * ctx_len == t and ctx_len % C_BLOCK == 0 and t % C_BLOCK == 0
    tt = ctx_len

    rows = 2 * SUB * ((bsz + 1 + 2 * SUB - 1) // (2 * SUB))
    cc = jnp.zeros((rows, D), F32).at[:bsz].set(c).at[bsz].set(c_ctx)
    mods = _adaln(cc, ada_w, ada_b)[:, :bsz + 1].reshape(depth, bsz + 1, N_MOD, D)
    mods = jnp.pad(mods, ((0, 0), (0, 0), (0, MOD_ROWS - N_MOD), (0, 0)))

    xs = (x, ctx.reshape(1, t, D))
    ones_bd = jnp.asarray(np.kron(np.eye(A_H), np.ones((A_HD, A_HD))), BF16)
    a_cols = e_w_in.shape[2] - (4 * B_W + 4 * B_H)
    cos_t, sin_t = _rope_tables(t)

    res = None
    for i in range(depth):
        j = i // 2
        mod = mods[i]
        last = i == depth - 1
        if i % 2 == 0:
            w_in = e_w_in[j]
            split_ab = a_cols + 3 * B_W
            w_all = jnp.concatenate([w_in[:, :split_ab], w_in[:, split_ab + 4 * B_H:],
                                     w_in[:, split_ab:split_ab + 4 * B_H],
                                     jnp.zeros((D, LANE - 4 * B_H), F32)], axis=1).astype(BF16)
            x_new, (pa, pb, zab) = _mm_split(xs, norm_mix[i], mod, 0, res, w_all, (a_cols, 3 * B_W, B_W + LANE),
                                             F32, 2 * tt)
            xs = xs if res is None else x_new
            prm = {
                "mu": a_mu[j].reshape(1, -1), "w0": a_w0[j].reshape(1, -1), "w2": _block_diag2(a_w2[j]),
                "a0": a_a0[j].reshape(1, -1), "a2": _block_diag2(a_a2[j]), "g2": a_g2[j],
                "k_k": a_k_k[j].reshape(1, -1), "k_a": jnp.tile(a_k_a[j].reshape(1, -1), (1, 2)),
                "r_k": a_r_k[j].reshape(1, -1), "ones_bd": ones_bd,
            }
            rvk, cum, kd, bd, bg = _rwkv_prep(pa, prm, tt, bsz)
            pad_row = lambda v: jnp.pad(v.reshape(1, -1), ((0, 0), (0, LANE - v.size)))
            qkv_c, gb = _gdn_prep(pb, zab, b_conv[j], pad_row(b_a_log[j]), pad_row(b_dt_bias[j]), tt, bsz)
            ya_f, ob_f, ya_b, ob_b = _even_scan(rvk, cum, kd, bd, qkv_c, gb, bsz, ctx_len)
            xs, hf, lg = _mix_out((ya_f, ya_b), bg, a_ln_w[j].reshape(1, -1), a_ln_b[j].reshape(1, -1), ones_bd,
                                  (ob_f, ob_b), zab, b_norm[j].reshape(1, -1), e_w_out[j].astype(BF16), xs, mod, 2,
                                  norm_ffn[i], moe_router[i], 3, 2 * tt)
        else:
            assert last
            w_in = o_w_in[j]
            q_w = C_QH * C_HD
            kv_cols = C_KVH * C_HD
            dup = lambda w: jnp.repeat(w.reshape(D, C_KVH, 1, C_HD), 2, axis=2).reshape(D, 2 * kv_cols)
            w_all = jnp.concatenate([w_in[:, :q_w], dup(w_in[:, q_w:q_w + kv_cols]), dup(w_in[:, q_w + kv_cols:])],
                                    axis=1).astype(BF16)
            tn = 2 * kv_cols
            x_new, qkv = _qkv_rope(xs, norm_mix[i], mod, 0, res, w_all, cos_t, sin_t, bsz, 2 * tt, tn, q_w, q_w + tn)
            xs = xs if res is None else x_new
            att = _attention(qkv, o_sink[j], bsz, ctx_len)
            xs, hf, lg = _mm_res(att, o_w_out[j].astype(BF16), xs, mod, 2, norm_ffn[i], moe_router[i], 3, 2 * tt,
                                 groups=bsz)
        if last:
            xs, hf, lg, mod = xs[:bsz], hf[:bsz], lg[:bsz], mod[:bsz]
        lg = lg[:, :, :N_EXP].transpose(0, 2, 1)
        delta = _moe(hf, lg, moe_w1, moe_w3, _to_bf16(moe_w2, i), i, bsz, ctx_len, last)
        res = (delta, mod, 5)
    return _final_norm(xs, res, final_norm, tt)
```

```python
import functools
import math

import jax
import jax.numpy as jnp
import numpy as np
from jax import lax
from jax.experimental import pallas as pl
from jax.experimental.pallas import tpu as pltpu

F32 = jnp.float32
BF16 = jnp.bfloat16
HIGHEST = lax.Precision.HIGHEST

D = 1024
N_MOD = 6
NORM_EPS = 1e-6
A_HD, A_H, A_W = 64, 8, 512
A_LN_EPS = 64e-5
B_HD, B_H, B_W = 128, 4, 512
B_CONV = 5
CHUNK = 64
C_HD, C_QH, C_KVH, C_GROUP = 64, 16, 4, 4
C_BLOCK = 128
ROPE_BASE = 10000.0
GRID_W = 64
N_EXP = 16
EC_CAP = 2
ROUTE_BISECT = 40
ROUTE_SNAP = 3
RWKV_PASSES = {"quad": 1, "apply": 1, "solve": "b", "square": 1, "state_in": "a", "state_out": "b"}
GDN_PASSES = {"quad": 1, "apply": 1, "solve": "b", "square": 1, "state": 1}
MOD_ROWS = 8
LANE = 128
SUB = 8
VMEM_LIMIT = 56 * 1024 * 1024

NT = (((1,), (1,)), ((), ()))
NN = (((1,), (0,)), ((), ()))
TN = (((0,), (0,)), ((), ()))


def _cparams(sem):
    return pltpu.CompilerParams(dimension_semantics=sem, vmem_limit_bytes=VMEM_LIMIT)


def _sigmoid(x):
    return 1.0 / (1.0 + jnp.exp(-x))


def _softplus(x):
    return jnp.maximum(x, 0.0) + jnp.log(1.0 + jnp.exp(-jnp.abs(x)))


def _split2(x):
    hi = x.astype(BF16)
    lo = (x - hi.astype(F32)).astype(BF16)
    return hi, lo


def _dot(a, b, dims=NN, passes=1):
    dg = functools.partial(lax.dot_general, dimension_numbers=dims, preferred_element_type=F32)
    if passes == 1:
        return dg(a.astype(BF16), b.astype(BF16))
    (ka,), (kb,) = dims[0]
    if passes == "a":
        ah, al = _split2(a)
        bh = b.astype(BF16)
        return dg(jnp.concatenate([ah, al], axis=ka), jnp.concatenate([bh, bh], axis=kb))
    if passes == "b":
        bh, bl = _split2(b)
        n = b.shape[1 - kb]
        res = dg(a.astype(BF16), jnp.concatenate([bh, bl], axis=1 - kb))
        return res[:, 0:n] + res[:, n:2 * n]
    ah, al = _split2(a)
    bh, bl = _split2(b)
    return dg(jnp.concatenate([ah, al, ah], axis=ka), jnp.concatenate([bh, bh, bl], axis=kb))


def _group_sum(x, ones_bd):
    hi, lo = _split2(x)
    return lax.dot_general(jnp.concatenate([hi, lo], axis=1), jnp.concatenate([ones_bd, ones_bd], axis=0), NN,
                           preferred_element_type=F32)


def _cast_kernel(w_ref, o_ref):
    o_ref[...] = w_ref[...].astype(o_ref.dtype)


def _to_bf16(w, layer):
    _, n, r, c = w.shape
    return pl.pallas_call(_cast_kernel, grid=(n,),
                          in_specs=[pl.BlockSpec((1, 1, r, c), lambda e: (layer, e, 0, 0))],
                          out_specs=pl.BlockSpec((1, 1, r, c), lambda e: (0, e, 0, 0)),
                          out_shape=jax.ShapeDtypeStruct((1, n, r, c), BF16),
                          compiler_params=_cparams(("parallel",)), name="cast_bf16")(w)[0]


def _adaln_kernel(c_ref, w_ref, b_ref, o_ref):
    c = c_ref[...]
    s = c * _sigmoid(c)
    o_ref[0] = jnp.dot(s, w_ref[0], precision=HIGHEST, preferred_element_type=F32) + b_ref[0]


def _adaln(cc, ada_w, ada_b):
    depth, _, n = ada_w.shape
    rows = cc.shape[0]
    tn = 768
    return pl.pallas_call(
        _adaln_kernel,
        grid=(depth, n // tn),
        in_specs=[pl.BlockSpec((rows, D), lambda i, j: (0, 0)),
                  pl.BlockSpec((1, D, tn), lambda i, j: (i, 0, j)),
                  pl.BlockSpec((1, 1, tn), lambda i, j: (i, 0, j))],
        out_specs=pl.BlockSpec((1, rows, tn), lambda i, j: (i, 0, j)),
        out_shape=jax.ShapeDtypeStruct((depth, rows, n), F32),
        compiler_params=_cparams(("parallel", "parallel")),
        name="adaln",
    )(cc, ada_w, ada_b.reshape(depth, 1, n))


def _final_norm_kernel(x_ref, d_ref, mres_ref, gain_ref, o_ref, *, kres):
    x = x_ref[0] + mres_ref[0, kres:kres + 1, :] * d_ref[0]
    o_ref[0] = x * lax.rsqrt(jnp.mean(x * x, axis=-1, keepdims=True) + NORM_EPS) * gain_ref[...]


def _final_norm(x, res, gain, tt):
    g, t, _ = x.shape
    tile = pl.BlockSpec((1, tt, D), lambda gi, i: (gi, i, 0))
    return pl.pallas_call(
        functools.partial(_final_norm_kernel, kres=res[2]),
        grid=(g, t // tt),
        in_specs=[tile, tile, pl.BlockSpec((1, MOD_ROWS, D), lambda gi, i: (gi, 0, 0)),
                  pl.BlockSpec((1, D), lambda gi, i: (0, 0))],
        out_specs=tile, out_shape=jax.ShapeDtypeStruct((g, t, D), F32),
        compiler_params=_cparams(("parallel", "parallel")), name="final_norm",
    )(x, res[0], res[1], gain.reshape(1, D))


def _rows_specs(x, tm):
    if not isinstance(x, tuple):
        return [x], [pl.BlockSpec((1, tm, D), lambda gi, i: (gi, i, 0))], x.shape[0], x.shape[1]
    lat, ctx = x
    n_lat, t = lat.shape[0], lat.shape[1]
    last = t // tm - 1
    specs = [pl.BlockSpec((1, tm, D), lambda gi, i: (jnp.minimum(gi, n_lat - 1), jnp.where(gi < n_lat, i, last), 0)),
             pl.BlockSpec((1, tm, D), lambda gi, i: (0, jnp.where(gi < n_lat, 0, i), 0))]
    return [lat, ctx], specs, n_lat + 1, t


def _rows(refs, n_lat):
    if n_lat is None:
        return next(refs)[0]
    lat_ref, ctx_ref = next(refs), next(refs)
    return jnp.where(pl.program_id(0) < n_lat, lat_ref[0], ctx_ref[0])


def _n_lat(x):
    return x[0].shape[0] if isinstance(x, tuple) else None


def _mixer_input(refs, has_res, k, kres, n_lat=None):
    x = _rows(refs, n_lat)
    if has_res:
        d_ref, mres_ref = next(refs), next(refs)
        x = x + mres_ref[0, kres:kres + 1, :] * d_ref[0]
    gain_ref, mod_ref = next(refs), next(refs)
    xn = x * lax.rsqrt(jnp.mean(x * x, axis=-1, keepdims=True) + NORM_EPS) * gain_ref[...]
    h = (xn * (1.0 + mod_ref[0, k + 1:k + 2, :]) + mod_ref[0, k:k + 1, :]).astype(BF16)
    return (x if has_res else None), h


def _mixer_input_specs(x, gain, mod, res, tm):
    tile = pl.BlockSpec((1, tm, D), lambda gi, i: (gi, i, 0))
    modspec = pl.BlockSpec((1, MOD_ROWS, D), lambda gi, i: (gi, 0, 0))
    ins, specs, g, t = _rows_specs(x, tm)
    if res is not None:
        ins += [res[0], res[1]]
        specs += [tile, modspec]
    ins += [gain.reshape(1, D), mod]
    specs += [pl.BlockSpec((1, D), lambda gi, i: (0, 0)), modspec]
    return ins, specs, tile, g, t


def _mm_split_kernel(*refs, widths, has_res, k, kres, n_lat):
    it = iter(refs)
    x_new, h = _mixer_input(it, has_res, k, kres, n_lat)
    w_ref = next(it)
    if has_res:
        next(it)[0] = x_new
    start = 0
    for o_ref, width in zip(it, widths):
        o_ref[0] = jnp.dot(h, w_ref[:, start:start + width], preferred_element_type=F32).astype(o_ref.dtype)
        start += width


def _mm_split(x, gain, mod, k, res, w, widths, out_dtype, tm):
    assert sum(widths) == w.shape[1] and all(wd % LANE == 0 for wd in widths)
    ins, specs, tile, g, t = _mixer_input_specs(x, gain, mod, res, tm)
    has_res = res is not None
    outs = list(pl.pallas_call(
        functools.partial(_mm_split_kernel, widths=widths, has_res=has_res, k=k, kres=res[2] if has_res else 0,
                          n_lat=_n_lat(x)),
        grid=(g, t // tm),
        in_specs=specs + [pl.BlockSpec(w.shape, lambda gi, i: (0, 0))],
        out_specs=([tile] if has_res else []) + [pl.BlockSpec((1, tm, wd), lambda gi, i: (gi, i, 0)) for wd in widths],
        out_shape=([jax.ShapeDtypeStruct((g, t, D), F32)] if has_res else [])
        + [jax.ShapeDtypeStruct((g, t, wd), out_dtype) for wd in widths],
        compiler_params=_cparams(("parallel", "parallel")), name="proj",
    )(*ins, w))
    return (outs.pop(0) if has_res else None), outs


def _staggered(stages):
    live = []
    for gen in stages:
        live = [g for g in live if next(g, "done") != "done"]
        live.append(gen)
    while live:
        live = [g for g in live if next(g, "done") != "done"]


def _ffn_input(x, gain_ref, mod_ref, rt_ref, k, h_ref, lg_ref, rows):
    xn = x * lax.rsqrt(jnp.mean(x * x, axis=-1, keepdims=True) + NORM_EPS) * gain_ref[...]
    xn = xn * (1.0 + mod_ref[0, k + 1:k + 2, :]) + mod_ref[0, k:k + 1, :]
    h_ref[0, rows, :] = xn.astype(h_ref.dtype)
    yield
    lg_ref[0, rows, :] = _dot(xn, rt_ref[...], NN, 3)


def _ffn_input_specs(g, t, tm, gain, router):
    ins = [gain.reshape(1, D), jnp.pad(router, ((0, 0), (0, LANE - router.shape[1])))]
    specs = [pl.BlockSpec((1, D), lambda gi, i: (0, 0)), pl.BlockSpec((D, LANE), lambda gi, i: (0, 0))]
    ospecs = [pl.BlockSpec((1, tm, D), lambda gi, i: (gi, i, 0)), pl.BlockSpec((1, tm, LANE), lambda gi, i: (gi, i, 0))]
    oshapes = [jax.ShapeDtypeStruct((g, t, D), BF16), jax.ShapeDtypeStruct((g, t, LANE), F32)]
    return ins, specs, ospecs, oshapes


def _mm_res_kernel(y_ref, w_ref, x_ref, mod_ref, gain_ref, rt_ref, o_ref, h_ref, lg_ref, *, k, kffn):
    def half(rows):
        acc = jnp.dot(y_ref[0, rows, :], w_ref[...], preferred_element_type=F32)
        yield
        x = x_ref[0, rows, :] + mod_ref[0, k:k + 1, :] * acc
        o_ref[0, rows, :] = x
        yield from _ffn_input(x, gain_ref, mod_ref, rt_ref, kffn, h_ref, lg_ref, rows)

    n = y_ref.shape[1]
    _staggered([half(slice(0, n // 2)), half(slice(n // 2, n))])


def _mm_res(y, w, x, mod, k, gain_ffn, router, kffn, tm, groups):
    _, t, kdim = y.shape
    g = groups
    ins, specs, ospecs, oshapes = _ffn_input_specs(g, t, tm, gain_ffn, router)
    return pl.pallas_call(
        functools.partial(_mm_res_kernel, k=k, kffn=kffn),
        grid=(g, t // tm),
        in_specs=[pl.BlockSpec((1, tm, kdim), lambda gi, i: (gi, i, 0)),
                  pl.BlockSpec((kdim, D), lambda gi, i: (0, 0)),
                  pl.BlockSpec((1, tm, D), lambda gi, i: (gi, i, 0)),
                  pl.BlockSpec((1, MOD_ROWS, D), lambda gi, i: (gi, 0, 0))] + specs,
        out_specs=[pl.BlockSpec((1, tm, D), lambda gi, i: (gi, i, 0))] + ospecs,
        out_shape=[jax.ShapeDtypeStruct((g, t, D), F32)] + oshapes,
        compiler_params=_cparams(("parallel", "parallel")), name="out_proj_res",
    )(y, w, x, mod, *ins)


def _fill_halo(buf, x, pv_ref, nx_ref, has_prev, has_next, tt):
    buf[0:SUB, :] = jnp.where(has_prev, pv_ref[0], 0.0)
    buf[SUB:SUB + tt, :] = x
    buf[SUB + tt:2 * SUB + tt, :] = jnp.where(has_next, nx_ref[0], 0.0)


def _rwkv_prep_kernel(pa_ref, pv_ref, nx_ref, mu_ref, w0_ref, w2_ref, a0_ref, a2_ref, g2_ref, kk_ref, ka_ref,
                      rk_ref, bd_ref, rvk_ref, cum_ref, kd_ref, bdir_ref, bg_ref, buf, *, tt, lat_groups, ntile):
    gi, i = pl.program_id(0), pl.program_id(1)
    is_lat = gi < lat_groups
    x = pa_ref[0]
    _fill_halo(buf, x, pv_ref, nx_ref, jnp.logical_and(is_lat, i > 0), jnp.logical_and(is_lat, i < ntile - 1), tt)
    xm = buf[SUB - 1:SUB - 1 + tt, :]
    xp = buf[SUB + 1:SUB + 1 + tt, :]
    x = x + mu_ref[...] * (0.5 * (xm + xp) - x)
    w = A_W
    r, kx, v = x[:, 0:w], x[:, w:2 * w], x[:, 2 * w:3 * w]
    wd, ad, gd = x[:, 3 * w:3 * w + LANE], x[:, 3 * w + LANE:3 * w + 2 * LANE], x[:, 3 * w + 2 * LANE:3 * w + 3 * LANE]
    u = w0_ref[...] + _dot(jnp.tanh(wd), w2_ref[...], NN, 3)
    lw = -math.exp(-0.5) * _sigmoid(u)
    a = _sigmoid(a0_ref[...] + _dot(ad, a2_ref[...], NN, 3))
    gate = _dot(_sigmoid(gd), g2_ref[...], NN, 3)
    ones_bd = bd_ref[...]
    kq = kx * kk_ref[...]
    kk = kq * lax.rsqrt(_group_sum(kq * kq, ones_bd) + 1e-6)
    k2 = jnp.concatenate([kx, kx], axis=1)
    kdir = k2 * (1.0 + (a - 1.0) * ka_ref[...])
    bdir = jnp.concatenate([kk, kk], axis=1) * a
    rr = r * rk_ref[...]
    bonus = _group_sum(rr * (kdir[:, 0:w] + kdir[:, w:2 * w]), ones_bd) * v
    rvk_ref[0] = jnp.concatenate([r, v, kk], axis=1)
    cum_ref[0] = jnp.concatenate([_chunk_cumsum(lw[:, 0:w], True), _chunk_cumsum(lw[:, w:2 * w], False)], axis=1)
    kd_ref[0] = kdir.astype(kd_ref.dtype)
    bdir_ref[0] = bdir.astype(bdir_ref.dtype)
    bg_ref[0] = jnp.concatenate([bonus, gate], axis=1).astype(bg_ref.dtype)


def _chunk_cumsum(x, fwd):
    _, _, incl = _order_masks(x.shape[0], CHUNK, fwd)
    mask = jnp.where(incl, 1.0, 0.0).astype(BF16)
    p1 = x.astype(BF16)
    r1 = x - p1.astype(F32)
    p2 = r1.astype(BF16)
    p3 = (r1 - p2.astype(F32)).astype(BF16)
    return lax.dot_general(jnp.concatenate([mask, mask, mask], axis=1), jnp.concatenate([p1, p2, p3], axis=0), NN,
                           preferred_element_type=F32)


def _halo_specs(width, tt, t):
    nb = tt // SUB
    last = t // SUB - 1
    prev = pl.BlockSpec((1, SUB, width), lambda gi, i: (gi, jnp.maximum(i * nb - 1, 0), 0))
    nxt = pl.BlockSpec((1, SUB, width), lambda gi, i: (gi, jnp.minimum((i + 1) * nb, last), 0))
    return prev, nxt


def _rwkv_prep(pa, p, tt, lat_groups):
    g, t, wa = pa.shape
    full = lambda arr: pl.BlockSpec(arr.shape, lambda gi, i: (0,) * arr.ndim)
    tile = lambda width: pl.BlockSpec((1, tt, width), lambda gi, i: (gi, i, 0))
    prev, nxt = _halo_specs(wa, tt, t)
    consts = [p["mu"], p["w0"], p["w2"], p["a0"], p["a2"], p["g2"], p["k_k"], p["k_a"], p["r_k"], p["ones_bd"]]
    widths = (3 * A_W, 2 * A_W, 2 * A_W, 2 * A_W, 2 * A_W)
    return pl.pallas_call(
        functools.partial(_rwkv_prep_kernel, tt=tt, lat_groups=lat_groups, ntile=t // tt),
        grid=(g, t // tt),
        in_specs=[tile(wa), prev, nxt] + [full(c) for c in consts],
        out_specs=[tile(wd) for wd in widths],
        out_shape=[jax.ShapeDtypeStruct((g, t, wd), dt) for wd, dt in zip(widths, (F32, F32, BF16, BF16, BF16))],
        scratch_shapes=[pltpu.VMEM((tt + 2 * SUB, wa), F32)],
        compiler_params=_cparams(("parallel", "parallel")), name="rwkv_prep",
    )(pa, pa, pa, *consts)


def _order_masks(n_rows, blk, fwd):
    ii = lax.broadcasted_iota(jnp.int32, (n_rows, n_rows), 0)
    jj = lax.broadcasted_iota(jnp.int32, (n_rows, n_rows), 1)
    same = (ii // blk) == (jj // blk)
    before = (jj < ii) if fwd else (jj > ii)
    strict = jnp.logical_and(same, before)
    incl = jnp.logical_and(same, jnp.logical_or(before, ii == jj))
    return same, strict, incl


def _even_scan_kernel(rvkf_ref, cumf_ref, kdf_ref, bdf_ref, qkvf_ref, gbf_ref,
                      rvkb_ref, cumb_ref, kdb_ref, bdb_ref, qkvb_ref, gbb_ref,
                      yf_ref, of_ref, yb_ref, ob_ref, sa_ref, sb_ref):
    @pl.when(pl.program_id(1) == 0)
    def _():
        sa_ref[...] = jnp.zeros_like(sa_ref)
        sb_ref[...] = jnp.zeros_like(sb_ref)

    na, nb = A_W // LANE, B_H // 2
    sa, sb = sa_ref[...], sb_ref[...]
    chains_a = (_rwkv_operands(rvkf_ref[0], cumf_ref[0], kdf_ref[0], bdf_ref[0], True)
                + _rwkv_operands(rvkb_ref[0], cumb_ref[0], kdb_ref[0], bdb_ref[0], False))
    chains_b = _gdn_operands(qkvf_ref[0], gbf_ref[0], 0) + _gdn_operands(qkvb_ref[0], gbb_ref[0], 1)
    states_a = [sa[d, p] for d in range(2) for p in range(na)]
    states_b = [(sb[d, 2 * p], sb[d, 2 * p + 1]) for d in range(2) for p in range(nb)]
    res_a, res_b = {}, {}
    lead = _rwkv_solve(chains_a, states_a, RWKV_PASSES, res_a)
    next(lead)
    _staggered([lead, _gdn_solve(chains_b, states_b, GDN_PASSES, res_b)])
    yf_ref[0] = jnp.concatenate(res_a["y"][0:na], axis=1)
    yb_ref[0] = jnp.concatenate(res_a["y"][na:2 * na], axis=1)
    of_ref[0] = jnp.concatenate([o for pair in res_b["o"][0:nb] for o in pair], axis=1)
    ob_ref[0] = jnp.concatenate([o for pair in res_b["o"][nb:2 * nb] for o in pair], axis=1)
    for i, s_i in enumerate(res_a["s"]):
        sa_ref[i // na, i % na] = s_i
    for i, pair in enumerate(res_b["s"]):
        for j, s_h in enumerate(pair):
            sb_ref[i // nb, 2 * (i % nb) + j] = s_h


def _rwkv_operands(rvk, cum, kd, bd, fwd):
    L = CHUNK
    L2 = 2 * L
    _, strict, incl = _order_masks(L2, L, fwd)
    row = lax.broadcasted_iota(jnp.int32, (L, 1), 0)
    if fwd:
        tot = cum[L - 1:L]
        cum_ex = jnp.where(row == 0, 0.0, pltpu.roll(cum, 1, axis=0))
    else:
        tot = cum[0:1]
        cum_ex = jnp.where(row == L - 1, 0.0, pltpu.roll(cum, L - 1, axis=0))
    g_inv = jnp.exp(-cum)
    g_rest = jnp.exp(tot - cum)
    g_tot = jnp.exp(tot)
    w = A_W
    r_t = rvk[:, 0:w] * jnp.exp(cum)
    v = rvk[:, w:2 * w]
    a_t = rvk[:, 2 * w:3 * w] * jnp.exp(cum_ex)
    b_t, k_t = bd * g_inv, kd * g_inv
    b_g, k_g = bd * g_rest, kd * g_rest
    low = lax.broadcasted_iota(jnp.int32, (1, LANE), 1) < A_HD

    def expand(x):
        return jnp.concatenate([jnp.where(low, x, 0.0), jnp.where(low, 0.0, x)], axis=0)

    chains = []
    for p in range(A_W // LANE):
        sl = slice(p * LANE, (p + 1) * LANE)
        chains.append(dict(
            ev=expand(v[:, sl]),
            lhs=jnp.concatenate([expand(a_t[:, sl]), expand(r_t[:, sl])], axis=0),
            rhs=jnp.concatenate([expand(b_t[:, sl]), expand(k_t[:, sl])], axis=0),
            bk=jnp.concatenate([expand(b_g[:, sl]), expand(k_g[:, sl])], axis=0),
            g_col=jnp.transpose(jnp.broadcast_to(g_tot[:, sl], (LANE, LANE))),
            strict=strict, incl=incl))
    return chains


def _square_pairs(mats, passes):
    if passes != 1 or len(mats) % 2:
        return [_dot(m, m, NN, passes) for m in mats]
    out = []
    for m0, m1 in zip(mats[0::2], mats[1::2]):
        n = m0.shape[0]
        m0, m1 = m0.astype(BF16), m1.astype(BF16)
        zero = jnp.zeros_like(m0)
        diag = jnp.concatenate([jnp.concatenate([m0, zero], axis=1), jnp.concatenate([zero, m1], axis=1)], axis=0)
        sq = lax.dot_general(jnp.concatenate([m0, m1], axis=1), diag, NN, preferred_element_type=F32)
        out += [sq[:, 0:n], sq[:, n:2 * n]]
    return out


def _rwkv_solve(chains, states, passes, res):
    L = CHUNK
    L2 = 2 * L
    each = lambda fn, *cols: [fn(*args) for args in zip(*cols)]
    quad = each(lambda c: _dot(c["lhs"], c["rhs"], NT, passes["quad"]), chains)
    yield
    es = each(lambda c, s: _dot(c["lhs"], s, NN, passes["state_in"]), chains, states)
    yield
    n_ab = each(lambda c, q: jnp.where(c["strict"], q[0:L2, 0:L2], 0.0), chains, quad)
    x = each(lambda c, q, e: e[0:L2] + _dot(jnp.where(c["strict"], q[0:L2, L2:2 * L2], 0.0), c["ev"], NN,
                                            passes["apply"]), chains, quad, es)
    yield
    x = each(lambda n, xi: xi - _dot(n, xi, NN, passes["solve"]), n_ab, x)
    yield
    npow = n_ab
    for _ in range(int(math.log2(L)) - 1):
        npow = _square_pairs(npow, passes["square"])
        yield
        x = each(lambda n, xi: xi + _dot(n, xi, NN, passes["solve"]), npow, x)
        yield
    pv = each(lambda c, xi: jnp.concatenate([-xi, c["ev"]], axis=0), chains, x)
    y2 = each(lambda c, q, e, pvi: e[L2:2 * L2] + _dot(
        jnp.concatenate([jnp.where(c["incl"], q[L2:2 * L2, 0:L2], 0.0),
                         jnp.where(c["incl"], q[L2:2 * L2, L2:2 * L2], 0.0)], axis=1), pvi, NN, passes["apply"]),
        chains, quad, es, pv)
    res["y"] = [y[0:L] + y[L:L2] for y in y2]
    yield
    res["s"] = each(lambda c, s, pvi: s * c["g_col"] + _dot(c["bk"], pvi, TN, passes["state_out"]),
                    chains, states, pv)


def _chunk_index(b, c, fwd, n_ctx_chunk, n_lat_chunk, lat_groups):
    in_ctx = c < n_ctx_chunk
    cc = c if fwd else n_ctx_chunk - 1 - c
    lc = c - n_ctx_chunk if fwd else n_lat_chunk - 1 - (c - n_ctx_chunk)
    grp = jnp.where(in_ctx, lat_groups, b)
    chunk = jnp.where(in_ctx, b * n_ctx_chunk + cc, lc)
    return grp, chunk


def _scan_specs(widths, bsz, ctx_len, t):
    ncc, nlc = ctx_len // CHUNK, t // CHUNK

    def spec(width, lane_blk, fwd):
        def index(b, c):
            grp, ch = _chunk_index(b, c, fwd, ncc, nlc, bsz)
            return grp, ch, lane_blk
        return pl.BlockSpec((1, CHUNK, width), index)

    return [[spec(w, (0 if fwd else 1) if blk is None else blk, fwd) for w, blk in widths] for fwd in (True, False)]


def _even_scan(rvk, cum, kd, bd, qkv, gb, bsz, ctx_len):
    g, t, _ = rvk.shape
    ins_f, ins_b = _scan_specs([(3 * A_W, 0), (A_W, None), (A_W, None), (A_W, None), (3 * B_W, 0), (LANE, 0)],
                               bsz, ctx_len, t)
    outs_f, outs_b = _scan_specs([(A_W, 0), (B_W, 0)], bsz, ctx_len, t)
    shapes = [jax.ShapeDtypeStruct((g, t, A_W), F32), jax.ShapeDtypeStruct((g, t, B_W), F32)]
    return pl.pallas_call(
        _even_scan_kernel,
        grid=(bsz, (ctx_len + t) // CHUNK),
        in_specs=ins_f + ins_b,
        out_specs=outs_f + outs_b,
        out_shape=shapes + shapes,
        scratch_shapes=[pltpu.VMEM((2, A_W // LANE, LANE, LANE), F32), pltpu.VMEM((2, B_H, B_HD, B_HD), F32)],
        compiler_params=_cparams(("parallel", "arbitrary")), name="even_scan",
    )(rvk, cum, kd, bd, qkv, gb, rvk, cum, kd, bd, qkv, gb)


def _gdn_prep_kernel(pb_ref, pv_ref, nx_ref, ab_ref, cw_ref, alog_ref, dtb_ref, qkv_ref, gb_ref, buf,
                     *, tt, lat_groups, ntile):
    gi, i = pl.program_id(0), pl.program_id(1)
    is_lat = gi < lat_groups
    _fill_halo(buf, pb_ref[0], pv_ref, nx_ref, jnp.logical_and(is_lat, i > 0),
               jnp.logical_and(is_lat, i < ntile - 1), tt)
    half = B_CONV // 2
    acc = buf[SUB - half:SUB - half + tt, :] * cw_ref[0:1, :]
    for j in range(1, B_CONV):
        acc = acc + buf[SUB - half + j:SUB - half + j + tt, :] * cw_ref[j:j + 1, :]
    act = acc * _sigmoid(acc)
    outs = []
    for h in range(2 * B_H):
        xh = act[:, h * B_HD:(h + 1) * B_HD]
        xh = xh * lax.rsqrt(jnp.sum(xh * xh, axis=-1, keepdims=True) + 1e-6)
        if h < B_H:
            xh = xh * (B_HD ** -0.5)
        outs.append(xh)
    outs.append(act[:, 2 * B_W:3 * B_W])
    qkv_ref[0] = jnp.concatenate(outs, axis=1)
    ab = ab_ref[0]
    lane = lax.broadcasted_iota(jnp.int32, ab.shape, 1)
    gval = -jnp.exp(alog_ref[...]) * _softplus(ab + dtb_ref[...])
    gb_ref[0] = jnp.where(lane < 2 * B_H, gval, _sigmoid(ab))


def _gdn_prep(pb, zab, conv_w, alog_row, dtb_row, tt, lat_groups):
    g, t, wb = pb.shape
    tile = lambda width: pl.BlockSpec((1, tt, width), lambda gi, i: (gi, i, 0))
    full = lambda arr: pl.BlockSpec(arr.shape, lambda gi, i: (0,) * arr.ndim)
    prev, nxt = _halo_specs(wb, tt, t)
    ab_spec = pl.BlockSpec((1, tt, LANE), lambda gi, i: (gi, i, B_W // LANE))
    return pl.pallas_call(
        functools.partial(_gdn_prep_kernel, tt=tt, lat_groups=lat_groups, ntile=t // tt),
        grid=(g, t // tt),
        in_specs=[tile(wb), prev, nxt, ab_spec, full(conv_w), full(alog_row), full(dtb_row)],
        out_specs=[tile(wb), tile(LANE)],
        out_shape=[jax.ShapeDtypeStruct((g, t, wb), F32), jax.ShapeDtypeStruct((g, t, LANE), F32)],
        scratch_shapes=[pltpu.VMEM((tt + 2 * SUB, wb), F32)],
        compiler_params=_cparams(("parallel", "parallel")), name="gdn_prep",
    )(pb, pb, pb, zab, conv_w, alog_row, dtb_row)


def _gdn_operands(qkv, gb, d):
    fwd = d == 0
    L = CHUNK
    L2 = 2 * L
    same, strict, incl = _order_masks(L2, L, fwd)
    _, _, incl_rev = _order_masks(L2, L, not fwd)
    incl_bf = jnp.where(incl, 1.0, 0.0).astype(BF16)
    incl_t_bf = jnp.where(incl_rev, 1.0, 0.0).astype(BF16)
    same_bf = jnp.where(same, 1.0, 0.0).astype(BF16)
    lane = lax.broadcasted_iota(jnp.int32, gb.shape, 1)

    def column(idx):
        return jnp.sum(jnp.where(lane == idx, gb, 0.0), axis=-1, keepdims=True)

    chains = []
    for pr in range(B_H // 2):
        heads = (2 * pr, 2 * pr + 1)
        stack = lambda off: jnp.concatenate([qkv[:, off + h * B_HD:off + (h + 1) * B_HD] for h in heads], axis=0)
        q2, k2, v2 = stack(0), stack(B_W), stack(2 * B_W)
        gcol = jnp.concatenate([column(d * B_H + h) for h in heads], axis=0)
        bcol = jnp.concatenate([column(2 * B_H + d * B_H + h) for h in heads], axis=0)
        chains.append(dict(q2=q2, k2=k2, v2=v2, bcol=bcol, g3=_three_pieces(gcol), strict=strict, incl=incl,
                           sum_bf=jnp.concatenate([incl_bf, same_bf], axis=0), incl_t_bf=incl_t_bf))
    return chains


def _three_pieces(col):
    lane = lax.broadcasted_iota(jnp.int32, (col.shape[0], LANE), 1)
    p1 = col.astype(BF16).astype(F32)
    p2 = (col - p1).astype(BF16).astype(F32)
    p3 = col - p1 - p2
    return jnp.where(lane == 0, p1, jnp.where(lane == 1, p2, jnp.where(lane == 2, p3, 0.0))).astype(BF16)


def _gdn_solve(chains, states, passes, res):
    L = CHUNK
    L2 = 2 * L
    each = lambda fn, *cols: [fn(*args) for args in zip(*cols)]
    dg = functools.partial(lax.dot_general, preferred_element_type=F32)
    sums = each(lambda c: dg(c["sum_bf"], c["g3"], NN), chains)
    sums_t = each(lambda c: dg(c["g3"], c["incl_t_bf"], TN), chains)
    k2b = each(lambda c: c["k2"] * c["bcol"], chains)
    kq = each(lambda c, kb: _dot(jnp.concatenate([kb, c["q2"]], axis=0), c["k2"], NT, passes["quad"]), chains, k2b)
    yield
    gc_row = each(lambda m: jnp.sum(m[0:L2], axis=-1, keepdims=True), sums)
    g_tot = each(lambda m: jnp.sum(m[L2:2 * L2], axis=-1, keepdims=True), sums)
    gc_col = each(lambda m: m[0:1] + m[1:2] + m[2:3], sums_t)
    decay = each(lambda c, r, cl: jnp.exp(jnp.where(c["incl"], r - cl, -1e30)), chains, gc_row, gc_col)
    egc = each(jnp.exp, gc_row)
    l_mat = each(lambda c, m, dc: jnp.where(c["strict"], m[0:L2] * dc, 0.0), chains, kq, decay)
    a_int = each(lambda c, m, dc: jnp.where(c["incl"], m[L2:2 * L2] * dc, 0.0), chains, kq, decay)
    x = each(lambda c, kb, eg: jnp.concatenate([c["v2"] * c["bcol"], kb * eg], axis=1), chains, k2b, egc)
    yield
    x = each(lambda n, xi: xi - _dot(n, xi, NN, passes["solve"]), l_mat, x)
    yield
    npow = l_mat
    for _ in range(int(math.log2(L)) - 1):
        npow = _square_pairs(npow, passes["square"])
        yield
        x = each(lambda n, xi: xi + _dot(n, xi, NN, passes["solve"]), npow, x)
        yield
    qe = each(lambda c, eg: c["q2"] * eg, chains, egc)
    rows = (slice(0, L), slice(L, L2))
    vnew = each(lambda xi, st: [xi[r, 0:B_HD] - _dot(xi[r, B_HD:2 * B_HD], s, NN, passes["state"])
                                for r, s in zip(rows, st)], x, states)
    o1 = each(lambda qi, st: [_dot(qi[r], s, NN, passes["state"]) for r, s in zip(rows, st)], qe, states)
    yield
    o2 = each(lambda a, vn, o: jnp.concatenate(o, axis=0) + _dot(a, jnp.concatenate(vn, axis=0), NN,
                                                                 passes["apply"]), a_int, vnew, o1)
    res["o"] = [[o[r] for r in rows] for o in o2]
    k_rest = each(lambda c, gt, r: c["k2"] * jnp.exp(gt - r), chains, g_tot, gc_row)
    res["s"] = each(lambda st, gt, kr, vn: [s * jnp.exp(gt[r][0:1, :]) + _dot(kr[r], v, TN, passes["state"])
                                            for r, s, v in zip(rows, st, vn)], states, g_tot, k_rest, vnew)


def _mix_out_kernel(yaf_ref, yab_ref, bg_ref, lnw_ref, lnb_ref, bd_ref, obf_ref, obb_ref, z_ref, nw_ref,
                    w_ref, *rest, k, kffn, n_lat):
    it = iter(rest)
    x_in = _rows(it, n_lat)
    mod_ref, gain_ref, rt_ref, o_ref, h_ref, lg_ref = it
    ones_bd = bd_ref[...]

    def half(rows):
        y = yaf_ref[0, rows, :] + yab_ref[0, rows, :]
        mean = _group_sum(y, ones_bd) * (1.0 / A_HD)
        yc = y - mean
        var = _group_sum(yc * yc, ones_bd) * (1.0 / A_HD)
        yn = yc * lax.rsqrt(var + A_LN_EPS) * lnw_ref[...] + lnb_ref[...]
        bg = bg_ref[0, rows, :].astype(F32)
        out_a = (yn + bg[:, 0:A_W]) * bg[:, A_W:2 * A_W]
        o = obf_ref[0, rows, :] + obb_ref[0, rows, :]
        z = z_ref[0, rows, :]
        outs = [out_a.astype(BF16)]
        for h in range(B_H):
            oh = o[:, h * B_HD:(h + 1) * B_HD]
            on = oh * lax.rsqrt(jnp.mean(oh * oh, axis=-1, keepdims=True) + NORM_EPS) * nw_ref[...]
            zh = z[:, h * B_HD:(h + 1) * B_HD]
            outs.append((on * (zh * _sigmoid(zh))).astype(BF16))
        mixed = jnp.concatenate(outs, axis=1)
        yield
        acc = jnp.dot(mixed, w_ref[...], preferred_element_type=F32)
        yield
        x = x_in[rows] + mod_ref[0, k:k + 1, :] * acc
        o_ref[0, rows, :] = x
        yield from _ffn_input(x, gain_ref, mod_ref, rt_ref, kffn, h_ref, lg_ref, rows)

    n = x_in.shape[0]
    _staggered([half(slice(0, n // 2)), half(slice(n // 2, n))])


def _mix_out(ya, bg, ln_w, ln_b, ones_bd, ob, zab, norm_w, w_out, x, mod, k, gain_ffn, router, kffn, tt):
    tile = lambda width: pl.BlockSpec((1, tt, width), lambda gi, i: (gi, i, 0))
    full = lambda arr: pl.BlockSpec(arr.shape, lambda gi, i: (0,) * arr.ndim)
    x_ins, x_specs, g, t = _rows_specs(x, tt)
    ins, specs, ospecs, oshapes = _ffn_input_specs(g, t, tt, gain_ffn, router)
    return pl.pallas_call(
        functools.partial(_mix_out_kernel, k=k, kffn=kffn, n_lat=_n_lat(x)),
        grid=(g, t // tt),
        in_specs=[tile(A_W), tile(A_W), tile(2 * A_W), full(ln_w), full(ln_b), full(ones_bd), tile(B_W),
                  tile(B_W), tile(B_W), full(norm_w), full(w_out)] + x_specs
        + [pl.BlockSpec((1, MOD_ROWS, D), lambda gi, i: (gi, 0, 0))] + specs,
        out_specs=[tile(D)] + ospecs,
        out_shape=[jax.ShapeDtypeStruct((g, t, D), F32)] + oshapes,
        compiler_params=_cparams(("parallel", "parallel")), name="mix_out_proj",
    )(ya[0], ya[1], bg, ln_w, ln_b, ones_bd, ob[0], ob[1], zab, norm_w, w_out, *x_ins, mod, *ins)


def _qkv_rope_kernel(*refs, lat_groups, q_cols, rope_cols, seg, has_res, k, kres):
    it = iter(refs)
    x_new, x = _mixer_input(it, has_res, k, kres)
    w_ref, cos_ref, sin_ref = next(it), next(it), next(it)
    if has_res:
        next(it)[0] = x_new
    o_ref = next(it)
    is_lat = pl.program_id(0) < lat_groups
    lane = lax.broadcasted_iota(jnp.int32, (1, LANE), 1)
    first = (lane % (C_HD // 2)) < (C_HD // 4)
    quarter = C_HD // 4
    cos_l = jnp.where(is_lat, cos_ref[...], 1.0)
    sin_l = jnp.where(is_lat, sin_ref[...], 0.0)
    starts = list(range(0, w_ref.shape[1], seg))
    product = lambda s0: jnp.dot(x, w_ref[:, s0:s0 + seg], preferred_element_type=F32)
    acc_next = product(starts[0])
    for idx, start in enumerate(starts):
        acc = acc_next
        if idx + 1 < len(starts):
            acc_next = product(starts[idx + 1])
        if start >= rope_cols:
            o_ref[0, :, start:start + seg] = acc.astype(o_ref.dtype)
            continue
        scale = C_HD ** -0.5 if start < q_cols else 1.0
        cos, sin = cos_l * scale, sin_l * scale
        for blk in range(seg // LANE):
            xb = acc[:, blk * LANE:(blk + 1) * LANE]
            partner = jnp.where(first, pltpu.roll(xb, LANE - quarter, axis=1), pltpu.roll(xb, quarter, axis=1))
            col = start + blk * LANE
            o_ref[0, :, col:col + LANE] = (xb * cos + partner * sin).astype(o_ref.dtype)


def _qkv_rope(x, gain, mod, k, res, w, cos, sin, lat_groups, tm, seg, q_cols, rope_cols):
    kdim, n = w.shape
    assert q_cols % seg == 0 and rope_cols % seg == 0 and n % seg == 0
    ins, specs, tile, g, t = _mixer_input_specs(x, gain, mod, res, tm)
    has_res = res is not None
    outs = list(pl.pallas_call(
        functools.partial(_qkv_rope_kernel, lat_groups=lat_groups, q_cols=q_cols, rope_cols=rope_cols, seg=seg,
                          has_res=has_res, k=k, kres=res[2] if has_res else 0),
        grid=(g, t // tm),
        in_specs=specs + [pl.BlockSpec((kdim, n), lambda gi, i: (0, 0)),
                          pl.BlockSpec((tm, LANE), lambda gi, i: (i, 0)),
                          pl.BlockSpec((tm, LANE), lambda gi, i: (i, 0))],
        out_specs=([tile] if has_res else []) + [pl.BlockSpec((1, tm, n), lambda gi, i: (gi, i, 0))],
        out_shape=([jax.ShapeDtypeStruct((g, t, D), F32)] if has_res else [])
        + [jax.ShapeDtypeStruct((g, t, n), BF16)],
        compiler_params=_cparams(("parallel", "parallel")), name="qkv_rope",
    )(*ins, w, cos, sin))
    return (outs.pop(0) if has_res else None), outs[0]


def _attn_kernel(sink_ref, q_ref, kp_ref, km_ref, kn_ref, kc_ref, o_ref, *, n_blocks, kv_w):
    n = pl.program_id(1)
    blk = C_BLOCK
    ii = lax.broadcasted_iota(jnp.int32, (blk, blk), 0)
    jj = lax.broadcasted_iota(jnp.int32, (blk, blk), 1)
    ok_prev = jnp.logical_and(jj >= ii, n > 0)
    ok_next = jnp.logical_and(jj <= ii, n < n_blocks - 1)
    n_ctx = kc_ref.shape[1]
    valid1 = jnp.concatenate([ok_prev, jnp.full((blk, blk), True), ok_next, jnp.full((blk, n_ctx), True)], axis=1)
    valid = jnp.concatenate([valid1] * C_GROUP, axis=0)
    low = lax.broadcasted_iota(jnp.int32, (1, LANE), 1) < C_HD
    row_head = lax.broadcasted_iota(jnp.int32, (C_GROUP * blk, 1), 0) // blk
    q = q_ref[0]
    kv = jnp.concatenate([kp_ref[0], km_ref[0], kn_ref[0], kc_ref[0]], axis=0)
    zero = jnp.zeros((), q.dtype)

    def scores(h):
        kh = kv[:, h * LANE:(h + 1) * LANE]
        qa = q[:, (2 * h) * LANE:(2 * h + 1) * LANE]
        qb = q[:, (2 * h + 1) * LANE:(2 * h + 2) * LANE]
        qs = jnp.concatenate([jnp.where(low, qa, zero), jnp.where(low, zero, qa),
                              jnp.where(low, qb, zero), jnp.where(low, zero, qb)], axis=0)
        return lax.dot_general(qs, kh, NT, preferred_element_type=F32)

    def softmax(h, s):
        s = jnp.where(valid, s, -1e30)
        sk = jnp.full((C_GROUP * blk, 1), 0.0, F32)
        for gq in range(C_GROUP):
            sk = jnp.where(row_head == gq, sink_ref[h * C_GROUP + gq], sk)
        m = jnp.maximum(jnp.max(s, axis=-1, keepdims=True), sk)
        p = jnp.exp(s - m)
        return p.astype(kv.dtype), jnp.sum(p, axis=-1, keepdims=True) + jnp.exp(sk - m)

    def values(h, p, denom):
        vh = kv[:, kv_w + h * LANE:kv_w + (h + 1) * LANE]
        o = lax.dot_general(p, vh, NN, preferred_element_type=F32) / denom
        return [jnp.where(low, o[0:blk], o[blk:2 * blk]).astype(o_ref.dtype),
                jnp.where(low, o[2 * blk:3 * blk], o[3 * blk:4 * blk]).astype(o_ref.dtype)]

    outs = []
    pending = [scores(0), scores(1)]
    for h in range(C_KVH):
        s_cur = pending.pop(0)
        if h + 2 < C_KVH:
            pending.append(scores(h + 2))
        outs += values(h, *softmax(h, s_cur))
    o_ref[0] = jnp.concatenate(outs, axis=1)


def _attention(qkv, sink, bsz, ctx_len):
    g, t, _ = qkv.shape
    q_w = C_QH * C_HD
    kv_w = C_KVH * LANE
    nq = q_w // (2 * kv_w)
    assert q_w % (2 * kv_w) == 0
    n_blocks = t // C_BLOCK
    cpb = ctx_len // C_BLOCK
    assert ctx_len % C_BLOCK == 0
    kvspec = lambda fn: pl.BlockSpec((1, C_BLOCK, 2 * kv_w), fn)
    return pl.pallas_call(
        functools.partial(_attn_kernel, n_blocks=n_blocks, kv_w=kv_w),
        grid=(bsz, n_blocks),
        in_specs=[pl.BlockSpec(memory_space=pltpu.SMEM),
                  pl.BlockSpec((1, C_BLOCK, q_w), lambda b, n: (b, n, 0)),
                  kvspec(lambda b, n: (b, jnp.maximum(n - 1, 0), nq)),
                  kvspec(lambda b, n: (b, n, nq)),
                  kvspec(lambda b, n: (b, jnp.minimum(n + 1, n_blocks - 1), nq)),
                  pl.BlockSpec((1, ctx_len, 2 * kv_w), lambda b, n: (bsz, b, nq))],
        out_specs=pl.BlockSpec((1, C_BLOCK, q_w), lambda b, n: (b, n, 0)),
        out_shape=jax.ShapeDtypeStruct((bsz, t, q_w), BF16),
        compiler_params=_cparams(("parallel", "parallel")), name="window_attn",
    )(sink, qkv, qkv, qkv, qkv, qkv)


def _lane_cumsum(x):
    n = x.shape[-1]
    lane = lax.broadcasted_iota(jnp.int32, x.shape, x.ndim - 1)
    k = 1
    while k < n:
        x = x + jnp.where(lane >= k, pltpu.roll(x, k, axis=x.ndim - 1), 0)
        k *= 2
    return x


def _route_kernel(lg_ref, rank_ref, aff_ref, *, cap, slot_stride):
    lg = lg_ref[0]
    m = jnp.max(lg, axis=0, keepdims=True)
    e = jnp.exp(lg - m)
    z = jnp.sum(e, axis=0, keepdims=True)
    aff = e / z
    key = (lg - m) - jnp.log(z)
    count_ge = lambda v: jnp.sum(jnp.where(key >= v, 1, 0), axis=-1, keepdims=True)

    def body(_, carry):
        lo, hi = carry
        span = hi - lo
        q1, q2, q3 = lo + 0.25 * span, lo + 0.5 * span, lo + 0.75 * span
        ok1, ok2, ok3 = count_ge(q1) >= cap, count_ge(q2) >= cap, count_ge(q3) >= cap
        new_lo = jnp.where(ok3, q3, jnp.where(ok2, q2, jnp.where(ok1, q1, lo)))
        new_hi = jnp.where(ok3, hi, jnp.where(ok2, q3, jnp.where(ok1, q2, q1)))
        return new_lo, new_hi

    lo0 = jnp.min(key, axis=-1, keepdims=True)
    lo, hi = lax.fori_loop(0, ROUTE_BISECT // 2, body, (lo0, jnp.ones_like(lo0)))
    thr, found = lo, jnp.zeros(lo.shape, jnp.int32)
    for _ in range(ROUTE_SNAP):
        v = jnp.max(jnp.where(key < hi, key, -3e38), axis=-1, keepdims=True)
        ok = jnp.where(count_ge(v) >= cap, 1, 0)
        thr = jnp.where(jnp.logical_and(found == 0, ok == 1), v, thr)
        hi = jnp.where(jnp.logical_or(found == 1, ok == 1), hi, v)
        found = jnp.maximum(found, ok)
    gt = key > thr
    eq = key == thr
    need = cap - jnp.sum(jnp.where(gt, 1, 0), axis=-1, keepdims=True)
    take_eq = jnp.logical_and(eq, _lane_cumsum(jnp.where(eq, 1, 0)) <= need)
    sel = jnp.logical_or(gt, take_eq)
    slot = _lane_cumsum(jnp.where(sel, 1, 0)) - 1 + pl.program_id(0) * slot_stride
    rank_ref[0] = jnp.where(sel, slot, -1)
    aff_ref[0] = aff


def _route(lg, cap, slot_stride):
    g, e, t = lg.shape
    spec = pl.BlockSpec((1, e, t), lambda gi: (gi, 0, 0))
    return pl.pallas_call(
        functools.partial(_route_kernel, cap=cap, slot_stride=slot_stride),
        grid=(g,), in_specs=[spec], out_specs=[spec, spec],
        out_shape=[jax.ShapeDtypeStruct((g, e, t), jnp.int32), jax.ShapeDtypeStruct((g, e, t), F32)],
        compiler_params=_cparams(("parallel",)), name="route",
    )(lg)


def _expert_kernel(h_ref, rank_ref, aff_ref, w1_ref, w3_ref, w2_ref, o_ref, *, cap):
    e = pl.program_id(1)

    @pl.when(e == 0)
    def _():
        o_ref[...] = jnp.zeros_like(o_ref)

    rank = rank_ref[0, 0]
    t = rank.shape[-1]
    hit = lax.broadcasted_iota(jnp.int32, (cap, t), 0) == rank
    pick = jnp.where(hit, 1.0, 0.0).astype(BF16)
    gate = jnp.sum(jnp.where(hit, aff_ref[0, 0], 0.0), axis=-1, keepdims=True)
    xe = jnp.dot(pick, h_ref[0], preferred_element_type=F32).astype(BF16)
    h1 = jnp.dot(xe, w1_ref[0, 0].astype(BF16), preferred_element_type=F32)
    h3 = jnp.dot(xe, w3_ref[0, 0].astype(BF16), preferred_element_type=F32)
    hid = (h1 * _sigmoid(h1) * h3).astype(BF16)
    ye = (jnp.dot(hid, w2_ref[0], preferred_element_type=F32) * gate).astype(BF16)
    o_ref[0] += lax.dot_general(pick, ye, TN, preferred_element_type=F32)


def _experts(h, rank, aff, w1, w3, w2, layer, cap):
    g, t, _ = h.shape
    _, n_exp, _, f = w1.shape
    sel = pl.BlockSpec((1, 1, 1, t), lambda gi, e: (gi, e, 0, 0))
    return pl.pallas_call(
        functools.partial(_expert_kernel, cap=cap),
        grid=(g, n_exp),
        in_specs=[pl.BlockSpec((1, t, D), lambda gi, e: (gi, 0, 0)), sel, sel,
                  pl.BlockSpec((1, 1, D, f), lambda gi, e: (layer, e, 0, 0)),
                  pl.BlockSpec((1, 1, D, f), lambda gi, e: (layer, e, 0, 0)),
                  pl.BlockSpec((1, f, D), lambda gi, e: (e, 0, 0))],
        out_specs=pl.BlockSpec((1, t, D), lambda gi, e: (gi, 0, 0)),
        out_shape=jax.ShapeDtypeStruct((g, t, D), F32),
        compiler_params=_cparams(("parallel", "arbitrary")), name="experts",
    )(h, rank.reshape(g, n_exp, 1, t), aff.reshape(g, n_exp, 1, t), w1, w3, w2)


def _moe(h, lg, w1, w3, w2, layer, bsz, ctx_len, lat_groups_only):
    g, t, _ = h.shape
    cap_lat = EC_CAP * t // N_EXP
    rank, aff = _route(lg[:bsz], cap_lat, 0)
    if not lat_groups_only:
        cap_ctx = EC_CAP * ctx_len // N_EXP
        lg_c = lg[bsz].reshape(N_EXP, bsz, ctx_len).transpose(1, 0, 2)
        rank_c, aff_c = _route(lg_c, cap_ctx, cap_ctx)
        back = lambda a: a.transpose(1, 0, 2).reshape(1, N_EXP, t)
        rank = jnp.concatenate([rank, back(rank_c)], axis=0)
        aff = jnp.concatenate([aff, back(aff_c)], axis=0)
        assert cap_ctx * bsz == cap_lat
    else:
        h = h[:bsz] if h.shape[0] != bsz else h
    return _experts(h, rank, aff, w1, w3, w2, layer, cap_lat)


def _block_diag2(w):
    z = jnp.zeros_like(w[0])
    return jnp.concatenate([jnp.concatenate([w[0], z], axis=1), jnp.concatenate([z, w[1]], axis=1)], axis=0)


def _rope_tables(t):
    quarter = C_HD // 4
    inv = ROPE_BASE ** (-jnp.arange(quarter, dtype=F32) / quarter)
    pos = jnp.arange(t)
    row = (pos // GRID_W).astype(F32)
    col = (pos % GRID_W).astype(F32)
    lane = np.arange(LANE)
    use_col = ((lane % C_HD) >= C_HD // 2)
    ang = jnp.where(use_col[None, :], col[:, None], row[:, None]) * inv[lane % quarter][None, :]
    sign = np.where((lane % (C_HD // 2)) < quarter, -1.0, 1.0).astype(np.float32)
    return jnp.cos(ang), jnp.sin(ang) * sign[None, :]


def kernel(x, c, ctx, c_ctx, ada_w, ada_b, norm_mix, norm_ffn, e_w_in, e_w_out, a_mu, a_w0, a_w2, a_a0, a_a2, a_g2,
           a_k_k, a_k_a, a_r_k, a_ln_w, a_ln_b, b_conv, b_a_log, b_dt_bias, b_norm, o_w_in, o_w_out, o_sink,
           moe_router, moe_w1, moe_w3, moe_w2, final_norm):
    bsz, t, _ = x.shape
    ctx_len = ctx.shape[1]
    depth = ada_w.shape[0]
    assert bsz * ctx_len == t and ctx_len % C_BLOCK == 0 and t % C_BLOCK == 0
    tt = ctx_len

    rows = 2 * SUB * ((bsz + 1 + 2 * SUB - 1) // (2 * SUB))
    cc = jnp.zeros((rows, D), F32).at[:bsz].set(c).at[bsz].set(c_ctx)
    mods = _adaln(cc, ada_w, ada_b)[:, :bsz + 1].reshape(depth, bsz + 1, N_MOD, D)
    mods = jnp.pad(mods, ((0, 0), (0, 0), (0, MOD_ROWS - N_MOD), (0, 0)))

    xs = (x, ctx.reshape(1, t, D))
    ones_bd = jnp.asarray(np.kron(np.eye(A_H), np.ones((A_HD, A_HD))), BF16)
    a_cols = e_w_in.shape[2] - (4 * B_W + 4 * B_H)
    cos_t, sin_t = _rope_tables(t)

    res = None
    for i in range(depth):
        j = i // 2
        mod = mods[i]
        last = i == depth - 1
        if i % 2 == 0:
            w_in = e_w_in[j]
            split_ab = a_cols + 3 * B_W
            w_all = jnp.concatenate([w_in[:, :split_ab], w_in[:, split_ab + 4 * B_H:],
                                     w_in[:, split_ab:split_ab + 4 * B_H],
                                     jnp.zeros((D, LANE - 4 * B_H), F32)], axis=1).astype(BF16)
            x_new, (pa, pb, zab) = _mm_split(xs, norm_mix[i], mod, 0, res, w_all, (a_cols, 3 * B_W, B_W + LANE),
                                             F32, 2 * tt)
            xs = xs if res is None else x_new
            prm = {
                "mu": a_mu[j].reshape(1, -1), "w0": a_w0[j].reshape(1, -1), "w2": _block_diag2(a_w2[j]),
                "a0": a_a0[j].reshape(1, -1), "a2": _block_diag2(a_a2[j]), "g2": a_g2[j],
                "k_k": a_k_k[j].reshape(1, -1), "k_a": jnp.tile(a_k_a[j].reshape(1, -1), (1, 2)),
                "r_k": a_r_k[j].reshape(1, -1), "ones_bd": ones_bd,
            }
            rvk, cum, kd, bd, bg = _rwkv_prep(pa, prm, tt, bsz)
            pad_row = lambda v: jnp.pad(v.reshape(1, -1), ((0, 0), (0, LANE - v.size)))
            qkv_c, gb = _gdn_prep(pb, zab, b_conv[j], pad_row(b_a_log[j]), pad_row(b_dt_bias[j]), tt, bsz)
            ya_f, ob_f, ya_b, ob_b = _even_scan(rvk, cum, kd, bd, qkv_c, gb, bsz, ctx_len)
            xs, hf, lg = _mix_out((ya_f, ya_b), bg, a_ln_w[j].reshape(1, -1), a_ln_b[j].reshape(1, -1), ones_bd,
                                  (ob_f, ob_b), zab, b_norm[j].reshape(1, -1), e_w_out[j].astype(BF16), xs, mod, 2,
                                  norm_ffn[i], moe_router[i], 3, 2 * tt)
        else:
            assert last
            w_in = o_w_in[j]
            q_w = C_QH * C_HD
            kv_cols = C_KVH * C_HD
            dup = lambda w: jnp.repeat(w.reshape(D, C_KVH, 1, C_HD), 2, axis=2).reshape(D, 2 * kv_cols)
            w_all = jnp.concatenate([w_in[:, :q_w], dup(w_in[:, q_w:q_w + kv_cols]), dup(w_in[:, q_w + kv_cols:])],
                                    axis=1).astype(BF16)
            tn = 2 * kv_cols
            x_new, qkv = _qkv_rope(xs, norm_mix[i], mod, 0, res, w_all, cos_t, sin_t, bsz, 2 * tt, tn, q_w, q_w + tn)
            xs = xs if res is None else x_new
            att = _attention(qkv, o_sink[j], bsz, ctx_len)
            xs, hf, lg = _mm_res(att, o_w_out[j].astype(BF16), xs, mod, 2, norm_ffn[i], moe_router[i], 3, 2 * tt,
                                 groups=bsz)
        if last:
            xs, hf, lg, mod = xs[:bsz], hf[:bsz], lg[:bsz], mod[:bsz]
        lg = lg[:, :, :N_EXP].transpose(0, 2, 1)
        delta = _moe(hf, lg, moe_w1, moe_w3, _to_bf16(moe_w2, i), i, bsz, ctx_len, last)
        res = (delta, mod, 5)
    return _final_norm(xs, res, final_norm, t // 2)
```
